```python
import math
import jax, jax.numpy as jnp
from jax import lax
import numpy as np

D_MODEL = 1024
BATCH = 4
SEQ = 4096
DEPTH = 2

CHUNK = 64
HEAD_DIM = 64
MIX_WIDTH = D_MODEL
N_HEADS_TOTAL = MIX_WIDTH // HEAD_DIM
SWA_HEADS = N_HEADS_TOTAL // 2
SWA_KV_HEADS = SWA_HEADS // 4
FOX_HEADS = N_HEADS_TOTAL // 4
MLSTM_HEADS = N_HEADS_TOTAL - SWA_HEADS - FOX_HEADS
SWA_Q = SWA_HEADS * HEAD_DIM
SWA_KV = SWA_KV_HEADS * HEAD_DIM
FOX_W = FOX_HEADS * HEAD_DIM
MLSTM_W = MLSTM_HEADS * HEAD_DIM
IN_WIDTH = SWA_Q + 2 * SWA_KV + 3 * FOX_W + FOX_HEADS + 4 * MLSTM_W + 2 * MLSTM_HEADS
WINDOW = 128
WIN_CHUNKS = WINDOW // CHUNK
QBLK = 128
D_FF = 4 * D_MODEL
ROPE_THETA = 10000.0
EPS = 1e-6

kernel_name = "hybrid_swa_fox_mlstm_parallel_heads"


def _rms(x, gain):
    xf = x.astype(jnp.float32)
    y = xf * lax.rsqrt(jnp.mean(xf * xf, axis=-1, keepdims=True) + EPS)
    return (y * gain.astype(jnp.float32)).astype(x.dtype)


def _rope(x, pos):
    half = HEAD_DIM // 2
    inv = ROPE_THETA ** (-jnp.arange(half, dtype=jnp.float32) / half)
    ang = pos.astype(jnp.float32)[:, None] * inv[None, :]
    cos = jnp.cos(ang)[None, :, None, :]
    sin = jnp.sin(ang)[None, :, None, :]
    xf = x.astype(jnp.float32)
    x1, x2 = xf[..., :half], xf[..., half:]
    return jnp.concatenate([x1 * cos - x2 * sin, x2 * cos + x1 * sin], axis=-1).astype(x.dtype)


def _split_in_proj(z):
    sizes = (SWA_Q, SWA_KV, SWA_KV, FOX_W, FOX_W, FOX_W, FOX_HEADS,
             MLSTM_W, MLSTM_W, MLSTM_W, MLSTM_HEADS, MLSTM_HEADS, MLSTM_W)
    idx = np.cumsum(sizes)[:-1].tolist()
    return jnp.split(z, idx, axis=-1)


def _swa_attention(q, k, v, sinks):
    B, S = q.shape[0], q.shape[1]
    nb = S // QBLK
    G = SWA_HEADS // SWA_KV_HEADS
    qb = q.reshape(B, nb, QBLK, SWA_KV_HEADS, G, HEAD_DIM)

    def band(t):
        tb = t.reshape(B, nb, QBLK, SWA_KV_HEADS, HEAD_DIM)
        prev = jnp.concatenate([jnp.zeros_like(tb[:, :1]), tb[:, :-1]], axis=1)
        return jnp.concatenate([prev, tb], axis=2)

    kb, vb = band(k), band(v)
    s = jnp.einsum('bnqhgd,bnkhd->bnhgqk', qb, kb).astype(jnp.float32) / math.sqrt(HEAD_DIM)
    qpos = jnp.arange(nb)[:, None] * QBLK + jnp.arange(QBLK)[None, :]
    kpos = jnp.arange(nb)[:, None] * QBLK - QBLK + jnp.arange(2 * QBLK)[None, :]
    qc = (qpos // CHUNK)[:, :, None]
    kc = (kpos // CHUNK)[:, None, :]
    allowed = (kpos[:, None, :] >= 0) & (kc <= qc) & (kc >= qc - WIN_CHUNKS)
    s = jnp.where(allowed[None, :, None, None], s, -jnp.inf)
    sink = sinks.astype(jnp.float32).reshape(SWA_KV_HEADS, G)[None, None, :, :, None, None]
    m = jnp.maximum(jnp.max(s, axis=-1, keepdims=True), sink)
    p = jnp.exp(s - m)
    p = p / (jnp.sum(p, axis=-1, keepdims=True) + jnp.exp(sink - m))
    o = jnp.einsum('bnhgqk,bnkhd->bnqhgd', p.astype(v.dtype), vb)
    return o.reshape(B, S, SWA_Q)


def _fox_attention(q, k, v, log_f):
    B, S, H, D = q.shape
    nb = S // QBLK
    FT = jnp.cumsum(log_f, axis=1).transpose(0, 2, 1)
    qb = q.reshape(B, nb, QBLK, H, D).transpose(1, 0, 2, 3, 4)
    Fb = FT.reshape(B, H, nb, QBLK).transpose(2, 0, 1, 3)
    kpos = jnp.arange(S)

    def block(args):
        qi, Fi, n = args
        s = jnp.einsum('bqhd,bkhd->bhqk', qi, k).astype(jnp.float32) / math.sqrt(D)
        s = s + Fi[..., None] - FT[:, :, None, :]
        qpos = n * QBLK + jnp.arange(QBLK)
        s = jnp.where((kpos[None, :] <= qpos[:, None])[None, None], s, -jnp.inf)
        p = jax.nn.softmax(s, axis=-1).astype(v.dtype)
        return jnp.einsum('bhqk,bkhd->bqhd', p, v)

    o = lax.map(block, (qb, Fb, jnp.arange(nb)))
    return o.transpose(1, 0, 2, 3, 4).reshape(B, S, H * D)


def _mlstm(q, k, v, i_pre, f_pre):
    B, S, H, D = q.shape
    nc, L = S // CHUNK, CHUNK

    def chunks(t):
        return t.astype(jnp.float32).reshape(B, nc, L, H, D).transpose(0, 3, 1, 2, 4)

    qc, kc, vc = chunks(q), chunks(k) / math.sqrt(D), chunks(v)
    ig = i_pre.reshape(B, nc, L, H).transpose(0, 3, 1, 2)
    lf = jax.nn.log_sigmoid(f_pre).reshape(B, nc, L, H).transpose(0, 3, 1, 2)
    b = jnp.cumsum(lf, axis=-1)
    g = b[..., -1]
    causal = jnp.tril(jnp.ones((L, L), dtype=bool))
    Dm = jnp.where(causal, b[..., :, None] - b[..., None, :] + ig[..., None, :], -jnp.inf)
    a = g[..., None] - b + ig
    a_max = jnp.max(a, axis=-1)
    wa = jnp.exp(a - a_max[..., None])
    C_loc = jnp.einsum('bhcl,bhclv,bhclk->bhcvk', wa, vc, kc)
    n_loc = jnp.einsum('bhcl,bhclk->bhck', wa, kc)

    def step(carry, inp):
        C, n, m = carry
        Cl, nl, gl, aml = inp
        m_new = jnp.maximum(gl + m, aml)
        s_old = jnp.exp(gl + m - m_new)
        s_loc = jnp.exp(aml - m_new)
        C_new = s_old[..., None, None] * C + s_loc[..., None, None] * Cl
        n_new = s_old[..., None] * n + s_loc[..., None] * nl
        return (C_new, n_new, m_new), (C, n, m)

    init = (jnp.zeros((B, H, D, D), jnp.float32), jnp.zeros((B, H, D), jnp.float32),
            jnp.zeros((B, H), jnp.float32))
    xs = (C_loc.transpose(2, 0, 1, 3, 4), n_loc.transpose(2, 0, 1, 3),
          g.transpose(2, 0, 1), a_max.transpose(2, 0, 1))
    _, (C_prev, n_prev, m_prev) = lax.scan(step, init, xs)
    C_prev = C_prev.transpose(1, 2, 0, 3, 4)
    n_prev = n_prev.transpose(1, 2, 0, 3)
    m_prev = m_prev.transpose(1, 2, 0)

    inter = b + m_prev[..., None]
    m_t = jnp.maximum(inter, jnp.max(Dm, axis=-1))
    w_inter = jnp.exp(inter - m_t)
    Sm = jnp.einsum('bhcld,bhcsd->bhcls', qc, kc) * jnp.exp(Dm - m_t[..., None])
    num = (w_inter[..., None] * jnp.einsum('bhcvk,bhclk->bhclv', C_prev, qc)
           + jnp.einsum('bhcls,bhcsv->bhclv', Sm, vc))
    den = w_inter * jnp.einsum('bhck,bhclk->bhcl', n_prev, qc) + jnp.sum(Sm, axis=-1)
    h = num / jnp.maximum(jnp.abs(den), jnp.exp(-m_t))[..., None]
    return h.transpose(0, 2, 3, 1, 4).reshape(B, S, H, D)


def _layer(x, norm1, w_in, swa_q_norm, swa_k_norm, swa_sinks, fox_q_norm, fox_k_norm,
           fox_f_bias, mlstm_i_bias, mlstm_f_bias, mlstm_out_norm, w_out, norm2, w_ff1, w_ff2):
    B, S, _ = x.shape
    h = _rms(x, norm1)
    z = h @ w_in
    (aq, ak, av, bq, bk, bv, bf, cq, ck, cv, ci, cf, co) = _split_in_proj(z)
    pos = jnp.arange(S)

    aq = _rope(_rms(aq.reshape(B, S, SWA_HEADS, HEAD_DIM), swa_q_norm), pos)
    ak = _rope(_rms(ak.reshape(B, S, SWA_KV_HEADS, HEAD_DIM), swa_k_norm), pos)
    ya = _swa_attention(aq, ak, av.reshape(B, S, SWA_KV_HEADS, HEAD_DIM), swa_sinks)

    bq = _rms(bq.reshape(B, S, FOX_HEADS, HEAD_DIM), fox_q_norm)
    bk = _rms(bk.reshape(B, S, FOX_HEADS, HEAD_DIM), fox_k_norm)
    log_f = jax.nn.log_sigmoid(bf.astype(jnp.float32) + fox_f_bias.astype(jnp.float32))
    yb = _fox_attention(bq, bk, bv.reshape(B, S, FOX_HEADS, HEAD_DIM), log_f)

    hc = _mlstm(cq.reshape(B, S, MLSTM_HEADS, HEAD_DIM), ck.reshape(B, S, MLSTM_HEADS, HEAD_DIM),
                cv.reshape(B, S, MLSTM_HEADS, HEAD_DIM),
                ci.astype(jnp.float32) + mlstm_i_bias.astype(jnp.float32),
                cf.astype(jnp.float32) + mlstm_f_bias.astype(jnp.float32))
    hc = _rms(hc, mlstm_out_norm)
    og = jax.nn.sigmoid(co.astype(jnp.float32)).reshape(B, S, MLSTM_HEADS, HEAD_DIM)
    yc = (og * hc).reshape(B, S, MLSTM_W).astype(x.dtype)

    x = x + jnp.concatenate([ya, yb, yc], axis=-1) @ w_out
    u = _rms(x, norm2) @ w_ff1
    return x + jnp.square(jax.nn.relu(u)) @ w_ff2


def setup_inputs(seed: int = 0) -> dict:
    key = jax.random.key(seed)
    ks = jax.random.split(key, 17)
    f32 = jnp.float32

    def gain(k, shape):
        return 1.0 + 0.02 * jax.random.normal(k, shape, f32)

    return {
        "x": jax.random.normal(ks[0], (BATCH, SEQ, D_MODEL), f32),
        "norm1": gain(ks[1], (DEPTH, D_MODEL)),
        "w_in": jax.random.normal(ks[2], (DEPTH, D_MODEL, IN_WIDTH), f32) * D_MODEL ** -0.5,
        "swa_q_norm": gain(ks[3], (DEPTH, HEAD_DIM)),
        "swa_k_norm": gain(ks[4], (DEPTH, HEAD_DIM)),
        "swa_sinks": jax.random.normal(ks[5], (DEPTH, SWA_HEADS), f32),
        "fox_q_norm": gain(ks[6], (DEPTH, HEAD_DIM)),
        "fox_k_norm": gain(ks[7], (DEPTH, HEAD_DIM)),
        "fox_f_bias": jax.random.uniform(ks[8], (DEPTH, FOX_HEADS), f32, 2.0, 6.0),
        "mlstm_i_bias": 0.1 * jax.random.normal(ks[9], (DEPTH, MLSTM_HEADS), f32),
        "mlstm_f_bias": jax.random.uniform(ks[10], (DEPTH, MLSTM_HEADS), f32, 3.0, 6.0),
        "mlstm_out_norm": gain(ks[11], (DEPTH, MLSTM_HEADS, HEAD_DIM)),
        "w_out": jax.random.normal(ks[12], (DEPTH, MIX_WIDTH, D_MODEL), f32) * MIX_WIDTH ** -0.5,
        "norm2": gain(ks[13], (DEPTH, D_MODEL)),
        "w_ff1": jax.random.normal(ks[14], (DEPTH, D_MODEL, D_FF), f32) * D_MODEL ** -0.5,
        "w_ff2": jax.random.normal(ks[15], (DEPTH, D_FF, D_MODEL), f32) * D_FF ** -0.5,
    }


def reference(x, norm1, w_in, swa_q_norm, swa_k_norm, swa_sinks, fox_q_norm, fox_k_norm,
              fox_f_bias, mlstm_i_bias, mlstm_f_bias, mlstm_out_norm, w_out, norm2, w_ff1, w_ff2):
    for l in range(DEPTH):
        x = _layer(x, norm1[l], w_in[l], swa_q_norm[l], swa_k_norm[l], swa_sinks[l],
                   fox_q_norm[l], fox_k_norm[l], fox_f_bias[l], mlstm_i_bias[l], mlstm_f_bias[l],
                   mlstm_out_norm[l], w_out[l], norm2[l], w_ff1[l], w_ff2[l])
    return x
```

```python
import functools
import math

import numpy as np
import jax
import jax.numpy as jnp
from jax import lax
from jax.experimental import pallas as pl
from jax.experimental.pallas import tpu as pltpu

F32 = jnp.float32
BF16 = jnp.bfloat16

D_MODEL = 1024
HEAD = 64
HALF = HEAD // 2
CHUNK = 64
SWA_HEADS = 8
SWA_KV_HEADS = 2
FOX_HEADS = 4
MLSTM_HEADS = 4
SWA_Q = SWA_HEADS * HEAD
SWA_KV = SWA_KV_HEADS * HEAD
FOX_W = FOX_HEADS * HEAD
MLSTM_W = MLSTM_HEADS * HEAD
D_FF = 4 * D_MODEL
QBLK = 128
ROPE_THETA = 10000.0
EPS = 1e-6
QK_SCALE = 1.0 / math.sqrt(HEAD)

LANES = 128
MXU_DIM = 256
VMEM_LIMIT = 56 * 1024 * 1024

SEC_A = 0
SEC_B = SEC_A + 768
SEC_C = SEC_B + 768
SEC_CO = SEC_C + 768
SEC_CI = SEC_CO + 256
SEC_CF = SEC_CI + 256
SEC_G = SEC_CF + 256
W_MAIN = SEC_G + LANES


def _in_proj_columns():
    o_aq, o_ak, o_av = 0, SWA_Q, SWA_Q + SWA_KV
    o_bq = o_av + SWA_KV
    o_bk, o_bv = o_bq + FOX_W, o_bq + 2 * FOX_W
    o_bf = o_bv + FOX_W
    o_cq = o_bf + FOX_HEADS
    o_ck, o_cv = o_cq + MLSTM_W, o_cq + 2 * MLSTM_W
    o_ci = o_cv + MLSTM_W
    o_cf = o_ci + MLSTM_HEADS
    o_co = o_cf + MLSTM_HEADS
    idx = np.full((W_MAIN,), -1, np.int64)
    lane = np.arange(LANES)
    quarter, i = lane // HALF, lane % HALF
    dim = i + HALF * (quarter // 2)
    is_b = quarter % 2
    for j in range(SWA_HEADS // 2):
        idx[SEC_A + j * LANES + lane] = o_aq + (j + 4 * is_b) * HEAD + dim
    idx[SEC_A + SWA_Q + lane] = o_ak + is_b * HEAD + dim
    idx[SEC_A + SWA_Q + SWA_KV + lane] = o_av + lane
    w = np.arange(FOX_W)
    idx[SEC_B + w] = o_bq + w
    idx[SEC_B + FOX_W + w] = o_bk + w
    idx[SEC_B + 2 * FOX_W + w] = o_bv + w
    idx[SEC_C + w] = o_cq + w
    idx[SEC_C + MLSTM_W + w] = o_ck + w
    idx[SEC_C + 2 * MLSTM_W + w] = o_cv + w
    idx[SEC_CO + w] = o_co + w
    idx[SEC_CI + w] = o_ci + w // HEAD
    idx[SEC_CF + w] = o_cf + w // HEAD
    for rep in range(3):
        idx[SEC_G + rep * FOX_HEADS + np.arange(FOX_HEADS)] = o_bf + np.arange(FOX_HEADS)
    return idx


def _swa_lane_dim():
    lane = np.arange(LANES)
    return lane % HALF + HALF * (lane // HEAD)


def _wout_rows():
    rows = np.arange(D_MODEL)
    r = np.arange(SWA_Q)
    j, half, d = r // LANES, (r % LANES) // HEAD, r % HEAD
    rows[:SWA_Q] = (j + 4 * half) * HEAD + d
    return rows


_IN_COLS = _in_proj_columns()
_WOUT_ROWS = _wout_rows()


def _block_diag_ones(head_of_lane):
    return (head_of_lane[:, None] == head_of_lane[None, :]).astype(np.float32)


_lane = np.arange(LANES)
_BD_SWA = np.concatenate([_block_diag_ones((_lane // HALF) % 2)] * 2, axis=0)
_BD_NAT = np.concatenate([_block_diag_ones(_lane // HEAD)] * 2, axis=0)
_TRI256 = np.tril(np.ones((MXU_DIM, MXU_DIM), np.float32))
_TRI64 = np.tril(np.ones((CHUNK, CHUNK), np.float32))
_w256 = np.arange(MLSTM_W)
_BD256 = _block_diag_ones(_w256 // HEAD)
_I4 = (np.arange(CHUNK)[:, None] == (_w256 % HEAD)[None, :]).astype(np.float32)


def _const_spec(shape):
    nd = len(shape)
    return pl.BlockSpec(shape, lambda *_: (0,) * nd, pipeline_mode=pl.Buffered(1))


def _log_sigmoid(x):
    return jnp.minimum(x, 0.0) - jnp.log(1.0 + jnp.exp(-jnp.abs(x)))


def _split2(x):
    hi = x.astype(BF16)
    lo = (x - hi.astype(F32)).astype(BF16)
    return hi, lo


def _split3_by_lane(x, lane):
    hi = x.astype(BF16).astype(F32)
    r1 = x - hi
    mid = r1.astype(BF16).astype(F32)
    low = r1 - mid
    return jnp.where(lane < 4, hi, jnp.where(lane < 8, mid, low)).astype(BF16)


def _dot(a, b):
    return jnp.dot(a, b, preferred_element_type=F32)


def _dot_nt(a, b):
    return lax.dot_general(a, b, (((1,), (1,)), ((), ())), preferred_element_type=F32)


def _dot_tn(a, b):
    return lax.dot_general(a, b, (((0,), (0,)), ((), ())), preferred_element_type=F32)


TM_IN = 512


def _in_proj_body(x_ref, n1_ref, w_ref, vec_ref, cos_ref, sin_ref, bds_ref, bdn_ref, tri_ref,
                  qa_ref, kva_ref, qb_ref, kb_ref, vb_ref, qkvc_ref, gc_ref, carry_ref):
    tm = x_ref.shape[1]

    @pl.when(pl.program_id(1) == 0)
    def _():
        carry_ref[...] = jnp.zeros_like(carry_ref)

    x = x_ref[0]
    ms = jnp.mean(x * x, axis=-1, keepdims=True)
    h = ((x * lax.rsqrt(ms + EPS)) * n1_ref[...]).astype(BF16)

    def proj(c0, width):
        return _dot(h, w_ref[:, c0:c0 + width])

    def head_norm(z, bd_ref, gain):
        hi, lo = _split2(z * z)
        ss = _dot(jnp.concatenate([hi, lo], axis=-1), bd_ref[...])
        return (z * lax.rsqrt(ss * (1.0 / HEAD) + EPS)) * gain

    cos = cos_ref[...]
    sin = sin_ref[...]

    def rope(z):
        return z * cos + pltpu.roll(z, HEAD, 1) * sin

    g_aq = vec_ref[0:1, 0:LANES]
    g_ak = vec_ref[1:2, 0:LANES]
    g_bq = vec_ref[2:3, 0:LANES]
    g_bk = vec_ref[3:4, 0:LANES]
    f_bias = vec_ref[4:5, 0:LANES]
    i_bias = vec_ref[5:6, :]
    cf_bias = vec_ref[6:7, :]

    za = proj(SEC_A, 768)
    for j in range(SWA_HEADS // 2):
        q = rope(head_norm(za[:, j * LANES:(j + 1) * LANES], bds_ref, g_aq)) * QK_SCALE
        qa_ref[0, :, j * LANES:(j + 1) * LANES] = q.astype(BF16)
    k = rope(head_norm(za[:, SWA_Q:SWA_Q + LANES], bds_ref, g_ak))
    kva_ref[0, :, 0:LANES] = k.astype(BF16)
    kva_ref[0, :, LANES:2 * LANES] = za[:, SWA_Q + SWA_KV:].astype(BF16)

    zb = proj(SEC_B, 768)
    for p in range(FOX_HEADS // 2):
        q = head_norm(zb[:, p * LANES:(p + 1) * LANES], bdn_ref, g_bq) * QK_SCALE
        qb_ref[0, :, p * LANES:(p + 1) * LANES] = q.astype(BF16)
        k = head_norm(zb[:, FOX_W + p * LANES:FOX_W + (p + 1) * LANES], bdn_ref, g_bk)
        kb_ref[0, :, 2 * p * LANES:(2 * p + 1) * LANES] = k.astype(BF16)
    vb_ref[0] = zb[:, 2 * FOX_W:].astype(BF16)

    lane = lax.broadcasted_iota(jnp.int32, (MXU_DIM, LANES), 1)
    zg = proj(SEC_G, LANES)
    log_f = jnp.where(lane[0:1] < 12, _log_sigmoid(zg + f_bias), 0.0)
    carry = carry_ref[0:1, :]
    for blk in range(tm // MXU_DIM):
        rows = slice(blk * MXU_DIM, (blk + 1) * MXU_DIM)
        cs = _dot(tri_ref[...], _split3_by_lane(log_f[rows], lane))
        f_loc = cs + pltpu.roll(cs, LANES - 4, 1) + pltpu.roll(cs, LANES - 8, 1)
        f_cum = jnp.where(lane < 4, f_loc, 0.0) + carry
        carry = f_cum[MXU_DIM - 1:MXU_DIM, :]
        f_rep = f_cum + pltpu.roll(f_cum, 4, 1) + pltpu.roll(f_cum, 8, 1)
        aug = _split3_by_lane(f_rep, lane)
        kb_ref[0, rows, LANES:2 * LANES] = aug
        kb_ref[0, rows, 3 * LANES:4 * LANES] = aug
    carry_ref[...] = jnp.broadcast_to(carry, carry_ref.shape)

    zc = proj(SEC_C, 768)
    qkvc_ref[0, :, 0:MLSTM_W] = zc[:, 0:MLSTM_W].astype(BF16)
    qkvc_ref[0, :, MLSTM_W:2 * MLSTM_W] = (zc[:, MLSTM_W:2 * MLSTM_W] * QK_SCALE).astype(BF16)
    qkvc_ref[0, :, 2 * MLSTM_W:] = zc[:, 2 * MLSTM_W:].astype(BF16)
    zgc = proj(SEC_CO, 768)
    gc_ref[0, :, 0:MLSTM_W] = zgc[:, MLSTM_W:2 * MLSTM_W] + i_bias
    gc_ref[0, :, MLSTM_W:2 * MLSTM_W] = _log_sigmoid(zgc[:, 2 * MLSTM_W:] + cf_bias)
    gc_ref[0, :, 2 * MLSTM_W:] = zgc[:, 0:MLSTM_W]


def _in_proj(x, n1, w_main, vecs, cos, sin, bds, bdn, tri):
    B, S, _ = x.shape
    tm = min(TM_IN, S)
    grid = (B, S // tm)
    tok = lambda w: pl.BlockSpec((1, tm, w), lambda b, i: (b, i, 0))
    out_shapes = (
        jax.ShapeDtypeStruct((B, S, SWA_Q), BF16),
        jax.ShapeDtypeStruct((B, S, 2 * SWA_KV), BF16),
        jax.ShapeDtypeStruct((B, S, FOX_W), BF16),
        jax.ShapeDtypeStruct((B, S, 2 * MXU_DIM), BF16),
        jax.ShapeDtypeStruct((B, S, FOX_W), BF16),
        jax.ShapeDtypeStruct((B, S, 3 * MLSTM_W), BF16),
        jax.ShapeDtypeStruct((B, S, 3 * MLSTM_W), F32),
    )
    return pl.pallas_call(
        _in_proj_body,
        grid=grid,
        in_specs=[
            tok(D_MODEL),
            _const_spec((1, D_MODEL)),
            _const_spec((D_MODEL, W_MAIN)),
            _const_spec((8, MLSTM_W)),
            pl.BlockSpec((tm, LANES), lambda b, i: (i, 0)),
            pl.BlockSpec((tm, LANES), lambda b, i: (i, 0)),
            _const_spec((MXU_DIM, LANES)),
            _const_spec((MXU_DIM, LANES)),
            _const_spec((MXU_DIM, MXU_DIM)),
        ],
        out_specs=[tok(s.shape[-1]) for s in out_shapes],
        out_shape=out_shapes,
        scratch_shapes=[pltpu.VMEM((8, LANES), F32)],
        compiler_params=pltpu.CompilerParams(
            dimension_semantics=("arbitrary", "arbitrary"), vmem_limit_bytes=VMEM_LIMIT),
        name="in_proj",
    )(x, n1, w_main, vecs, cos, sin, bds, bdn, tri)


def _swa_body(sink_ref, q_ref, kvc_ref, kvp_ref, o_ref):
    n = pl.program_id(1)
    rows = 2 * QBLK
    row = lax.broadcasted_iota(jnp.int32, (rows, 2 * QBLK), 0)
    col = lax.broadcasted_iota(jnp.int32, (rows, 2 * QBLK), 1)
    dchunk = col // CHUNK - (row % QBLK) // CHUNK
    allowed = (dchunk >= 0) & (dchunk <= 2) & ((col >= QBLK) | (n > 0))
    lane = lax.broadcasted_iota(jnp.int32, (QBLK, LANES), 1)
    lane_a = (lane // HALF) % 2 == 0
    row1 = lax.broadcasted_iota(jnp.int32, (rows, 1), 0)

    kband = jnp.concatenate([kvp_ref[0, :, 0:LANES], kvc_ref[0, :, 0:LANES]], axis=0)
    vband = jnp.concatenate([kvp_ref[0, :, LANES:], kvc_ref[0, :, LANES:]], axis=0)
    zero = jnp.zeros((QBLK, LANES), BF16)
    for j in range(SWA_HEADS // 2):
        q2 = q_ref[0, :, j * LANES:(j + 1) * LANES]
        lhs = jnp.concatenate([jnp.where(lane_a, q2, zero), jnp.where(lane_a, zero, q2)], axis=0)
        s = jnp.where(allowed, _dot_nt(lhs, kband), -jnp.inf)
        sink = jnp.where(row1 < QBLK, sink_ref[j], sink_ref[j + 4])
        m = jnp.maximum(jnp.max(s, axis=-1, keepdims=True), sink)
        p = jnp.exp(s - m)
        den = jnp.sum(p, axis=-1, keepdims=True) + jnp.exp(sink - m)
        o = _dot((p / den).astype(BF16), vband)
        o_ref[0, :, j * LANES:(j + 1) * LANES] = jnp.where(lane < HEAD, o[:QBLK], o[QBLK:]).astype(BF16)


def _swa(sinks, qa, kva):
    B, S, _ = qa.shape
    return pl.pallas_call(
        _swa_body,
        grid=(B, S // QBLK),
        in_specs=[
            pl.BlockSpec(memory_space=pltpu.SMEM),
            pl.BlockSpec((1, QBLK, SWA_Q), lambda b, n: (b, n, 0)),
            pl.BlockSpec((1, QBLK, 2 * SWA_KV), lambda b, n: (b, n, 0)),
            pl.BlockSpec((1, QBLK, 2 * SWA_KV), lambda b, n: (b, jnp.maximum(n - 1, 0), 0)),
        ],
        out_specs=pl.BlockSpec((1, QBLK, SWA_Q), lambda b, n: (b, n, 0)),
        out_shape=jax.ShapeDtypeStruct((B, S, SWA_Q), BF16),
        compiler_params=pltpu.CompilerParams(
            dimension_semantics=("arbitrary", "arbitrary"), vmem_limit_bytes=VMEM_LIMIT),
        name="swa",
    )(sinks, qa, kva, kva)


TQ_FOX = 256


def _fox_body(q_ref, k_ref, v_ref, o_ref):
    tq = q_ref.shape[1]
    pair = pl.program_id(1)
    qi = pl.program_id(2)
    lane = lax.broadcasted_iota(jnp.int32, (tq, LANES), 1)
    lane_a = lane < HEAD
    q2 = q_ref[0]
    zero = jnp.zeros_like(q2)

    def minus_one_at(head):
        hit = (lane == head) | (lane == head + 4) | (lane == head + 8)
        return jnp.where(hit, -1.0, 0.0).astype(BF16)

    lhs = jnp.concatenate([
        jnp.concatenate([jnp.where(lane_a, q2, zero), minus_one_at(2 * pair)], axis=1),
        jnp.concatenate([jnp.where(lane_a, zero, q2), minus_one_at(2 * pair + 1)], axis=1),
    ], axis=0)

    row = lax.broadcasted_iota(jnp.int32, (2 * tq, tq), 0) % tq
    col = lax.broadcasted_iota(jnp.int32, (2 * tq, tq), 1)

    def step(j, carry, diagonal):
        m, l, acc = carry
        r0 = pl.multiple_of(j * tq, tq)
        s = _dot_nt(lhs, k_ref[0, pl.ds(r0, tq), :])
        if diagonal:
            s = jnp.where(col <= row, s, -jnp.inf)
        m_new = jnp.maximum(m, jnp.max(s, axis=-1, keepdims=True))
        alpha = jnp.exp(m - m_new)
        p = jnp.exp(s - m_new)
        l = alpha * l + jnp.sum(p, axis=-1, keepdims=True)
        acc = alpha * acc + _dot(p.astype(BF16), v_ref[0, pl.ds(r0, tq), :])
        return m_new, l, acc

    init = (jnp.full((2 * tq, 1), -jnp.inf, F32), jnp.zeros((2 * tq, 1), F32),
            jnp.zeros((2 * tq, LANES), F32))
    carry = lax.fori_loop(0, qi, lambda j, c: step(j, c, False), init)
    _, l, acc = step(qi, carry, True)
    o = acc / l
    o_ref[0] = jnp.where(lane_a, o[:tq], o[tq:]).astype(BF16)


def _fox(qb, kb, vb):
    B, S, _ = qb.shape
    tq = min(TQ_FOX, S)
    return pl.pallas_call(
        _fox_body,
        grid=(B, FOX_HEADS // 2, S // tq),
        in_specs=[
            pl.BlockSpec((1, tq, LANES), lambda b, p, i: (b, i, p)),
            pl.BlockSpec((1, S, MXU_DIM), lambda b, p, i: (b, 0, p)),
            pl.BlockSpec((1, S, LANES), lambda b, p, i: (b, 0, p)),
        ],
        out_specs=pl.BlockSpec((1, tq, LANES), lambda b, p, i: (b, i, p)),
        out_shape=jax.ShapeDtypeStruct((B, S, FOX_W), BF16),
        compiler_params=pltpu.CompilerParams(
            dimension_semantics=("arbitrary", "arbitrary", "arbitrary"),
            vmem_limit_bytes=VMEM_LIMIT),
        name="fox",
    )(qb, kb, vb)


TS_MLSTM = 512


def _mlstm_body(qkv_ref, g_ref, onorm_ref, tri_ref, i4_ref, bd_ref, o_ref, c_ref, n_ref, m_ref):
    ts = qkv_ref.shape[1]
    W = MLSTM_W

    @pl.when(pl.program_id(1) == 0)
    def _():
        c_ref[...] = jnp.zeros_like(c_ref)
        n_ref[...] = jnp.zeros_like(n_ref)
        m_ref[...] = jnp.zeros_like(m_ref)

    row = lax.broadcasted_iota(jnp.int32, (CHUNK, W), 0)
    lane = lax.broadcasted_iota(jnp.int32, (CHUNK, W), 1)
    causal = (lane % HEAD) <= row
    lane_head = lane // HEAD
    bd = bd_ref[...]
    bd_mask = bd > 0
    zero_w = jnp.zeros((W, W), BF16)

    def seg_max(d):
        out = jnp.zeros_like(d)
        for hd in range(MLSTM_HEADS):
            sel = lane_head == hd
            mx = jnp.max(jnp.where(sel, d, -jnp.inf), axis=-1, keepdims=True)
            out = jnp.where(sel, mx, out)
        return out

    def chunk(c, _):
        r0 = pl.multiple_of(c * CHUNK, CHUNK)
        q = qkv_ref[0, pl.ds(r0, CHUNK), 0:W]
        k = qkv_ref[0, pl.ds(r0, CHUNK), W:2 * W]
        v = qkv_ref[0, pl.ds(r0, CHUNK), 2 * W:]
        ig = g_ref[0, pl.ds(r0, CHUNK), 0:W]
        lf = g_ref[0, pl.ds(r0, CHUNK), W:2 * W]
        co = g_ref[0, pl.ds(r0, CHUNK), 2 * W:]

        lf_hi, lf_lo = _split2(lf)
        b = _dot(tri_ref[...], lf_hi) + _dot(tri_ref[...], lf_lo)
        g = b[CHUNK - 1:CHUNK, :]
        u = b - ig
        u_row = jnp.sum(u * i4_ref[...], axis=0, keepdims=True)
        dm = jnp.where(causal, b - u_row, -jnp.inf)
        m_prev = m_ref[0:1, :]
        inter = b + m_prev
        m_t = jnp.maximum(inter, seg_max(dm))
        w_inter = jnp.exp(inter - m_t)
        decay = jnp.exp(dm - m_t)

        kb = jnp.where(bd_mask, jnp.concatenate([k] * MLSTM_HEADS, axis=0), zero_w)
        vb = jnp.where(bd_mask, jnp.concatenate([v] * MLSTM_HEADS, axis=0), zero_w)
        sm = _dot_nt(q, kb) * decay
        sm_hi, sm_lo = _split2(sm)
        c_prev = c_ref[...]
        n_prev = n_ref[...]
        num = w_inter * _dot(q, c_prev.astype(BF16)) + _dot(sm_hi, vb)
        den = w_inter * _dot(q, n_prev.astype(BF16)) + _dot(sm_hi, bd) + _dot(sm_lo, bd)
        hval = num / jnp.maximum(jnp.abs(den), jnp.exp(-m_t))
        hs_hi, hs_lo = _split2(hval * hval)
        ss = _dot(hs_hi, bd) + _dot(hs_lo, bd)
        hc = (hval * lax.rsqrt(ss * (1.0 / HEAD) + EPS)) * onorm_ref[...]
        o_ref[0, pl.ds(r0, CHUNK), :] = (jax.nn.sigmoid(co) * hc).astype(BF16)

        a = g - u
        a_max = jnp.max(a, axis=0, keepdims=True)
        wa = jnp.exp(a - a_max)
        c_loc = jnp.where(bd_mask, _dot_tn(k, (wa * v.astype(F32)).astype(BF16)), 0.0)
        n_loc = jnp.where(bd_mask, _dot_tn(k, wa.astype(BF16)), 0.0)
        gm = g + m_prev
        m_new = jnp.maximum(gm, a_max)
        s_old = jnp.exp(gm - m_new)
        s_loc = jnp.exp(a_max - m_new)
        c_ref[...] = s_old * c_prev + s_loc * c_loc
        n_ref[...] = s_old * n_prev + s_loc * n_loc
        m_ref[...] = jnp.broadcast_to(m_new, m_ref.shape)
        return 0

    lax.fori_loop(0, ts // CHUNK, chunk, 0)


def _mlstm(qkvc, gc, onorm, tri, i4, bd):
    B, S, _ = qkvc.shape
    ts = min(TS_MLSTM, S)
    return pl.pallas_call(
        _mlstm_body,
        grid=(B, S // ts),
        in_specs=[
            pl.BlockSpec((1, ts, 3 * MLSTM_W), lambda b, i: (b, i, 0)),
            pl.BlockSpec((1, ts, 3 * MLSTM_W), lambda b, i: (b, i, 0)),
            _const_spec((1, MLSTM_W)),
            _const_spec((CHUNK, CHUNK)),
            _const_spec((CHUNK, MLSTM_W)),
            _const_spec((MLSTM_W, MLSTM_W)),
        ],
        out_specs=pl.BlockSpec((1, ts, MLSTM_W), lambda b, i: (b, i, 0)),
        out_shape=jax.ShapeDtypeStruct((B, S, MLSTM_W), BF16),
        scratch_shapes=[pltpu.VMEM((MLSTM_W, MLSTM_W), F32), pltpu.VMEM((MLSTM_W, MLSTM_W), F32),
                        pltpu.VMEM((8, MLSTM_W), F32)],
        compiler_params=pltpu.CompilerParams(
            dimension_semantics=("arbitrary", "arbitrary"), vmem_limit_bytes=VMEM_LIMIT),
        name="mlstm",
    )(qkvc, gc, onorm, tri, i4, bd)


TM_FFN = 512
FF_CHUNK = 1024


def _out_ffn_body(x_ref, ya_ref, yb_ref, yc_ref, wo_ref, n2_ref, w1_ref, w2_ref, o_ref, a_ref):
    x1 = (x_ref[...] + _dot(ya_ref[...], wo_ref[0:SWA_Q, :])
          + _dot(yb_ref[...], wo_ref[SWA_Q:SWA_Q + FOX_W, :])
          + _dot(yc_ref[...], wo_ref[SWA_Q + FOX_W:, :]))
    ms = jnp.mean(x1 * x1, axis=-1, keepdims=True)
    h2 = ((x1 * lax.rsqrt(ms + EPS)) * n2_ref[...]).astype(BF16)
    for j in range(D_FF // FF_CHUNK):
        cols = slice(j * FF_CHUNK, (j + 1) * FF_CHUNK)
        r = jnp.maximum(_dot(h2, w1_ref[:, cols]), 0.0)
        a_ref[:, cols] = (r * r).astype(BF16)
    o_ref[...] = x1 + _dot(a_ref[...], w2_ref[...])


def _out_ffn(x2d, ya, yb, yc, wo, n2, w1, w2):
    T = x2d.shape[0]
    tm = min(TM_FFN, T)
    tok = lambda w: pl.BlockSpec((tm, w), lambda i: (i, 0))
    return pl.pallas_call(
        _out_ffn_body,
        grid=(T // tm,),
        in_specs=[
            tok(D_MODEL), tok(SWA_Q), tok(FOX_W), tok(MLSTM_W),
            _const_spec((D_MODEL, D_MODEL)),
            _const_spec((1, D_MODEL)),
            _const_spec((D_MODEL, D_FF)),
            _const_spec((D_FF, D_MODEL)),
        ],
        out_specs=tok(D_MODEL),
        out_shape=jax.ShapeDtypeStruct((T, D_MODEL), F32),
        scratch_shapes=[pltpu.VMEM((tm, D_FF), BF16)],
        compiler_params=pltpu.CompilerParams(
            dimension_semantics=("arbitrary",), vmem_limit_bytes=VMEM_LIMIT),
        name="out_ffn",
    )(x2d, ya, yb, yc, wo, n2, w1, w2)


def _rope_tables(S):
    inv = ROPE_THETA ** (-jnp.arange(HALF, dtype=F32) / HALF)
    ang = jnp.arange(S).astype(F32)[:, None] * inv[None, :]
    cos = jnp.tile(jnp.cos(ang), (1, LANES // HALF))
    sin = jnp.tile(jnp.sin(ang), (1, LANES // HALF))
    sign = jnp.where(jnp.arange(LANES) < HEAD, -1.0, 1.0).astype(F32)
    return cos, sin * sign[None, :]


def _layer(x, cos, sin, consts, norm1, w_in, swa_q_norm, swa_k_norm, swa_sinks, fox_q_norm,
           fox_k_norm, fox_f_bias, mlstm_i_bias, mlstm_f_bias, mlstm_out_norm, w_out, norm2,
           w_ff1, w_ff2):
    B, S, D = x.shape
    bds, bdn, tri256, tri64, i4, bd256 = consts
    cols = jnp.asarray(np.maximum(_IN_COLS, 0))
    valid = jnp.asarray((_IN_COLS >= 0).astype(np.float32))
    w_main = (jnp.take(w_in, cols, axis=1) * valid[None, :]).astype(BF16)

    swa_dim = jnp.asarray(_swa_lane_dim())
    pad = lambda v: jnp.pad(v, (0, MLSTM_W - v.shape[0]))
    vecs = jnp.stack([
        pad(swa_q_norm[swa_dim]),
        pad(swa_k_norm[swa_dim]),
        pad(jnp.tile(fox_q_norm, 2)),
        pad(jnp.tile(fox_k_norm, 2)),
        pad(jnp.tile(fox_f_bias, 3)),
        jnp.repeat(mlstm_i_bias, HEAD),
        jnp.repeat(mlstm_f_bias, HEAD),
        jnp.zeros((MLSTM_W,), F32),
    ]).astype(F32)

    qa, kva, qb, kb, vb, qkvc, gc = _in_proj(
        x, norm1.reshape(1, D), w_main, vecs, cos, sin, bds, bdn, tri256)
    ya = _swa(swa_sinks.astype(F32), qa, kva)
    yb = _fox(qb, kb, vb)
    yc = _mlstm(qkvc, gc, mlstm_out_norm.reshape(1, MLSTM_W).astype(F32), tri64, i4, bd256)

    wo = jnp.take(w_out, jnp.asarray(_WOUT_ROWS), axis=0).astype(BF16)
    T = B * S
    out = _out_ffn(x.reshape(T, D), ya.reshape(T, SWA_Q), yb.reshape(T, FOX_W),
                   yc.reshape(T, MLSTM_W), wo, norm2.reshape(1, D), w_ff1.astype(BF16),
                   w_ff2.astype(BF16))
    return out.reshape(B, S, D)


def kernel(x, norm1, w_in, swa_q_norm, swa_k_norm, swa_sinks, fox_q_norm, fox_k_norm, fox_f_bias,
           mlstm_i_bias, mlstm_f_bias, mlstm_out_norm, w_out, norm2, w_ff1, w_ff2):
    S = x.shape[1]
    cos, sin = _rope_tables(S)
    consts = (jnp.asarray(_BD_SWA, BF16), jnp.asarray(_BD_NAT, BF16), jnp.asarray(_TRI256, BF16),
              jnp.asarray(_TRI64, BF16), jnp.asarray(_I4, F32), jnp.asarray(_BD256, BF16))
    for l in range(norm1.shape[0]):
        x = _layer(x, cos, sin, consts, norm1[l], w_in[l], swa_q_norm[l], swa_k_norm[l],
                   swa_sinks[l], fox_q_norm[l], fox_k_norm[l], fox_f_bias[l], mlstm_i_bias[l],
                   mlstm_f_bias[l], mlstm_out_norm[l], w_out[l], norm2[l], w_ff1[l], w_ff2[l])
    return x
```

```python
import functools
import math

import numpy as np
import jax
import jax.numpy as jnp
from jax import lax
from jax.experimental import pallas as pl
from jax.experimental.pallas import tpu as pltpu

F32 = jnp.float32
BF16 = jnp.bfloat16

D_MODEL = 1024
HEAD = 64
HALF = HEAD // 2
CHUNK = 64
SWA_HEADS = 8
SWA_KV_HEADS = 2
FOX_HEADS = 4
MLSTM_HEADS = 4
SWA_Q = SWA_HEADS * HEAD
SWA_KV = SWA_KV_HEADS * HEAD
FOX_W = FOX_HEADS * HEAD
MLSTM_W = MLSTM_HEADS * HEAD
D_FF = 4 * D_MODEL
QBLK = 128
ROPE_THETA = 10000.0
EPS = 1e-6
QK_SCALE = 1.0 / math.sqrt(HEAD)

LANES = 128
MXU_DIM = 256
VMEM_LIMIT = 56 * 1024 * 1024

SEC_A = 0
SEC_B = SEC_A + 768
SEC_C = SEC_B + 768
SEC_CO = SEC_C + 768
SEC_CI = SEC_CO + 256
SEC_CF = SEC_CI + 256
SEC_G = SEC_CF + 256
W_MAIN = SEC_G + LANES


def _in_proj_columns():
    o_aq, o_ak, o_av = 0, SWA_Q, SWA_Q + SWA_KV
    o_bq = o_av + SWA_KV
    o_bk, o_bv = o_bq + FOX_W, o_bq + 2 * FOX_W
    o_bf = o_bv + FOX_W
    o_cq = o_bf + FOX_HEADS
    o_ck, o_cv = o_cq + MLSTM_W, o_cq + 2 * MLSTM_W
    o_ci = o_cv + MLSTM_W
    o_cf = o_ci + MLSTM_HEADS
    o_co = o_cf + MLSTM_HEADS
    idx = np.full((W_MAIN,), -1, np.int64)
    lane = np.arange(LANES)
    quarter, i = lane // HALF, lane % HALF
    dim = i + HALF * (quarter // 2)
    is_b = quarter % 2
    for j in range(SWA_HEADS // 2):
        idx[SEC_A + j * LANES + lane] = o_aq + (j + 4 * is_b) * HEAD + dim
    idx[SEC_A + SWA_Q + lane] = o_ak + is_b * HEAD + dim
    idx[SEC_A + SWA_Q + SWA_KV + lane] = o_av + lane
    w = np.arange(FOX_W)
    idx[SEC_B + w] = o_bq + w
    idx[SEC_B + FOX_W + w] = o_bk + w
    idx[SEC_B + 2 * FOX_W + w] = o_bv + w
    idx[SEC_C + w] = o_cq + w
    idx[SEC_C + MLSTM_W + w] = o_ck + w
    idx[SEC_C + 2 * MLSTM_W + w] = o_cv + w
    idx[SEC_CO + w] = o_co + w
    idx[SEC_CI + w] = o_ci + w // HEAD
    idx[SEC_CF + w] = o_cf + w // HEAD
    for rep in range(3):
        idx[SEC_G + rep * FOX_HEADS + np.arange(FOX_HEADS)] = o_bf + np.arange(FOX_HEADS)
    return idx


def _swa_lane_dim():
    lane = np.arange(LANES)
    return lane % HALF + HALF * (lane // HEAD)


def _wout_rows():
    rows = np.arange(D_MODEL)
    r = np.arange(SWA_Q)
    j, half, d = r // LANES, (r % LANES) // HEAD, r % HEAD
    rows[:SWA_Q] = (j + 4 * half) * HEAD + d
    return rows


_IN_COLS = _in_proj_columns()
_WOUT_ROWS = _wout_rows()


def _block_diag_ones(head_of_lane):
    return (head_of_lane[:, None] == head_of_lane[None, :]).astype(np.float32)


_lane = np.arange(LANES)
_BD_SWA = np.concatenate([_block_diag_ones((_lane // HALF) % 2)] * 2, axis=0)
_BD_NAT = np.concatenate([_block_diag_ones(_lane // HEAD)] * 2, axis=0)
_TRI256 = np.tril(np.ones((MXU_DIM, MXU_DIM), np.float32))
_TRI64 = np.tril(np.ones((CHUNK, CHUNK), np.float32))
_w256 = np.arange(MLSTM_W)
_BD256 = _block_diag_ones(_w256 // HEAD)
_I4 = (np.arange(CHUNK)[:, None] == (_w256 % HEAD)[None, :]).astype(np.float32)


def _const_spec(shape):
    nd = len(shape)
    return pl.BlockSpec(shape, lambda *_: (0,) * nd, pipeline_mode=pl.Buffered(1))


def _log_sigmoid(x):
    return jnp.minimum(x, 0.0) - jnp.log(1.0 + jnp.exp(-jnp.abs(x)))


def _split2(x):
    hi = x.astype(BF16)
    lo = (x - hi.astype(F32)).astype(BF16)
    return hi, lo


def _split3_by_lane(x, lane):
    hi = x.astype(BF16).astype(F32)
    r1 = x - hi
    mid = r1.astype(BF16).astype(F32)
    low = r1 - mid
    return jnp.where(lane < 4, hi, jnp.where(lane < 8, mid, low)).astype(BF16)


def _dot(a, b):
    return jnp.dot(a, b, preferred_element_type=F32)


def _dot_nt(a, b):
    return lax.dot_general(a, b, (((1,), (1,)), ((), ())), preferred_element_type=F32)


def _dot_tn(a, b):
    return lax.dot_general(a, b, (((0,), (0,)), ((), ())), preferred_element_type=F32)


TM_IN = 512


def _in_proj_body(x_ref, n1_ref, w_ref, vec_ref, cos_ref, sin_ref, bds_ref, bdn_ref, tri_ref,
                  qa_ref, kva_ref, qb_ref, kb_ref, vb_ref, qkvc_ref, gc_ref, carry_ref):
    tm = x_ref.shape[1]

    @pl.when(pl.program_id(1) == 0)
    def _():
        carry_ref[...] = jnp.zeros_like(carry_ref)

    x = x_ref[0]
    ms = jnp.mean(x * x, axis=-1, keepdims=True)
    h = ((x * lax.rsqrt(ms + EPS)) * n1_ref[...]).astype(BF16)

    def proj(c0, width):
        return _dot(h, w_ref[:, c0:c0 + width])

    def head_norm(z, bd_ref, gain):
        hi, lo = _split2(z * z)
        ss = _dot(jnp.concatenate([hi, lo], axis=-1), bd_ref[...])
        return (z * lax.rsqrt(ss * (1.0 / HEAD) + EPS)) * gain

    cos = cos_ref[...]
    sin = sin_ref[...]

    def rope(z):
        return z * cos + pltpu.roll(z, HEAD, 1) * sin

    g_aq = vec_ref[0:1, 0:LANES]
    g_ak = vec_ref[1:2, 0:LANES]
    g_bq = vec_ref[2:3, 0:LANES]
    g_bk = vec_ref[3:4, 0:LANES]
    f_bias = vec_ref[4:5, 0:LANES]
    i_bias = vec_ref[5:6, :]
    cf_bias = vec_ref[6:7, :]

    za = proj(SEC_A, 768)
    for j in range(SWA_HEADS // 2):
        q = rope(head_norm(za[:, j * LANES:(j + 1) * LANES], bds_ref, g_aq)) * QK_SCALE
        qa_ref[0, :, j * LANES:(j + 1) * LANES] = q.astype(BF16)
    k = rope(head_norm(za[:, SWA_Q:SWA_Q + LANES], bds_ref, g_ak))
    kva_ref[0, :, 0:LANES] = k.astype(BF16)
    kva_ref[0, :, LANES:2 * LANES] = za[:, SWA_Q + SWA_KV:].astype(BF16)

    zb = proj(SEC_B, 768)
    for p in range(FOX_HEADS // 2):
        q = head_norm(zb[:, p * LANES:(p + 1) * LANES], bdn_ref, g_bq) * QK_SCALE
        qb_ref[0, :, p * LANES:(p + 1) * LANES] = q.astype(BF16)
        k = head_norm(zb[:, FOX_W + p * LANES:FOX_W + (p + 1) * LANES], bdn_ref, g_bk)
        kb_ref[0, :, 2 * p * LANES:(2 * p + 1) * LANES] = k.astype(BF16)
    vb_ref[0] = zb[:, 2 * FOX_W:].astype(BF16)

    lane = lax.broadcasted_iota(jnp.int32, (MXU_DIM, LANES), 1)
    zg = proj(SEC_G, LANES)
    log_f = jnp.where(lane[0:1] < 12, _log_sigmoid(zg + f_bias), 0.0)
    carry = carry_ref[0:1, :]
    for blk in range(tm // MXU_DIM):
        rows = slice(blk * MXU_DIM, (blk + 1) * MXU_DIM)
        cs = _dot(tri_ref[...], _split3_by_lane(log_f[rows], lane))
        f_loc = cs + pltpu.roll(cs, LANES - 4, 1) + pltpu.roll(cs, LANES - 8, 1)
        f_cum = jnp.where(lane < 4, f_loc, 0.0) + carry
        carry = f_cum[MXU_DIM - 1:MXU_DIM, :]
        f_rep = f_cum + pltpu.roll(f_cum, 4, 1) + pltpu.roll(f_cum, 8, 1)
        aug = _split3_by_lane(f_rep, lane)
        kb_ref[0, rows, LANES:2 * LANES] = aug
        kb_ref[0, rows, 3 * LANES:4 * LANES] = aug
    carry_ref[...] = jnp.broadcast_to(carry, carry_ref.shape)

    zc = proj(SEC_C, 768)
    qkvc_ref[0, :, 0:MLSTM_W] = zc[:, 0:MLSTM_W].astype(BF16)
    qkvc_ref[0, :, MLSTM_W:2 * MLSTM_W] = (zc[:, MLSTM_W:2 * MLSTM_W] * QK_SCALE).astype(BF16)
    qkvc_ref[0, :, 2 * MLSTM_W:] = zc[:, 2 * MLSTM_W:].astype(BF16)
    zgc = proj(SEC_CO, 768)
    gc_ref[0, :, 0:MLSTM_W] = zgc[:, MLSTM_W:2 * MLSTM_W] + i_bias
    gc_ref[0, :, MLSTM_W:2 * MLSTM_W] = _log_sigmoid(zgc[:, 2 * MLSTM_W:] + cf_bias)
    gc_ref[0, :, 2 * MLSTM_W:] = zgc[:, 0:MLSTM_W]


def _in_proj(x, n1, w_main, vecs, cos, sin, bds, bdn, tri):
    B, S, _ = x.shape
    tm = min(TM_IN, S)
    grid = (B, S // tm)
    tok = lambda w: pl.BlockSpec((1, tm, w), lambda b, i: (b, i, 0))
    out_shapes = (
        jax.ShapeDtypeStruct((B, S, SWA_Q), BF16),
        jax.ShapeDtypeStruct((B, S, 2 * SWA_KV), BF16),
        jax.ShapeDtypeStruct((B, S, FOX_W), BF16),
        jax.ShapeDtypeStruct((B, S, 2 * MXU_DIM), BF16),
        jax.ShapeDtypeStruct((B, S, FOX_W), BF16),
        jax.ShapeDtypeStruct((B, S, 3 * MLSTM_W), BF16),
        jax.ShapeDtypeStruct((B, S, 3 * MLSTM_W), F32),
    )
    return pl.pallas_call(
        _in_proj_body,
        grid=grid,
        in_specs=[
            tok(D_MODEL),
            _const_spec((1, D_MODEL)),
            _const_spec((D_MODEL, W_MAIN)),
            _const_spec((8, MLSTM_W)),
            pl.BlockSpec((tm, LANES), lambda b, i: (i, 0)),
            pl.BlockSpec((tm, LANES), lambda b, i: (i, 0)),
            _const_spec((MXU_DIM, LANES)),
            _const_spec((MXU_DIM, LANES)),
            _const_spec((MXU_DIM, MXU_DIM)),
        ],
        out_specs=[tok(s.shape[-1]) for s in out_shapes],
        out_shape=out_shapes,
        scratch_shapes=[pltpu.VMEM((8, LANES), F32)],
        compiler_params=pltpu.CompilerParams(
            dimension_semantics=("arbitrary", "arbitrary"), vmem_limit_bytes=VMEM_LIMIT),
        name="in_proj",
    )(x, n1, w_main, vecs, cos, sin, bds, bdn, tri)


TS_SWA = 512


def _swa_body(sink_ref, q_ref, kvc_ref, kvp_ref, eye_ref, o_ref):
    step = pl.program_id(1)
    nblk = q_ref.shape[1] // QBLK
    per_group = SWA_HEADS // SWA_KV_HEADS
    width = per_group * QBLK
    eye = eye_ref[0:LANES, 0:LANES]

    key = lax.broadcasted_iota(jnp.int32, (2 * QBLK, width), 0)
    qry = lax.broadcasted_iota(jnp.int32, (2 * QBLK, width), 1) % QBLK
    dchunk = key // CHUNK - qry // CHUNK
    in_band = (dchunk >= 0) & (dchunk <= 2)
    lane = lax.broadcasted_iota(jnp.int32, (QBLK, LANES), 1)
    lane_a = (lane // HALF) % 2 == 0
    head_lane = lax.broadcasted_iota(jnp.int32, (1, width), 1) // QBLK
    zero = jnp.zeros((QBLK, LANES), BF16)

    def sink_row(group):
        row = jnp.zeros((1, width), F32)
        for h in range(per_group):
            row = jnp.where(head_lane == h, sink_ref[group * per_group + h], row)
        return row

    sinks = [sink_row(g) for g in range(SWA_KV_HEADS)]
    kv_all = jnp.concatenate([kvp_ref[0], kvc_ref[0]], axis=0)

    for blk in range(nblk):
        rows = slice(blk * QBLK, (blk + 1) * QBLK)
        band = kv_all[blk * QBLK:(blk + 2) * QBLK]
        kband = band[:, 0:LANES]
        vt = _dot_nt(eye, band[:, LANES:]).astype(BF16)
        allowed = in_band if blk > 0 else in_band & ((key >= QBLK) | (step > 0))
        q_cols = [q_ref[0, rows, j * LANES:(j + 1) * LANES] for j in range(SWA_HEADS // 2)]
        outs = []
        for g in range(SWA_KV_HEADS):
            keep = lane_a if g == 0 else jnp.logical_not(lane_a)
            q_g = jnp.concatenate([jnp.where(keep, q2, zero) for q2 in q_cols], axis=0)
            s = jnp.where(allowed, _dot_nt(kband, q_g), -jnp.inf)
            m = jnp.maximum(jnp.max(s, axis=0, keepdims=True), sinks[g])
            p = jnp.exp(s - m)
            den = jnp.sum(p, axis=0, keepdims=True) + jnp.exp(sinks[g] - m)
            o_t = _dot(vt[g * HEAD:(g + 1) * HEAD], p.astype(BF16)) / den
            outs.append(o_t.astype(BF16))
        for j in range(SWA_HEADS // 2):
            cols = slice(j * QBLK, (j + 1) * QBLK)
            pair_t = jnp.concatenate([outs[0][:, cols], outs[1][:, cols]], axis=0)
            o_ref[0, rows, j * LANES:(j + 1) * LANES] = _dot_nt(eye, pair_t).astype(BF16)


def _swa(sinks, qa, kva, eye):
    B, S, _ = qa.shape
    ts = min(TS_SWA, S)
    per_step = ts // QBLK
    return pl.pallas_call(
        _swa_body,
        grid=(B, S // ts),
        in_specs=[
            pl.BlockSpec(memory_space=pltpu.SMEM),
            pl.BlockSpec((1, ts, SWA_Q), lambda b, i: (b, i, 0)),
            pl.BlockSpec((1, ts, 2 * SWA_KV), lambda b, i: (b, i, 0)),
            pl.BlockSpec((1, QBLK, 2 * SWA_KV), lambda b, i: (b, jnp.maximum(i * per_step - 1, 0), 0)),
            _const_spec((MXU_DIM, MXU_DIM)),
        ],
        out_specs=pl.BlockSpec((1, ts, SWA_Q), lambda b, i: (b, i, 0)),
        out_shape=jax.ShapeDtypeStruct((B, S, SWA_Q), BF16),
        compiler_params=pltpu.CompilerParams(
            dimension_semantics=("arbitrary", "arbitrary"), vmem_limit_bytes=VMEM_LIMIT),
        name="swa",
    )(sinks, qa, kva, kva, eye)


TQ_FOX = 512
TK_FOX = 256
NEG_BIG = -1e30


def _fox_body(q_ref, k_ref, v_ref, eye_ref, o_ref, vt_ref):
    tq = q_ref.shape[1]
    tk = TK_FOX
    S = k_ref.shape[1]
    pair = pl.program_id(1)
    qi = pl.program_id(2)
    eye = eye_ref[...]

    @pl.when(qi == 0)
    def _():
        for j in range(S // tk):
            blk = slice(j * tk, (j + 1) * tk)
            vt_ref[:, blk] = _dot_nt(eye[0:LANES, 0:LANES], v_ref[0, blk, :]).astype(BF16)

    lane = lax.broadcasted_iota(jnp.int32, (tq, LANES), 1)
    lane_a = lane < HEAD
    q2 = q_ref[0]
    zero = jnp.zeros_like(q2)

    def minus_one_at(head):
        hit = (lane == head) | (lane == head + 4) | (lane == head + 8)
        return jnp.where(hit, -1.0, 0.0).astype(BF16)

    q_aug = jnp.concatenate([
        jnp.concatenate([jnp.where(lane_a, q2, zero), minus_one_at(2 * pair)], axis=1),
        jnp.concatenate([jnp.where(lane_a, zero, q2), minus_one_at(2 * pair + 1)], axis=1),
    ], axis=0)

    key = lax.broadcasted_iota(jnp.int32, (tk, 2 * tq), 0)
    qry = lax.broadcasted_iota(jnp.int32, (tk, 2 * tq), 1) % tq

    def step(j, state, key_offset):
        m, l, acc_a, acc_b = state
        r0 = pl.multiple_of(j * tk, tk)
        s = _dot_nt(k_ref[0, pl.ds(r0, tk), :], q_aug)
        if key_offset is not None:
            valid = key + key_offset <= qry
            s = jnp.where(valid, s, NEG_BIG)
        m_new = jnp.maximum(m, jnp.max(s, axis=0, keepdims=True))
        alpha = jnp.exp(m - m_new)
        p = jnp.exp(s - m_new)
        if key_offset is not None:
            p = jnp.where(valid, p, 0.0)
        l = alpha * l + jnp.sum(p, axis=0, keepdims=True)
        pb = p.astype(BF16)
        vt = vt_ref[:, pl.ds(r0, tk)]
        acc_a = alpha[:, :tq] * acc_a + _dot(vt[0:HEAD], pb[:, :tq])
        acc_b = alpha[:, tq:] * acc_b + _dot(vt[HEAD:], pb[:, tq:])
        return m_new, l, acc_a, acc_b

    def init():
        return (jnp.full((1, 2 * tq), NEG_BIG, F32), jnp.zeros((1, 2 * tq), F32),
                jnp.zeros((HEAD, tq), F32), jnp.zeros((HEAD, tq), F32))

    per_tile = tq // tk

    def body(i, states):
        return tuple(step(per_tile * i + c, states[c], None) for c in range(per_tile))

    states = lax.fori_loop(0, qi, body, tuple(init() for _ in range(per_tile)))
    states = [step(per_tile * qi + c, states[c], c * tk) for c in range(per_tile)]

    m = functools.reduce(jnp.maximum, [st[0] for st in states])
    l = jnp.zeros_like(m)
    acc_a = jnp.zeros((HEAD, tq), F32)
    acc_b = jnp.zeros((HEAD, tq), F32)
    for m_c, l_c, a_c, b_c in states:
        w = jnp.exp(m_c - m)
        l = l + w * l_c
        acc_a = acc_a + w[:, :tq] * a_c
        acc_b = acc_b + w[:, tq:] * b_c
    o_t = jnp.concatenate([acc_a / l[:, :tq], acc_b / l[:, tq:]], axis=0).astype(BF16)
    for c in range(tq // MXU_DIM):
        rows = slice(c * MXU_DIM, (c + 1) * MXU_DIM)
        o_ref[0, rows, :] = _dot_nt(eye, o_t[:, rows]).astype(BF16)


def _fox(qb, kb, vb, eye):
    B, S, _ = qb.shape
    tq = min(TQ_FOX, S)
    return pl.pallas_call(
        _fox_body,
        grid=(B, FOX_HEADS // 2, S // tq),
        in_specs=[
            pl.BlockSpec((1, tq, LANES), lambda b, p, i: (b, i, p)),
            pl.BlockSpec((1, S, MXU_DIM), lambda b, p, i: (b, 0, p)),
            pl.BlockSpec((1, S, LANES), lambda b, p, i: (b, 0, p)),
            _const_spec((MXU_DIM, MXU_DIM)),
        ],
        out_specs=pl.BlockSpec((1, tq, LANES), lambda b, p, i: (b, i, p)),
        out_shape=jax.ShapeDtypeStruct((B, S, FOX_W), BF16),
        scratch_shapes=[pltpu.VMEM((LANES, S), BF16)],
        compiler_params=pltpu.CompilerParams(
            dimension_semantics=("arbitrary", "arbitrary", "arbitrary"),
            vmem_limit_bytes=VMEM_LIMIT),
        name="fox",
    )(qb, kb, vb, eye)


TS_MLSTM = 512


def _mlstm_body(qkv_ref, g_ref, onorm_ref, tri_ref, i4_ref, bd_ref, o_ref, c_ref, n_ref, m_ref):
    ts = qkv_ref.shape[1]
    W = MLSTM_W

    @pl.when(pl.program_id(1) == 0)
    def _():
        c_ref[...] = jnp.zeros_like(c_ref)
        n_ref[...] = jnp.zeros_like(n_ref)
        m_ref[...] = jnp.zeros_like(m_ref)

    row = lax.broadcasted_iota(jnp.int32, (CHUNK, W), 0)
    lane = lax.broadcasted_iota(jnp.int32, (CHUNK, W), 1)
    causal = (lane % HEAD) <= row
    lane_head = lane // HEAD
    bd = bd_ref[...]
    bd_mask = bd > 0
    zero_w = jnp.zeros((W, W), BF16)

    def seg_max(d):
        out = jnp.zeros_like(d)
        for hd in range(MLSTM_HEADS):
            sel = lane_head == hd
            mx = jnp.max(jnp.where(sel, d, -jnp.inf), axis=-1, keepdims=True)
            out = jnp.where(sel, mx, out)
        return out

    def chunk(c, _):
        r0 = pl.multiple_of(c * CHUNK, CHUNK)
        q = qkv_ref[0, pl.ds(r0, CHUNK), 0:W]
        k = qkv_ref[0, pl.ds(r0, CHUNK), W:2 * W]
        v = qkv_ref[0, pl.ds(r0, CHUNK), 2 * W:]
        ig = g_ref[0, pl.ds(r0, CHUNK), 0:W]
        lf = g_ref[0, pl.ds(r0, CHUNK), W:2 * W]
        co = g_ref[0, pl.ds(r0, CHUNK), 2 * W:]

        lf_hi, lf_lo = _split2(lf)
        b = _dot(tri_ref[...], lf_hi) + _dot(tri_ref[...], lf_lo)
        g = b[CHUNK - 1:CHUNK, :]
        u = b - ig
        u_row = jnp.sum(u * i4_ref[...], axis=0, keepdims=True)
        dm = jnp.where(causal, b - u_row, -jnp.inf)
        m_prev = m_ref[0:1, :]
        inter = b + m_prev
        m_t = jnp.maximum(inter, seg_max(dm))
        w_inter = jnp.exp(inter - m_t)
        decay = jnp.exp(dm - m_t)

        kb = jnp.where(bd_mask, jnp.concatenate([k] * MLSTM_HEADS, axis=0), zero_w)
        vb = jnp.where(bd_mask, jnp.concatenate([v] * MLSTM_HEADS, axis=0), zero_w)
        sm = _dot_nt(q, kb) * decay
        sm_hi, sm_lo = _split2(sm)
        c_prev = c_ref[...]
        n_prev = n_ref[...]
        num = w_inter * _dot(q, c_prev.astype(BF16)) + _dot(sm_hi, vb)
        den = w_inter * _dot(q, n_prev.astype(BF16)) + _dot(sm_hi, bd) + _dot(sm_lo, bd)
        hval = num / jnp.maximum(jnp.abs(den), jnp.exp(-m_t))
        hs_hi, hs_lo = _split2(hval * hval)
        ss = _dot(hs_hi, bd) + _dot(hs_lo, bd)
        hc = (hval * lax.rsqrt(ss * (1.0 / HEAD) + EPS)) * onorm_ref[...]
        o_ref[0, pl.ds(r0, CHUNK), :] = (jax.nn.sigmoid(co) * hc).astype(BF16)

        a = g - u
        a_max = jnp.max(a, axis=0, keepdims=True)
        wa = jnp.exp(a - a_max)
        c_loc = jnp.where(bd_mask, _dot_tn(k, (wa * v.astype(F32)).astype(BF16)), 0.0)
        n_loc = jnp.where(bd_mask, _dot_tn(k, wa.astype(BF16)), 0.0)
        gm = g + m_prev
        m_new = jnp.maximum(gm, a_max)
        s_old = jnp.exp(gm - m_new)
        s_loc = jnp.exp(a_max - m_new)
        c_ref[...] = s_old * c_prev + s_loc * c_loc
        n_ref[...] = s_old * n_prev + s_loc * n_loc
        m_ref[...] = jnp.broadcast_to(m_new, m_ref.shape)
        return 0

    lax.fori_loop(0, ts // CHUNK, chunk, 0)


def _mlstm(qkvc, gc, onorm, tri, i4, bd):
    B, S, _ = qkvc.shape
    ts = min(TS_MLSTM, S)
    return pl.pallas_call(
        _mlstm_body,
        grid=(B, S // ts),
        in_specs=[
            pl.BlockSpec((1, ts, 3 * MLSTM_W), lambda b, i: (b, i, 0)),
            pl.BlockSpec((1, ts, 3 * MLSTM_W), lambda b, i: (b, i, 0)),
            _const_spec((1, MLSTM_W)),
            _const_spec((CHUNK, CHUNK)),
            _const_spec((CHUNK, MLSTM_W)),
            _const_spec((MLSTM_W, MLSTM_W)),
        ],
        out_specs=pl.BlockSpec((1, ts, MLSTM_W), lambda b, i: (b, i, 0)),
        out_shape=jax.ShapeDtypeStruct((B, S, MLSTM_W), BF16),
        scratch_shapes=[pltpu.VMEM((MLSTM_W, MLSTM_W), F32), pltpu.VMEM((MLSTM_W, MLSTM_W), F32),
                        pltpu.VMEM((8, MLSTM_W), F32)],
        compiler_params=pltpu.CompilerParams(
            dimension_semantics=("arbitrary", "arbitrary"), vmem_limit_bytes=VMEM_LIMIT),
        name="mlstm",
    )(qkvc, gc, onorm, tri, i4, bd)


TM_FFN = 512
FF_CHUNK = 1024


def _out_ffn_body(x_ref, ya_ref, yb_ref, yc_ref, wo_ref, n2_ref, w1_ref, w2_ref, o_ref, a_ref):
    x1 = (x_ref[...] + _dot(ya_ref[...], wo_ref[0:SWA_Q, :])
          + _dot(yb_ref[...], wo_ref[SWA_Q:SWA_Q + FOX_W, :])
          + _dot(yc_ref[...], wo_ref[SWA_Q + FOX_W:, :]))
    ms = jnp.mean(x1 * x1, axis=-1, keepdims=True)
    h2 = ((x1 * lax.rsqrt(ms + EPS)) * n2_ref[...]).astype(BF16)
    for j in range(D_FF // FF_CHUNK):
        cols = slice(j * FF_CHUNK, (j + 1) * FF_CHUNK)
        r = jnp.maximum(_dot(h2, w1_ref[:, cols]), 0.0)
        a_ref[:, cols] = (r * r).astype(BF16)
    o_ref[...] = x1 + _dot(a_ref[...], w2_ref[...])


def _out_ffn(x2d, ya, yb, yc, wo, n2, w1, w2):
    T = x2d.shape[0]
    tm = min(TM_FFN, T)
    tok = lambda w: pl.BlockSpec((tm, w), lambda i: (i, 0))
    return pl.pallas_call(
        _out_ffn_body,
        grid=(T // tm,),
        in_specs=[
            tok(D_MODEL), tok(SWA_Q), tok(FOX_W), tok(MLSTM_W),
            _const_spec((D_MODEL, D_MODEL)),
            _const_spec((1, D_MODEL)),
            _const_spec((D_MODEL, D_FF)),
            _const_spec((D_FF, D_MODEL)),
        ],
        out_specs=tok(D_MODEL),
        out_shape=jax.ShapeDtypeStruct((T, D_MODEL), F32),
        scratch_shapes=[pltpu.VMEM((tm, D_FF), BF16)],
        compiler_params=pltpu.CompilerParams(
            dimension_semantics=("arbitrary",), vmem_limit_bytes=VMEM_LIMIT),
        name="out_ffn",
    )(x2d, ya, yb, yc, wo, n2, w1, w2)


def _rope_tables(S):
    inv = ROPE_THETA ** (-jnp.arange(HALF, dtype=F32) / HALF)
    ang = jnp.arange(S).astype(F32)[:, None] * inv[None, :]
    cos = jnp.tile(jnp.cos(ang), (1, LANES // HALF))
    sin = jnp.tile(jnp.sin(ang), (1, LANES // HALF))
    sign = jnp.where(jnp.arange(LANES) < HEAD, -1.0, 1.0).astype(F32)
    return cos, sin * sign[None, :]


def _layer(x, cos, sin, consts, norm1, w_in, swa_q_norm, swa_k_norm, swa_sinks, fox_q_norm,
           fox_k_norm, fox_f_bias, mlstm_i_bias, mlstm_f_bias, mlstm_out_norm, w_out, norm2,
           w_ff1, w_ff2):
    B, S, D = x.shape
    bds, bdn, tri256, tri64, i4, bd256, eye256 = consts
    cols = jnp.asarray(np.maximum(_IN_COLS, 0))
    valid = jnp.asarray((_IN_COLS >= 0).astype(np.float32))
    w_main = (jnp.take(w_in, cols, axis=1) * valid[None, :]).astype(BF16)

    swa_dim = jnp.asarray(_swa_lane_dim())
    pad = lambda v: jnp.pad(v, (0, MLSTM_W - v.shape[0]))
    vecs = jnp.stack([
        pad(swa_q_norm[swa_dim]),
        pad(swa_k_norm[swa_dim]),
        pad(jnp.tile(fox_q_norm, 2)),
        pad(jnp.tile(fox_k_norm, 2)),
        pad(jnp.tile(fox_f_bias, 3)),
        jnp.repeat(mlstm_i_bias, HEAD),
        jnp.repeat(mlstm_f_bias, HEAD),
        jnp.zeros((MLSTM_W,), F32),
    ]).astype(F32)

    qa, kva, qb, kb, vb, qkvc, gc = _in_proj(
        x, norm1.reshape(1, D), w_main, vecs, cos, sin, bds, bdn, tri256)
    ya = _swa(swa_sinks.astype(F32), qa, kva, eye256)
    yb = _fox(qb, kb, vb, eye256)
    yc = _mlstm(qkvc, gc, mlstm_out_norm.reshape(1, MLSTM_W).astype(F32), tri64, i4, bd256)

    wo = jnp.take(w_out, jnp.asarray(_WOUT_ROWS), axis=0).astype(BF16)
    T = B * S
    out = _out_ffn(x.reshape(T, D), ya.reshape(T, SWA_Q), yb.reshape(T, FOX_W),
                   yc.reshape(T, MLSTM_W), wo, norm2.reshape(1, D), w_ff1.astype(BF16),
                   w_ff2.astype(BF16))
    return out.reshape(B, S, D)


def kernel(x, norm1, w_in, swa_q_norm, swa_k_norm, swa_sinks, fox_q_norm, fox_k_norm, fox_f_bias,
           mlstm_i_bias, mlstm_f_bias, mlstm_out_norm, w_out, norm2, w_ff1, w_ff2):
    S = x.shape[1]
    cos, sin = _rope_tables(S)
    consts = (jnp.asarray(_BD_SWA, BF16), jnp.asarray(_BD_NAT, BF16), jnp.asarray(_TRI256, BF16),
              jnp.asarray(_TRI64, BF16), jnp.asarray(_I4, F32), jnp.asarray(_BD256, BF16),
              jnp.eye(MXU_DIM, dtype=BF16))
    for l in range(norm1.shape[0]):
        x = _layer(x, cos, sin, consts, norm1[l], w_in[l], swa_q_norm[l], swa_k_norm[l],
                   swa_sinks[l], fox_q_norm[l], fox_k_norm[l], fox_f_bias[l], mlstm_i_bias[l],
                   mlstm_f_bias[l], mlstm_out_norm[l], w_out[l], norm2[l], w_ff1[l], w_ff2[l])
    return x
```

```python
import functools
import math

import numpy as np
import jax
import jax.numpy as jnp
from jax import lax
from jax.experimental import pallas as pl
from jax.experimental.pallas import tpu as pltpu

F32 = jnp.float32
BF16 = jnp.bfloat16

D_MODEL = 1024
HEAD = 64
HALF = HEAD // 2
CHUNK = 64
SWA_HEADS = 8
SWA_KV_HEADS = 2
FOX_HEADS = 4
MLSTM_HEADS = 4
SWA_Q = SWA_HEADS * HEAD
SWA_KV = SWA_KV_HEADS * HEAD
FOX_W = FOX_HEADS * HEAD
MLSTM_W = MLSTM_HEADS * HEAD
D_FF = 4 * D_MODEL
QBLK = 128
ROPE_THETA = 10000.0
EPS = 1e-6
QK_SCALE = 1.0 / math.sqrt(HEAD)

LANES = 128
MXU_DIM = 256
VMEM_LIMIT = 56 * 1024 * 1024

SEC_A = 0
SEC_B = SEC_A + 768
SEC_C = SEC_B + 768
SEC_CO = SEC_C + 768
SEC_CI = SEC_CO + 256
SEC_CF = SEC_CI + 256
SEC_G = SEC_CF + 256
W_MAIN = SEC_G + LANES


def _in_proj_columns():
    o_aq, o_ak, o_av = 0, SWA_Q, SWA_Q + SWA_KV
    o_bq = o_av + SWA_KV
    o_bk, o_bv = o_bq + FOX_W, o_bq + 2 * FOX_W
    o_bf = o_bv + FOX_W
    o_cq = o_bf + FOX_HEADS
    o_ck, o_cv = o_cq + MLSTM_W, o_cq + 2 * MLSTM_W
    o_ci = o_cv + MLSTM_W
    o_cf = o_ci + MLSTM_HEADS
    o_co = o_cf + MLSTM_HEADS
    idx = np.full((W_MAIN,), -1, np.int64)
    lane = np.arange(LANES)
    quarter, i = lane // HALF, lane % HALF
    dim = i + HALF * (quarter // 2)
    is_b = quarter % 2
    for j in range(SWA_HEADS // 2):
        idx[SEC_A + j * LANES + lane] = o_aq + (j + 4 * is_b) * HEAD + dim
    idx[SEC_A + SWA_Q + lane] = o_ak + is_b * HEAD + dim
    idx[SEC_A + SWA_Q + SWA_KV + lane] = o_av + lane
    w = np.arange(FOX_W)
    idx[SEC_B + w] = o_bq + w
    idx[SEC_B + FOX_W + w] = o_bk + w
    idx[SEC_B + 2 * FOX_W + w] = o_bv + w
    idx[SEC_C + w] = o_cq + w
    idx[SEC_C + MLSTM_W + w] = o_ck + w
    idx[SEC_C + 2 * MLSTM_W + w] = o_cv + w
    idx[SEC_CO + w] = o_co + w
    idx[SEC_CI + w] = o_ci + w // HEAD
    idx[SEC_CF + w] = o_cf + w // HEAD
    for rep in range(3):
        idx[SEC_G + rep * FOX_HEADS + np.arange(FOX_HEADS)] = o_bf + np.arange(FOX_HEADS)
    return idx


def _swa_lane_dim():
    lane = np.arange(LANES)
    return lane % HALF + HALF * (lane // HEAD)


def _wout_rows():
    rows = np.arange(D_MODEL)
    r = np.arange(SWA_Q)
    j, half, d = r // LANES, (r % LANES) // HEAD, r % HEAD
    rows[:SWA_Q] = (j + 4 * half) * HEAD + d
    return rows


_IN_COLS = _in_proj_columns()
_WOUT_ROWS = _wout_rows()


def _block_diag_ones(head_of_lane):
    return (head_of_lane[:, None] == head_of_lane[None, :]).astype(np.float32)


_lane = np.arange(LANES)
_BD_SWA = np.concatenate([_block_diag_ones((_lane // HALF) % 2)] * 2, axis=0)
_BD_NAT = np.concatenate([_block_diag_ones(_lane // HEAD)] * 2, axis=0)
_TRI256 = np.tril(np.ones((MXU_DIM, MXU_DIM), np.float32))
_TRI64 = np.tril(np.ones((CHUNK, CHUNK), np.float32))
_w256 = np.arange(MLSTM_W)
_BD256 = _block_diag_ones(_w256 // HEAD)
_I4 = (np.arange(CHUNK)[:, None] == (_w256 % HEAD)[None, :]).astype(np.float32)


def _const_spec(shape):
    nd = len(shape)
    return pl.BlockSpec(shape, lambda *_: (0,) * nd, pipeline_mode=pl.Buffered(1))


def _log_sigmoid(x):
    return jnp.minimum(x, 0.0) - jnp.log(1.0 + jnp.exp(-jnp.abs(x)))


def _split2(x):
    hi = x.astype(BF16)
    lo = (x - hi.astype(F32)).astype(BF16)
    return hi, lo


def _split3_by_lane(x, lane):
    hi = x.astype(BF16).astype(F32)
    r1 = x - hi
    mid = r1.astype(BF16).astype(F32)
    low = r1 - mid
    return jnp.where(lane < 4, hi, jnp.where(lane < 8, mid, low)).astype(BF16)


def _dot(a, b):
    return jnp.dot(a, b, preferred_element_type=F32)


def _dot_nt(a, b):
    return lax.dot_general(a, b, (((1,), (1,)), ((), ())), preferred_element_type=F32)


def _dot_tn(a, b):
    return lax.dot_general(a, b, (((0,), (0,)), ((), ())), preferred_element_type=F32)


TM_IN = 512


def _in_proj_body(x_ref, n1_ref, w_ref, vec_ref, cos_ref, sin_ref, bds_ref, bdn_ref, tri_ref,
                  qa_ref, kva_ref, qb_ref, kb_ref, vb_ref, qkvc_ref, gc_ref, carry_ref):
    tm = x_ref.shape[1]

    @pl.when(pl.program_id(1) == 0)
    def _():
        carry_ref[...] = jnp.zeros_like(carry_ref)

    x = x_ref[0]
    ms = jnp.mean(x * x, axis=-1, keepdims=True)
    h = ((x * lax.rsqrt(ms + EPS)) * n1_ref[...]).astype(BF16)

    def proj(c0, width):
        return _dot(h, w_ref[:, c0:c0 + width])

    def head_norm(z, bd_ref, gain):
        hi, lo = _split2(z * z)
        ss = _dot(jnp.concatenate([hi, lo], axis=-1), bd_ref[...])
        return (z * lax.rsqrt(ss * (1.0 / HEAD) + EPS)) * gain

    cos = cos_ref[...]
    sin = sin_ref[...]

    def rope(z):
        return z * cos + pltpu.roll(z, HEAD, 1) * sin

    g_aq = vec_ref[0:1, 0:LANES]
    g_ak = vec_ref[1:2, 0:LANES]
    g_bq = vec_ref[2:3, 0:LANES]
    g_bk = vec_ref[3:4, 0:LANES]
    f_bias = vec_ref[4:5, 0:LANES]
    i_bias = vec_ref[5:6, :]
    cf_bias = vec_ref[6:7, :]

    za = proj(SEC_A, 768)
    for j in range(SWA_HEADS // 2):
        q = rope(head_norm(za[:, j * LANES:(j + 1) * LANES], bds_ref, g_aq)) * QK_SCALE
        qa_ref[0, :, j * LANES:(j + 1) * LANES] = q.astype(BF16)
    k = rope(head_norm(za[:, SWA_Q:SWA_Q + LANES], bds_ref, g_ak))
    kva_ref[0, :, 0:LANES] = k.astype(BF16)
    kva_ref[0, :, LANES:2 * LANES] = za[:, SWA_Q + SWA_KV:].astype(BF16)

    zb = proj(SEC_B, 768)
    for p in range(FOX_HEADS // 2):
        q = head_norm(zb[:, p * LANES:(p + 1) * LANES], bdn_ref, g_bq) * QK_SCALE
        qb_ref[0, :, p * LANES:(p + 1) * LANES] = q.astype(BF16)
        k = head_norm(zb[:, FOX_W + p * LANES:FOX_W + (p + 1) * LANES], bdn_ref, g_bk)
        kb_ref[0, :, 2 * p * LANES:(2 * p + 1) * LANES] = k.astype(BF16)
    vb_ref[0] = zb[:, 2 * FOX_W:].astype(BF16)

    lane = lax.broadcasted_iota(jnp.int32, (MXU_DIM, LANES), 1)
    zg = proj(SEC_G, LANES)
    log_f = jnp.where(lane[0:1] < 12, _log_sigmoid(zg + f_bias), 0.0)
    carry = carry_ref[0:1, :]
    for blk in range(tm // MXU_DIM):
        rows = slice(blk * MXU_DIM, (blk + 1) * MXU_DIM)
        cs = _dot(tri_ref[...], _split3_by_lane(log_f[rows], lane))
        f_loc = cs + pltpu.roll(cs, LANES - 4, 1) + pltpu.roll(cs, LANES - 8, 1)
        f_cum = jnp.where(lane < 4, f_loc, 0.0) + carry
        carry = f_cum[MXU_DIM - 1:MXU_DIM, :]
        f_rep = f_cum + pltpu.roll(f_cum, 4, 1) + pltpu.roll(f_cum, 8, 1)
        aug = _split3_by_lane(f_rep, lane)
        kb_ref[0, rows, LANES:2 * LANES] = aug
        kb_ref[0, rows, 3 * LANES:4 * LANES] = aug
    carry_ref[...] = jnp.broadcast_to(carry, carry_ref.shape)

    zc = proj(SEC_C, 768)
    qkvc_ref[0, :, 0:MLSTM_W] = zc[:, 0:MLSTM_W].astype(BF16)
    qkvc_ref[0, :, MLSTM_W:2 * MLSTM_W] = (zc[:, MLSTM_W:2 * MLSTM_W] * QK_SCALE).astype(BF16)
    qkvc_ref[0, :, 2 * MLSTM_W:] = zc[:, 2 * MLSTM_W:].astype(BF16)
    zgc = proj(SEC_CO, 768)
    gc_ref[0, :, 0:MLSTM_W] = zgc[:, MLSTM_W:2 * MLSTM_W] + i_bias
    gc_ref[0, :, MLSTM_W:2 * MLSTM_W] = _log_sigmoid(zgc[:, 2 * MLSTM_W:] + cf_bias)
    gc_ref[0, :, 2 * MLSTM_W:] = zgc[:, 0:MLSTM_W]


def _in_proj(x, n1, w_main, vecs, cos, sin, bds, bdn, tri):
    B, S, _ = x.shape
    tm = min(TM_IN, S)
    grid = (B, S // tm)
    tok = lambda w: pl.BlockSpec((1, tm, w), lambda b, i: (b, i, 0))
    out_shapes = (
        jax.ShapeDtypeStruct((B, S, SWA_Q), BF16),
        jax.ShapeDtypeStruct((B, S, 2 * SWA_KV), BF16),
        jax.ShapeDtypeStruct((B, S, FOX_W), BF16),
        jax.ShapeDtypeStruct((B, S, 2 * MXU_DIM), BF16),
        jax.ShapeDtypeStruct((B, S, FOX_W), BF16),
        jax.ShapeDtypeStruct((B, S, 3 * MLSTM_W), BF16),
        jax.ShapeDtypeStruct((B, S, 3 * MLSTM_W), F32),
    )
    return pl.pallas_call(
        _in_proj_body,
        grid=grid,
        in_specs=[
            tok(D_MODEL),
            _const_spec((1, D_MODEL)),
            _const_spec((D_MODEL, W_MAIN)),
            _const_spec((8, MLSTM_W)),
            pl.BlockSpec((tm, LANES), lambda b, i: (i, 0)),
            pl.BlockSpec((tm, LANES), lambda b, i: (i, 0)),
            _const_spec((MXU_DIM, LANES)),
            _const_spec((MXU_DIM, LANES)),
            _const_spec((MXU_DIM, MXU_DIM)),
        ],
        out_specs=[tok(s.shape[-1]) for s in out_shapes],
        out_shape=out_shapes,
        scratch_shapes=[pltpu.VMEM((8, LANES), F32)],
        compiler_params=pltpu.CompilerParams(
            dimension_semantics=("arbitrary", "arbitrary"), vmem_limit_bytes=VMEM_LIMIT),
        name="in_proj",
    )(x, n1, w_main, vecs, cos, sin, bds, bdn, tri)


TS_SWA = 512
SWA_DEPTH = 4
VT_ROWS = HEAD + 16


def _swa_body(sink_ref, q_ref, kvc_ref, kvp_ref, eye_ref, o_ref):
    step = pl.program_id(1)
    nblk = q_ref.shape[1] // QBLK
    per_group = SWA_HEADS // SWA_KV_HEADS
    eye = eye_ref[0:LANES, 0:LANES]

    key = lax.broadcasted_iota(jnp.int32, (2 * QBLK, MXU_DIM), 0)
    qry = lax.broadcasted_iota(jnp.int32, (2 * QBLK, MXU_DIM), 1) % QBLK
    dchunk = key // CHUNK - qry // CHUNK
    in_band = (dchunk >= 0) & (dchunk <= 2)
    lane = lax.broadcasted_iota(jnp.int32, (QBLK, LANES), 1)
    lane_a = (lane // HALF) % 2 == 0
    first_head = lax.broadcasted_iota(jnp.int32, (1, MXU_DIM), 1) < QBLK
    zero = jnp.zeros((QBLK, LANES), BF16)
    ones = jnp.ones((VT_ROWS - HEAD, 2 * QBLK), BF16)
    kv_all = jnp.concatenate([kvp_ref[0], kvc_ref[0]], axis=0)
    rows = [slice(blk * QBLK, (blk + 1) * QBLK) for blk in range(nblk)]
    bands = [kv_all[blk * QBLK:(blk + 2) * QBLK] for blk in range(nblk)]

    vt_aug = []
    for blk in range(nblk):
        v_t = _dot_nt(eye, bands[blk][:, LANES:]).astype(BF16)
        vt_aug.append([jnp.concatenate([v_t[g * HEAD:(g + 1) * HEAD], ones], axis=0)
                       for g in range(SWA_KV_HEADS)])

    units = [(blk, g, part) for blk in range(nblk) for g in range(SWA_KV_HEADS) for part in range(2)]

    def score(blk, g, part):
        keep = lane_a if g == 0 else jnp.logical_not(lane_a)
        q_cols = [q_ref[0, rows[blk], j * LANES:(j + 1) * LANES] for j in (2 * part, 2 * part + 1)]
        q_g = jnp.concatenate([jnp.where(keep, q2, zero) for q2 in q_cols], axis=0)
        return _dot_nt(bands[blk][:, 0:LANES], q_g)

    def fold(s, blk, g, part):
        allowed = in_band if blk > 0 else in_band & ((key >= QBLK) | (step > 0))
        s = jnp.where(allowed, s, -jnp.inf)
        head = g * per_group + 2 * part
        sink = jnp.where(first_head, sink_ref[head], sink_ref[head + 1])
        m = jnp.maximum(jnp.max(s, axis=0, keepdims=True), sink)
        p = jnp.exp(s - m)
        o_aug = _dot(vt_aug[blk][g], p.astype(BF16))
        den = o_aug[HEAD:HEAD + 1] + jnp.exp(sink - m)
        return (o_aug[0:HEAD] / den).astype(BF16)

    def write_block(blk, outs):
        for j in range(SWA_HEADS // 2):
            cols = slice((j % 2) * QBLK, (j % 2 + 1) * QBLK)
            pair_t = jnp.concatenate([outs[(blk, 0, j // 2)][:, cols],
                                      outs[(blk, 1, j // 2)][:, cols]], axis=0)
            o_ref[0, rows[blk], j * LANES:(j + 1) * LANES] = _dot_nt(eye, pair_t).astype(BF16)

    outs = {}
    queue = [score(*unit) for unit in units[:SWA_DEPTH]]
    for n, unit in enumerate(units):
        s_cur = queue.pop(0)
        if n + SWA_DEPTH < len(units):
            queue.append(score(*units[n + SWA_DEPTH]))
        outs[unit] = fold(s_cur, *unit)
        blk = unit[0]
        if unit[1:] == (SWA_KV_HEADS - 1, 1) and blk > 0:
            write_block(blk - 1, outs)
    write_block(nblk - 1, outs)


def _swa(sinks, qa, kva, eye):
    B, S, _ = qa.shape
    ts = min(TS_SWA, S)
    per_step = ts // QBLK
    return pl.pallas_call(
        _swa_body,
        grid=(B, S // ts),
        in_specs=[
            pl.BlockSpec(memory_space=pltpu.SMEM),
            pl.BlockSpec((1, ts, SWA_Q), lambda b, i: (b, i, 0)),
            pl.BlockSpec((1, ts, 2 * SWA_KV), lambda b, i: (b, i, 0)),
            pl.BlockSpec((1, QBLK, 2 * SWA_KV), lambda b, i: (b, jnp.maximum(i * per_step - 1, 0), 0)),
            _const_spec((MXU_DIM, MXU_DIM)),
        ],
        out_specs=pl.BlockSpec((1, ts, SWA_Q), lambda b, i: (b, i, 0)),
        out_shape=jax.ShapeDtypeStruct((B, S, SWA_Q), BF16),
        compiler_params=pltpu.CompilerParams(
            dimension_semantics=("arbitrary", "arbitrary"), vmem_limit_bytes=VMEM_LIMIT),
        name="swa",
    )(sinks, qa, kva, kva, eye)


TQ_FOX = 512
TK_FOX = 256
NEG_BIG = -1e30


FOX_UNROLL = 4
FOX_DEPTH = 4


def _fox_body(q_ref, k_ref, v_ref, eye_ref, o_ref, vt_ref, qt_ref, acc_ref):
    tq = q_ref.shape[1]
    tk = TK_FOX
    S = k_ref.shape[1]
    pair = pl.program_id(1)
    qi = pl.program_id(2)
    eye = eye_ref[...]
    n_tiles = 2 * tq // MXU_DIM
    tiles_per_head = tq // MXU_DIM
    half = tk // 2

    @pl.when(qi == 0)
    def _():
        ones = jnp.ones((VT_ROWS - HEAD, tk), BF16)
        for j in range(S // tk):
            blk = slice(j * tk, (j + 1) * tk)
            v_t = _dot_nt(eye[0:LANES, 0:LANES], v_ref[0, blk, :]).astype(BF16)
            vt_ref[:, blk] = jnp.concatenate([v_t[0:HEAD], ones, v_t[HEAD:], ones], axis=0)

    lane = lax.broadcasted_iota(jnp.int32, (tq, LANES), 1)
    lane_a = lane < HEAD
    q2 = q_ref[0]
    zero = jnp.zeros_like(q2)

    def minus_one_at(head):
        hit = (lane == head) | (lane == head + 4) | (lane == head + 8)
        return jnp.where(hit, -1.0, 0.0).astype(BF16)

    q_aug = [jnp.concatenate([jnp.where(lane_a, q2, zero), minus_one_at(2 * pair)], axis=1),
             jnp.concatenate([jnp.where(lane_a, zero, q2), minus_one_at(2 * pair + 1)], axis=1)]
    for t in range(n_tiles):
        hd, part = divmod(t, tiles_per_head)
        rows = q_aug[hd][part * MXU_DIM:(part + 1) * MXU_DIM]
        qt_ref[:, t * MXU_DIM:(t + 1) * MXU_DIM] = _dot_nt(eye, rows).astype(BF16)
    acc_ref[...] = jnp.zeros_like(acc_ref)

    key = lax.broadcasted_iota(jnp.int32, (half, MXU_DIM), 0)
    qry = lax.broadcasted_iota(jnp.int32, (half, MXU_DIM), 1)

    def fold(s, t, vt_blk, m, triangular):
        hd = t // tiles_per_head
        vt_h = vt_blk[hd * VT_ROWS:(hd + 1) * VT_ROWS]
        for h in range(2):
            sh = s[h * half:(h + 1) * half]
            if triangular:
                valid = key + h * half <= qry
                sh = jnp.where(valid, sh, NEG_BIG)
            m_new = jnp.maximum(m, jnp.max(sh, axis=0, keepdims=True))
            alpha = jnp.exp(m - m_new)
            p = jnp.exp(sh - m_new)
            if triangular:
                p = jnp.where(valid, p, 0.0)
            acc_ref[t] = alpha * acc_ref[t] + _dot(vt_h[:, h * half:(h + 1) * half], p.astype(BF16))
            m = m_new
        return m

    def region(j0, ms, plans):
        k_blks, vt_blks, todo = [], [], []
        for b, plan in enumerate(plans):
            r0 = pl.multiple_of((j0 + b) * tk, tk)
            k_blks.append(k_ref[0, pl.ds(r0, tk), :])
            vt_blks.append(vt_ref[:, pl.ds(r0, tk)])
            todo += [(b, t) for t in range(n_tiles) if plan[t] is not None]
        score = lambda b, t: _dot(k_blks[b], qt_ref[:, t * MXU_DIM:(t + 1) * MXU_DIM])
        ms = list(ms)
        queue = [score(*unit) for unit in todo[:FOX_DEPTH]]
        for n, (b, t) in enumerate(todo):
            s_cur = queue.pop(0)
            if n + FOX_DEPTH < len(todo):
                queue.append(score(*todo[n + FOX_DEPTH]))
            ms[t] = fold(s_cur, t, vt_blks[b], ms[t], plans[b][t])
        return tuple(ms)

    blocks_per_tile = tq // tk
    full = (False,) * n_tiles
    n_full = blocks_per_tile * qi
    m_init = jnp.full((1, MXU_DIM), NEG_BIG, F32)
    ms = lax.fori_loop(0, n_full // FOX_UNROLL,
                       lambda g, c: region(g * FOX_UNROLL, c, (full,) * FOX_UNROLL), (m_init,) * n_tiles)
    done = (n_full // FOX_UNROLL) * FOX_UNROLL
    ms = lax.fori_loop(0, (n_full - done) // blocks_per_tile,
                       lambda g, c: region(done + g * blocks_per_tile, c, (full,) * blocks_per_tile), ms)
    diag = [tuple(None if t % tiles_per_head < d else t % tiles_per_head == d for t in range(n_tiles))
            for d in range(blocks_per_tile)]
    ms = region(n_full, ms, diag)

    for hd in range(2):
        cols = []
        for part in range(tiles_per_head):
            acc = acc_ref[hd * tiles_per_head + part]
            cols.append(acc[0:HEAD] / acc[HEAD:HEAD + 1])
        o_h = jnp.concatenate(cols, axis=1)
        if hd == 0:
            o_t = o_h
        else:
            o_t = jnp.concatenate([o_t, o_h], axis=0).astype(BF16)
    for c in range(tq // MXU_DIM):
        rows = slice(c * MXU_DIM, (c + 1) * MXU_DIM)
        o_ref[0, rows, :] = _dot_nt(eye, o_t[:, rows]).astype(BF16)


def _fox_body_old(q_ref, k_ref, v_ref, eye_ref, o_ref, vt_ref, qaug_ref, s_ref, acc_ref):
    tq = q_ref.shape[1]
    tk = TK_FOX
    S = k_ref.shape[1]
    pair = pl.program_id(1)
    qi = pl.program_id(2)
    eye = eye_ref[...]

    @pl.when(qi == 0)
    def _():
        ones = jnp.ones((VT_ROWS - HEAD, tk), BF16)
        for j in range(S // tk):
            blk = slice(j * tk, (j + 1) * tk)
            v_t = _dot_nt(eye[0:LANES, 0:LANES], v_ref[0, blk, :]).astype(BF16)
            vt_ref[:, blk] = jnp.concatenate([v_t[0:HEAD], ones, v_t[HEAD:], ones], axis=0)

    lane = lax.broadcasted_iota(jnp.int32, (tq, LANES), 1)
    lane_a = lane < HEAD
    q2 = q_ref[0]
    zero = jnp.zeros_like(q2)

    def minus_one_at(head):
        hit = (lane == head) | (lane == head + 4) | (lane == head + 8)
        return jnp.where(hit, -1.0, 0.0).astype(BF16)

    q_aug = jnp.concatenate([
        jnp.concatenate([jnp.where(lane_a, q2, zero), minus_one_at(2 * pair)], axis=1),
        jnp.concatenate([jnp.where(lane_a, zero, q2), minus_one_at(2 * pair + 1)], axis=1),
    ], axis=0)

    key = lax.broadcasted_iota(jnp.int32, (tk, 2 * tq), 0)
    qry = lax.broadcasted_iota(jnp.int32, (tk, 2 * tq), 1) % tq

    per_tile = tq // tk
    qaug_ref[...] = q_aug
    acc_ref[...] = jnp.zeros_like(acc_ref)

    def scores(blk):
        r0 = pl.multiple_of(blk * tk, tk)
        return _dot_nt(k_ref[0, pl.ds(r0, tk), :], qaug_ref[...])

    def update(c, blk, slot, m, key_offset):
        r0 = pl.multiple_of(blk * tk, tk)
        s = s_ref[slot, c]
        if key_offset is not None:
            valid = key + key_offset <= qry
            s = jnp.where(valid, s, NEG_BIG)
        m_new = jnp.maximum(m, jnp.max(s, axis=0, keepdims=True))
        alpha = jnp.exp(m - m_new)
        p = jnp.exp(s - m_new)
        if key_offset is not None:
            p = jnp.where(valid, p, 0.0)
        pb = p.astype(BF16)
        vt = vt_ref[:, pl.ds(r0, tk)]
        for hd in range(2):
            rows = slice(hd * VT_ROWS, (hd + 1) * VT_ROWS)
            cols = slice(hd * tq, (hd + 1) * tq)
            acc_ref[c, rows, :] = alpha[:, cols] * acc_ref[c, rows, :] + _dot(vt[rows], pb[:, cols])
        return m_new

    for c in range(per_tile):
        s_ref[0, c] = scores(c)

    def body(i, ms):
        slot = i % 2
        for c in range(per_tile):
            s_ref[1 - slot, c] = scores(per_tile * (i + 1) + c)
        return tuple(update(c, per_tile * i + c, slot, ms[c], None) for c in range(per_tile))

    m_init = jnp.full((1, 2 * tq), NEG_BIG, F32)
    ms = lax.fori_loop(0, qi, body, (m_init,) * per_tile)
    ms = [update(c, per_tile * qi + c, qi % 2, ms[c], c * tk) for c in range(per_tile)]

    m = functools.reduce(jnp.maximum, ms)
    acc = jnp.zeros(acc_ref.shape[1:], F32)
    for c in range(per_tile):
        w = jnp.exp(ms[c] - m)
        w_rows = jnp.concatenate([jnp.broadcast_to(w[:, :tq], (VT_ROWS, tq)),
                                  jnp.broadcast_to(w[:, tq:], (VT_ROWS, tq))], axis=0)
        acc = acc + w_rows * acc_ref[c]
    o_heads = []
    for hd in range(2):
        num = acc[hd * VT_ROWS:hd * VT_ROWS + HEAD]
        den = acc[hd * VT_ROWS + HEAD:hd * VT_ROWS + HEAD + 1]
        o_heads.append(num / den)
    o_t = jnp.concatenate(o_heads, axis=0).astype(BF16)
    for c in range(tq // MXU_DIM):
        rows = slice(c * MXU_DIM, (c + 1) * MXU_DIM)
        o_ref[0, rows, :] = _dot_nt(eye, o_t[:, rows]).astype(BF16)


def _fox(qb, kb, vb, eye):
    B, S, _ = qb.shape
    tq = min(TQ_FOX, S)
    return pl.pallas_call(
        _fox_body,
        grid=(B, FOX_HEADS // 2, S // tq),
        in_specs=[
            pl.BlockSpec((1, tq, LANES), lambda b, p, i: (b, i, p)),
            pl.BlockSpec((1, S, MXU_DIM), lambda b, p, i: (b, 0, p)),
            pl.BlockSpec((1, S, LANES), lambda b, p, i: (b, 0, p)),
            _const_spec((MXU_DIM, MXU_DIM)),
        ],
        out_specs=pl.BlockSpec((1, tq, LANES), lambda b, p, i: (b, i, p)),
        out_shape=jax.ShapeDtypeStruct((B, S, FOX_W), BF16),
        scratch_shapes=[
            pltpu.VMEM((2 * VT_ROWS, S), BF16),
            pltpu.VMEM((MXU_DIM, 2 * tq), BF16),
            pltpu.VMEM((2 * tq // MXU_DIM, VT_ROWS, MXU_DIM), F32),
        ],
        compiler_params=pltpu.CompilerParams(
            dimension_semantics=("arbitrary", "arbitrary", "arbitrary"),
            vmem_limit_bytes=VMEM_LIMIT),
        name="fox",
    )(qb, kb, vb, eye)


TS_MLSTM = 512


def _mlstm_body(qkv_ref, g_ref, onorm_ref, tri_ref, i4_ref, bd_ref, o_ref, c_ref, n_ref, m_ref):
    ts = qkv_ref.shape[1]
    W = MLSTM_W

    @pl.when(pl.program_id(1) == 0)
    def _():
        c_ref[...] = jnp.zeros_like(c_ref)
        n_ref[...] = jnp.zeros_like(n_ref)
        m_ref[...] = jnp.zeros_like(m_ref)

    row = lax.broadcasted_iota(jnp.int32, (CHUNK, W), 0)
    lane = lax.broadcasted_iota(jnp.int32, (CHUNK, W), 1)
    causal = (lane % HEAD) <= row
    lane_head = lane // HEAD
    bd = bd_ref[...]
    bd2 = jnp.concatenate([bd, bd], axis=1)
    i4_bf = i4_ref[...].astype(BF16)

    def block_diag(x):
        tiled = jnp.concatenate([x] * MLSTM_HEADS, axis=0)
        return tiled * (bd if x.shape[1] == W else bd2)

    def seg_max(d):
        out = jnp.zeros_like(d)
        for hd in range(MLSTM_HEADS):
            sel = lane_head == hd
            mx = jnp.max(jnp.where(sel, d, -jnp.inf), axis=-1, keepdims=True)
            out = jnp.where(sel, mx, out)
        return out

    chunks = range(ts // CHUNK)
    rows = [pl.ds(c * CHUNK, CHUNK) for c in chunks]

    b, g, u, dm, dm_max, a_max, wa = [], [], [], [], [], [], []
    for c in chunks:
        ig = g_ref[0, rows[c], 0:W]
        lf_hi, lf_lo = _split2(g_ref[0, rows[c], W:2 * W])
        b_c = _dot(tri_ref[...], lf_hi) + _dot(tri_ref[...], lf_lo)
        g_c = b_c[CHUNK - 1:CHUNK, :]
        u_c = b_c - ig
        u_row = jnp.sum(u_c * i4_ref[...], axis=0, keepdims=True)
        dm_c = jnp.where(causal, b_c - u_row, -jnp.inf)
        a_c = g_c - u_c
        am_c = jnp.max(a_c, axis=0, keepdims=True)
        b.append(b_c); g.append(g_c); u.append(u_c); dm.append(dm_c)
        dm_max.append(seg_max(dm_c)); a_max.append(am_c); wa.append(jnp.exp(a_c - am_c))

    m_prev, s_old, s_loc = [m_ref[0:1, :]], [], []
    for c in chunks:
        gm = g[c] + m_prev[c]
        m_new = jnp.maximum(gm, a_max[c])
        s_old.append(jnp.exp(gm - m_new))
        s_loc.append(jnp.exp(a_max[c] - m_new))
        m_prev.append(m_new)

    m_t, w_inter, sm_hi, sm_lo, k_t = [], [], [], [], []
    for c in chunks:
        inter = b[c] + m_prev[c]
        mt_c = jnp.maximum(inter, dm_max[c])
        decay = jnp.exp(dm[c] - mt_c)
        q = qkv_ref[0, rows[c], 0:W]
        k = qkv_ref[0, rows[c], W:2 * W]
        qk = _dot_nt(jnp.concatenate([q, i4_bf], axis=0), block_diag(k))
        hi, lo = _split2(qk[0:CHUNK] * decay)
        m_t.append(mt_c); w_inter.append(jnp.exp(inter - mt_c)); sm_hi.append(hi); sm_lo.append(lo)
        k_t.append(qk[CHUNK:].astype(BF16))

    loc = []
    for c in chunks:
        v = qkv_ref[0, rows[c], 2 * W:]
        weighted = jnp.concatenate([(wa[c] * v.astype(F32)).astype(BF16), wa[c].astype(BF16)], axis=1)
        loc.append(_dot(k_t[c], block_diag(weighted)))

    c_prev, n_prev = [c_ref[...]], [n_ref[...]]
    for c in chunks:
        c_prev.append(s_old[c] * c_prev[c] + s_loc[c] * loc[c][:, 0:W])
        n_prev.append(s_old[c] * n_prev[c] + s_loc[c] * loc[c][:, W:])
    c_ref[...] = c_prev[-1]
    n_ref[...] = n_prev[-1]
    m_ref[...] = jnp.broadcast_to(m_prev[-1], m_ref.shape)

    hval = []
    for c in chunks:
        q = qkv_ref[0, rows[c], 0:W]
        v = qkv_ref[0, rows[c], 2 * W:]
        state = block_diag(jnp.concatenate([c_prev[c].astype(BF16), n_prev[c].astype(BF16)], axis=1))
        inter_cn = _dot(q, state)
        intra = _dot(sm_hi[c], jnp.concatenate([block_diag(v), bd], axis=1))
        num = w_inter[c] * inter_cn[:, 0:W] + intra[:, 0:W]
        den = w_inter[c] * inter_cn[:, W:] + intra[:, W:] + _dot(sm_lo[c], bd)
        hval.append(num / jnp.maximum(jnp.abs(den), jnp.exp(-m_t[c])))

    for c in chunks:
        hs_hi, hs_lo = _split2(hval[c] * hval[c])
        ss = _dot(hs_hi, bd) + _dot(hs_lo, bd)
        hc = (hval[c] * lax.rsqrt(ss * (1.0 / HEAD) + EPS)) * onorm_ref[...]
        co = g_ref[0, rows[c], 2 * W:]
        o_ref[0, rows[c], :] = (jax.nn.sigmoid(co) * hc).astype(BF16)


def _mlstm(qkvc, gc, onorm, tri, i4, bd):
    B, S, _ = qkvc.shape
    ts = min(TS_MLSTM, S)
    return pl.pallas_call(
        _mlstm_body,
        grid=(B, S // ts),
        in_specs=[
            pl.BlockSpec((1, ts, 3 * MLSTM_W), lambda b, i: (b, i, 0)),
            pl.BlockSpec((1, ts, 3 * MLSTM_W), lambda b, i: (b, i, 0)),
            _const_spec((1, MLSTM_W)),
            _const_spec((CHUNK, CHUNK)),
            _const_spec((CHUNK, MLSTM_W)),
            _const_spec((MLSTM_W, MLSTM_W)),
        ],
        out_specs=pl.BlockSpec((1, ts, MLSTM_W), lambda b, i: (b, i, 0)),
        out_shape=jax.ShapeDtypeStruct((B, S, MLSTM_W), BF16),
        scratch_shapes=[pltpu.VMEM((HEAD, MLSTM_W), F32), pltpu.VMEM((HEAD, MLSTM_W), F32),
                        pltpu.VMEM((8, MLSTM_W), F32)],
        compiler_params=pltpu.CompilerParams(
            dimension_semantics=("arbitrary", "arbitrary"), vmem_limit_bytes=VMEM_LIMIT),
        name="mlstm",
    )(qkvc, gc, onorm, tri, i4, bd)


TM_FFN = 512
FF_CHUNK = 1024


def _out_ffn_body(x_ref, ya_ref, yb_ref, yc_ref, wo_ref, n2_ref, w1_ref, w2_ref, o_ref, a_ref):
    x1 = (x_ref[...] + _dot(ya_ref[...], wo_ref[0:SWA_Q, :])
          + _dot(yb_ref[...], wo_ref[SWA_Q:SWA_Q + FOX_W, :])
          + _dot(yc_ref[...], wo_ref[SWA_Q + FOX_W:, :]))
    ms = jnp.mean(x1 * x1, axis=-1, keepdims=True)
    h2 = ((x1 * lax.rsqrt(ms + EPS)) * n2_ref[...]).astype(BF16)
    for j in range(D_FF // FF_CHUNK):
        cols = slice(j * FF_CHUNK, (j + 1) * FF_CHUNK)
        r = jnp.maximum(_dot(h2, w1_ref[:, cols]), 0.0)
        a_ref[:, cols] = (r * r).astype(BF16)
    o_ref[...] = x1 + _dot(a_ref[...], w2_ref[...])


def _out_ffn(x2d, ya, yb, yc, wo, n2, w1, w2):
    T = x2d.shape[0]
    tm = min(TM_FFN, T)
    tok = lambda w: pl.BlockSpec((tm, w), lambda i: (i, 0))
    return pl.pallas_call(
        _out_ffn_body,
        grid=(T // tm,),
        in_specs=[
            tok(D_MODEL), tok(SWA_Q), tok(FOX_W), tok(MLSTM_W),
            _const_spec((D_MODEL, D_MODEL)),
            _const_spec((1, D_MODEL)),
            _const_spec((D_MODEL, D_FF)),
            _const_spec((D_FF, D_MODEL)),
        ],
        out_specs=tok(D_MODEL),
        out_shape=jax.ShapeDtypeStruct((T, D_MODEL), F32),
        scratch_shapes=[pltpu.VMEM((tm, D_FF), BF16)],
        compiler_params=pltpu.CompilerParams(
            dimension_semantics=("arbitrary",), vmem_limit_bytes=VMEM_LIMIT),
        name="out_ffn",
    )(x2d, ya, yb, yc, wo, n2, w1, w2)


def _rope_tables(S):
    inv = ROPE_THETA ** (-jnp.arange(HALF, dtype=F32) / HALF)
    ang = jnp.arange(S).astype(F32)[:, None] * inv[None, :]
    cos = jnp.tile(jnp.cos(ang), (1, LANES // HALF))
    sin = jnp.tile(jnp.sin(ang), (1, LANES // HALF))
    sign = jnp.where(jnp.arange(LANES) < HEAD, -1.0, 1.0).astype(F32)
    return cos, sin * sign[None, :]


def _layer(x, cos, sin, consts, norm1, w_in, swa_q_norm, swa_k_norm, swa_sinks, fox_q_norm,
           fox_k_norm, fox_f_bias, mlstm_i_bias, mlstm_f_bias, mlstm_out_norm, w_out, norm2,
           w_ff1, w_ff2):
    B, S, D = x.shape
    bds, bdn, tri256, tri64, i4, bd256, eye256 = consts
    cols = jnp.asarray(np.maximum(_IN_COLS, 0))
    valid = jnp.asarray((_IN_COLS >= 0).astype(np.float32))
    w_main = (jnp.take(w_in, cols, axis=1) * valid[None, :]).astype(BF16)

    swa_dim = jnp.asarray(_swa_lane_dim())
    pad = lambda v: jnp.pad(v, (0, MLSTM_W - v.shape[0]))
    vecs = jnp.stack([
        pad(swa_q_norm[swa_dim]),
        pad(swa_k_norm[swa_dim]),
        pad(jnp.tile(fox_q_norm, 2)),
        pad(jnp.tile(fox_k_norm, 2)),
        pad(jnp.tile(fox_f_bias, 3)),
        jnp.repeat(mlstm_i_bias, HEAD),
        jnp.repeat(mlstm_f_bias, HEAD),
        jnp.zeros((MLSTM_W,), F32),
    ]).astype(F32)

    qa, kva, qb, kb, vb, qkvc, gc = _in_proj(
        x, norm1.reshape(1, D), w_main, vecs, cos, sin, bds, bdn, tri256)
    ya = _swa(swa_sinks.astype(F32), qa, kva, eye256)
    yb = _fox(qb, kb, vb, eye256)
    yc = _mlstm(qkvc, gc, mlstm_out_norm.reshape(1, MLSTM_W).astype(F32), tri64, i4, bd256)

    wo = jnp.take(w_out, jnp.asarray(_WOUT_ROWS), axis=0).astype(BF16)
    T = B * S
    out = _out_ffn(x.reshape(T, D), ya.reshape(T, SWA_Q), yb.reshape(T, FOX_W),
                   yc.reshape(T, MLSTM_W), wo, norm2.reshape(1, D), w_ff1.astype(BF16),
                   w_ff2.astype(BF16))
    return out.reshape(B, S, D)


def kernel(x, norm1, w_in, swa_q_norm, swa_k_norm, swa_sinks, fox_q_norm, fox_k_norm, fox_f_bias,
           mlstm_i_bias, mlstm_f_bias, mlstm_out_norm, w_out, norm2, w_ff1, w_ff2):
    S = x.shape[1]
    cos, sin = _rope_tables(S)
    consts = (jnp.asarray(_BD_SWA, BF16), jnp.asarray(_BD_NAT, BF16), jnp.asarray(_TRI256, BF16),
              jnp.asarray(_TRI64, BF16), jnp.asarray(_I4, F32), jnp.asarray(_BD256, BF16),
              jnp.eye(MXU_DIM, dtype=BF16))
    for l in range(norm1.shape[0]):
        x = _layer(x, cos, sin, consts, norm1[l], w_in[l], swa_q_norm[l], swa_k_norm[l],
                   swa_sinks[l], fox_q_norm[l], fox_k_norm[l], fox_f_bias[l], mlstm_i_bias[l],
                   mlstm_f_bias[l], mlstm_out_norm[l], w_out[l], norm2[l], w_ff1[l], w_ff2[l])
    return x
```

```python
import functools
import math

import numpy as np
import jax
import jax.numpy as jnp
from jax import lax
from jax.experimental import pallas as pl
from jax.experimental.pallas import tpu as pltpu

F32 = jnp.float32
BF16 = jnp.bfloat16

D_MODEL = 1024
HEAD = 64
HALF = HEAD // 2
CHUNK = 64
SWA_HEADS = 8
SWA_KV_HEADS = 2
FOX_HEADS = 4
MLSTM_HEADS = 4
SWA_Q = SWA_HEADS * HEAD
SWA_KV = SWA_KV_HEADS * HEAD
FOX_W = FOX_HEADS * HEAD
MLSTM_W = MLSTM_HEADS * HEAD
D_FF = 4 * D_MODEL
QBLK = 128
ROPE_THETA = 10000.0
EPS = 1e-6
QK_SCALE = 1.0 / math.sqrt(HEAD)

LANES = 128
MXU_DIM = 256
VMEM_LIMIT = 56 * 1024 * 1024

SEC_A = 0
SEC_B = SEC_A + 768
SEC_C = SEC_B + 768
SEC_CO = SEC_C + 768
SEC_CI = SEC_CO + 256
SEC_CF = SEC_CI + 256
SEC_G = SEC_CF + 256
W_MAIN = SEC_G + LANES


def _block_diag_ones(head_of_lane):
    return (head_of_lane[:, None] == head_of_lane[None, :]).astype(np.float32)


_TRI256 = np.tril(np.ones((MXU_DIM, MXU_DIM), np.float32))
_TRI64 = np.tril(np.ones((CHUNK, CHUNK), np.float32))
_w256 = np.arange(MLSTM_W)
_BD256 = _block_diag_ones(_w256 // HEAD)
_I4 = (np.arange(CHUNK)[:, None] == (_w256 % HEAD)[None, :]).astype(np.float32)


def _const_spec(shape):
    nd = len(shape)
    return pl.BlockSpec(shape, lambda *_: (0,) * nd, pipeline_mode=pl.Buffered(1))


def _layer_spec(shape, layer):
    nd = len(shape)
    return pl.BlockSpec((None,) + tuple(shape), lambda *_: (layer,) + (0,) * nd,
                        pipeline_mode=pl.Buffered(1))


def _log_sigmoid(x):
    return jnp.minimum(x, 0.0) - jnp.log(1.0 + jnp.exp(-jnp.abs(x)))


def _split2(x):
    hi = x.astype(BF16)
    lo = (x - hi.astype(F32)).astype(BF16)
    return hi, lo


def _split3_by_lane(x, lane):
    hi = x.astype(BF16).astype(F32)
    r1 = x - hi
    mid = r1.astype(BF16).astype(F32)
    low = r1 - mid
    return jnp.where(lane < 4, hi, jnp.where(lane < 8, mid, low)).astype(BF16)


def _dot(a, b):
    return jnp.dot(a, b, preferred_element_type=F32)


def _dot_nt(a, b):
    return lax.dot_general(a, b, (((1,), (1,)), ((), ())), preferred_element_type=F32)


def _dot_tn(a, b):
    return lax.dot_general(a, b, (((0,), (0,)), ((), ())), preferred_element_type=F32)


TM_IN = 512


def _in_proj_body(x_ref, n1_ref, w_ref, vec_ref, cos_ref, sin_ref, tri_ref,
                  qa_ref, kva_ref, qb_ref, kb_ref, vb_ref, qkvc_ref, gc_ref, carry_ref):
    tm = x_ref.shape[1]

    @pl.when(pl.program_id(1) == 0)
    def _():
        carry_ref[...] = jnp.zeros_like(carry_ref)

    x = x_ref[0]
    ms = jnp.mean(x * x, axis=-1, keepdims=True)
    h = ((x * lax.rsqrt(ms + EPS)) * n1_ref[...]).astype(BF16)

    def proj(c0, width):
        return _dot(h, w_ref[:, c0:c0 + width])

    lane_row = lax.broadcasted_iota(jnp.int32, (1, LANES), 1)
    first_swa = (lane_row // HALF) % 2 == 0
    first_nat = lane_row < HEAD

    def head_norm(z, first, gain):
        sq = z * z
        s_first = jnp.sum(jnp.where(first, sq, 0.0), axis=-1, keepdims=True)
        s_second = jnp.sum(jnp.where(first, 0.0, sq), axis=-1, keepdims=True)
        ss = jnp.where(first, s_first, s_second)
        return (z * lax.rsqrt(ss * (1.0 / HEAD) + EPS)) * gain

    cos = cos_ref[...]
    sin = sin_ref[...]

    def rope(z):
        return z * cos + pltpu.roll(z, HEAD, 1) * sin

    g_aq = vec_ref[0:1, 0:LANES]
    g_ak = vec_ref[1:2, 0:LANES]
    g_bq = vec_ref[2:3, 0:LANES]
    g_bk = vec_ref[3:4, 0:LANES]
    f_bias = vec_ref[4:5, 0:LANES]
    i_bias = vec_ref[5:6, :]
    cf_bias = vec_ref[6:7, :]

    za = proj(SEC_A, 768)
    for j in range(SWA_HEADS // 2):
        q = rope(head_norm(za[:, j * LANES:(j + 1) * LANES], first_swa, g_aq)) * QK_SCALE
        qa_ref[0, :, j * LANES:(j + 1) * LANES] = q.astype(BF16)
    k = rope(head_norm(za[:, SWA_Q:SWA_Q + LANES], first_swa, g_ak))
    kva_ref[0, :, 0:LANES] = k.astype(BF16)
    kva_ref[0, :, LANES:2 * LANES] = za[:, SWA_Q + SWA_KV:].astype(BF16)

    zb = proj(SEC_B, 768)
    for p in range(FOX_HEADS // 2):
        q = head_norm(zb[:, p * LANES:(p + 1) * LANES], first_nat, g_bq) * QK_SCALE
        qb_ref[0, :, p * LANES:(p + 1) * LANES] = q.astype(BF16)
        k = head_norm(zb[:, FOX_W + p * LANES:FOX_W + (p + 1) * LANES], first_nat, g_bk)
        kb_ref[0, :, 2 * p * LANES:(2 * p + 1) * LANES] = k.astype(BF16)
    vb_ref[0] = zb[:, 2 * FOX_W:].astype(BF16)

    lane = lax.broadcasted_iota(jnp.int32, (MXU_DIM, LANES), 1)
    zg = proj(SEC_G, LANES)
    log_f = jnp.where(lane[0:1] < 12, _log_sigmoid(zg + f_bias), 0.0)
    carry = carry_ref[0:1, :]
    for blk in range(tm // MXU_DIM):
        rows = slice(blk * MXU_DIM, (blk + 1) * MXU_DIM)
        cs = _dot(tri_ref[...], _split3_by_lane(log_f[rows], lane))
        f_loc = cs + pltpu.roll(cs, LANES - 4, 1) + pltpu.roll(cs, LANES - 8, 1)
        f_cum = jnp.where(lane < 4, f_loc, 0.0) + carry
        carry = f_cum[MXU_DIM - 1:MXU_DIM, :]
        f_rep = f_cum + pltpu.roll(f_cum, 4, 1) + pltpu.roll(f_cum, 8, 1)
        aug = _split3_by_lane(f_rep, lane)
        kb_ref[0, rows, LANES:2 * LANES] = aug
        kb_ref[0, rows, 3 * LANES:4 * LANES] = aug
    carry_ref[...] = jnp.broadcast_to(carry, carry_ref.shape)

    zc = proj(SEC_C, 768)
    qkvc_ref[0, :, 0:MLSTM_W] = zc[:, 0:MLSTM_W].astype(BF16)
    qkvc_ref[0, :, MLSTM_W:2 * MLSTM_W] = (zc[:, MLSTM_W:2 * MLSTM_W] * QK_SCALE).astype(BF16)
    qkvc_ref[0, :, 2 * MLSTM_W:] = zc[:, 2 * MLSTM_W:].astype(BF16)
    zgc = proj(SEC_CO, 768)
    gc_ref[0, :, 0:MLSTM_W] = zgc[:, MLSTM_W:2 * MLSTM_W] + i_bias
    gc_ref[0, :, MLSTM_W:2 * MLSTM_W] = _log_sigmoid(zgc[:, 2 * MLSTM_W:] + cf_bias)
    gc_ref[0, :, 2 * MLSTM_W:] = zgc[:, 0:MLSTM_W]


def _in_proj(layer, x, n1, w_main, vecs, cos, sin, tri):
    B, S, _ = x.shape
    tm = min(TM_IN, S)
    grid = (B, S // tm)
    tok = lambda w: pl.BlockSpec((1, tm, w), lambda b, i: (b, i, 0))
    out_shapes = (
        jax.ShapeDtypeStruct((B, S, SWA_Q), BF16),
        jax.ShapeDtypeStruct((B, S, 2 * SWA_KV), BF16),
        jax.ShapeDtypeStruct((B, S, FOX_W), BF16),
        jax.ShapeDtypeStruct((B, S, 2 * MXU_DIM), BF16),
        jax.ShapeDtypeStruct((B, S, FOX_W), BF16),
        jax.ShapeDtypeStruct((B, S, 3 * MLSTM_W), BF16),
        jax.ShapeDtypeStruct((B, S, 3 * MLSTM_W), F32),
    )
    return pl.pallas_call(
        _in_proj_body,
        grid=grid,
        in_specs=[
            tok(D_MODEL),
            _layer_spec((1, D_MODEL), layer),
            _layer_spec((D_MODEL, W_MAIN), layer),
            _layer_spec((8, MLSTM_W), layer),
            pl.BlockSpec((tm, LANES), lambda b, i: (i, 0)),
            pl.BlockSpec((tm, LANES), lambda b, i: (i, 0)),
            _const_spec((MXU_DIM, MXU_DIM)),
        ],
        out_specs=[tok(s.shape[-1]) for s in out_shapes],
        out_shape=out_shapes,
        scratch_shapes=[pltpu.VMEM((8, LANES), F32)],
        compiler_params=pltpu.CompilerParams(
            dimension_semantics=("arbitrary", "arbitrary"), vmem_limit_bytes=VMEM_LIMIT),
        name="in_proj",
    )(x, n1, w_main, vecs, cos, sin, tri)


TS_SWA = 512
SWA_DEPTH = 4
VT_ROWS = HEAD + 16


def _swa_body(sink_base, sink_ref, q_ref, kvc_ref, kvp_ref, eye_ref, o_ref):
    step = pl.program_id(1)
    nblk = q_ref.shape[1] // QBLK
    per_group = SWA_HEADS // SWA_KV_HEADS
    eye = eye_ref[0:LANES, 0:LANES]

    key = lax.broadcasted_iota(jnp.int32, (2 * QBLK, MXU_DIM), 0)
    qry = lax.broadcasted_iota(jnp.int32, (2 * QBLK, MXU_DIM), 1) % QBLK
    dchunk = key // CHUNK - qry // CHUNK
    in_band = (dchunk >= 0) & (dchunk <= 2)
    lane = lax.broadcasted_iota(jnp.int32, (QBLK, LANES), 1)
    lane_a = (lane // HALF) % 2 == 0
    first_head = lax.broadcasted_iota(jnp.int32, (1, MXU_DIM), 1) < QBLK
    zero = jnp.zeros((QBLK, LANES), BF16)
    ones = jnp.ones((VT_ROWS - HEAD, 2 * QBLK), BF16)
    kv_all = jnp.concatenate([kvp_ref[0], kvc_ref[0]], axis=0)
    rows = [slice(blk * QBLK, (blk + 1) * QBLK) for blk in range(nblk)]
    bands = [kv_all[blk * QBLK:(blk + 2) * QBLK] for blk in range(nblk)]

    vt_aug = []
    for blk in range(nblk):
        v_t = _dot_nt(eye, bands[blk][:, LANES:]).astype(BF16)
        vt_aug.append([jnp.concatenate([v_t[g * HEAD:(g + 1) * HEAD], ones], axis=0)
                       for g in range(SWA_KV_HEADS)])

    units = [(blk, g, part) for blk in range(nblk) for g in range(SWA_KV_HEADS) for part in range(2)]

    def score(blk, g, part):
        keep = lane_a if g == 0 else jnp.logical_not(lane_a)
        q_cols = [q_ref[0, rows[blk], j * LANES:(j + 1) * LANES] for j in (2 * part, 2 * part + 1)]
        q_g = jnp.concatenate([jnp.where(keep, q2, zero) for q2 in q_cols], axis=0)
        return _dot_nt(bands[blk][:, 0:LANES], q_g)

    def fold(s, blk, g, part):
        allowed = in_band if blk > 0 else in_band & ((key >= QBLK) | (step > 0))
        s = jnp.where(allowed, s, -jnp.inf)
        head = sink_base + g * per_group + 2 * part
        sink = jnp.where(first_head, sink_ref[head], sink_ref[head + 1])
        m = jnp.maximum(jnp.max(s, axis=0, keepdims=True), sink)
        p = jnp.exp(s - m)
        o_aug = _dot(vt_aug[blk][g], p.astype(BF16))
        den = o_aug[HEAD:HEAD + 1] + jnp.exp(sink - m)
        return (o_aug[0:HEAD] / den).astype(BF16)

    def write_block(blk, outs):
        for j in range(SWA_HEADS // 2):
            cols = slice((j % 2) * QBLK, (j % 2 + 1) * QBLK)
            pair_t = jnp.concatenate([outs[(blk, 0, j // 2)][:, cols],
                                      outs[(blk, 1, j // 2)][:, cols]], axis=0)
            o_ref[0, rows[blk], j * LANES:(j + 1) * LANES] = _dot_nt(eye, pair_t).astype(BF16)

    outs = {}
    queue = [score(*unit) for unit in units[:SWA_DEPTH]]
    for n, unit in enumerate(units):
        s_cur = queue.pop(0)
        if n + SWA_DEPTH < len(units):
            queue.append(score(*units[n + SWA_DEPTH]))
        outs[unit] = fold(s_cur, *unit)
        blk = unit[0]
        if unit[1:] == (SWA_KV_HEADS - 1, 1) and blk > 0:
            write_block(blk - 1, outs)
    write_block(nblk - 1, outs)


def _swa(layer, sinks, qa, kva, eye):
    B, S, _ = qa.shape
    ts = min(TS_SWA, S)
    per_step = ts // QBLK
    return pl.pallas_call(
        functools.partial(_swa_body, layer * SWA_HEADS),
        grid=(B, S // ts),
        in_specs=[
            pl.BlockSpec(memory_space=pltpu.SMEM),
            pl.BlockSpec((1, ts, SWA_Q), lambda b, i: (b, i, 0)),
            pl.BlockSpec((1, ts, 2 * SWA_KV), lambda b, i: (b, i, 0)),
            pl.BlockSpec((1, QBLK, 2 * SWA_KV), lambda b, i: (b, jnp.maximum(i * per_step - 1, 0), 0)),
            _const_spec((MXU_DIM, MXU_DIM)),
        ],
        out_specs=pl.BlockSpec((1, ts, SWA_Q), lambda b, i: (b, i, 0)),
        out_shape=jax.ShapeDtypeStruct((B, S, SWA_Q), BF16),
        compiler_params=pltpu.CompilerParams(
            dimension_semantics=("arbitrary", "arbitrary"), vmem_limit_bytes=VMEM_LIMIT),
        name="swa",
    )(sinks, qa, kva, kva, eye)


TQ_FOX = 512
TK_FOX = 256
NEG_BIG = -1e30


FOX_UNROLL = 4
FOX_DEPTH = 4


def _fox_body(q_ref, k_ref, v_ref, eye_ref, o_ref, vt_ref, qt_ref, acc_ref):
    tq = q_ref.shape[1]
    tk = TK_FOX
    S = k_ref.shape[1]
    pair = pl.program_id(1)
    qi = pl.program_id(2)
    eye = eye_ref[...]
    n_tiles = 2 * tq // MXU_DIM
    tiles_per_head = tq // MXU_DIM
    half = tk // 2

    @pl.when(qi == 0)
    def _():
        ones = jnp.ones((VT_ROWS - HEAD, tk), BF16)
        for j in range(S // tk):
            blk = slice(j * tk, (j + 1) * tk)
            v_t = _dot_nt(eye[0:LANES, 0:LANES], v_ref[0, blk, :]).astype(BF16)
            vt_ref[:, blk] = jnp.concatenate([v_t[0:HEAD], ones, v_t[HEAD:], ones], axis=0)

    lane = lax.broadcasted_iota(jnp.int32, (tq, LANES), 1)
    lane_a = lane < HEAD
    q2 = q_ref[0]
    zero = jnp.zeros_like(q2)

    def minus_one_at(head):
        hit = (lane == head) | (lane == head + 4) | (lane == head + 8)
        return jnp.where(hit, -1.0, 0.0).astype(BF16)

    q_aug = [jnp.concatenate([jnp.where(lane_a, q2, zero), minus_one_at(2 * pair)], axis=1),
             jnp.concatenate([jnp.where(lane_a, zero, q2), minus_one_at(2 * pair + 1)], axis=1)]
    for t in range(n_tiles):
        hd, part = divmod(t, tiles_per_head)
        rows = q_aug[hd][part * MXU_DIM:(part + 1) * MXU_DIM]
        qt_ref[:, t * MXU_DIM:(t + 1) * MXU_DIM] = _dot_nt(eye, rows).astype(BF16)
    acc_ref[...] = jnp.zeros_like(acc_ref)

    key = lax.broadcasted_iota(jnp.int32, (half, MXU_DIM), 0)
    qry = lax.broadcasted_iota(jnp.int32, (half, MXU_DIM), 1)

    def fold(s, t, vt_blk, m, triangular):
        hd = t // tiles_per_head
        vt_h = vt_blk[hd * VT_ROWS:(hd + 1) * VT_ROWS]
        for h in range(2):
            sh = s[h * half:(h + 1) * half]
            if triangular:
                valid = key + h * half <= qry
                sh = jnp.where(valid, sh, NEG_BIG)
            m_new = jnp.maximum(m, jnp.max(sh, axis=0, keepdims=True))
            alpha = jnp.exp(m - m_new)
            p = jnp.exp(sh - m_new)
            if triangular:
                p = jnp.where(valid, p, 0.0)
            acc_ref[t] = alpha * acc_ref[t] + _dot(vt_h[:, h * half:(h + 1) * half], p.astype(BF16))
            m = m_new
        return m

    def region(j0, ms, plans):
        k_blks, vt_blks, todo = [], [], []
        for b, plan in enumerate(plans):
            r0 = pl.multiple_of((j0 + b) * tk, tk)
            k_blks.append(k_ref[0, pl.ds(r0, tk), :])
            vt_blks.append(vt_ref[:, pl.ds(r0, tk)])
            todo += [(b, t) for t in range(n_tiles) if plan[t] is not None]
        score = lambda b, t: _dot(k_blks[b], qt_ref[:, t * MXU_DIM:(t + 1) * MXU_DIM])
        ms = list(ms)
        queue = [score(*unit) for unit in todo[:FOX_DEPTH]]
        for n, (b, t) in enumerate(todo):
            s_cur = queue.pop(0)
            if n + FOX_DEPTH < len(todo):
                queue.append(score(*todo[n + FOX_DEPTH]))
            ms[t] = fold(s_cur, t, vt_blks[b], ms[t], plans[b][t])
        return tuple(ms)

    blocks_per_tile = tq // tk
    full = (False,) * n_tiles
    n_full = blocks_per_tile * qi
    m_init = jnp.full((1, MXU_DIM), NEG_BIG, F32)
    ms = lax.fori_loop(0, n_full // FOX_UNROLL,
                       lambda g, c: region(g * FOX_UNROLL, c, (full,) * FOX_UNROLL), (m_init,) * n_tiles)
    done = (n_full // FOX_UNROLL) * FOX_UNROLL
    ms = lax.fori_loop(0, (n_full - done) // blocks_per_tile,
                       lambda g, c: region(done + g * blocks_per_tile, c, (full,) * blocks_per_tile), ms)
    diag = [tuple(None if t % tiles_per_head < d else t % tiles_per_head == d for t in range(n_tiles))
            for d in range(blocks_per_tile)]
    ms = region(n_full, ms, diag)

    for hd in range(2):
        cols = []
        for part in range(tiles_per_head):
            acc = acc_ref[hd * tiles_per_head + part]
            cols.append(acc[0:HEAD] / acc[HEAD:HEAD + 1])
        o_h = jnp.concatenate(cols, axis=1)
        if hd == 0:
            o_t = o_h
        else:
            o_t = jnp.concatenate([o_t, o_h], axis=0).astype(BF16)
    for c in range(tq // MXU_DIM):
        rows = slice(c * MXU_DIM, (c + 1) * MXU_DIM)
        o_ref[0, rows, :] = _dot_nt(eye, o_t[:, rows]).astype(BF16)


def _fox(qb, kb, vb, eye):
    B, S, _ = qb.shape
    tq = min(TQ_FOX, S)
    return pl.pallas_call(
        _fox_body,
        grid=(B, FOX_HEADS // 2, S // tq),
        in_specs=[
            pl.BlockSpec((1, tq, LANES), lambda b, p, i: (b, i, p)),
            pl.BlockSpec((1, S, MXU_DIM), lambda b, p, i: (b, 0, p)),
            pl.BlockSpec((1, S, LANES), lambda b, p, i: (b, 0, p)),
            _const_spec((MXU_DIM, MXU_DIM)),
        ],
        out_specs=pl.BlockSpec((1, tq, LANES), lambda b, p, i: (b, i, p)),
        out_shape=jax.ShapeDtypeStruct((B, S, FOX_W), BF16),
        scratch_shapes=[
            pltpu.VMEM((2 * VT_ROWS, S), BF16),
            pltpu.VMEM((MXU_DIM, 2 * tq), BF16),
            pltpu.VMEM((2 * tq // MXU_DIM, VT_ROWS, MXU_DIM), F32),
        ],
        compiler_params=pltpu.CompilerParams(
            dimension_semantics=("arbitrary", "arbitrary", "arbitrary"),
            vmem_limit_bytes=VMEM_LIMIT),
        name="fox",
    )(qb, kb, vb, eye)


TS_MLSTM = 512


def _mlstm_body(qkv_ref, g_ref, onorm_ref, tri_ref, i4_ref, bd_ref, o_ref, c_ref, n_ref, m_ref):
    ts = qkv_ref.shape[1]
    W = MLSTM_W

    @pl.when(pl.program_id(1) == 0)
    def _():
        c_ref[...] = jnp.zeros_like(c_ref)
        n_ref[...] = jnp.zeros_like(n_ref)
        m_ref[...] = jnp.zeros_like(m_ref)

    row = lax.broadcasted_iota(jnp.int32, (CHUNK, W), 0)
    lane = lax.broadcasted_iota(jnp.int32, (CHUNK, W), 1)
    causal = (lane % HEAD) <= row
    lane_head = lane // HEAD
    bd = bd_ref[...]
    bd2 = jnp.concatenate([bd, bd], axis=1)
    i4_bf = i4_ref[...].astype(BF16)

    def block_diag(x):
        tiled = jnp.concatenate([x] * MLSTM_HEADS, axis=0)
        return tiled * (bd if x.shape[1] == W else bd2)

    def seg_max(d):
        out = jnp.zeros_like(d)
        for hd in range(MLSTM_HEADS):
            sel = lane_head == hd
            mx = jnp.max(jnp.where(sel, d, -jnp.inf), axis=-1, keepdims=True)
            out = jnp.where(sel, mx, out)
        return out

    chunks = range(ts // CHUNK)
    rows = [pl.ds(c * CHUNK, CHUNK) for c in chunks]

    b, g, u, dm, dm_max, a_max, wa = [], [], [], [], [], [], []
    for c in chunks:
        ig = g_ref[0, rows[c], 0:W]
        lf_hi, lf_lo = _split2(g_ref[0, rows[c], W:2 * W])
        b_c = _dot(tri_ref[...], lf_hi) + _dot(tri_ref[...], lf_lo)
        g_c = b_c[CHUNK - 1:CHUNK, :]
        u_c = b_c - ig
        u_row = jnp.sum(u_c * i4_ref[...], axis=0, keepdims=True)
        dm_c = jnp.where(causal, b_c - u_row, -jnp.inf)
        a_c = g_c - u_c
        am_c = jnp.max(a_c, axis=0, keepdims=True)
        b.append(b_c); g.append(g_c); u.append(u_c); dm.append(dm_c)
        dm_max.append(seg_max(dm_c)); a_max.append(am_c); wa.append(jnp.exp(a_c - am_c))

    m_prev, s_old, s_loc = [m_ref[0:1, :]], [], []
    for c in chunks:
        gm = g[c] + m_prev[c]
        m_new = jnp.maximum(gm, a_max[c])
        s_old.append(jnp.exp(gm - m_new))
        s_loc.append(jnp.exp(a_max[c] - m_new))
        m_prev.append(m_new)

    m_t, w_inter, sm_hi, sm_lo, k_t = [], [], [], [], []
    for c in chunks:
        inter = b[c] + m_prev[c]
        mt_c = jnp.maximum(inter, dm_max[c])
        decay = jnp.exp(dm[c] - mt_c)
        q = qkv_ref[0, rows[c], 0:W]
        k = qkv_ref[0, rows[c], W:2 * W]
        qk = _dot_nt(jnp.concatenate([q, i4_bf], axis=0), block_diag(k))
        hi, lo = _split2(qk[0:CHUNK] * decay)
        m_t.append(mt_c); w_inter.append(jnp.exp(inter - mt_c)); sm_hi.append(hi); sm_lo.append(lo)
        k_t.append(qk[CHUNK:].astype(BF16))

    loc = []
    for c in chunks:
        v = qkv_ref[0, rows[c], 2 * W:]
        weighted = jnp.concatenate([(wa[c] * v.astype(F32)).astype(BF16), wa[c].astype(BF16)], axis=1)
        loc.append(_dot(k_t[c], block_diag(weighted)))

    c_prev, n_prev = [c_ref[...]], [n_ref[...]]
    for c in chunks:
        c_prev.append(s_old[c] * c_prev[c] + s_loc[c] * loc[c][:, 0:W])
        n_prev.append(s_old[c] * n_prev[c] + s_loc[c] * loc[c][:, W:])
    c_ref[...] = c_prev[-1]
    n_ref[...] = n_prev[-1]
    m_ref[...] = jnp.broadcast_to(m_prev[-1], m_ref.shape)

    hval = []
    for c in chunks:
        q = qkv_ref[0, rows[c], 0:W]
        v = qkv_ref[0, rows[c], 2 * W:]
        state = block_diag(jnp.concatenate([c_prev[c].astype(BF16), n_prev[c].astype(BF16)], axis=1))
        inter_cn = _dot(q, state)
        intra = _dot(sm_hi[c], jnp.concatenate([block_diag(v), bd], axis=1))
        num = w_inter[c] * inter_cn[:, 0:W] + intra[:, 0:W]
        den = w_inter[c] * inter_cn[:, W:] + intra[:, W:] + _dot(sm_lo[c], bd)
        hval.append(num / jnp.maximum(jnp.abs(den), jnp.exp(-m_t[c])))

    for c in chunks:
        hs_hi, hs_lo = _split2(hval[c] * hval[c])
        ss = _dot(hs_hi, bd) + _dot(hs_lo, bd)
        hc = (hval[c] * lax.rsqrt(ss * (1.0 / HEAD) + EPS)) * onorm_ref[...]
        co = g_ref[0, rows[c], 2 * W:]
        o_ref[0, rows[c], :] = (jax.nn.sigmoid(co) * hc).astype(BF16)


def _mlstm(layer, qkvc, gc, onorm, tri, i4, bd):
    B, S, _ = qkvc.shape
    ts = min(TS_MLSTM, S)
    return pl.pallas_call(
        _mlstm_body,
        grid=(B, S // ts),
        in_specs=[
            pl.BlockSpec((1, ts, 3 * MLSTM_W), lambda b, i: (b, i, 0)),
            pl.BlockSpec((1, ts, 3 * MLSTM_W), lambda b, i: (b, i, 0)),
            _layer_spec((1, MLSTM_W), layer),
            _const_spec((CHUNK, CHUNK)),
            _const_spec((CHUNK, MLSTM_W)),
            _const_spec((MLSTM_W, MLSTM_W)),
        ],
        out_specs=pl.BlockSpec((1, ts, MLSTM_W), lambda b, i: (b, i, 0)),
        out_shape=jax.ShapeDtypeStruct((B, S, MLSTM_W), BF16),
        scratch_shapes=[pltpu.VMEM((HEAD, MLSTM_W), F32), pltpu.VMEM((HEAD, MLSTM_W), F32),
                        pltpu.VMEM((8, MLSTM_W), F32)],
        compiler_params=pltpu.CompilerParams(
            dimension_semantics=("arbitrary", "arbitrary"), vmem_limit_bytes=VMEM_LIMIT),
        name="mlstm",
    )(qkvc, gc, onorm, tri, i4, bd)


TM_FFN = 512
FF_CHUNK = 1024


def _out_ffn_body(x_ref, ya_ref, yb_ref, yc_ref, wo_ref, n2_ref, w1_ref, w2_ref, o_ref, a_ref):
    x1 = (x_ref[...] + _dot(ya_ref[...], wo_ref[0:SWA_Q, :])
          + _dot(yb_ref[...], wo_ref[SWA_Q:SWA_Q + FOX_W, :])
          + _dot(yc_ref[...], wo_ref[SWA_Q + FOX_W:, :]))
    ms = jnp.mean(x1 * x1, axis=-1, keepdims=True)
    h2 = ((x1 * lax.rsqrt(ms + EPS)) * n2_ref[...]).astype(BF16)
    for j in range(D_FF // FF_CHUNK):
        cols = slice(j * FF_CHUNK, (j + 1) * FF_CHUNK)
        r = jnp.maximum(_dot(h2, w1_ref[:, cols]), 0.0)
        a_ref[:, cols] = (r * r).astype(BF16)
    o_ref[...] = x1 + _dot(a_ref[...], w2_ref[...])


def _out_ffn(layer, x2d, ya, yb, yc, wo, n2, w1, w2):
    T = x2d.shape[0]
    tm = min(TM_FFN, T)
    tok = lambda w: pl.BlockSpec((tm, w), lambda i: (i, 0))
    return pl.pallas_call(
        _out_ffn_body,
        grid=(T // tm,),
        in_specs=[
            tok(D_MODEL), tok(SWA_Q), tok(FOX_W), tok(MLSTM_W),
            _layer_spec((D_MODEL, D_MODEL), layer),
            _layer_spec((1, D_MODEL), layer),
            _layer_spec((D_MODEL, D_FF), layer),
            _layer_spec((D_FF, D_MODEL), layer),
        ],
        out_specs=tok(D_MODEL),
        out_shape=jax.ShapeDtypeStruct((T, D_MODEL), F32),
        scratch_shapes=[pltpu.VMEM((tm, D_FF), BF16)],
        compiler_params=pltpu.CompilerParams(
            dimension_semantics=("arbitrary",), vmem_limit_bytes=VMEM_LIMIT),
        name="out_ffn",
    )(x2d, ya, yb, yc, wo, n2, w1, w2)


def _rope_tables(S):
    inv = ROPE_THETA ** (-jnp.arange(HALF, dtype=F32) / HALF)
    ang = jnp.arange(S).astype(F32)[:, None] * inv[None, :]
    cos = jnp.tile(jnp.cos(ang), (1, LANES // HALF))
    sin = jnp.tile(jnp.sin(ang), (1, LANES // HALF))
    sign = jnp.where(jnp.arange(LANES) < HEAD, -1.0, 1.0).astype(F32)
    return cos, sin * sign[None, :]


def _prep_w_in(w_in):
    L, D, _ = w_in.shape
    sizes = (SWA_Q, SWA_KV, SWA_KV, 3 * FOX_W, FOX_HEADS, 3 * MLSTM_W, MLSTM_HEADS, MLSTM_HEADS,
             MLSTM_W)
    aq, ak, av, b_qkv, bf, c_qkv, ci, cf, co = jnp.split(w_in, np.cumsum(sizes)[:-1].tolist(), axis=-1)
    aq = aq.reshape(L, D, 2, SWA_HEADS // 2, 2, HALF).transpose(0, 1, 3, 4, 2, 5).reshape(L, D, SWA_Q)
    ak = ak.reshape(L, D, SWA_KV_HEADS, 2, HALF).transpose(0, 1, 3, 2, 4).reshape(L, D, SWA_KV)
    gate_pad = jnp.zeros((L, D, LANES - 3 * FOX_HEADS), w_in.dtype)
    return jnp.concatenate(
        [aq, ak, av, b_qkv, c_qkv, co, jnp.repeat(ci, HEAD, axis=-1), jnp.repeat(cf, HEAD, axis=-1),
         bf, bf, bf, gate_pad], axis=-1).astype(BF16)


def _prep_w_out(w_out):
    L = w_out.shape[0]
    ya_rows = w_out[:, :SWA_Q].reshape(L, 2, SWA_HEADS // 2, HEAD, D_MODEL)
    ya_rows = ya_rows.transpose(0, 2, 1, 3, 4).reshape(L, SWA_Q, D_MODEL)
    return jnp.concatenate([ya_rows, w_out[:, SWA_Q:]], axis=1).astype(BF16)


def _prep_vecs(swa_q_norm, swa_k_norm, fox_q_norm, fox_k_norm, fox_f_bias, mlstm_i_bias, mlstm_f_bias):
    L = swa_q_norm.shape[0]

    def swa_lanes(g):
        return jnp.concatenate([g[:, :HALF], g[:, :HALF], g[:, HALF:], g[:, HALF:]], axis=-1)

    pad = lambda v: jnp.pad(v, ((0, 0), (0, MLSTM_W - v.shape[1])))
    return jnp.stack([
        pad(swa_lanes(swa_q_norm)), pad(swa_lanes(swa_k_norm)),
        pad(jnp.tile(fox_q_norm, (1, 2))), pad(jnp.tile(fox_k_norm, (1, 2))),
        pad(jnp.tile(fox_f_bias, (1, 3))),
        jnp.repeat(mlstm_i_bias, HEAD, axis=-1), jnp.repeat(mlstm_f_bias, HEAD, axis=-1),
        jnp.zeros((L, MLSTM_W), F32),
    ], axis=1).astype(F32)


def kernel(x, norm1, w_in, swa_q_norm, swa_k_norm, swa_sinks, fox_q_norm, fox_k_norm, fox_f_bias,
           mlstm_i_bias, mlstm_f_bias, mlstm_out_norm, w_out, norm2, w_ff1, w_ff2):
    B, S, D = x.shape
    L = norm1.shape[0]
    T = B * S
    cos, sin = _rope_tables(S)
    tri256, tri64, bd256 = (jnp.asarray(c, BF16) for c in (_TRI256, _TRI64, _BD256))
    i4 = jnp.asarray(_I4, F32)
    eye256 = jnp.eye(MXU_DIM, dtype=BF16)

    w_main = _prep_w_in(w_in)
    wo = _prep_w_out(w_out)
    w1 = w_ff1.astype(BF16)
    w2 = w_ff2.astype(BF16)
    vecs = _prep_vecs(swa_q_norm, swa_k_norm, fox_q_norm, fox_k_norm, fox_f_bias, mlstm_i_bias,
                      mlstm_f_bias)
    n1 = norm1.reshape(L, 1, D).astype(F32)
    n2 = norm2.reshape(L, 1, D).astype(F32)
    onorm = mlstm_out_norm.reshape(L, 1, MLSTM_W).astype(F32)
    sinks = swa_sinks.reshape(L * SWA_HEADS).astype(F32)

    for l in range(L):
        qa, kva, qb, kb, vb, qkvc, gc = _in_proj(l, x, n1, w_main, vecs, cos, sin, tri256)
        ya = _swa(l, sinks, qa, kva, eye256)
        yb = _fox(qb, kb, vb, eye256)
        yc = _mlstm(l, qkvc, gc, onorm, tri64, i4, bd256)
        x = _out_ffn(l, x.reshape(T, D), ya.reshape(T, SWA_Q), yb.reshape(T, FOX_W),
                     yc.reshape(T, MLSTM_W), wo, n2, w1, w2).reshape(B, S, D)
    return x
```

```python
import functools
import math

import numpy as np
import jax
import jax.numpy as jnp
from jax import lax
from jax.experimental import pallas as pl
from jax.experimental.pallas import tpu as pltpu

F32 = jnp.float32
BF16 = jnp.bfloat16

D_MODEL = 1024
HEAD = 64
HALF = HEAD // 2
CHUNK = 64
SWA_HEADS = 8
SWA_KV_HEADS = 2
FOX_HEADS = 4
MLSTM_HEADS = 4
SWA_Q = SWA_HEADS * HEAD
SWA_KV = SWA_KV_HEADS * HEAD
FOX_W = FOX_HEADS * HEAD
MLSTM_W = MLSTM_HEADS * HEAD
D_FF = 4 * D_MODEL
QBLK = 128
ROPE_THETA = 10000.0
EPS = 1e-6
QK_SCALE = 1.0 / math.sqrt(HEAD)
LOG2E = math.log2(math.e)

LANES = 128
MXU_DIM = 256
VMEM_LIMIT = 56 * 1024 * 1024

SEC_A = 0
SEC_B = SEC_A + 768
SEC_C = SEC_B + 768
SEC_CO = SEC_C + 768
SEC_CI = SEC_CO + 256
SEC_CF = SEC_CI + 256
SEC_G = SEC_CF + 256
W_MAIN = SEC_G + LANES


def _block_diag_ones(head_of_lane):
    return (head_of_lane[:, None] == head_of_lane[None, :]).astype(np.float32)


_TRI256 = np.tril(np.ones((MXU_DIM, MXU_DIM), np.float32))
_TRI64 = np.tril(np.ones((CHUNK, CHUNK), np.float32))
_w256 = np.arange(MLSTM_W)
_BD256 = _block_diag_ones(_w256 // HEAD)
_I4 = (np.arange(CHUNK)[:, None] == (_w256 % HEAD)[None, :]).astype(np.float32)


def _const_spec(shape):
    nd = len(shape)
    return pl.BlockSpec(shape, lambda *_: (0,) * nd, pipeline_mode=pl.Buffered(1))


def _layer_spec(shape, layer):
    nd = len(shape)
    return pl.BlockSpec((None,) + tuple(shape), lambda *_: (layer,) + (0,) * nd,
                        pipeline_mode=pl.Buffered(1))


def _log_sigmoid(x):
    return jnp.minimum(x, 0.0) - jnp.log(1.0 + jnp.exp(-jnp.abs(x)))


def _split2(x):
    hi = x.astype(BF16)
    lo = (x - hi.astype(F32)).astype(BF16)
    return hi, lo


def _split3_by_lane(x, lane):
    hi = x.astype(BF16).astype(F32)
    r1 = x - hi
    mid = r1.astype(BF16).astype(F32)
    low = r1 - mid
    return jnp.where(lane < 4, hi, jnp.where(lane < 8, mid, low)).astype(BF16)


def _dot(a, b):
    return jnp.dot(a, b, preferred_element_type=F32)


def _dot_nt(a, b):
    return lax.dot_general(a, b, (((1,), (1,)), ((), ())), preferred_element_type=F32)


def _dot_tn(a, b):
    return lax.dot_general(a, b, (((0,), (0,)), ((), ())), preferred_element_type=F32)


TM_IN = 512


def _in_proj_body(x_ref, n1_ref, w_ref, vec_ref, cos_ref, sin_ref, tri_ref,
                  qa_ref, kva_ref, qb_ref, kb_ref, vb_ref, qkvc_ref, gc_ref, carry_ref):
    tm = x_ref.shape[1]

    @pl.when(pl.program_id(1) == 0)
    def _():
        carry_ref[...] = jnp.zeros_like(carry_ref)

    x = x_ref[0]
    ms = jnp.mean(x * x, axis=-1, keepdims=True)
    h = ((x * lax.rsqrt(ms + EPS)) * n1_ref[...]).astype(BF16)

    def proj(c0, width):
        return _dot(h, w_ref[:, c0:c0 + width])

    lane_row = lax.broadcasted_iota(jnp.int32, (1, LANES), 1)
    first_swa = (lane_row // HALF) % 2 == 0
    first_nat = lane_row < HEAD

    def head_norm(z, first, gain):
        sq = z * z
        s_first = jnp.sum(jnp.where(first, sq, 0.0), axis=-1, keepdims=True)
        s_second = jnp.sum(jnp.where(first, 0.0, sq), axis=-1, keepdims=True)
        ss = jnp.where(first, s_first, s_second)
        return (z * lax.rsqrt(ss * (1.0 / HEAD) + EPS)) * gain

    cos = cos_ref[...]
    sin = sin_ref[...]

    def rope(z):
        return z * cos + pltpu.roll(z, HEAD, 1) * sin

    g_aq = vec_ref[0:1, 0:LANES]
    g_ak = vec_ref[1:2, 0:LANES]
    g_bq = vec_ref[2:3, 0:LANES]
    g_bk = vec_ref[3:4, 0:LANES]
    f_bias = vec_ref[4:5, 0:LANES]
    i_bias = vec_ref[5:6, :]
    cf_bias = vec_ref[6:7, :]

    za = proj(SEC_A, 768)
    for j in range(SWA_HEADS // 2):
        q = rope(head_norm(za[:, j * LANES:(j + 1) * LANES], first_swa, g_aq)) * (QK_SCALE * LOG2E)
        qa_ref[0, :, j * LANES:(j + 1) * LANES] = q.astype(BF16)
    k = rope(head_norm(za[:, SWA_Q:SWA_Q + LANES], first_swa, g_ak))
    kva_ref[0, :, 0:LANES] = k.astype(BF16)
    kva_ref[0, :, LANES:2 * LANES] = za[:, SWA_Q + SWA_KV:].astype(BF16)

    zb = proj(SEC_B, 768)
    for p in range(FOX_HEADS // 2):
        q = head_norm(zb[:, p * LANES:(p + 1) * LANES], first_nat, g_bq) * (QK_SCALE * LOG2E)
        qb_ref[0, :, p * LANES:(p + 1) * LANES] = q.astype(BF16)
        k = head_norm(zb[:, FOX_W + p * LANES:FOX_W + (p + 1) * LANES], first_nat, g_bk)
        kb_ref[0, :, 2 * p * LANES:(2 * p + 1) * LANES] = k.astype(BF16)
    vb_ref[0] = zb[:, 2 * FOX_W:].astype(BF16)

    lane = lax.broadcasted_iota(jnp.int32, (MXU_DIM, LANES), 1)
    zg = proj(SEC_G, LANES)
    log_f = jnp.where(lane[0:1] < 12, _log_sigmoid(zg + f_bias), 0.0)
    carry = carry_ref[0:1, :]
    for blk in range(tm // MXU_DIM):
        rows = slice(blk * MXU_DIM, (blk + 1) * MXU_DIM)
        cs = _dot(tri_ref[...], _split3_by_lane(log_f[rows], lane))
        f_loc = cs + pltpu.roll(cs, LANES - 4, 1) + pltpu.roll(cs, LANES - 8, 1)
        f_cum = jnp.where(lane < 4, f_loc, 0.0) + carry
        carry = f_cum[MXU_DIM - 1:MXU_DIM, :]
        f_rep = (f_cum + pltpu.roll(f_cum, 4, 1) + pltpu.roll(f_cum, 8, 1)) * LOG2E
        aug = _split3_by_lane(f_rep, lane)
        kb_ref[0, rows, LANES:2 * LANES] = aug
        kb_ref[0, rows, 3 * LANES:4 * LANES] = aug
    carry_ref[...] = jnp.broadcast_to(carry, carry_ref.shape)

    zc = proj(SEC_C, 768)
    qkvc_ref[0, :, 0:MLSTM_W] = zc[:, 0:MLSTM_W].astype(BF16)
    qkvc_ref[0, :, MLSTM_W:2 * MLSTM_W] = (zc[:, MLSTM_W:2 * MLSTM_W] * QK_SCALE).astype(BF16)
    qkvc_ref[0, :, 2 * MLSTM_W:] = zc[:, 2 * MLSTM_W:].astype(BF16)
    zgc = proj(SEC_CO, 768)
    gc_ref[0, :, 0:MLSTM_W] = zgc[:, MLSTM_W:2 * MLSTM_W] + i_bias
    gc_ref[0, :, MLSTM_W:2 * MLSTM_W] = _log_sigmoid(zgc[:, 2 * MLSTM_W:] + cf_bias)
    gc_ref[0, :, 2 * MLSTM_W:] = zgc[:, 0:MLSTM_W]


def _in_proj(layer, x, n1, w_main, vecs, cos, sin, tri):
    B, S, _ = x.shape
    tm = min(TM_IN, S)
    grid = (B, S // tm)
    tok = lambda w: pl.BlockSpec((1, tm, w), lambda b, i: (b, i, 0))
    out_shapes = (
        jax.ShapeDtypeStruct((B, S, SWA_Q), BF16),
        jax.ShapeDtypeStruct((B, S, 2 * SWA_KV), BF16),
        jax.ShapeDtypeStruct((B, S, FOX_W), BF16),
        jax.ShapeDtypeStruct((B, S, 2 * MXU_DIM), BF16),
        jax.ShapeDtypeStruct((B, S, FOX_W), BF16),
        jax.ShapeDtypeStruct((B, S, 3 * MLSTM_W), BF16),
        jax.ShapeDtypeStruct((B, S, 3 * MLSTM_W), F32),
    )
    return pl.pallas_call(
        _in_proj_body,
        grid=grid,
        in_specs=[
            tok(D_MODEL),
            _layer_spec((1, D_MODEL), layer),
            _layer_spec((D_MODEL, W_MAIN), layer),
            _layer_spec((8, MLSTM_W), layer),
            pl.BlockSpec((tm, LANES), lambda b, i: (i, 0)),
            pl.BlockSpec((tm, LANES), lambda b, i: (i, 0)),
            _const_spec((MXU_DIM, MXU_DIM)),
        ],
        out_specs=[tok(s.shape[-1]) for s in out_shapes],
        out_shape=out_shapes,
        scratch_shapes=[pltpu.VMEM((8, LANES), F32)],
        compiler_params=pltpu.CompilerParams(
            dimension_semantics=("arbitrary", "arbitrary"), vmem_limit_bytes=VMEM_LIMIT),
        name="in_proj",
    )(x, n1, w_main, vecs, cos, sin, tri)


TS_SWA = 512
SWA_DEPTH = 4
VT_ROWS = HEAD + 16


def _swa_body(sink_base, sink_ref, q_ref, kvc_ref, kvp_ref, eye_ref, o_ref):
    step = pl.program_id(1)
    nblk = q_ref.shape[1] // QBLK
    per_group = SWA_HEADS // SWA_KV_HEADS
    eye = eye_ref[0:LANES, 0:LANES]

    key = lax.broadcasted_iota(jnp.int32, (2 * QBLK, MXU_DIM), 0)
    qry = lax.broadcasted_iota(jnp.int32, (2 * QBLK, MXU_DIM), 1) % QBLK
    dchunk = key // CHUNK - qry // CHUNK
    in_band = (dchunk >= 0) & (dchunk <= 2)
    lane = lax.broadcasted_iota(jnp.int32, (QBLK, LANES), 1)
    lane_a = (lane // HALF) % 2 == 0
    first_head = lax.broadcasted_iota(jnp.int32, (1, MXU_DIM), 1) < QBLK
    zero = jnp.zeros((QBLK, LANES), BF16)
    ones = jnp.ones((VT_ROWS - HEAD, 2 * QBLK), BF16)
    kv_all = jnp.concatenate([kvp_ref[0], kvc_ref[0]], axis=0)
    rows = [slice(blk * QBLK, (blk + 1) * QBLK) for blk in range(nblk)]
    bands = [kv_all[blk * QBLK:(blk + 2) * QBLK] for blk in range(nblk)]

    vt_aug = []
    for blk in range(nblk):
        v_t = _dot_nt(eye, bands[blk][:, LANES:]).astype(BF16)
        vt_aug.append([jnp.concatenate([v_t[g * HEAD:(g + 1) * HEAD], ones], axis=0)
                       for g in range(SWA_KV_HEADS)])

    units = [(blk, g, part) for blk in range(nblk) for g in range(SWA_KV_HEADS) for part in range(2)]

    def score(blk, g, part):
        keep = lane_a if g == 0 else jnp.logical_not(lane_a)
        q_cols = [q_ref[0, rows[blk], j * LANES:(j + 1) * LANES] for j in (2 * part, 2 * part + 1)]
        q_g = jnp.concatenate([jnp.where(keep, q2, zero) for q2 in q_cols], axis=0)
        return _dot_nt(bands[blk][:, 0:LANES], q_g)

    def fold(s, blk, g, part):
        allowed = in_band if blk > 0 else in_band & ((key >= QBLK) | (step > 0))
        s = jnp.where(allowed, s, -jnp.inf)
        head = sink_base + g * per_group + 2 * part
        sink = jnp.where(first_head, sink_ref[head], sink_ref[head + 1]) * LOG2E
        m = jnp.maximum(jnp.max(s, axis=0, keepdims=True), sink)
        p = jnp.exp2(s - m)
        o_aug = _dot(vt_aug[blk][g], p.astype(BF16))
        den = o_aug[HEAD:HEAD + 1] + jnp.exp2(sink - m)
        return (o_aug[0:HEAD] / den).astype(BF16)

    def write_block(blk, outs):
        for j in range(SWA_HEADS // 2):
            cols = slice((j % 2) * QBLK, (j % 2 + 1) * QBLK)
            pair_t = jnp.concatenate([outs[(blk, 0, j // 2)][:, cols],
                                      outs[(blk, 1, j // 2)][:, cols]], axis=0)
            o_ref[0, rows[blk], j * LANES:(j + 1) * LANES] = _dot_nt(eye, pair_t).astype(BF16)

    outs = {}
    queue = [score(*unit) for unit in units[:SWA_DEPTH]]
    for n, unit in enumerate(units):
        s_cur = queue.pop(0)
        if n + SWA_DEPTH < len(units):
            queue.append(score(*units[n + SWA_DEPTH]))
        outs[unit] = fold(s_cur, *unit)
        blk = unit[0]
        if unit[1:] == (SWA_KV_HEADS - 1, 1) and blk > 0:
            write_block(blk - 1, outs)
    write_block(nblk - 1, outs)


def _swa(layer, sinks, qa, kva, eye):
    B, S, _ = qa.shape
    ts = min(TS_SWA, S)
    per_step = ts // QBLK
    return pl.pallas_call(
        functools.partial(_swa_body, layer * SWA_HEADS),
        grid=(B, S // ts),
        in_specs=[
            pl.BlockSpec(memory_space=pltpu.SMEM),
            pl.BlockSpec((1, ts, SWA_Q), lambda b, i: (b, i, 0)),
            pl.BlockSpec((1, ts, 2 * SWA_KV), lambda b, i: (b, i, 0)),
            pl.BlockSpec((1, QBLK, 2 * SWA_KV), lambda b, i: (b, jnp.maximum(i * per_step - 1, 0), 0)),
            _const_spec((MXU_DIM, MXU_DIM)),
        ],
        out_specs=pl.BlockSpec((1, ts, SWA_Q), lambda b, i: (b, i, 0)),
        out_shape=jax.ShapeDtypeStruct((B, S, SWA_Q), BF16),
        compiler_params=pltpu.CompilerParams(
            dimension_semantics=("arbitrary", "arbitrary"), vmem_limit_bytes=VMEM_LIMIT),
        name="swa",
    )(sinks, qa, kva, kva, eye)


TQ_FOX = 1024
TK_FOX = 256
NEG_BIG = -1e30


FOX_UNROLL = 4
FOX_DEPTH = 4


def _fox_body(q_ref, k_ref, v_ref, eye_ref, o_ref, vt_ref, qt_ref, acc_ref):
    tq = q_ref.shape[1]
    tk = TK_FOX
    S = k_ref.shape[1]
    pair = pl.program_id(1)
    qi = pl.program_id(2)
    eye = eye_ref[...]
    n_tiles = 2 * tq // MXU_DIM
    tiles_per_head = tq // MXU_DIM

    @pl.when(qi == 0)
    def _():
        ones = jnp.ones((VT_ROWS - HEAD, tk), BF16)
        for j in range(S // tk):
            blk = slice(j * tk, (j + 1) * tk)
            v_t = _dot_nt(eye[0:LANES, 0:LANES], v_ref[0, blk, :]).astype(BF16)
            vt_ref[:, blk] = jnp.concatenate([v_t[0:HEAD], ones, v_t[HEAD:], ones], axis=0)

    lane = lax.broadcasted_iota(jnp.int32, (tq, LANES), 1)
    lane_a = lane < HEAD
    q2 = q_ref[0]
    zero = jnp.zeros_like(q2)

    def minus_one_at(head):
        hit = (lane == head) | (lane == head + 4) | (lane == head + 8)
        return jnp.where(hit, -1.0, 0.0).astype(BF16)

    q_aug = [jnp.concatenate([jnp.where(lane_a, q2, zero), minus_one_at(2 * pair)], axis=1),
             jnp.concatenate([jnp.where(lane_a, zero, q2), minus_one_at(2 * pair + 1)], axis=1)]
    for t in range(n_tiles):
        hd, part = divmod(t, tiles_per_head)
        rows = q_aug[hd][part * MXU_DIM:(part + 1) * MXU_DIM]
        qt_ref[:, t * MXU_DIM:(t + 1) * MXU_DIM] = _dot_nt(eye, rows).astype(BF16)
    acc_ref[...] = jnp.zeros_like(acc_ref)

    key = lax.broadcasted_iota(jnp.int32, (tk // 2, MXU_DIM), 0)
    qry = lax.broadcasted_iota(jnp.int32, (tk // 2, MXU_DIM), 1)

    def fold(s, t, vt_blk, m, triangular):
        hd = t // tiles_per_head
        vt_h = vt_blk[hd * VT_ROWS:(hd + 1) * VT_ROWS]
        half = tk // 2
        for h in range(2):
            sh = s[h * half:(h + 1) * half]
            if triangular:
                valid = key + h * half <= qry
                sh = jnp.where(valid, sh, NEG_BIG)
            m_new = jnp.maximum(m, jnp.max(sh, axis=0, keepdims=True))
            alpha = jnp.exp2(m - m_new)
            p = jnp.exp2(sh - m_new)
            if triangular:
                p = jnp.where(valid, p, 0.0)
            acc_ref[t] = alpha * acc_ref[t] + _dot(vt_h[:, h * half:(h + 1) * half], p.astype(BF16))
            m = m_new
        return m

    def region(j0, ms, plans):
        k_blks, vt_blks, todo = [], [], []
        for b, plan in enumerate(plans):
            r0 = pl.multiple_of((j0 + b) * tk, tk)
            k_blks.append(k_ref[0, pl.ds(r0, tk), :])
            vt_blks.append(vt_ref[:, pl.ds(r0, tk)])
            todo += [(b, t) for t in range(n_tiles) if plan[t] is not None]
        score = lambda b, t: _dot(k_blks[b], qt_ref[:, t * MXU_DIM:(t + 1) * MXU_DIM])
        ms = list(ms)
        queue = [score(*unit) for unit in todo[:FOX_DEPTH]]
        for n, (b, t) in enumerate(todo):
            s_cur = queue.pop(0)
            if n + FOX_DEPTH < len(todo):
                queue.append(score(*todo[n + FOX_DEPTH]))
            ms[t] = fold(s_cur, t, vt_blks[b], ms[t], plans[b][t])
        return tuple(ms)

    blocks_per_tile = tq // tk
    full = (False,) * n_tiles
    n_full = blocks_per_tile * qi
    m_init = jnp.full((1, MXU_DIM), NEG_BIG, F32)
    ms = lax.fori_loop(0, n_full // FOX_UNROLL,
                       lambda g, c: region(g * FOX_UNROLL, c, (full,) * FOX_UNROLL), (m_init,) * n_tiles)
    if FOX_UNROLL > blocks_per_tile:
        done = (n_full // FOX_UNROLL) * FOX_UNROLL
        ms = lax.fori_loop(0, (n_full - done) // blocks_per_tile,
                           lambda g, c: region(done + g * blocks_per_tile, c, (full,) * blocks_per_tile), ms)
    diag = [tuple(None if t % tiles_per_head < d else t % tiles_per_head == d for t in range(n_tiles))
            for d in range(blocks_per_tile)]
    ms = region(n_full, ms, diag)

    for hd in range(2):
        cols = []
        for part in range(tiles_per_head):
            acc = acc_ref[hd * tiles_per_head + part]
            cols.append(acc[0:HEAD] / acc[HEAD:HEAD + 1])
        o_h = jnp.concatenate(cols, axis=1)
        if hd == 0:
            o_t = o_h
        else:
            o_t = jnp.concatenate([o_t, o_h], axis=0).astype(BF16)
    for c in range(tq // MXU_DIM):
        rows = slice(c * MXU_DIM, (c + 1) * MXU_DIM)
        o_ref[0, rows, :] = _dot_nt(eye, o_t[:, rows]).astype(BF16)


def _fox(qb, kb, vb, eye):
    B, S, _ = qb.shape
    tq = min(TQ_FOX, S)
    return pl.pallas_call(
        _fox_body,
        grid=(B, FOX_HEADS // 2, S // tq),
        in_specs=[
            pl.BlockSpec((1, tq, LANES), lambda b, p, i: (b, i, p)),
            pl.BlockSpec((1, S, MXU_DIM), lambda b, p, i: (b, 0, p)),
            pl.BlockSpec((1, S, LANES), lambda b, p, i: (b, 0, p)),
            _const_spec((MXU_DIM, MXU_DIM)),
        ],
        out_specs=pl.BlockSpec((1, tq, LANES), lambda b, p, i: (b, i, p)),
        out_shape=jax.ShapeDtypeStruct((B, S, FOX_W), BF16),
        scratch_shapes=[
            pltpu.VMEM((2 * VT_ROWS, S), BF16),
            pltpu.VMEM((MXU_DIM, 2 * tq), BF16),
            pltpu.VMEM((2 * tq // MXU_DIM, VT_ROWS, MXU_DIM), F32),
        ],
        compiler_params=pltpu.CompilerParams(
            dimension_semantics=("arbitrary", "arbitrary", "arbitrary"),
            vmem_limit_bytes=VMEM_LIMIT),
        name="fox",
    )(qb, kb, vb, eye)


TS_MLSTM = 512


def _mlstm_body(qkv_ref, g_ref, onorm_ref, tri_ref, i4_ref, bd_ref, o_ref, c_ref, n_ref, m_ref):
    ts = qkv_ref.shape[1]
    W = MLSTM_W

    @pl.when(pl.program_id(1) == 0)
    def _():
        c_ref[...] = jnp.zeros_like(c_ref)
        n_ref[...] = jnp.zeros_like(n_ref)
        m_ref[...] = jnp.zeros_like(m_ref)

    row = lax.broadcasted_iota(jnp.int32, (CHUNK, W), 0)
    lane = lax.broadcasted_iota(jnp.int32, (CHUNK, W), 1)
    causal = (lane % HEAD) <= row
    lane_head = lane // HEAD
    bd = bd_ref[...]
    bd2 = jnp.concatenate([bd, bd], axis=1)
    i4_bf = i4_ref[...].astype(BF16)

    def block_diag(x):
        tiled = jnp.concatenate([x] * MLSTM_HEADS, axis=0)
        return tiled * (bd if x.shape[1] == W else bd2)

    def seg_max(d):
        out = jnp.zeros_like(d)
        for hd in range(MLSTM_HEADS):
            sel = lane_head == hd
            mx = jnp.max(jnp.where(sel, d, -jnp.inf), axis=-1, keepdims=True)
            out = jnp.where(sel, mx, out)
        return out

    chunks = range(ts // CHUNK)
    rows = [pl.ds(c * CHUNK, CHUNK) for c in chunks]

    b, g, u, dm, dm_max, a_max, wa = [], [], [], [], [], [], []
    for c in chunks:
        ig = g_ref[0, rows[c], 0:W]
        lf_hi, lf_lo = _split2(g_ref[0, rows[c], W:2 * W])
        b_c = _dot(tri_ref[...], lf_hi) + _dot(tri_ref[...], lf_lo)
        g_c = b_c[CHUNK - 1:CHUNK, :]
        u_c = b_c - ig
        u_row = jnp.sum(u_c * i4_ref[...], axis=0, keepdims=True)
        dm_c = jnp.where(causal, b_c - u_row, -jnp.inf)
        a_c = g_c - u_c
        am_c = jnp.max(a_c, axis=0, keepdims=True)
        b.append(b_c); g.append(g_c); u.append(u_c); dm.append(dm_c)
        dm_max.append(seg_max(dm_c)); a_max.append(am_c); wa.append(jnp.exp(a_c - am_c))

    m_prev, s_old, s_loc = [m_ref[0:1, :]], [], []
    for c in chunks:
        gm = g[c] + m_prev[c]
        m_new = jnp.maximum(gm, a_max[c])
        s_old.append(jnp.exp(gm - m_new))
        s_loc.append(jnp.exp(a_max[c] - m_new))
        m_prev.append(m_new)

    m_t, w_inter, sm_hi, sm_lo, k_t = [], [], [], [], []
    for c in chunks:
        inter = b[c] + m_prev[c]
        mt_c = jnp.maximum(inter, dm_max[c])
        decay = jnp.exp(dm[c] - mt_c)
        q = qkv_ref[0, rows[c], 0:W]
        k = qkv_ref[0, rows[c], W:2 * W]
        qk = _dot_nt(jnp.concatenate([q, i4_bf], axis=0), block_diag(k))
        hi, lo = _split2(qk[0:CHUNK] * decay)
        m_t.append(mt_c); w_inter.append(jnp.exp(inter - mt_c)); sm_hi.append(hi); sm_lo.append(lo)
        k_t.append(qk[CHUNK:].astype(BF16))

    loc = []
    for c in chunks:
        v = qkv_ref[0, rows[c], 2 * W:]
        weighted = jnp.concatenate([(wa[c] * v.astype(F32)).astype(BF16), wa[c].astype(BF16)], axis=1)
        loc.append(_dot(k_t[c], block_diag(weighted)))

    c_prev, n_prev = [c_ref[...]], [n_ref[...]]
    for c in chunks:
        c_prev.append(s_old[c] * c_prev[c] + s_loc[c] * loc[c][:, 0:W])
        n_prev.append(s_old[c] * n_prev[c] + s_loc[c] * loc[c][:, W:])
    c_ref[...] = c_prev[-1]
    n_ref[...] = n_prev[-1]
    m_ref[...] = jnp.broadcast_to(m_prev[-1], m_ref.shape)

    hval = []
    for c in chunks:
        q = qkv_ref[0, rows[c], 0:W]
        v = qkv_ref[0, rows[c], 2 * W:]
        state = block_diag(jnp.concatenate([c_prev[c].astype(BF16), n_prev[c].astype(BF16)], axis=1))
        inter_cn = _dot(q, state)
        intra = _dot(sm_hi[c], jnp.concatenate([block_diag(v), bd], axis=1))
        num = w_inter[c] * inter_cn[:, 0:W] + intra[:, 0:W]
        den = w_inter[c] * inter_cn[:, W:] + intra[:, W:] + _dot(sm_lo[c], bd)
        hval.append(num / jnp.maximum(jnp.abs(den), jnp.exp(-m_t[c])))

    for c in chunks:
        hs_hi, hs_lo = _split2(hval[c] * hval[c])
        ss = _dot(hs_hi, bd) + _dot(hs_lo, bd)
        hc = (hval[c] * lax.rsqrt(ss * (1.0 / HEAD) + EPS)) * onorm_ref[...]
        co = g_ref[0, rows[c], 2 * W:]
        o_ref[0, rows[c], :] = (jax.nn.sigmoid(co) * hc).astype(BF16)


def _mlstm(layer, qkvc, gc, onorm, tri, i4, bd):
    B, S, _ = qkvc.shape
    ts = min(TS_MLSTM, S)
    return pl.pallas_call(
        _mlstm_body,
        grid=(B, S // ts),
        in_specs=[
            pl.BlockSpec((1, ts, 3 * MLSTM_W), lambda b, i: (b, i, 0)),
            pl.BlockSpec((1, ts, 3 * MLSTM_W), lambda b, i: (b, i, 0)),
            _layer_spec((1, MLSTM_W), layer),
            _const_spec((CHUNK, CHUNK)),
            _const_spec((CHUNK, MLSTM_W)),
            _const_spec((MLSTM_W, MLSTM_W)),
        ],
        out_specs=pl.BlockSpec((1, ts, MLSTM_W), lambda b, i: (b, i, 0)),
        out_shape=jax.ShapeDtypeStruct((B, S, MLSTM_W), BF16),
        scratch_shapes=[pltpu.VMEM((HEAD, MLSTM_W), F32), pltpu.VMEM((HEAD, MLSTM_W), F32),
                        pltpu.VMEM((8, MLSTM_W), F32)],
        compiler_params=pltpu.CompilerParams(
            dimension_semantics=("arbitrary", "arbitrary"), vmem_limit_bytes=VMEM_LIMIT),
        name="mlstm",
    )(qkvc, gc, onorm, tri, i4, bd)


TM_FFN = 512
FF_CHUNK = 1024


def _out_ffn_body(x_ref, ya_ref, yb_ref, yc_ref, wo_ref, n2_ref, w1_ref, w2_ref, o_ref, a_ref):
    x1 = (x_ref[...] + _dot(ya_ref[...], wo_ref[0:SWA_Q, :])
          + _dot(yb_ref[...], wo_ref[SWA_Q:SWA_Q + FOX_W, :])
          + _dot(yc_ref[...], wo_ref[SWA_Q + FOX_W:, :]))
    ms = jnp.mean(x1 * x1, axis=-1, keepdims=True)
    h2 = ((x1 * lax.rsqrt(ms + EPS)) * n2_ref[...]).astype(BF16)
    for j in range(D_FF // FF_CHUNK):
        cols = slice(j * FF_CHUNK, (j + 1) * FF_CHUNK)
        r = jnp.maximum(_dot(h2, w1_ref[:, cols]), 0.0)
        a_ref[:, cols] = (r * r).astype(BF16)
    o_ref[...] = x1 + _dot(a_ref[...], w2_ref[...])


def _out_ffn(layer, x2d, ya, yb, yc, wo, n2, w1, w2):
    T = x2d.shape[0]
    tm = min(TM_FFN, T)
    tok = lambda w: pl.BlockSpec((tm, w), lambda i: (i, 0))
    return pl.pallas_call(
        _out_ffn_body,
        grid=(T // tm,),
        in_specs=[
            tok(D_MODEL), tok(SWA_Q), tok(FOX_W), tok(MLSTM_W),
            _layer_spec((D_MODEL, D_MODEL), layer),
            _layer_spec((1, D_MODEL), layer),
            _layer_spec((D_MODEL, D_FF), layer),
            _layer_spec((D_FF, D_MODEL), layer),
        ],
        out_specs=tok(D_MODEL),
        out_shape=jax.ShapeDtypeStruct((T, D_MODEL), F32),
        scratch_shapes=[pltpu.VMEM((tm, D_FF), BF16)],
        compiler_params=pltpu.CompilerParams(
            dimension_semantics=("arbitrary",), vmem_limit_bytes=VMEM_LIMIT),
        name="out_ffn",
    )(x2d, ya, yb, yc, wo, n2, w1, w2)


def _rope_tables(S):
    inv = ROPE_THETA ** (-jnp.arange(HALF, dtype=F32) / HALF)
    ang = jnp.arange(S).astype(F32)[:, None] * inv[None, :]
    cos = jnp.tile(jnp.cos(ang), (1, LANES // HALF))
    sin = jnp.tile(jnp.sin(ang), (1, LANES // HALF))
    sign = jnp.where(jnp.arange(LANES) < HEAD, -1.0, 1.0).astype(F32)
    return cos, sin * sign[None, :]


def _prep_w_in(w_in):
    L, D, _ = w_in.shape
    sizes = (SWA_Q, SWA_KV, SWA_KV, 3 * FOX_W, FOX_HEADS, 3 * MLSTM_W, MLSTM_HEADS, MLSTM_HEADS,
             MLSTM_W)
    aq, ak, av, b_qkv, bf, c_qkv, ci, cf, co = jnp.split(w_in, np.cumsum(sizes)[:-1].tolist(), axis=-1)
    aq = aq.reshape(L, D, 2, SWA_HEADS // 2, 2, HALF).transpose(0, 1, 3, 4, 2, 5).reshape(L, D, SWA_Q)
    ak = ak.reshape(L, D, SWA_KV_HEADS, 2, HALF).transpose(0, 1, 3, 2, 4).reshape(L, D, SWA_KV)
    gate_pad = jnp.zeros((L, D, LANES - 3 * FOX_HEADS), w_in.dtype)
    return jnp.concatenate(
        [aq, ak, av, b_qkv, c_qkv, co, jnp.repeat(ci, HEAD, axis=-1), jnp.repeat(cf, HEAD, axis=-1),
         bf, bf, bf, gate_pad], axis=-1).astype(BF16)


def _prep_w_out(w_out):
    L = w_out.shape[0]
    ya_rows = w_out[:, :SWA_Q].reshape(L, 2, SWA_HEADS // 2, HEAD, D_MODEL)
    ya_rows = ya_rows.transpose(0, 2, 1, 3, 4).reshape(L, SWA_Q, D_MODEL)
    return jnp.concatenate([ya_rows, w_out[:, SWA_Q:]], axis=1).astype(BF16)


def _prep_vecs(swa_q_norm, swa_k_norm, fox_q_norm, fox_k_norm, fox_f_bias, mlstm_i_bias, mlstm_f_bias):
    L = swa_q_norm.shape[0]

    def swa_lanes(g):
        return jnp.concatenate([g[:, :HALF], g[:, :HALF], g[:, HALF:], g[:, HALF:]], axis=-1)

    pad = lambda v: jnp.pad(v, ((0, 0), (0, MLSTM_W - v.shape[1])))
    return jnp.stack([
        pad(swa_lanes(swa_q_norm)), pad(swa_lanes(swa_k_norm)),
        pad(jnp.tile(fox_q_norm, (1, 2))), pad(jnp.tile(fox_k_norm, (1, 2))),
        pad(jnp.tile(fox_f_bias, (1, 3))),
        jnp.repeat(mlstm_i_bias, HEAD, axis=-1), jnp.repeat(mlstm_f_bias, HEAD, axis=-1),
        jnp.zeros((L, MLSTM_W), F32),
    ], axis=1).astype(F32)


def kernel(x, norm1, w_in, swa_q_norm, swa_k_norm, swa_sinks, fox_q_norm, fox_k_norm, fox_f_bias,
           mlstm_i_bias, mlstm_f_bias, mlstm_out_norm, w_out, norm2, w_ff1, w_ff2):
    B, S, D = x.shape
    L = norm1.shape[0]
    T = B * S
    cos, sin = _rope_tables(S)
    tri256, tri64, bd256 = (jnp.asarray(c, BF16) for c in (_TRI256, _TRI64, _BD256))
    i4 = jnp.asarray(_I4, F32)
    eye256 = jnp.eye(MXU_DIM, dtype=BF16)

    w_main = _prep_w_in(w_in.astype(BF16))
    wo = _prep_w_out(w_out.astype(BF16))
    w1 = w_ff1.astype(BF16)
    w2 = w_ff2.astype(BF16)
    vecs = _prep_vecs(swa_q_norm, swa_k_norm, fox_q_norm, fox_k_norm, fox_f_bias, mlstm_i_bias,
                      mlstm_f_bias)
    n1 = norm1.reshape(L, 1, D).astype(F32)
    n2 = norm2.reshape(L, 1, D).astype(F32)
    onorm = mlstm_out_norm.reshape(L, 1, MLSTM_W).astype(F32)
    sinks = swa_sinks.reshape(L * SWA_HEADS).astype(F32)

    for l in range(L):
        qa, kva, qb, kb, vb, qkvc, gc = _in_proj(l, x, n1, w_main, vecs, cos, sin, tri256)
        ya = _swa(l, sinks, qa, kva, eye256)
        yb = _fox(qb, kb, vb, eye256)
        yc = _mlstm(l, qkvc, gc, onorm, tri64, i4, bd256)
        x = _out_ffn(l, x.reshape(T, D), ya.reshape(T, SWA_Q), yb.reshape(T, FOX_W),
                     yc.reshape(T, MLSTM_W), wo, n2, w1, w2).reshape(B, S, D)
    return x
```

```python
import functools
import math

import numpy as np
import jax
import jax.numpy as jnp
from jax import lax
from jax.experimental import pallas as pl
from jax.experimental.pallas import tpu as pltpu

F32 = jnp.float32
BF16 = jnp.bfloat16

D_MODEL = 1024
HEAD = 64
HALF = HEAD // 2
CHUNK = 64
SWA_HEADS = 8
SWA_KV_HEADS = 2
FOX_HEADS = 4
MLSTM_HEADS = 4
SWA_Q = SWA_HEADS * HEAD
SWA_KV = SWA_KV_HEADS * HEAD
FOX_W = FOX_HEADS * HEAD
MLSTM_W = MLSTM_HEADS * HEAD
D_FF = 4 * D_MODEL
QBLK = 128
ROPE_THETA = 10000.0
EPS = 1e-6
QK_SCALE = 1.0 / math.sqrt(HEAD)
LOG2E = math.log2(math.e)

LANES = 128
MXU_DIM = 256
VMEM_LIMIT = 56 * 1024 * 1024

W_A = SWA_Q + 2 * SWA_KV
W_B = 3 * FOX_W
SEC_C = 0
SEC_CO = SEC_C + 768
SEC_G = SEC_CO + 256
W_C = SEC_G + LANES
GATE_BF = 0
GATE_CI = 12
GATE_CF = 16
IN_WIDTH = W_A + W_B + FOX_HEADS + 4 * MLSTM_W + 2 * MLSTM_HEADS


def _perm_a():
    p = np.zeros((W_A, W_A), np.float32)
    lane = np.arange(LANES)
    quarter, i = lane // HALF, lane % HALF
    dim, is_b = i + HALF * (quarter // 2), quarter % 2
    for j in range(SWA_HEADS // 2):
        p[(j + 4 * is_b) * HEAD + dim, j * LANES + lane] = 1.0
    p[SWA_Q + is_b * HEAD + dim, SWA_Q + lane] = 1.0
    p[SWA_Q + SWA_KV + lane, SWA_Q + SWA_KV + lane] = 1.0
    return p


def _perm_c():
    src = IN_WIDTH - W_A - W_B
    o_bf, o_cqkv = 0, FOX_HEADS
    o_ci = o_cqkv + 3 * MLSTM_W
    o_cf, o_co = o_ci + MLSTM_HEADS, o_ci + 2 * MLSTM_HEADS
    p = np.zeros((src, W_C), np.float32)
    w = np.arange(3 * MLSTM_W)
    p[o_cqkv + w, SEC_C + w] = 1.0
    w = np.arange(MLSTM_W)
    p[o_co + w, SEC_CO + w] = 1.0
    h = np.arange(FOX_HEADS)
    for rep in range(3):
        p[o_bf + h, SEC_G + GATE_BF + rep * FOX_HEADS + h] = 1.0
    h = np.arange(MLSTM_HEADS)
    p[o_ci + h, SEC_G + GATE_CI + h] = 1.0
    p[o_cf + h, SEC_G + GATE_CF + h] = 1.0
    return p


def _gate_expand():
    e = np.zeros((LANES, 2 * MLSTM_W), np.float32)
    for h in range(MLSTM_HEADS):
        e[GATE_CI + h, h * HEAD:(h + 1) * HEAD] = 1.0
        e[GATE_CF + h, MLSTM_W + h * HEAD:MLSTM_W + (h + 1) * HEAD] = 1.0
    return np.concatenate([e, e, e], axis=0)


_PERM_A = _perm_a()
_PERM_C = _perm_c()


def _block_diag_ones(head_of_lane):
    return (head_of_lane[:, None] == head_of_lane[None, :]).astype(np.float32)


_TRI256 = np.tril(np.ones((MXU_DIM, MXU_DIM), np.float32))
_TRI64 = np.tril(np.ones((CHUNK, CHUNK), np.float32))
_w256 = np.arange(MLSTM_W)
_BD256 = _block_diag_ones(_w256 // HEAD)
_I4 = (np.arange(CHUNK)[:, None] == (_w256 % HEAD)[None, :]).astype(np.float32)


def _const_spec(shape):
    nd = len(shape)
    return pl.BlockSpec(shape, lambda *_: (0,) * nd, pipeline_mode=pl.Buffered(1))


def _layer_spec(shape, layer):
    nd = len(shape)
    return pl.BlockSpec((None,) + tuple(shape), lambda *_: (layer,) + (0,) * nd,
                        pipeline_mode=pl.Buffered(1))


def _log_sigmoid(x):
    return jnp.minimum(x, 0.0) - jnp.log(1.0 + jnp.exp(-jnp.abs(x)))


def _split2(x):
    hi = x.astype(BF16)
    lo = (x - hi.astype(F32)).astype(BF16)
    return hi, lo


def _split3_by_lane(x, lane):
    hi = x.astype(BF16).astype(F32)
    r1 = x - hi
    mid = r1.astype(BF16).astype(F32)
    low = r1 - mid
    return jnp.where(lane < 4, hi, jnp.where(lane < 8, mid, low)).astype(BF16)


def _dot(a, b):
    return jnp.dot(a, b, preferred_element_type=F32)


def _dot_nt(a, b):
    return lax.dot_general(a, b, (((1,), (1,)), ((), ())), preferred_element_type=F32)


def _dot_tn(a, b):
    return lax.dot_general(a, b, (((0,), (0,)), ((), ())), preferred_element_type=F32)


TM_IN = 1024


def _in_proj_body(x_ref, n1_ref, wa_ref, wb_ref, wc_ref, vec_ref, cos_ref, sin_ref, tri_ref, e3_ref,
                  qa_ref, kva_ref, qb_ref, kb_ref, vb_ref, qkvc_ref, gc_ref, carry_ref):
    tm = x_ref.shape[1]

    @pl.when(pl.program_id(1) == 0)
    def _():
        carry_ref[...] = jnp.zeros_like(carry_ref)

    x = x_ref[0]
    ms = jnp.mean(x * x, axis=-1, keepdims=True)
    h = ((x * lax.rsqrt(ms + EPS)) * n1_ref[...]).astype(BF16)

    def proj(w_ref, c0, width):
        return _dot(h, w_ref[:, c0:c0 + width])

    lane_row = lax.broadcasted_iota(jnp.int32, (1, LANES), 1)
    first_swa = (lane_row // HALF) % 2 == 0
    first_nat = lane_row < HEAD

    def head_norm(z, first, gain):
        sq = z * z
        s_first = jnp.sum(jnp.where(first, sq, 0.0), axis=-1, keepdims=True)
        s_second = jnp.sum(jnp.where(first, 0.0, sq), axis=-1, keepdims=True)
        ss = jnp.where(first, s_first, s_second)
        return (z * lax.rsqrt(ss * (1.0 / HEAD) + EPS)) * gain

    cos = cos_ref[...]
    sin = sin_ref[...]

    def rope(z):
        return z * cos + pltpu.roll(z, HEAD, 1) * sin

    g_aq = vec_ref[0:1, 0:LANES]
    g_ak = vec_ref[1:2, 0:LANES]
    g_bq = vec_ref[2:3, 0:LANES]
    g_bk = vec_ref[3:4, 0:LANES]
    gate_bias = vec_ref[4:5, 0:LANES]

    lane = lax.broadcasted_iota(jnp.int32, (MXU_DIM, LANES), 1)
    zg = proj(wc_ref, SEC_G, LANES) + gate_bias
    log_sig = _log_sigmoid(zg)
    log_f = jnp.where(lane[0:1] < GATE_CI, log_sig, 0.0)
    f_terms = [_split3_by_lane(log_f[blk * MXU_DIM:(blk + 1) * MXU_DIM], lane)
               for blk in range(tm // MXU_DIM)]
    gz = jnp.where(lane[0:1] < GATE_CI, 0.0,
                   jnp.where(lane[0:1] < GATE_CF, zg,
                             jnp.where(lane[0:1] < GATE_CF + MLSTM_HEADS, log_sig, 0.0)))
    gz_hi = gz.astype(BF16)
    gz_r = gz - gz_hi.astype(F32)
    gz_mid = gz_r.astype(BF16)
    gz_terms = jnp.concatenate([gz_hi, gz_mid, (gz_r - gz_mid.astype(F32)).astype(BF16)], axis=1)

    za = proj(wa_ref, 0, W_A)
    for j in range(SWA_HEADS // 2):
        q = rope(head_norm(za[:, j * LANES:(j + 1) * LANES], first_swa, g_aq)) * (QK_SCALE * LOG2E)
        qa_ref[0, :, j * LANES:(j + 1) * LANES] = q.astype(BF16)
    k = rope(head_norm(za[:, SWA_Q:SWA_Q + LANES], first_swa, g_ak))
    kva_ref[0, :, 0:LANES] = k.astype(BF16)
    kva_ref[0, :, LANES:2 * LANES] = za[:, SWA_Q + SWA_KV:].astype(BF16)

    zb = proj(wb_ref, 0, W_B)
    for p in range(FOX_HEADS // 2):
        q = head_norm(zb[:, p * LANES:(p + 1) * LANES], first_nat, g_bq) * (QK_SCALE * LOG2E)
        qb_ref[0, :, p * LANES:(p + 1) * LANES] = q.astype(BF16)
        k = head_norm(zb[:, FOX_W + p * LANES:FOX_W + (p + 1) * LANES], first_nat, g_bk)
        kb_ref[0, :, 2 * p * LANES:(2 * p + 1) * LANES] = k.astype(BF16)
    vb_ref[0] = zb[:, 2 * FOX_W:].astype(BF16)

    zc = proj(wc_ref, SEC_C, 768)
    qkvc_ref[0, :, 0:MLSTM_W] = zc[:, 0:MLSTM_W].astype(BF16)
    qkvc_ref[0, :, MLSTM_W:2 * MLSTM_W] = (zc[:, MLSTM_W:2 * MLSTM_W] * QK_SCALE).astype(BF16)
    qkvc_ref[0, :, 2 * MLSTM_W:] = zc[:, 2 * MLSTM_W:].astype(BF16)
    gc_ref[0, :, 2 * MLSTM_W:] = proj(wc_ref, SEC_CO, MLSTM_W)

    gc_ref[0, :, 0:2 * MLSTM_W] = _dot(gz_terms, e3_ref[...])

    carry = carry_ref[0:1, :]
    for blk in range(tm // MXU_DIM):
        rows = slice(blk * MXU_DIM, (blk + 1) * MXU_DIM)
        cs = _dot(tri_ref[...], f_terms[blk])
        f_loc = cs + pltpu.roll(cs, LANES - 4, 1) + pltpu.roll(cs, LANES - 8, 1)
        f_cum = jnp.where(lane < 4, f_loc, 0.0) + carry
        carry = f_cum[MXU_DIM - 1:MXU_DIM, :]
        f_rep = (f_cum + pltpu.roll(f_cum, 4, 1) + pltpu.roll(f_cum, 8, 1)) * LOG2E
        aug = _split3_by_lane(f_rep, lane)
        kb_ref[0, rows, LANES:2 * LANES] = aug
        kb_ref[0, rows, 3 * LANES:4 * LANES] = aug
    carry_ref[...] = jnp.broadcast_to(carry, carry_ref.shape)


def _in_proj(layer, x, n1, w_a, w_in_bf, w_c, vecs, cos, sin, tri, e3):
    B, S, _ = x.shape
    tm = min(TM_IN, S)
    grid = (B, S // tm)
    tok = lambda w: pl.BlockSpec((1, tm, w), lambda b, i: (b, i, 0))
    out_shapes = (
        jax.ShapeDtypeStruct((B, S, SWA_Q), BF16),
        jax.ShapeDtypeStruct((B, S, 2 * SWA_KV), BF16),
        jax.ShapeDtypeStruct((B, S, FOX_W), BF16),
        jax.ShapeDtypeStruct((B, S, 2 * MXU_DIM), BF16),
        jax.ShapeDtypeStruct((B, S, FOX_W), BF16),
        jax.ShapeDtypeStruct((B, S, 3 * MLSTM_W), BF16),
        jax.ShapeDtypeStruct((B, S, 3 * MLSTM_W), F32),
    )
    return pl.pallas_call(
        _in_proj_body,
        grid=grid,
        in_specs=[
            tok(D_MODEL),
            _layer_spec((1, D_MODEL), layer),
            _layer_spec((D_MODEL, W_A), layer),
            pl.BlockSpec((None, D_MODEL, W_B), lambda b, i: (layer, 0, W_A // W_B),
                         pipeline_mode=pl.Buffered(1)),
            _layer_spec((D_MODEL, W_C), layer),
            _layer_spec((8, MLSTM_W), layer),
            pl.BlockSpec((tm, LANES), lambda b, i: (i, 0)),
            pl.BlockSpec((tm, LANES), lambda b, i: (i, 0)),
            _const_spec((MXU_DIM, MXU_DIM)),
            _const_spec((3 * LANES, 2 * MLSTM_W)),
        ],
        out_specs=[tok(s.shape[-1]) for s in out_shapes],
        out_shape=out_shapes,
        scratch_shapes=[pltpu.VMEM((8, LANES), F32)],
        compiler_params=pltpu.CompilerParams(
            dimension_semantics=("arbitrary", "arbitrary"), vmem_limit_bytes=VMEM_LIMIT),
        name="in_proj",
    )(x, n1, w_a, w_in_bf, w_c, vecs, cos, sin, tri, e3)


TS_SWA = 1024
SWA_DEPTH = 4
VT_ROWS = HEAD + 16


def _swa_body(sink_base, sink_ref, q_ref, kvc_ref, kvp_ref, eye_ref, o_ref):
    step = pl.program_id(1)
    nblk = q_ref.shape[1] // QBLK
    per_group = SWA_HEADS // SWA_KV_HEADS
    eye = eye_ref[0:LANES, 0:LANES]

    key = lax.broadcasted_iota(jnp.int32, (2 * QBLK, MXU_DIM), 0)
    qry = lax.broadcasted_iota(jnp.int32, (2 * QBLK, MXU_DIM), 1) % QBLK
    dchunk = key // CHUNK - qry // CHUNK
    in_band = (dchunk >= 0) & (dchunk <= 2)
    lane = lax.broadcasted_iota(jnp.int32, (QBLK, LANES), 1)
    lane_a = (lane // HALF) % 2 == 0
    first_head = lax.broadcasted_iota(jnp.int32, (1, MXU_DIM), 1) < QBLK
    zero = jnp.zeros((QBLK, LANES), BF16)
    ones = jnp.ones((VT_ROWS - HEAD, 2 * QBLK), BF16)
    kv_all = jnp.concatenate([kvp_ref[0], kvc_ref[0]], axis=0)
    rows = [slice(blk * QBLK, (blk + 1) * QBLK) for blk in range(nblk)]
    bands = [kv_all[blk * QBLK:(blk + 2) * QBLK] for blk in range(nblk)]

    vt_aug = []
    for blk in range(nblk):
        v_t = _dot_nt(eye, bands[blk][:, LANES:]).astype(BF16)
        vt_aug.append([jnp.concatenate([v_t[g * HEAD:(g + 1) * HEAD], ones], axis=0)
                       for g in range(SWA_KV_HEADS)])

    units = [(blk, g, part) for blk in range(nblk) for g in range(SWA_KV_HEADS) for part in range(2)]

    def score(blk, g, part):
        keep = lane_a if g == 0 else jnp.logical_not(lane_a)
        q_cols = [q_ref[0, rows[blk], j * LANES:(j + 1) * LANES] for j in (2 * part, 2 * part + 1)]
        q_g = jnp.concatenate([jnp.where(keep, q2, zero) for q2 in q_cols], axis=0)
        return _dot_nt(bands[blk][:, 0:LANES], q_g)

    def fold(s, blk, g, part):
        allowed = in_band if blk > 0 else in_band & ((key >= QBLK) | (step > 0))
        s = jnp.where(allowed, s, -jnp.inf)
        head = sink_base + g * per_group + 2 * part
        sink = jnp.where(first_head, sink_ref[head], sink_ref[head + 1]) * LOG2E
        m = jnp.maximum(jnp.max(s, axis=0, keepdims=True), sink)
        p = jnp.exp2(s - m)
        o_aug = _dot(vt_aug[blk][g], p.astype(BF16))
        den = o_aug[HEAD:HEAD + 1] + jnp.exp2(sink - m)
        return (o_aug[0:HEAD] / den).astype(BF16)

    def write_block(blk, outs):
        for j in range(SWA_HEADS // 2):
            cols = slice((j % 2) * QBLK, (j % 2 + 1) * QBLK)
            pair_t = jnp.concatenate([outs[(blk, 0, j // 2)][:, cols],
                                      outs[(blk, 1, j // 2)][:, cols]], axis=0)
            o_ref[0, rows[blk], j * LANES:(j + 1) * LANES] = _dot_nt(eye, pair_t).astype(BF16)

    outs = {}
    queue = [score(*unit) for unit in units[:SWA_DEPTH]]
    for n, unit in enumerate(units):
        s_cur = queue.pop(0)
        if n + SWA_DEPTH < len(units):
            queue.append(score(*units[n + SWA_DEPTH]))
        outs[unit] = fold(s_cur, *unit)
        blk = unit[0]
        if unit[1:] == (SWA_KV_HEADS - 1, 1) and blk > 0:
            write_block(blk - 1, outs)
    write_block(nblk - 1, outs)


def _swa(layer, sinks, qa, kva, eye):
    B, S, _ = qa.shape
    ts = min(TS_SWA, S)
    per_step = ts // QBLK
    return pl.pallas_call(
        functools.partial(_swa_body, layer * SWA_HEADS),
        grid=(B, S // ts),
        in_specs=[
            pl.BlockSpec(memory_space=pltpu.SMEM),
            pl.BlockSpec((1, ts, SWA_Q), lambda b, i: (b, i, 0)),
            pl.BlockSpec((1, ts, 2 * SWA_KV), lambda b, i: (b, i, 0)),
            pl.BlockSpec((1, QBLK, 2 * SWA_KV), lambda b, i: (b, jnp.maximum(i * per_step - 1, 0), 0)),
            _const_spec((MXU_DIM, MXU_DIM)),
        ],
        out_specs=pl.BlockSpec((1, ts, SWA_Q), lambda b, i: (b, i, 0)),
        out_shape=jax.ShapeDtypeStruct((B, S, SWA_Q), BF16),
        compiler_params=pltpu.CompilerParams(
            dimension_semantics=("arbitrary", "arbitrary"), vmem_limit_bytes=VMEM_LIMIT),
        name="swa",
    )(sinks, qa, kva, kva, eye)


TQ_FOX = 1024
TK_FOX = 256
NEG_BIG = -1e30


FOX_UNROLL = 4
FOX_DEPTH = 4


def _fox_body(q_ref, k_ref, v_ref, eye_ref, o_ref, vt_ref, qt_ref, acc_ref):
    tq = q_ref.shape[1]
    tk = TK_FOX
    S = k_ref.shape[1]
    pair = pl.program_id(1)
    qi = pl.program_id(2)
    eye = eye_ref[...]
    n_tiles = 2 * tq // MXU_DIM
    tiles_per_head = tq // MXU_DIM

    @pl.when(qi == 0)
    def _():
        ones = jnp.ones((VT_ROWS - HEAD, tk), BF16)
        for j in range(S // tk):
            blk = slice(j * tk, (j + 1) * tk)
            v_t = _dot_nt(eye[0:LANES, 0:LANES], v_ref[0, blk, :]).astype(BF16)
            vt_ref[:, blk] = jnp.concatenate([v_t[0:HEAD], ones, v_t[HEAD:], ones], axis=0)

    lane = lax.broadcasted_iota(jnp.int32, (tq, LANES), 1)
    lane_a = lane < HEAD
    q2 = q_ref[0]
    zero = jnp.zeros_like(q2)

    def minus_one_at(head):
        hit = (lane == head) | (lane == head + 4) | (lane == head + 8)
        return jnp.where(hit, -1.0, 0.0).astype(BF16)

    q_aug = [jnp.concatenate([jnp.where(lane_a, q2, zero), minus_one_at(2 * pair)], axis=1),
             jnp.concatenate([jnp.where(lane_a, zero, q2), minus_one_at(2 * pair + 1)], axis=1)]
    for t in range(n_tiles):
        hd, part = divmod(t, tiles_per_head)
        rows = q_aug[hd][part * MXU_DIM:(part + 1) * MXU_DIM]
        qt_ref[:, t * MXU_DIM:(t + 1) * MXU_DIM] = _dot_nt(eye, rows).astype(BF16)
    acc_ref[...] = jnp.zeros_like(acc_ref)

    key = lax.broadcasted_iota(jnp.int32, (tk // 2, MXU_DIM), 0)
    qry = lax.broadcasted_iota(jnp.int32, (tk // 2, MXU_DIM), 1)

    def fold(s, t, vt_blk, m, triangular):
        hd = t // tiles_per_head
        vt_h = vt_blk[hd * VT_ROWS:(hd + 1) * VT_ROWS]
        half = tk // 2
        for h in range(2):
            sh = s[h * half:(h + 1) * half]
            if triangular:
                valid = key + h * half <= qry
                sh = jnp.where(valid, sh, NEG_BIG)
            m_new = jnp.maximum(m, jnp.max(sh, axis=0, keepdims=True))
            alpha = jnp.exp2(m - m_new)
            p = jnp.exp2(sh - m_new)
            if triangular:
                p = jnp.where(valid, p, 0.0)
            acc_ref[t] = alpha * acc_ref[t] + _dot(vt_h[:, h * half:(h + 1) * half], p.astype(BF16))
            m = m_new
        return m

    def region(j0, ms, plans):
        k_blks, vt_blks, todo = [], [], []
        for b, plan in enumerate(plans):
            r0 = pl.multiple_of((j0 + b) * tk, tk)
            k_blks.append(k_ref[0, pl.ds(r0, tk), :])
            vt_blks.append(vt_ref[:, pl.ds(r0, tk)])
            todo += [(b, t) for t in range(n_tiles) if plan[t] is not None]
        score = lambda b, t: _dot(k_blks[b], qt_ref[:, t * MXU_DIM:(t + 1) * MXU_DIM])
        ms = list(ms)
        queue = [score(*unit) for unit in todo[:FOX_DEPTH]]
        for n, (b, t) in enumerate(todo):
            s_cur = queue.pop(0)
            if n + FOX_DEPTH < len(todo):
                queue.append(score(*todo[n + FOX_DEPTH]))
            ms[t] = fold(s_cur, t, vt_blks[b], ms[t], plans[b][t])
        return tuple(ms)

    blocks_per_tile = tq // tk
    full = (False,) * n_tiles
    n_full = blocks_per_tile * qi
    m_init = jnp.full((1, MXU_DIM), NEG_BIG, F32)
    ms = lax.fori_loop(0, n_full // FOX_UNROLL,
                       lambda g, c: region(g * FOX_UNROLL, c, (full,) * FOX_UNROLL), (m_init,) * n_tiles)
    if FOX_UNROLL > blocks_per_tile:
        done = (n_full // FOX_UNROLL) * FOX_UNROLL
        ms = lax.fori_loop(0, (n_full - done) // blocks_per_tile,
                           lambda g, c: region(done + g * blocks_per_tile, c, (full,) * blocks_per_tile), ms)
    diag = [tuple(None if t % tiles_per_head < d else t % tiles_per_head == d for t in range(n_tiles))
            for d in range(blocks_per_tile)]
    ms = region(n_full, ms, diag)

    for hd in range(2):
        cols = []
        for part in range(tiles_per_head):
            acc = acc_ref[hd * tiles_per_head + part]
            cols.append(acc[0:HEAD] / acc[HEAD:HEAD + 1])
        o_h = jnp.concatenate(cols, axis=1)
        if hd == 0:
            o_t = o_h
        else:
            o_t = jnp.concatenate([o_t, o_h], axis=0).astype(BF16)
    for c in range(tq // MXU_DIM):
        rows = slice(c * MXU_DIM, (c + 1) * MXU_DIM)
        o_ref[0, rows, :] = _dot_nt(eye, o_t[:, rows]).astype(BF16)


def _fox(qb, kb, vb, eye):
    B, S, _ = qb.shape
    tq = min(TQ_FOX, S)
    return pl.pallas_call(
        _fox_body,
        grid=(B, FOX_HEADS // 2, S // tq),
        in_specs=[
            pl.BlockSpec((1, tq, LANES), lambda b, p, i: (b, i, p)),
            pl.BlockSpec((1, S, MXU_DIM), lambda b, p, i: (b, 0, p)),
            pl.BlockSpec((1, S, LANES), lambda b, p, i: (b, 0, p)),
            _const_spec((MXU_DIM, MXU_DIM)),
        ],
        out_specs=pl.BlockSpec((1, tq, LANES), lambda b, p, i: (b, i, p)),
        out_shape=jax.ShapeDtypeStruct((B, S, FOX_W), BF16),
        scratch_shapes=[
            pltpu.VMEM((2 * VT_ROWS, S), BF16),
            pltpu.VMEM((MXU_DIM, 2 * tq), BF16),
            pltpu.VMEM((2 * tq // MXU_DIM, VT_ROWS, MXU_DIM), F32),
        ],
        compiler_params=pltpu.CompilerParams(
            dimension_semantics=("arbitrary", "arbitrary", "arbitrary"),
            vmem_limit_bytes=VMEM_LIMIT),
        name="fox",
    )(qb, kb, vb, eye)


TS_MLSTM = 1024


def _mlstm_body(qkv_ref, g_ref, onorm_ref, tri_ref, i4_ref, bd_ref, o_ref, c_ref, n_ref, m_ref):
    ts = qkv_ref.shape[1]
    W = MLSTM_W

    @pl.when(pl.program_id(1) == 0)
    def _():
        c_ref[...] = jnp.zeros_like(c_ref)
        n_ref[...] = jnp.zeros_like(n_ref)
        m_ref[...] = jnp.zeros_like(m_ref)

    row = lax.broadcasted_iota(jnp.int32, (CHUNK, W), 0)
    lane = lax.broadcasted_iota(jnp.int32, (CHUNK, W), 1)
    causal = (lane % HEAD) <= row
    lane_head = lane // HEAD
    bd = bd_ref[...]
    bd2 = jnp.concatenate([bd, bd], axis=1)
    i4_bf = i4_ref[...].astype(BF16)

    def block_diag(x):
        tiled = jnp.concatenate([x] * MLSTM_HEADS, axis=0)
        return tiled * (bd if x.shape[1] == W else bd2)

    def seg_max(d):
        out = jnp.zeros_like(d)
        for hd in range(MLSTM_HEADS):
            sel = lane_head == hd
            mx = jnp.max(jnp.where(sel, d, -jnp.inf), axis=-1, keepdims=True)
            out = jnp.where(sel, mx, out)
        return out

    chunks = range(ts // CHUNK)
    rows = [pl.ds(c * CHUNK, CHUNK) for c in chunks]

    b, g, u, dm, dm_max, a_max, wa = [], [], [], [], [], [], []
    for c in chunks:
        ig = g_ref[0, rows[c], 0:W]
        lf_hi, lf_lo = _split2(g_ref[0, rows[c], W:2 * W])
        b_c = _dot(tri_ref[...], lf_hi) + _dot(tri_ref[...], lf_lo)
        g_c = b_c[CHUNK - 1:CHUNK, :]
        u_c = b_c - ig
        u_row = jnp.sum(u_c * i4_ref[...], axis=0, keepdims=True)
        dm_c = jnp.where(causal, b_c - u_row, -jnp.inf)
        a_c = g_c - u_c
        am_c = jnp.max(a_c, axis=0, keepdims=True)
        b.append(b_c); g.append(g_c); u.append(u_c); dm.append(dm_c)
        dm_max.append(seg_max(dm_c)); a_max.append(am_c); wa.append(jnp.exp(a_c - am_c))

    m_prev, s_old, s_loc = [m_ref[0:1, :]], [], []
    for c in chunks:
        gm = g[c] + m_prev[c]
        m_new = jnp.maximum(gm, a_max[c])
        s_old.append(jnp.exp(gm - m_new))
        s_loc.append(jnp.exp(a_max[c] - m_new))
        m_prev.append(m_new)

    m_t, w_inter, sm_hi, sm_lo, k_t = [], [], [], [], []
    for c in chunks:
        inter = b[c] + m_prev[c]
        mt_c = jnp.maximum(inter, dm_max[c])
        decay = jnp.exp(dm[c] - mt_c)
        q = qkv_ref[0, rows[c], 0:W]
        k = qkv_ref[0, rows[c], W:2 * W]
        qk = _dot_nt(jnp.concatenate([q, i4_bf], axis=0), block_diag(k))
        hi, lo = _split2(qk[0:CHUNK] * decay)
        m_t.append(mt_c); w_inter.append(jnp.exp(inter - mt_c)); sm_hi.append(hi); sm_lo.append(lo)
        k_t.append(qk[CHUNK:].astype(BF16))

    loc = []
    for c in chunks:
        v = qkv_ref[0, rows[c], 2 * W:]
        weighted = jnp.concatenate([(wa[c] * v.astype(F32)).astype(BF16), wa[c].astype(BF16)], axis=1)
        loc.append(_dot(k_t[c], block_diag(weighted)))

    c_prev, n_prev = [c_ref[...]], [n_ref[...]]
    for c in chunks:
        c_prev.append(s_old[c] * c_prev[c] + s_loc[c] * loc[c][:, 0:W])
        n_prev.append(s_old[c] * n_prev[c] + s_loc[c] * loc[c][:, W:])
    c_ref[...] = c_prev[-1]
    n_ref[...] = n_prev[-1]
    m_ref[...] = jnp.broadcast_to(m_prev[-1], m_ref.shape)

    hval = []
    for c in chunks:
        q = qkv_ref[0, rows[c], 0:W]
        v = qkv_ref[0, rows[c], 2 * W:]
        state = block_diag(jnp.concatenate([c_prev[c].astype(BF16), n_prev[c].astype(BF16)], axis=1))
        inter_cn = _dot(q, state)
        intra = _dot(sm_hi[c], jnp.concatenate([block_diag(v), bd], axis=1))
        num = w_inter[c] * inter_cn[:, 0:W] + intra[:, 0:W]
        den = w_inter[c] * inter_cn[:, W:] + intra[:, W:] + _dot(sm_lo[c], bd)
        hval.append(num / jnp.maximum(jnp.abs(den), jnp.exp(-m_t[c])))

    for c in chunks:
        hs_hi, hs_lo = _split2(hval[c] * hval[c])
        ss = _dot(hs_hi, bd) + _dot(hs_lo, bd)
        hc = (hval[c] * lax.rsqrt(ss * (1.0 / HEAD) + EPS)) * onorm_ref[...]
        co = g_ref[0, rows[c], 2 * W:]
        o_ref[0, rows[c], :] = (jax.nn.sigmoid(co) * hc).astype(BF16)


def _mlstm(layer, qkvc, gc, onorm, tri, i4, bd):
    B, S, _ = qkvc.shape
    ts = min(TS_MLSTM, S)
    return pl.pallas_call(
        _mlstm_body,
        grid=(B, S // ts),
        in_specs=[
            pl.BlockSpec((1, ts, 3 * MLSTM_W), lambda b, i: (b, i, 0)),
            pl.BlockSpec((1, ts, 3 * MLSTM_W), lambda b, i: (b, i, 0)),
            _layer_spec((1, MLSTM_W), layer),
            _const_spec((CHUNK, CHUNK)),
            _const_spec((CHUNK, MLSTM_W)),
            _const_spec((MLSTM_W, MLSTM_W)),
        ],
        out_specs=pl.BlockSpec((1, ts, MLSTM_W), lambda b, i: (b, i, 0)),
        out_shape=jax.ShapeDtypeStruct((B, S, MLSTM_W), BF16),
        scratch_shapes=[pltpu.VMEM((HEAD, MLSTM_W), F32), pltpu.VMEM((HEAD, MLSTM_W), F32),
                        pltpu.VMEM((8, MLSTM_W), F32)],
        compiler_params=pltpu.CompilerParams(
            dimension_semantics=("arbitrary", "arbitrary"), vmem_limit_bytes=VMEM_LIMIT),
        name="mlstm",
    )(qkvc, gc, onorm, tri, i4, bd)


TM_FFN = 512
FF_CHUNK = 1024


def _out_ffn_body(x_ref, ya_ref, yb_ref, yc_ref, wo_ref, n2_ref, w1_ref, w2_ref, o_ref, a_ref):
    x1 = (x_ref[...] + _dot(ya_ref[...], wo_ref[0:SWA_Q, :])
          + _dot(yb_ref[...], wo_ref[SWA_Q:SWA_Q + FOX_W, :])
          + _dot(yc_ref[...], wo_ref[SWA_Q + FOX_W:, :]))
    ms = jnp.mean(x1 * x1, axis=-1, keepdims=True)
    h2 = ((x1 * lax.rsqrt(ms + EPS)) * n2_ref[...]).astype(BF16)
    for j in range(D_FF // FF_CHUNK):
        cols = slice(j * FF_CHUNK, (j + 1) * FF_CHUNK)
        r = jnp.maximum(_dot(h2, w1_ref[:, cols]), 0.0)
        a_ref[:, cols] = (r * r).astype(BF16)
    o_ref[...] = x1 + _dot(a_ref[...], w2_ref[...])


def _out_ffn(layer, x2d, ya, yb, yc, wo, n2, w1, w2):
    T = x2d.shape[0]
    tm = min(TM_FFN, T)
    tok = lambda w: pl.BlockSpec((tm, w), lambda i: (i, 0))
    return pl.pallas_call(
        _out_ffn_body,
        grid=(T // tm,),
        in_specs=[
            tok(D_MODEL), tok(SWA_Q), tok(FOX_W), tok(MLSTM_W),
            _layer_spec((D_MODEL, D_MODEL), layer),
            _layer_spec((1, D_MODEL), layer),
            _layer_spec((D_MODEL, D_FF), layer),
            _layer_spec((D_FF, D_MODEL), layer),
        ],
        out_specs=tok(D_MODEL),
        out_shape=jax.ShapeDtypeStruct((T, D_MODEL), F32),
        scratch_shapes=[pltpu.VMEM((tm, D_FF), BF16)],
        compiler_params=pltpu.CompilerParams(
            dimension_semantics=("arbitrary",), vmem_limit_bytes=VMEM_LIMIT),
        name="out_ffn",
    )(x2d, ya, yb, yc, wo, n2, w1, w2)


def _rope_tables(S):
    inv = ROPE_THETA ** (-jnp.arange(HALF, dtype=F32) / HALF)
    ang = jnp.arange(S).astype(F32)[:, None] * inv[None, :]
    cos = jnp.tile(jnp.cos(ang), (1, LANES // HALF))
    sin = jnp.tile(jnp.sin(ang), (1, LANES // HALF))
    sign = jnp.where(jnp.arange(LANES) < HEAD, -1.0, 1.0).astype(F32)
    return cos, sin * sign[None, :]


def _prep_w_in(w_in_bf):
    select = lambda w, p: jnp.einsum("ldk,kn->ldn", w, jnp.asarray(p, BF16),
                                     preferred_element_type=BF16)
    return select(w_in_bf[:, :, :W_A], _PERM_A), select(w_in_bf[:, :, W_A + W_B:], _PERM_C)


def _prep_w_out(w_out):
    L = w_out.shape[0]
    ya_rows = w_out[:, :SWA_Q].reshape(L, 2, SWA_HEADS // 2, HEAD, D_MODEL)
    ya_rows = ya_rows.transpose(0, 2, 1, 3, 4).reshape(L, SWA_Q, D_MODEL)
    return jnp.concatenate([ya_rows, w_out[:, SWA_Q:]], axis=1).astype(BF16)


def _prep_vecs(swa_q_norm, swa_k_norm, fox_q_norm, fox_k_norm, fox_f_bias, mlstm_i_bias, mlstm_f_bias):
    L = swa_q_norm.shape[0]

    def swa_lanes(g):
        return jnp.concatenate([g[:, :HALF], g[:, :HALF], g[:, HALF:], g[:, HALF:]], axis=-1)

    pad = lambda v: jnp.pad(v, ((0, 0), (0, MLSTM_W - v.shape[1])))
    return jnp.stack([
        pad(swa_lanes(swa_q_norm)), pad(swa_lanes(swa_k_norm)),
        pad(jnp.tile(fox_q_norm, (1, 2))), pad(jnp.tile(fox_k_norm, (1, 2))),
        pad(jnp.concatenate([jnp.tile(fox_f_bias, (1, 3)), mlstm_i_bias, mlstm_f_bias], axis=-1)),
        jnp.zeros((L, MLSTM_W), F32), jnp.zeros((L, MLSTM_W), F32), jnp.zeros((L, MLSTM_W), F32),
    ], axis=1).astype(F32)


def kernel(x, norm1, w_in, swa_q_norm, swa_k_norm, swa_sinks, fox_q_norm, fox_k_norm, fox_f_bias,
           mlstm_i_bias, mlstm_f_bias, mlstm_out_norm, w_out, norm2, w_ff1, w_ff2):
    B, S, D = x.shape
    L = norm1.shape[0]
    T = B * S
    cos, sin = _rope_tables(S)
    tri256, tri64, bd256 = (jnp.asarray(c, BF16) for c in (_TRI256, _TRI64, _BD256))
    i4 = jnp.asarray(_I4, F32)
    e3 = jnp.asarray(_gate_expand(), BF16)
    eye256 = jnp.eye(MXU_DIM, dtype=BF16)

    w_in_bf = w_in.astype(BF16)
    w_a, w_c = _prep_w_in(w_in_bf)
    wo = _prep_w_out(w_out.astype(BF16))
    w1 = w_ff1.astype(BF16)
    w2 = w_ff2.astype(BF16)
    vecs = _prep_vecs(swa_q_norm, swa_k_norm, fox_q_norm, fox_k_norm, fox_f_bias, mlstm_i_bias,
                      mlstm_f_bias)
    n1 = norm1.reshape(L, 1, D).astype(F32)
    n2 = norm2.reshape(L, 1, D).astype(F32)
    onorm = mlstm_out_norm.reshape(L, 1, MLSTM_W).astype(F32)
    sinks = swa_sinks.reshape(L * SWA_HEADS).astype(F32)

    for l in range(L):
        qa, kva, qb, kb, vb, qkvc, gc = _in_proj(l, x, n1, w_a, w_in_bf, w_c, vecs, cos, sin, tri256, e3)
        ya = _swa(l, sinks, qa, kva, eye256)
        yb = _fox(qb, kb, vb, eye256)
        yc = _mlstm(l, qkvc, gc, onorm, tri64, i4, bd256)
        x = _out_ffn(l, x.reshape(T, D), ya.reshape(T, SWA_Q), yb.reshape(T, FOX_W),
                     yc.reshape(T, MLSTM_W), wo, n2, w1, w2).reshape(B, S, D)
    return x
```

```python
import functools
import math

import numpy as np
import jax
import jax.numpy as jnp
from jax import lax
from jax.experimental import pallas as pl
from jax.experimental.pallas import tpu as pltpu

F32 = jnp.float32
BF16 = jnp.bfloat16

D_MODEL = 1024
HEAD = 64
HALF = HEAD // 2
CHUNK = 64
SWA_HEADS = 8
SWA_KV_HEADS = 2
FOX_HEADS = 4
MLSTM_HEADS = 4
SWA_Q = SWA_HEADS * HEAD
SWA_KV = SWA_KV_HEADS * HEAD
FOX_W = FOX_HEADS * HEAD
MLSTM_W = MLSTM_HEADS * HEAD
D_FF = 4 * D_MODEL
QBLK = 128
ROPE_THETA = 10000.0
EPS = 1e-6
QK_SCALE = 1.0 / math.sqrt(HEAD)
LOG2E = math.log2(math.e)

LANES = 128
MXU_DIM = 256
VMEM_LIMIT = 56 * 1024 * 1024

W_A = SWA_Q + 2 * SWA_KV
W_B = 3 * FOX_W
SEC_C = 0
SEC_CO = SEC_C + 768
SEC_G = SEC_CO + 256
W_C = SEC_G + LANES
GATE_BF = 0
GATE_CI = 12
GATE_CF = 16
IN_WIDTH = W_A + W_B + FOX_HEADS + 4 * MLSTM_W + 2 * MLSTM_HEADS


def _perm_a():
    p = np.zeros((W_A, W_A), np.float32)
    lane = np.arange(LANES)
    quarter, i = lane // HALF, lane % HALF
    dim, is_b = i + HALF * (quarter // 2), quarter % 2
    for j in range(SWA_HEADS // 2):
        p[(j + 4 * is_b) * HEAD + dim, j * LANES + lane] = 1.0
    p[SWA_Q + is_b * HEAD + dim, SWA_Q + lane] = 1.0
    p[SWA_Q + SWA_KV + lane, SWA_Q + SWA_KV + lane] = 1.0
    return p


def _perm_c():
    src = IN_WIDTH - W_A - W_B
    o_bf, o_cqkv = 0, FOX_HEADS
    o_ci = o_cqkv + 3 * MLSTM_W
    o_cf, o_co = o_ci + MLSTM_HEADS, o_ci + 2 * MLSTM_HEADS
    p = np.zeros((src, W_C), np.float32)
    w = np.arange(3 * MLSTM_W)
    p[o_cqkv + w, SEC_C + w] = 1.0
    w = np.arange(MLSTM_W)
    p[o_co + w, SEC_CO + w] = 1.0
    h = np.arange(FOX_HEADS)
    for rep in range(3):
        p[o_bf + h, SEC_G + GATE_BF + rep * FOX_HEADS + h] = 1.0
    h = np.arange(MLSTM_HEADS)
    p[o_ci + h, SEC_G + GATE_CI + h] = 1.0
    p[o_cf + h, SEC_G + GATE_CF + h] = 1.0
    return p


def _gate_expand():
    e = np.zeros((LANES, 2 * MLSTM_W), np.float32)
    for h in range(MLSTM_HEADS):
        e[GATE_CI + h, h * HEAD:(h + 1) * HEAD] = 1.0
        e[GATE_CF + h, MLSTM_W + h * HEAD:MLSTM_W + (h + 1) * HEAD] = 1.0
    return np.concatenate([e, e, e], axis=0)


_PERM_A = _perm_a()
_PERM_C = _perm_c()


def _block_diag_ones(head_of_lane):
    return (head_of_lane[:, None] == head_of_lane[None, :]).astype(np.float32)


_TRI256 = np.tril(np.ones((MXU_DIM, MXU_DIM), np.float32))
_TRI64 = np.tril(np.ones((CHUNK, CHUNK), np.float32))
_w256 = np.arange(MLSTM_W)
_BD256 = _block_diag_ones(_w256 // HEAD)
_I4 = (np.arange(CHUNK)[:, None] == (_w256 % HEAD)[None, :]).astype(np.float32)


def _const_spec(shape):
    nd = len(shape)
    return pl.BlockSpec(shape, lambda *_: (0,) * nd, pipeline_mode=pl.Buffered(1))


def _layer_spec(shape, layer):
    nd = len(shape)
    return pl.BlockSpec((None,) + tuple(shape), lambda *_: (layer,) + (0,) * nd,
                        pipeline_mode=pl.Buffered(1))


def _log_sigmoid(x):
    return jnp.minimum(x, 0.0) - jnp.log(1.0 + jnp.exp(-jnp.abs(x)))


def _split2(x):
    hi = x.astype(BF16)
    lo = (x - hi.astype(F32)).astype(BF16)
    return hi, lo


def _split3_by_lane(x, lane):
    hi = x.astype(BF16).astype(F32)
    r1 = x - hi
    mid = r1.astype(BF16).astype(F32)
    low = r1 - mid
    return jnp.where(lane < 4, hi, jnp.where(lane < 8, mid, low)).astype(BF16)


def _dot(a, b):
    return jnp.dot(a, b, preferred_element_type=F32)


def _dot_nt(a, b):
    return lax.dot_general(a, b, (((1,), (1,)), ((), ())), preferred_element_type=F32)


def _dot_tn(a, b):
    return lax.dot_general(a, b, (((0,), (0,)), ((), ())), preferred_element_type=F32)


TM_IN = 1024


def _in_proj_body(x_ref, n1_ref, wa_ref, wb_ref, wc_ref, vec_ref, cos_ref, sin_ref, tri_ref, e3_ref,
                  qa_ref, kva_ref, qb_ref, kb_ref, vb_ref, qkvc_ref, gc_ref, carry_ref):
    tm = x_ref.shape[1]

    @pl.when(pl.program_id(1) == 0)
    def _():
        carry_ref[...] = jnp.zeros_like(carry_ref)

    x = x_ref[0]
    ms = jnp.mean(x * x, axis=-1, keepdims=True)
    h = ((x * lax.rsqrt(ms + EPS)) * n1_ref[...]).astype(BF16)

    def proj(w_ref, c0, width):
        return _dot(h, w_ref[:, c0:c0 + width])

    lane_row = lax.broadcasted_iota(jnp.int32, (1, LANES), 1)
    first_swa = (lane_row // HALF) % 2 == 0
    first_nat = lane_row < HEAD

    def head_norm(z, first, gain):
        sq = z * z
        s_first = jnp.sum(jnp.where(first, sq, 0.0), axis=-1, keepdims=True)
        s_second = jnp.sum(jnp.where(first, 0.0, sq), axis=-1, keepdims=True)
        ss = jnp.where(first, s_first, s_second)
        return (z * lax.rsqrt(ss * (1.0 / HEAD) + EPS)) * gain

    cos = cos_ref[...]
    sin = sin_ref[...]

    def rope(z):
        return z * cos + pltpu.roll(z, HEAD, 1) * sin

    g_aq = vec_ref[0:1, 0:LANES]
    g_ak = vec_ref[1:2, 0:LANES]
    g_bq = vec_ref[2:3, 0:LANES]
    g_bk = vec_ref[3:4, 0:LANES]
    gate_bias = vec_ref[4:5, 0:LANES]

    lane = lax.broadcasted_iota(jnp.int32, (MXU_DIM, LANES), 1)
    zg = proj(wc_ref, SEC_G, LANES) + gate_bias
    log_sig = _log_sigmoid(zg)
    log_f = jnp.where(lane[0:1] < GATE_CI, log_sig, 0.0)
    f_terms = [_split3_by_lane(log_f[blk * MXU_DIM:(blk + 1) * MXU_DIM], lane)
               for blk in range(tm // MXU_DIM)]
    gz = jnp.where(lane[0:1] < GATE_CI, 0.0,
                   jnp.where(lane[0:1] < GATE_CF, zg,
                             jnp.where(lane[0:1] < GATE_CF + MLSTM_HEADS, log_sig, 0.0)))
    gz_hi = gz.astype(BF16)
    gz_r = gz - gz_hi.astype(F32)
    gz_mid = gz_r.astype(BF16)
    gz_terms = jnp.concatenate([gz_hi, gz_mid, (gz_r - gz_mid.astype(F32)).astype(BF16)], axis=1)

    za = proj(wa_ref, 0, W_A)
    for j in range(SWA_HEADS // 2):
        q = rope(head_norm(za[:, j * LANES:(j + 1) * LANES], first_swa, g_aq)) * (QK_SCALE * LOG2E)
        qa_ref[0, :, j * LANES:(j + 1) * LANES] = q.astype(BF16)
    k = rope(head_norm(za[:, SWA_Q:SWA_Q + LANES], first_swa, g_ak))
    kva_ref[0, :, 0:LANES] = k.astype(BF16)
    kva_ref[0, :, LANES:2 * LANES] = za[:, SWA_Q + SWA_KV:].astype(BF16)

    zb = proj(wb_ref, 0, W_B)
    for p in range(FOX_HEADS // 2):
        q = head_norm(zb[:, p * LANES:(p + 1) * LANES], first_nat, g_bq) * (QK_SCALE * LOG2E)
        qb_ref[0, :, p * LANES:(p + 1) * LANES] = q.astype(BF16)
        k = head_norm(zb[:, FOX_W + p * LANES:FOX_W + (p + 1) * LANES], first_nat, g_bk)
        kb_ref[0, :, 2 * p * LANES:(2 * p + 1) * LANES] = k.astype(BF16)
    vb_ref[0] = zb[:, 2 * FOX_W:].astype(BF16)

    zc = proj(wc_ref, SEC_C, 768)
    qkvc_ref[0, :, 0:MLSTM_W] = zc[:, 0:MLSTM_W].astype(BF16)
    qkvc_ref[0, :, MLSTM_W:2 * MLSTM_W] = (zc[:, MLSTM_W:2 * MLSTM_W] * QK_SCALE).astype(BF16)
    qkvc_ref[0, :, 2 * MLSTM_W:] = zc[:, 2 * MLSTM_W:].astype(BF16)
    gc_ref[0, :, 2 * MLSTM_W:] = proj(wc_ref, SEC_CO, MLSTM_W)

    gc_ref[0, :, 0:2 * MLSTM_W] = _dot(gz_terms, e3_ref[...])

    carry = carry_ref[0:1, :]
    for blk in range(tm // MXU_DIM):
        rows = slice(blk * MXU_DIM, (blk + 1) * MXU_DIM)
        cs = _dot(tri_ref[...], f_terms[blk])
        f_loc = cs + pltpu.roll(cs, LANES - 4, 1) + pltpu.roll(cs, LANES - 8, 1)
        f_cum = jnp.where(lane < 4, f_loc, 0.0) + carry
        carry = f_cum[MXU_DIM - 1:MXU_DIM, :]
        f_rep = (f_cum + pltpu.roll(f_cum, 4, 1) + pltpu.roll(f_cum, 8, 1)) * LOG2E
        aug = _split3_by_lane(f_rep, lane)
        kb_ref[0, rows, LANES:2 * LANES] = aug
        kb_ref[0, rows, 3 * LANES:4 * LANES] = aug
    carry_ref[...] = jnp.broadcast_to(carry, carry_ref.shape)


def _in_proj(layer, x, n1, w_a, w_b, w_c, vecs, cos, sin, tri, e3):
    B, S, _ = x.shape
    tm = min(TM_IN, S)
    grid = (B, S // tm)
    tok = lambda w: pl.BlockSpec((1, tm, w), lambda b, i: (b, i, 0))
    out_shapes = (
        jax.ShapeDtypeStruct((B, S, SWA_Q), BF16),
        jax.ShapeDtypeStruct((B, S, 2 * SWA_KV), BF16),
        jax.ShapeDtypeStruct((B, S, FOX_W), BF16),
        jax.ShapeDtypeStruct((B, S, 2 * MXU_DIM), BF16),
        jax.ShapeDtypeStruct((B, S, FOX_W), BF16),
        jax.ShapeDtypeStruct((B, S, 3 * MLSTM_W), BF16),
        jax.ShapeDtypeStruct((B, S, 3 * MLSTM_W), F32),
    )
    return pl.pallas_call(
        _in_proj_body,
        grid=grid,
        in_specs=[
            tok(D_MODEL),
            _layer_spec((1, D_MODEL), layer),
            _layer_spec((D_MODEL, W_A), layer),
            _layer_spec((D_MODEL, W_B), layer),
            _layer_spec((D_MODEL, W_C), layer),
            _layer_spec((8, MLSTM_W), layer),
            pl.BlockSpec((tm, LANES), lambda b, i: (i, 0)),
            pl.BlockSpec((tm, LANES), lambda b, i: (i, 0)),
            _const_spec((MXU_DIM, MXU_DIM)),
            _const_spec((3 * LANES, 2 * MLSTM_W)),
        ],
        out_specs=[tok(s.shape[-1]) for s in out_shapes],
        out_shape=out_shapes,
        scratch_shapes=[pltpu.VMEM((8, LANES), F32)],
        compiler_params=pltpu.CompilerParams(
            dimension_semantics=("arbitrary", "arbitrary"), vmem_limit_bytes=VMEM_LIMIT),
        name="in_proj",
    )(x, n1, w_a, w_b, w_c, vecs, cos, sin, tri, e3)


TS_SWA = 1024
SWA_DEPTH = 4
VT_ROWS = HEAD + 16


def _swa_body(sink_base, sink_ref, q_ref, kvc_ref, kvp_ref, eye_ref, o_ref):
    step = pl.program_id(1)
    nblk = q_ref.shape[1] // QBLK
    per_group = SWA_HEADS // SWA_KV_HEADS
    eye = eye_ref[0:LANES, 0:LANES]

    key = lax.broadcasted_iota(jnp.int32, (2 * QBLK, MXU_DIM), 0)
    qry = lax.broadcasted_iota(jnp.int32, (2 * QBLK, MXU_DIM), 1) % QBLK
    dchunk = key // CHUNK - qry // CHUNK
    in_band = (dchunk >= 0) & (dchunk <= 2)
    lane = lax.broadcasted_iota(jnp.int32, (QBLK, LANES), 1)
    lane_a = (lane // HALF) % 2 == 0
    first_head = lax.broadcasted_iota(jnp.int32, (1, MXU_DIM), 1) < QBLK
    zero = jnp.zeros((QBLK, LANES), BF16)
    ones = jnp.ones((VT_ROWS - HEAD, 2 * QBLK), BF16)
    kv_all = jnp.concatenate([kvp_ref[0], kvc_ref[0]], axis=0)
    rows = [slice(blk * QBLK, (blk + 1) * QBLK) for blk in range(nblk)]
    bands = [kv_all[blk * QBLK:(blk + 2) * QBLK] for blk in range(nblk)]

    vt_aug = []
    for blk in range(nblk):
        v_t = _dot_nt(eye, bands[blk][:, LANES:]).astype(BF16)
        vt_aug.append([jnp.concatenate([v_t[g * HEAD:(g + 1) * HEAD], ones], axis=0)
                       for g in range(SWA_KV_HEADS)])

    units = [(blk, g, part) for blk in range(nblk) for g in range(SWA_KV_HEADS) for part in range(2)]

    def score(blk, g, part):
        keep = lane_a if g == 0 else jnp.logical_not(lane_a)
        q_cols = [q_ref[0, rows[blk], j * LANES:(j + 1) * LANES] for j in (2 * part, 2 * part + 1)]
        q_g = jnp.concatenate([jnp.where(keep, q2, zero) for q2 in q_cols], axis=0)
        return _dot_nt(bands[blk][:, 0:LANES], q_g)

    def fold(s, blk, g, part):
        allowed = in_band if blk > 0 else in_band & ((key >= QBLK) | (step > 0))
        s = jnp.where(allowed, s, -jnp.inf)
        head = sink_base + g * per_group + 2 * part
        sink = jnp.where(first_head, sink_ref[head], sink_ref[head + 1]) * LOG2E
        m = jnp.maximum(jnp.max(s, axis=0, keepdims=True), sink)
        p = jnp.exp2(s - m)
        o_aug = _dot(vt_aug[blk][g], p.astype(BF16))
        den = o_aug[HEAD:HEAD + 1] + jnp.exp2(sink - m)
        return (o_aug[0:HEAD] / den).astype(BF16)

    def write_block(blk, outs):
        for g in range(SWA_KV_HEADS):
            for part in range(2):
                o = outs[(blk, g, part)]
                pair_t = jnp.concatenate([o[:, 0:QBLK], o[:, QBLK:]], axis=0)
                j = 2 * g + part
                o_ref[0, rows[blk], j * LANES:(j + 1) * LANES] = _dot_nt(eye, pair_t).astype(BF16)

    outs = {}
    queue = [score(*unit) for unit in units[:SWA_DEPTH]]
    for n, unit in enumerate(units):
        s_cur = queue.pop(0)
        if n + SWA_DEPTH < len(units):
            queue.append(score(*units[n + SWA_DEPTH]))
        outs[unit] = fold(s_cur, *unit)
        blk = unit[0]
        if unit[1:] == (SWA_KV_HEADS - 1, 1) and blk > 0:
            write_block(blk - 1, outs)
    write_block(nblk - 1, outs)


def _swa(layer, sinks, qa, kva, eye):
    B, S, _ = qa.shape
    ts = min(TS_SWA, S)
    per_step = ts // QBLK
    return pl.pallas_call(
        functools.partial(_swa_body, layer * SWA_HEADS),
        grid=(B, S // ts),
        in_specs=[
            pl.BlockSpec(memory_space=pltpu.SMEM),
            pl.BlockSpec((1, ts, SWA_Q), lambda b, i: (b, i, 0)),
            pl.BlockSpec((1, ts, 2 * SWA_KV), lambda b, i: (b, i, 0)),
            pl.BlockSpec((1, QBLK, 2 * SWA_KV), lambda b, i: (b, jnp.maximum(i * per_step - 1, 0), 0)),
            _const_spec((MXU_DIM, MXU_DIM)),
        ],
        out_specs=pl.BlockSpec((1, ts, SWA_Q), lambda b, i: (b, i, 0)),
        out_shape=jax.ShapeDtypeStruct((B, S, SWA_Q), BF16),
        compiler_params=pltpu.CompilerParams(
            dimension_semantics=("arbitrary", "arbitrary"), vmem_limit_bytes=VMEM_LIMIT),
        name="swa",
    )(sinks, qa, kva, kva, eye)


TQ_FOX = 1024
TK_FOX = 256
NEG_BIG = -1e30


FOX_UNROLL = 4
FOX_DEPTH = 4


def _fox_body(q_ref, k_ref, v_ref, eye_ref, o_ref, vt_ref, qt_ref, acc_ref):
    tq = q_ref.shape[1]
    tk = TK_FOX
    S = k_ref.shape[1]
    pair = pl.program_id(1)
    qi = pl.program_id(2)
    eye = eye_ref[...]
    n_tiles = 2 * tq // MXU_DIM
    tiles_per_head = tq // MXU_DIM

    @pl.when(qi == 0)
    def _():
        ones = jnp.ones((VT_ROWS - HEAD, tk), BF16)
        for j in range(S // tk):
            blk = slice(j * tk, (j + 1) * tk)
            v_t = _dot_nt(eye[0:LANES, 0:LANES], v_ref[0, blk, :]).astype(BF16)
            vt_ref[:, blk] = jnp.concatenate([v_t[0:HEAD], ones, v_t[HEAD:], ones], axis=0)

    lane = lax.broadcasted_iota(jnp.int32, (tq, LANES), 1)
    lane_a = lane < HEAD
    q2 = q_ref[0]
    zero = jnp.zeros_like(q2)

    def minus_one_at(head):
        hit = (lane == head) | (lane == head + 4) | (lane == head + 8)
        return jnp.where(hit, -1.0, 0.0).astype(BF16)

    q_aug = [jnp.concatenate([jnp.where(lane_a, q2, zero), minus_one_at(2 * pair)], axis=1),
             jnp.concatenate([jnp.where(lane_a, zero, q2), minus_one_at(2 * pair + 1)], axis=1)]
    for t in range(n_tiles):
        hd, part = divmod(t, tiles_per_head)
        rows = q_aug[hd][part * MXU_DIM:(part + 1) * MXU_DIM]
        qt_ref[:, t * MXU_DIM:(t + 1) * MXU_DIM] = _dot_nt(eye, rows).astype(BF16)
    acc_ref[...] = jnp.zeros_like(acc_ref)

    key = lax.broadcasted_iota(jnp.int32, (tk // 2, MXU_DIM), 0)
    qry = lax.broadcasted_iota(jnp.int32, (tk // 2, MXU_DIM), 1)

    def fold(s, t, vt_blk, m, triangular):
        hd = t // tiles_per_head
        vt_h = vt_blk[hd * VT_ROWS:(hd + 1) * VT_ROWS]
        half = tk // 2
        for h in range(2):
            sh = s[h * half:(h + 1) * half]
            if triangular:
                valid = key + h * half <= qry
                sh = jnp.where(valid, sh, NEG_BIG)
            m_new = jnp.maximum(m, jnp.max(sh, axis=0, keepdims=True))
            alpha = jnp.exp2(m - m_new)
            p = jnp.exp2(sh - m_new)
            if triangular:
                p = jnp.where(valid, p, 0.0)
            acc_ref[t] = alpha * acc_ref[t] + _dot(vt_h[:, h * half:(h + 1) * half], p.astype(BF16))
            m = m_new
        return m

    def region(j0, ms, plans):
        k_blks, vt_blks, todo = [], [], []
        for b, plan in enumerate(plans):
            r0 = pl.multiple_of((j0 + b) * tk, tk)
            k_blks.append(k_ref[0, pl.ds(r0, tk), :])
            vt_blks.append(vt_ref[:, pl.ds(r0, tk)])
            todo += [(b, t) for t in range(n_tiles) if plan[t] is not None]
        score = lambda b, t: _dot(k_blks[b], qt_ref[:, t * MXU_DIM:(t + 1) * MXU_DIM])
        ms = list(ms)
        queue = [score(*unit) for unit in todo[:FOX_DEPTH]]
        for n, (b, t) in enumerate(todo):
            s_cur = queue.pop(0)
            if n + FOX_DEPTH < len(todo):
                queue.append(score(*todo[n + FOX_DEPTH]))
            ms[t] = fold(s_cur, t, vt_blks[b], ms[t], plans[b][t])
        return tuple(ms)

    blocks_per_tile = tq // tk
    full = (False,) * n_tiles
    n_full = blocks_per_tile * qi
    m_init = jnp.full((1, MXU_DIM), NEG_BIG, F32)
    ms = lax.fori_loop(0, n_full // FOX_UNROLL,
                       lambda g, c: region(g * FOX_UNROLL, c, (full,) * FOX_UNROLL), (m_init,) * n_tiles)
    if FOX_UNROLL > blocks_per_tile:
        done = (n_full // FOX_UNROLL) * FOX_UNROLL
        ms = lax.fori_loop(0, (n_full - done) // blocks_per_tile,
                           lambda g, c: region(done + g * blocks_per_tile, c, (full,) * blocks_per_tile), ms)
    diag = [tuple(None if t % tiles_per_head < d else t % tiles_per_head == d for t in range(n_tiles))
            for d in range(blocks_per_tile)]
    ms = region(n_full, ms, diag)

    for hd in range(2):
        cols = []
        for part in range(tiles_per_head):
            acc = acc_ref[hd * tiles_per_head + part]
            cols.append(acc[0:HEAD] / acc[HEAD:HEAD + 1])
        o_h = jnp.concatenate(cols, axis=1)
        if hd == 0:
            o_t = o_h
        else:
            o_t = jnp.concatenate([o_t, o_h], axis=0).astype(BF16)
    for c in range(tq // MXU_DIM):
        rows = slice(c * MXU_DIM, (c + 1) * MXU_DIM)
        o_ref[0, rows, :] = _dot_nt(eye, o_t[:, rows]).astype(BF16)


def _fox(qb, kb, vb, eye):
    B, S, _ = qb.shape
    tq = min(TQ_FOX, S)
    return pl.pallas_call(
        _fox_body,
        grid=(B, FOX_HEADS // 2, S // tq),
        in_specs=[
            pl.BlockSpec((1, tq, LANES), lambda b, p, i: (b, i, p)),
            pl.BlockSpec((1, S, MXU_DIM), lambda b, p, i: (b, 0, p)),
            pl.BlockSpec((1, S, LANES), lambda b, p, i: (b, 0, p)),
            _const_spec((MXU_DIM, MXU_DIM)),
        ],
        out_specs=pl.BlockSpec((1, tq, LANES), lambda b, p, i: (b, i, p)),
        out_shape=jax.ShapeDtypeStruct((B, S, FOX_W), BF16),
        scratch_shapes=[
            pltpu.VMEM((2 * VT_ROWS, S), BF16),
            pltpu.VMEM((MXU_DIM, 2 * tq), BF16),
            pltpu.VMEM((2 * tq // MXU_DIM, VT_ROWS, MXU_DIM), F32),
        ],
        compiler_params=pltpu.CompilerParams(
            dimension_semantics=("arbitrary", "arbitrary", "arbitrary"),
            vmem_limit_bytes=VMEM_LIMIT),
        name="fox",
    )(qb, kb, vb, eye)


TS_MLSTM = 1024


def _mlstm_body(qkv_ref, g_ref, onorm_ref, tri_ref, i4_ref, bd_ref, o_ref, c_ref, n_ref, m_ref):
    ts = qkv_ref.shape[1]
    W = MLSTM_W

    @pl.when(pl.program_id(1) == 0)
    def _():
        c_ref[...] = jnp.zeros_like(c_ref)
        n_ref[...] = jnp.zeros_like(n_ref)
        m_ref[...] = jnp.zeros_like(m_ref)

    row = lax.broadcasted_iota(jnp.int32, (CHUNK, W), 0)
    lane = lax.broadcasted_iota(jnp.int32, (CHUNK, W), 1)
    causal = (lane % HEAD) <= row
    lane_head = lane // HEAD
    bd = bd_ref[...]
    bd2 = jnp.concatenate([bd, bd], axis=1)
    i4_bf = i4_ref[...].astype(BF16)

    def block_diag(x):
        tiled = jnp.concatenate([x] * MLSTM_HEADS, axis=0)
        return tiled * (bd if x.shape[1] == W else bd2)

    def seg_max(d):
        out = jnp.zeros_like(d)
        for hd in range(MLSTM_HEADS):
            sel = lane_head == hd
            mx = jnp.max(jnp.where(sel, d, -jnp.inf), axis=-1, keepdims=True)
            out = jnp.where(sel, mx, out)
        return out

    chunks = range(ts // CHUNK)
    rows = [pl.ds(c * CHUNK, CHUNK) for c in chunks]

    b, g, u, dm, dm_max, a_max, wa = [], [], [], [], [], [], []
    for c in chunks:
        ig = g_ref[0, rows[c], 0:W]
        lf_hi, lf_lo = _split2(g_ref[0, rows[c], W:2 * W])
        b_c = _dot(tri_ref[...], lf_hi) + _dot(tri_ref[...], lf_lo)
        g_c = b_c[CHUNK - 1:CHUNK, :]
        u_c = b_c - ig
        u_row = jnp.sum(u_c * i4_ref[...], axis=0, keepdims=True)
        dm_c = jnp.where(causal, b_c - u_row, -jnp.inf)
        a_c = g_c - u_c
        am_c = jnp.max(a_c, axis=0, keepdims=True)
        b.append(b_c); g.append(g_c); u.append(u_c); dm.append(dm_c)
        dm_max.append(seg_max(dm_c)); a_max.append(am_c); wa.append(jnp.exp(a_c - am_c))

    m_prev, s_old, s_loc = [m_ref[0:1, :]], [], []
    for c in chunks:
        gm = g[c] + m_prev[c]
        m_new = jnp.maximum(gm, a_max[c])
        s_old.append(jnp.exp(gm - m_new))
        s_loc.append(jnp.exp(a_max[c] - m_new))
        m_prev.append(m_new)

    m_t, w_inter, sm_hi, sm_lo, k_t = [], [], [], [], []
    for c in chunks:
        inter = b[c] + m_prev[c]
        mt_c = jnp.maximum(inter, dm_max[c])
        decay = jnp.exp(dm[c] - mt_c)
        q = qkv_ref[0, rows[c], 0:W]
        k = qkv_ref[0, rows[c], W:2 * W]
        qk = _dot_nt(jnp.concatenate([q, i4_bf], axis=0), block_diag(k))
        hi, lo = _split2(qk[0:CHUNK] * decay)
        m_t.append(mt_c); w_inter.append(jnp.exp(inter - mt_c)); sm_hi.append(hi); sm_lo.append(lo)
        k_t.append(qk[CHUNK:].astype(BF16))

    loc = []
    for c in chunks:
        v = qkv_ref[0, rows[c], 2 * W:]
        weighted = jnp.concatenate([(wa[c] * v.astype(F32)).astype(BF16), wa[c].astype(BF16)], axis=1)
        loc.append(_dot(k_t[c], block_diag(weighted)))

    c_prev, n_prev = [c_ref[...]], [n_ref[...]]
    for c in chunks:
        c_prev.append(s_old[c] * c_prev[c] + s_loc[c] * loc[c][:, 0:W])
        n_prev.append(s_old[c] * n_prev[c] + s_loc[c] * loc[c][:, W:])
    c_ref[...] = c_prev[-1]
    n_ref[...] = n_prev[-1]
    m_ref[...] = jnp.broadcast_to(m_prev[-1], m_ref.shape)

    hval = []
    for c in chunks:
        q = qkv_ref[0, rows[c], 0:W]
        v = qkv_ref[0, rows[c], 2 * W:]
        state = block_diag(jnp.concatenate([c_prev[c].astype(BF16), n_prev[c].astype(BF16)], axis=1))
        inter_cn = _dot(q, state)
        intra = _dot(sm_hi[c], jnp.concatenate([block_diag(v), bd], axis=1))
        num = w_inter[c] * inter_cn[:, 0:W] + intra[:, 0:W]
        den = w_inter[c] * inter_cn[:, W:] + intra[:, W:] + _dot(sm_lo[c], bd)
        hval.append(num / jnp.maximum(jnp.abs(den), jnp.exp(-m_t[c])))

    for c in chunks:
        hs_hi, hs_lo = _split2(hval[c] * hval[c])
        ss = _dot(hs_hi, bd) + _dot(hs_lo, bd)
        hc = (hval[c] * lax.rsqrt(ss * (1.0 / HEAD) + EPS)) * onorm_ref[...]
        co = g_ref[0, rows[c], 2 * W:]
        o_ref[0, rows[c], :] = (jax.nn.sigmoid(co) * hc).astype(BF16)


def _mlstm(layer, qkvc, gc, onorm, tri, i4, bd):
    B, S, _ = qkvc.shape
    ts = min(TS_MLSTM, S)
    return pl.pallas_call(
        _mlstm_body,
        grid=(B, S // ts),
        in_specs=[
            pl.BlockSpec((1, ts, 3 * MLSTM_W), lambda b, i: (b, i, 0)),
            pl.BlockSpec((1, ts, 3 * MLSTM_W), lambda b, i: (b, i, 0)),
            _layer_spec((1, MLSTM_W), layer),
            _const_spec((CHUNK, CHUNK)),
            _const_spec((CHUNK, MLSTM_W)),
            _const_spec((MLSTM_W, MLSTM_W)),
        ],
        out_specs=pl.BlockSpec((1, ts, MLSTM_W), lambda b, i: (b, i, 0)),
        out_shape=jax.ShapeDtypeStruct((B, S, MLSTM_W), BF16),
        scratch_shapes=[pltpu.VMEM((HEAD, MLSTM_W), F32), pltpu.VMEM((HEAD, MLSTM_W), F32),
                        pltpu.VMEM((8, MLSTM_W), F32)],
        compiler_params=pltpu.CompilerParams(
            dimension_semantics=("arbitrary", "arbitrary"), vmem_limit_bytes=VMEM_LIMIT),
        name="mlstm",
    )(qkvc, gc, onorm, tri, i4, bd)


TM_FFN = 512
FF_CHUNK = 1024


def _out_ffn_body(x_ref, ya_ref, yb_ref, yc_ref, wo_ref, n2_ref, w1_ref, w2_ref, o_ref, a_ref):
    x1 = (x_ref[...] + _dot(ya_ref[...], wo_ref[0:SWA_Q, :])
          + _dot(yb_ref[...], wo_ref[SWA_Q:SWA_Q + FOX_W, :])
          + _dot(yc_ref[...], wo_ref[SWA_Q + FOX_W:, :]))
    ms = jnp.mean(x1 * x1, axis=-1, keepdims=True)
    h2 = ((x1 * lax.rsqrt(ms + EPS)) * n2_ref[...]).astype(BF16)
    for j in range(D_FF // FF_CHUNK):
        cols = slice(j * FF_CHUNK, (j + 1) * FF_CHUNK)
        r = jnp.maximum(_dot(h2, w1_ref[:, cols]), 0.0)
        a_ref[:, cols] = (r * r).astype(BF16)
    o_ref[...] = x1 + _dot(a_ref[...], w2_ref[...])


def _out_ffn(layer, x2d, ya, yb, yc, wo, n2, w1, w2):
    T = x2d.shape[0]
    tm = min(TM_FFN, T)
    tok = lambda w: pl.BlockSpec((tm, w), lambda i: (i, 0))
    return pl.pallas_call(
        _out_ffn_body,
        grid=(T // tm,),
        in_specs=[
            tok(D_MODEL), tok(SWA_Q), tok(FOX_W), tok(MLSTM_W),
            _layer_spec((D_MODEL, D_MODEL), layer),
            _layer_spec((1, D_MODEL), layer),
            _layer_spec((D_MODEL, D_FF), layer),
            _layer_spec((D_FF, D_MODEL), layer),
        ],
        out_specs=tok(D_MODEL),
        out_shape=jax.ShapeDtypeStruct((T, D_MODEL), F32),
        scratch_shapes=[pltpu.VMEM((tm, D_FF), BF16)],
        compiler_params=pltpu.CompilerParams(
            dimension_semantics=("arbitrary",), vmem_limit_bytes=VMEM_LIMIT),
        name="out_ffn",
    )(x2d, ya, yb, yc, wo, n2, w1, w2)


def _rope_tables(S):
    inv = ROPE_THETA ** (-jnp.arange(HALF, dtype=F32) / HALF)
    ang = jnp.arange(S).astype(F32)[:, None] * inv[None, :]
    cos = jnp.tile(jnp.cos(ang), (1, LANES // HALF))
    sin = jnp.tile(jnp.sin(ang), (1, LANES // HALF))
    sign = jnp.where(jnp.arange(LANES) < HEAD, -1.0, 1.0).astype(F32)
    return cos, sin * sign[None, :]


def _prep_w_in(w_in):
    select = lambda w, p: jnp.einsum("ldk,kn->ldn", w.astype(BF16), jnp.asarray(p, BF16),
                                     preferred_element_type=BF16)
    w_b = w_in[:, :, W_A:W_A + W_B].astype(BF16)
    return select(w_in[:, :, :W_A], _PERM_A), w_b, select(w_in[:, :, W_A + W_B:], _PERM_C)


def _prep_vecs(swa_q_norm, swa_k_norm, fox_q_norm, fox_k_norm, fox_f_bias, mlstm_i_bias, mlstm_f_bias):
    L = swa_q_norm.shape[0]

    def swa_lanes(g):
        return jnp.concatenate([g[:, :HALF], g[:, :HALF], g[:, HALF:], g[:, HALF:]], axis=-1)

    pad = lambda v: jnp.pad(v, ((0, 0), (0, MLSTM_W - v.shape[1])))
    return jnp.stack([
        pad(swa_lanes(swa_q_norm)), pad(swa_lanes(swa_k_norm)),
        pad(jnp.tile(fox_q_norm, (1, 2))), pad(jnp.tile(fox_k_norm, (1, 2))),
        pad(jnp.concatenate([jnp.tile(fox_f_bias, (1, 3)), mlstm_i_bias, mlstm_f_bias], axis=-1)),
        jnp.zeros((L, MLSTM_W), F32), jnp.zeros((L, MLSTM_W), F32), jnp.zeros((L, MLSTM_W), F32),
    ], axis=1).astype(F32)


def kernel(x, norm1, w_in, swa_q_norm, swa_k_norm, swa_sinks, fox_q_norm, fox_k_norm, fox_f_bias,
           mlstm_i_bias, mlstm_f_bias, mlstm_out_norm, w_out, norm2, w_ff1, w_ff2):
    B, S, D = x.shape
    L = norm1.shape[0]
    T = B * S
    cos, sin = _rope_tables(S)
    tri256, tri64, bd256 = (jnp.asarray(c, BF16) for c in (_TRI256, _TRI64, _BD256))
    i4 = jnp.asarray(_I4, F32)
    e3 = jnp.asarray(_gate_expand(), BF16)
    eye256 = jnp.eye(MXU_DIM, dtype=BF16)

    w_a, w_b, w_c = _prep_w_in(w_in)
    wo = w_out.astype(BF16)
    w1 = w_ff1.astype(BF16)
    w2 = w_ff2.astype(BF16)
    vecs = _prep_vecs(swa_q_norm, swa_k_norm, fox_q_norm, fox_k_norm, fox_f_bias, mlstm_i_bias,
                      mlstm_f_bias)
    n1 = norm1.reshape(L, 1, D).astype(F32)
    n2 = norm2.reshape(L, 1, D).astype(F32)
    onorm = mlstm_out_norm.reshape(L, 1, MLSTM_W).astype(F32)
    sinks = swa_sinks.reshape(L * SWA_HEADS).astype(F32)

    for l in range(L):
        qa, kva, qb, kb, vb, qkvc, gc = _in_proj(l, x, n1, w_a, w_b, w_c, vecs, cos, sin, tri256, e3)
        ya = _swa(l, sinks, qa, kva, eye256)
        yb = _fox(qb, kb, vb, eye256)
        yc = _mlstm(l, qkvc, gc, onorm, tri64, i4, bd256)
        x = _out_ffn(l, x.reshape(T, D), ya.reshape(T, SWA_Q), yb.reshape(T, FOX_W),
                     yc.reshape(T, MLSTM_W), wo, n2, w1, w2).reshape(B, S, D)
    return x
```

```python
import functools
import math

import numpy as np
import jax
import jax.numpy as jnp
from jax import lax
from jax.experimental import pallas as pl
from jax.experimental.pallas import tpu as pltpu

F32 = jnp.float32
BF16 = jnp.bfloat16

D_MODEL = 1024
HEAD = 64
HALF = HEAD // 2
CHUNK = 64
SWA_HEADS = 8
SWA_KV_HEADS = 2
FOX_HEADS = 4
MLSTM_HEADS = 4
SWA_Q = SWA_HEADS * HEAD
SWA_KV = SWA_KV_HEADS * HEAD
FOX_W = FOX_HEADS * HEAD
MLSTM_W = MLSTM_HEADS * HEAD
D_FF = 4 * D_MODEL
QBLK = 128
ROPE_THETA = 10000.0
EPS = 1e-6
QK_SCALE = 1.0 / math.sqrt(HEAD)
LOG2E = math.log2(math.e)

LANES = 128
MXU_DIM = 256
VMEM_LIMIT = 56 * 1024 * 1024

W_A = SWA_Q + 2 * SWA_KV
W_B = 3 * FOX_W
SEC_C = 0
SEC_CO = SEC_C + 768
SEC_G = SEC_CO + 256
W_C = SEC_G + LANES
GATE_BF = 0
GATE_CI = 12
GATE_CF = 16
IN_WIDTH = W_A + W_B + FOX_HEADS + 4 * MLSTM_W + 2 * MLSTM_HEADS


def _perm_a():
    p = np.zeros((W_A, W_A), np.float32)
    lane = np.arange(LANES)
    quarter, i = lane // HALF, lane % HALF
    dim, is_b = i + HALF * (quarter // 2), quarter % 2
    for j in range(SWA_HEADS // 2):
        p[(j + 4 * is_b) * HEAD + dim, j * LANES + lane] = 1.0
    p[SWA_Q + is_b * HEAD + dim, SWA_Q + lane] = 1.0
    p[SWA_Q + SWA_KV + lane, SWA_Q + SWA_KV + lane] = 1.0
    return p


def _perm_c():
    src = IN_WIDTH - W_A - W_B
    o_bf, o_cqkv = 0, FOX_HEADS
    o_ci = o_cqkv + 3 * MLSTM_W
    o_cf, o_co = o_ci + MLSTM_HEADS, o_ci + 2 * MLSTM_HEADS
    p = np.zeros((src, W_C), np.float32)
    w = np.arange(3 * MLSTM_W)
    p[o_cqkv + w, SEC_C + w] = 1.0
    w = np.arange(MLSTM_W)
    p[o_co + w, SEC_CO + w] = 1.0
    h = np.arange(FOX_HEADS)
    for rep in range(3):
        p[o_bf + h, SEC_G + GATE_BF + rep * FOX_HEADS + h] = 1.0
    h = np.arange(MLSTM_HEADS)
    p[o_ci + h, SEC_G + GATE_CI + h] = 1.0
    p[o_cf + h, SEC_G + GATE_CF + h] = 1.0
    return p


def _gate_expand():
    e = np.zeros((LANES, 2 * MLSTM_W), np.float32)
    for h in range(MLSTM_HEADS):
        e[GATE_CI + h, h * HEAD:(h + 1) * HEAD] = 1.0
        e[GATE_CF + h, MLSTM_W + h * HEAD:MLSTM_W + (h + 1) * HEAD] = 1.0
    return np.concatenate([e, e, e], axis=0)


_PERM_A = _perm_a()
_PERM_C = _perm_c()


def _block_diag_ones(head_of_lane):
    return (head_of_lane[:, None] == head_of_lane[None, :]).astype(np.float32)


_TRI256 = np.tril(np.ones((MXU_DIM, MXU_DIM), np.float32))
_TRI64 = np.tril(np.ones((CHUNK, CHUNK), np.float32))
_w256 = np.arange(MLSTM_W)
_BD256 = _block_diag_ones(_w256 // HEAD)
_I4 = (np.arange(CHUNK)[:, None] == (_w256 % HEAD)[None, :]).astype(np.float32)


def _swa_band_cap():
    key_chunk = np.arange(2 * QBLK)[:, None] // CHUNK
    qry_chunk = (np.arange(MXU_DIM)[None, :] % QBLK) // CHUNK
    ahead = key_chunk - qry_chunk
    return np.where((ahead >= 0) & (ahead <= 2), np.inf, -np.inf).astype(np.float32)


def _const_spec(shape):
    nd = len(shape)
    return pl.BlockSpec(shape, lambda *_: (0,) * nd, pipeline_mode=pl.Buffered(1))


def _layer_spec(shape, layer):
    nd = len(shape)
    return pl.BlockSpec((None,) + tuple(shape), lambda *_: (layer,) + (0,) * nd,
                        pipeline_mode=pl.Buffered(1))


def _log_sigmoid(x):
    return jnp.minimum(x, 0.0) - jnp.log(1.0 + jnp.exp(-jnp.abs(x)))


def _split2(x):
    hi = x.astype(BF16)
    lo = (x - hi.astype(F32)).astype(BF16)
    return hi, lo


def _split3_by_lane(x, lane):
    hi = x.astype(BF16).astype(F32)
    r1 = x - hi
    mid = r1.astype(BF16).astype(F32)
    low = r1 - mid
    return jnp.where(lane < 4, hi, jnp.where(lane < 8, mid, low)).astype(BF16)


def _dot(a, b):
    return jnp.dot(a, b, preferred_element_type=F32)


def _dot_nt(a, b):
    return lax.dot_general(a, b, (((1,), (1,)), ((), ())), preferred_element_type=F32)


def _dot_tn(a, b):
    return lax.dot_general(a, b, (((0,), (0,)), ((), ())), preferred_element_type=F32)


TM_IN = 1024


def _in_proj_body(x_ref, n1_ref, wa_ref, wb_ref, wc_ref, vec_ref, cos_ref, sin_ref, tri_ref, e3_ref,
                  qa_ref, kva_ref, qb_ref, kb_ref, vb_ref, qkvc_ref, gc_ref, carry_ref):
    tm = x_ref.shape[1]

    @pl.when(pl.program_id(1) == 0)
    def _():
        carry_ref[...] = jnp.zeros_like(carry_ref)

    x = x_ref[0]
    ms = jnp.mean(x * x, axis=-1, keepdims=True)
    h = ((x * lax.rsqrt(ms + EPS)) * n1_ref[...]).astype(BF16)

    def proj(w_ref, c0, width):
        return _dot(h, w_ref[:, c0:c0 + width])

    lane_row = lax.broadcasted_iota(jnp.int32, (1, LANES), 1)
    first_swa = (lane_row // HALF) % 2 == 0
    first_nat = lane_row < HEAD

    def head_norm(z, first, gain):
        sq = z * z
        s_first = jnp.sum(jnp.where(first, sq, 0.0), axis=-1, keepdims=True)
        s_second = jnp.sum(jnp.where(first, 0.0, sq), axis=-1, keepdims=True)
        ss = jnp.where(first, s_first, s_second)
        return (z * lax.rsqrt(ss * (1.0 / HEAD) + EPS)) * gain

    cos = cos_ref[...]
    sin = sin_ref[...]

    def rope(z):
        return z * cos + pltpu.roll(z, HEAD, 1) * sin

    g_aq = vec_ref[0:1, 0:LANES]
    g_ak = vec_ref[1:2, 0:LANES]
    g_bq = vec_ref[2:3, 0:LANES]
    g_bk = vec_ref[3:4, 0:LANES]
    gate_bias = vec_ref[4:5, 0:LANES]

    lane = lax.broadcasted_iota(jnp.int32, (MXU_DIM, LANES), 1)
    zg = proj(wc_ref, SEC_G, LANES) + gate_bias
    log_sig = _log_sigmoid(zg)
    log_f = jnp.where(lane[0:1] < GATE_CI, log_sig, 0.0)
    f_terms = [_split3_by_lane(log_f[blk * MXU_DIM:(blk + 1) * MXU_DIM], lane)
               for blk in range(tm // MXU_DIM)]
    gz = jnp.where(lane[0:1] < GATE_CI, 0.0,
                   jnp.where(lane[0:1] < GATE_CF, zg,
                             jnp.where(lane[0:1] < GATE_CF + MLSTM_HEADS, log_sig, 0.0)))
    gz_hi = gz.astype(BF16)
    gz_r = gz - gz_hi.astype(F32)
    gz_mid = gz_r.astype(BF16)
    gz_terms = jnp.concatenate([gz_hi, gz_mid, (gz_r - gz_mid.astype(F32)).astype(BF16)], axis=1)

    za = proj(wa_ref, 0, W_A)
    for j in range(SWA_HEADS // 2):
        q = rope(head_norm(za[:, j * LANES:(j + 1) * LANES], first_swa, g_aq)) * (QK_SCALE * LOG2E)
        qa_ref[0, :, j * LANES:(j + 1) * LANES] = q.astype(BF16)
    k = rope(head_norm(za[:, SWA_Q:SWA_Q + LANES], first_swa, g_ak))
    kva_ref[0, :, 0:LANES] = k.astype(BF16)
    kva_ref[0, :, LANES:2 * LANES] = za[:, SWA_Q + SWA_KV:].astype(BF16)

    zb = proj(wb_ref, 0, W_B)
    for p in range(FOX_HEADS // 2):
        q = head_norm(zb[:, p * LANES:(p + 1) * LANES], first_nat, g_bq) * (QK_SCALE * LOG2E)
        qb_ref[0, :, p * LANES:(p + 1) * LANES] = q.astype(BF16)
        k = head_norm(zb[:, FOX_W + p * LANES:FOX_W + (p + 1) * LANES], first_nat, g_bk)
        kb_ref[0, :, 2 * p * LANES:(2 * p + 1) * LANES] = k.astype(BF16)
    vb_ref[0] = zb[:, 2 * FOX_W:].astype(BF16)

    zc = proj(wc_ref, SEC_C, 768)
    qkvc_ref[0, :, 0:MLSTM_W] = zc[:, 0:MLSTM_W].astype(BF16)
    qkvc_ref[0, :, MLSTM_W:2 * MLSTM_W] = (zc[:, MLSTM_W:2 * MLSTM_W] * QK_SCALE).astype(BF16)
    qkvc_ref[0, :, 2 * MLSTM_W:] = zc[:, 2 * MLSTM_W:].astype(BF16)
    gc_ref[0, :, 2 * MLSTM_W:] = proj(wc_ref, SEC_CO, MLSTM_W)

    gc_ref[0, :, 0:2 * MLSTM_W] = _dot(gz_terms, e3_ref[...])

    carry = carry_ref[0:1, :]
    for blk in range(tm // MXU_DIM):
        rows = slice(blk * MXU_DIM, (blk + 1) * MXU_DIM)
        cs = _dot(tri_ref[...], f_terms[blk])
        f_loc = cs + pltpu.roll(cs, LANES - 4, 1) + pltpu.roll(cs, LANES - 8, 1)
        f_cum = jnp.where(lane < 4, f_loc, 0.0) + carry
        carry = f_cum[MXU_DIM - 1:MXU_DIM, :]
        f_rep = (f_cum + pltpu.roll(f_cum, 4, 1) + pltpu.roll(f_cum, 8, 1)) * LOG2E
        aug = _split3_by_lane(f_rep, lane)
        kb_ref[0, rows, LANES:2 * LANES] = aug
        kb_ref[0, rows, 3 * LANES:4 * LANES] = aug
    carry_ref[...] = jnp.broadcast_to(carry, carry_ref.shape)


def _in_proj(layer, x, n1, w_a, w_b, w_c, vecs, cos, sin, tri, e3):
    B, S, _ = x.shape
    tm = min(TM_IN, S)
    grid = (B, S // tm)
    tok = lambda w: pl.BlockSpec((1, tm, w), lambda b, i: (b, i, 0))
    out_shapes = (
        jax.ShapeDtypeStruct((B, S, SWA_Q), BF16),
        jax.ShapeDtypeStruct((B, S, 2 * SWA_KV), BF16),
        jax.ShapeDtypeStruct((B, S, FOX_W), BF16),
        jax.ShapeDtypeStruct((B, S, 2 * MXU_DIM), BF16),
        jax.ShapeDtypeStruct((B, S, FOX_W), BF16),
        jax.ShapeDtypeStruct((B, S, 3 * MLSTM_W), BF16),
        jax.ShapeDtypeStruct((B, S, 3 * MLSTM_W), F32),
    )
    return pl.pallas_call(
        _in_proj_body,
        grid=grid,
        in_specs=[
            tok(D_MODEL),
            _layer_spec((1, D_MODEL), layer),
            _layer_spec((D_MODEL, W_A), layer),
            _layer_spec((D_MODEL, W_B), layer),
            _layer_spec((D_MODEL, W_C), layer),
            _layer_spec((8, MLSTM_W), layer),
            pl.BlockSpec((tm, LANES), lambda b, i: (i, 0)),
            pl.BlockSpec((tm, LANES), lambda b, i: (i, 0)),
            _const_spec((MXU_DIM, MXU_DIM)),
            _const_spec((3 * LANES, 2 * MLSTM_W)),
        ],
        out_specs=[tok(s.shape[-1]) for s in out_shapes],
        out_shape=out_shapes,
        scratch_shapes=[pltpu.VMEM((8, LANES), F32)],
        compiler_params=pltpu.CompilerParams(
            dimension_semantics=("arbitrary", "arbitrary"), vmem_limit_bytes=VMEM_LIMIT),
        name="in_proj",
    )(x, n1, w_a, w_b, w_c, vecs, cos, sin, tri, e3)


TS_SWA = 1024
SWA_DEPTH = 6
VT_ROWS = HEAD + 16


def _swa_body(sink_base, sink_ref, q_ref, kvc_ref, kvp_ref, eye_ref, band_ref, o_ref):
    step = pl.program_id(1)
    nblk = q_ref.shape[1] // QBLK
    per_group = SWA_HEADS // SWA_KV_HEADS
    eye = eye_ref[0:LANES, 0:LANES]

    band_cap = band_ref[...]
    key = lax.broadcasted_iota(jnp.int32, (2 * QBLK, MXU_DIM), 0)
    first_cap = jnp.where((key >= QBLK) | (step > 0), band_cap, -jnp.inf)
    lane = lax.broadcasted_iota(jnp.int32, (QBLK, LANES), 1)
    lane_a = (lane // HALF) % 2 == 0
    first_head = lax.broadcasted_iota(jnp.int32, (1, MXU_DIM), 1) < QBLK
    zero = jnp.zeros((QBLK, LANES), BF16)
    ones = jnp.ones((VT_ROWS - HEAD, 2 * QBLK), BF16)
    kv_all = jnp.concatenate([kvp_ref[0], kvc_ref[0]], axis=0)
    rows = [slice(blk * QBLK, (blk + 1) * QBLK) for blk in range(nblk)]
    bands = [kv_all[blk * QBLK:(blk + 2) * QBLK] for blk in range(nblk)]

    vt_aug = []
    for blk in range(nblk):
        v_t = _dot_nt(eye, bands[blk][:, LANES:]).astype(BF16)
        vt_aug.append([jnp.concatenate([v_t[g * HEAD:(g + 1) * HEAD], ones], axis=0)
                       for g in range(SWA_KV_HEADS)])

    units = [(blk, g, part) for blk in range(nblk) for g in range(SWA_KV_HEADS) for part in range(2)]

    def score(blk, g, part):
        keep = lane_a if g == 0 else jnp.logical_not(lane_a)
        q_cols = [q_ref[0, rows[blk], j * LANES:(j + 1) * LANES] for j in (2 * part, 2 * part + 1)]
        q_g = jnp.concatenate([jnp.where(keep, q2, zero) for q2 in q_cols], axis=0)
        return _dot_nt(bands[blk][:, 0:LANES], q_g)

    def fold(s, blk, g, part):
        s = jnp.minimum(s, band_cap if blk > 0 else first_cap)
        head = sink_base + g * per_group + 2 * part
        sink = jnp.where(first_head, sink_ref[head], sink_ref[head + 1]) * LOG2E
        m = jnp.maximum(jnp.max(s, axis=0, keepdims=True), sink)
        p = jnp.exp2(s - m)
        o_aug = _dot(vt_aug[blk][g], p.astype(BF16))
        den = o_aug[HEAD:HEAD + 1] + jnp.exp2(sink - m)
        return (o_aug[0:HEAD] / den).astype(BF16)

    def write_block(blk, outs):
        for g in range(SWA_KV_HEADS):
            for part in range(2):
                o = outs[(blk, g, part)]
                pair_t = jnp.concatenate([o[:, 0:QBLK], o[:, QBLK:]], axis=0)
                j = 2 * g + part
                o_ref[0, rows[blk], j * LANES:(j + 1) * LANES] = _dot_nt(eye, pair_t).astype(BF16)

    outs = {}
    queue = [score(*unit) for unit in units[:SWA_DEPTH]]
    for n, unit in enumerate(units):
        s_cur = queue.pop(0)
        if n + SWA_DEPTH < len(units):
            queue.append(score(*units[n + SWA_DEPTH]))
        outs[unit] = fold(s_cur, *unit)
        blk = unit[0]
        if unit[1:] == (SWA_KV_HEADS - 1, 1) and blk > 0:
            write_block(blk - 1, outs)
    write_block(nblk - 1, outs)


def _swa(layer, sinks, qa, kva, eye, band):
    B, S, _ = qa.shape
    ts = min(TS_SWA, S)
    per_step = ts // QBLK
    return pl.pallas_call(
        functools.partial(_swa_body, layer * SWA_HEADS),
        grid=(B, S // ts),
        in_specs=[
            pl.BlockSpec(memory_space=pltpu.SMEM),
            pl.BlockSpec((1, ts, SWA_Q), lambda b, i: (b, i, 0)),
            pl.BlockSpec((1, ts, 2 * SWA_KV), lambda b, i: (b, i, 0)),
            pl.BlockSpec((1, QBLK, 2 * SWA_KV), lambda b, i: (b, jnp.maximum(i * per_step - 1, 0), 0)),
            _const_spec((MXU_DIM, MXU_DIM)),
            _const_spec((2 * QBLK, MXU_DIM)),
        ],
        out_specs=pl.BlockSpec((1, ts, SWA_Q), lambda b, i: (b, i, 0)),
        out_shape=jax.ShapeDtypeStruct((B, S, SWA_Q), BF16),
        compiler_params=pltpu.CompilerParams(
            dimension_semantics=("arbitrary", "arbitrary"), vmem_limit_bytes=VMEM_LIMIT),
        name="swa",
    )(sinks, qa, kva, kva, eye, band)


TQ_FOX = 1024
TK_FOX = 256
NEG_BIG = -1e30


FOX_UNROLL = 4
FOX_DEPTH = 6


def _fox_body(q_ref, k_ref, v_ref, eye_ref, o_ref, vt_ref, qt_ref, acc_ref):
    tq = q_ref.shape[1]
    tk = TK_FOX
    S = k_ref.shape[1]
    pair = pl.program_id(1)
    qi = pl.program_id(2)
    eye = eye_ref[...]
    n_tiles = 2 * tq // MXU_DIM
    tiles_per_head = tq // MXU_DIM

    @pl.when(qi == 0)
    def _():
        ones = jnp.ones((VT_ROWS - HEAD, tk), BF16)
        for j in range(S // tk):
            blk = slice(j * tk, (j + 1) * tk)
            v_t = _dot_nt(eye[0:LANES, 0:LANES], v_ref[0, blk, :]).astype(BF16)
            vt_ref[:, blk] = jnp.concatenate([v_t[0:HEAD], ones, v_t[HEAD:], ones], axis=0)

    lane = lax.broadcasted_iota(jnp.int32, (tq, LANES), 1)
    lane_a = lane < HEAD
    q2 = q_ref[0]
    zero = jnp.zeros_like(q2)

    def minus_one_at(head):
        hit = (lane == head) | (lane == head + 4) | (lane == head + 8)
        return jnp.where(hit, -1.0, 0.0).astype(BF16)

    q_aug = [jnp.concatenate([jnp.where(lane_a, q2, zero), minus_one_at(2 * pair)], axis=1),
             jnp.concatenate([jnp.where(lane_a, zero, q2), minus_one_at(2 * pair + 1)], axis=1)]
    for t in range(n_tiles):
        hd, part = divmod(t, tiles_per_head)
        rows = q_aug[hd][part * MXU_DIM:(part + 1) * MXU_DIM]
        qt_ref[:, t * MXU_DIM:(t + 1) * MXU_DIM] = _dot_nt(eye, rows).astype(BF16)
    acc_ref[...] = jnp.zeros_like(acc_ref)

    key = lax.broadcasted_iota(jnp.int32, (tk // 2, MXU_DIM), 0)
    qry = lax.broadcasted_iota(jnp.int32, (tk // 2, MXU_DIM), 1)

    def fold(s, t, vt_blk, m, triangular):
        hd = t // tiles_per_head
        vt_h = vt_blk[hd * VT_ROWS:(hd + 1) * VT_ROWS]
        half = tk // 2
        for h in range(2):
            sh = s[h * half:(h + 1) * half]
            if triangular:
                valid = key + h * half <= qry
                sh = jnp.where(valid, sh, NEG_BIG)
            m_new = jnp.maximum(m, jnp.max(sh, axis=0, keepdims=True))
            alpha = jnp.exp2(m - m_new)
            p = jnp.exp2(sh - m_new)
            if triangular:
                p = jnp.where(valid, p, 0.0)
            acc_ref[t] = alpha * acc_ref[t] + _dot(vt_h[:, h * half:(h + 1) * half], p.astype(BF16))
            m = m_new
        return m

    def region(j0, ms, plans):
        k_blks, vt_blks, todo = [], [], []
        for b, plan in enumerate(plans):
            r0 = pl.multiple_of((j0 + b) * tk, tk)
            k_blks.append(k_ref[0, pl.ds(r0, tk), :])
            vt_blks.append(vt_ref[:, pl.ds(r0, tk)])
            todo += [(b, t) for t in range(n_tiles) if plan[t] is not None]
        score = lambda b, t: _dot(k_blks[b], qt_ref[:, t * MXU_DIM:(t + 1) * MXU_DIM])
        ms = list(ms)
        queue = [score(*unit) for unit in todo[:FOX_DEPTH]]
        for n, (b, t) in enumerate(todo):
            s_cur = queue.pop(0)
            if n + FOX_DEPTH < len(todo):
                queue.append(score(*todo[n + FOX_DEPTH]))
            ms[t] = fold(s_cur, t, vt_blks[b], ms[t], plans[b][t])
        return tuple(ms)

    blocks_per_tile = tq // tk
    full = (False,) * n_tiles
    n_full = blocks_per_tile * qi
    m_init = jnp.full((1, MXU_DIM), NEG_BIG, F32)
    ms = lax.fori_loop(0, n_full // FOX_UNROLL,
                       lambda g, c: region(g * FOX_UNROLL, c, (full,) * FOX_UNROLL), (m_init,) * n_tiles)
    if FOX_UNROLL > blocks_per_tile:
        done = (n_full // FOX_UNROLL) * FOX_UNROLL
        ms = lax.fori_loop(0, (n_full - done) // blocks_per_tile,
                           lambda g, c: region(done + g * blocks_per_tile, c, (full,) * blocks_per_tile), ms)
    diag = [tuple(None if t % tiles_per_head < d else t % tiles_per_head == d for t in range(n_tiles))
            for d in range(blocks_per_tile)]
    ms = region(n_full, ms, diag)

    for hd in range(2):
        cols = []
        for part in range(tiles_per_head):
            acc = acc_ref[hd * tiles_per_head + part]
            cols.append(acc[0:HEAD] / acc[HEAD:HEAD + 1])
        o_h = jnp.concatenate(cols, axis=1)
        if hd == 0:
            o_t = o_h
        else:
            o_t = jnp.concatenate([o_t, o_h], axis=0).astype(BF16)
    for c in range(tq // MXU_DIM):
        rows = slice(c * MXU_DIM, (c + 1) * MXU_DIM)
        o_ref[0, rows, :] = _dot_nt(eye, o_t[:, rows]).astype(BF16)


def _fox(qb, kb, vb, eye):
    B, S, _ = qb.shape
    tq = min(TQ_FOX, S)
    return pl.pallas_call(
        _fox_body,
        grid=(B, FOX_HEADS // 2, S // tq),
        in_specs=[
            pl.BlockSpec((1, tq, LANES), lambda b, p, i: (b, i, p)),
            pl.BlockSpec((1, S, MXU_DIM), lambda b, p, i: (b, 0, p)),
            pl.BlockSpec((1, S, LANES), lambda b, p, i: (b, 0, p)),
            _const_spec((MXU_DIM, MXU_DIM)),
        ],
        out_specs=pl.BlockSpec((1, tq, LANES), lambda b, p, i: (b, i, p)),
        out_shape=jax.ShapeDtypeStruct((B, S, FOX_W), BF16),
        scratch_shapes=[
            pltpu.VMEM((2 * VT_ROWS, S), BF16),
            pltpu.VMEM((MXU_DIM, 2 * tq), BF16),
            pltpu.VMEM((2 * tq // MXU_DIM, VT_ROWS, MXU_DIM), F32),
        ],
        compiler_params=pltpu.CompilerParams(
            dimension_semantics=("arbitrary", "arbitrary", "arbitrary"),
            vmem_limit_bytes=VMEM_LIMIT),
        name="fox",
    )(qb, kb, vb, eye)


TS_MLSTM = 1024


def _mlstm_body(qkv_ref, g_ref, onorm_ref, tri_ref, i4_ref, bd_ref, o_ref, c_ref, n_ref, m_ref):
    ts = qkv_ref.shape[1]
    W = MLSTM_W

    @pl.when(pl.program_id(1) == 0)
    def _():
        c_ref[...] = jnp.zeros_like(c_ref)
        n_ref[...] = jnp.zeros_like(n_ref)
        m_ref[...] = jnp.zeros_like(m_ref)

    row = lax.broadcasted_iota(jnp.int32, (CHUNK, W), 0)
    lane = lax.broadcasted_iota(jnp.int32, (CHUNK, W), 1)
    causal = (lane % HEAD) <= row
    lane_head = lane // HEAD
    bd = bd_ref[...]
    bd2 = jnp.concatenate([bd, bd], axis=1)
    i4_bf = i4_ref[...].astype(BF16)

    def block_diag(x):
        tiled = jnp.concatenate([x] * MLSTM_HEADS, axis=0)
        return tiled * (bd if x.shape[1] == W else bd2)

    def seg_max(d):
        out = jnp.zeros_like(d)
        for hd in range(MLSTM_HEADS):
            sel = lane_head == hd
            mx = jnp.max(jnp.where(sel, d, -jnp.inf), axis=-1, keepdims=True)
            out = jnp.where(sel, mx, out)
        return out

    chunks = range(ts // CHUNK)
    rows = [pl.ds(c * CHUNK, CHUNK) for c in chunks]

    b, g, u, dm, dm_max, a_max, wa = [], [], [], [], [], [], []
    for c in chunks:
        ig = g_ref[0, rows[c], 0:W]
        lf_hi, lf_lo = _split2(g_ref[0, rows[c], W:2 * W])
        b_c = _dot(tri_ref[...], lf_hi) + _dot(tri_ref[...], lf_lo)
        g_c = b_c[CHUNK - 1:CHUNK, :]
        u_c = b_c - ig
        u_row = jnp.sum(u_c * i4_ref[...], axis=0, keepdims=True)
        dm_c = jnp.where(causal, b_c - u_row, -jnp.inf)
        a_c = g_c - u_c
        am_c = jnp.max(a_c, axis=0, keepdims=True)
        b.append(b_c); g.append(g_c); u.append(u_c); dm.append(dm_c)
        dm_max.append(seg_max(dm_c)); a_max.append(am_c); wa.append(jnp.exp(a_c - am_c))

    m_prev, s_old, s_loc = [m_ref[0:1, :]], [], []
    for c in chunks:
        gm = g[c] + m_prev[c]
        m_new = jnp.maximum(gm, a_max[c])
        s_old.append(jnp.exp(gm - m_new))
        s_loc.append(jnp.exp(a_max[c] - m_new))
        m_prev.append(m_new)

    m_t, w_inter, sm_hi, sm_lo, k_t = [], [], [], [], []
    for c in chunks:
        inter = b[c] + m_prev[c]
        mt_c = jnp.maximum(inter, dm_max[c])
        decay = jnp.exp(dm[c] - mt_c)
        q = qkv_ref[0, rows[c], 0:W]
        k = qkv_ref[0, rows[c], W:2 * W]
        qk = _dot_nt(jnp.concatenate([q, i4_bf], axis=0), block_diag(k))
        hi, lo = _split2(qk[0:CHUNK] * decay)
        m_t.append(mt_c); w_inter.append(jnp.exp(inter - mt_c)); sm_hi.append(hi); sm_lo.append(lo)
        k_t.append(qk[CHUNK:].astype(BF16))

    c_cur, n_cur = c_ref[...], n_ref[...]
    state_bf = []
    for c in chunks:
        state_bf.append(jnp.concatenate([c_cur.astype(BF16), n_cur.astype(BF16)], axis=1))
        v = qkv_ref[0, rows[c], 2 * W:]
        weighted = jnp.concatenate([(wa[c] * v.astype(F32)).astype(BF16), wa[c].astype(BF16)], axis=1)
        loc = _dot(k_t[c], block_diag(weighted))
        c_cur = s_old[c] * c_cur + s_loc[c] * loc[:, 0:W]
        n_cur = s_old[c] * n_cur + s_loc[c] * loc[:, W:]
    c_ref[...] = c_cur
    n_ref[...] = n_cur
    m_ref[...] = jnp.broadcast_to(m_prev[-1], m_ref.shape)

    hval = []
    for c in chunks:
        q = qkv_ref[0, rows[c], 0:W]
        v = qkv_ref[0, rows[c], 2 * W:]
        inter_cn = _dot(q, block_diag(state_bf[c]))
        intra = _dot(sm_hi[c], jnp.concatenate([block_diag(v), bd], axis=1))
        num = w_inter[c] * inter_cn[:, 0:W] + intra[:, 0:W]
        den = w_inter[c] * inter_cn[:, W:] + intra[:, W:] + _dot(sm_lo[c], bd)
        hval.append(num / jnp.maximum(jnp.abs(den), jnp.exp(-m_t[c])))

    for c in chunks:
        hs_hi, hs_lo = _split2(hval[c] * hval[c])
        ss = _dot(hs_hi, bd) + _dot(hs_lo, bd)
        hc = (hval[c] * lax.rsqrt(ss * (1.0 / HEAD) + EPS)) * onorm_ref[...]
        co = g_ref[0, rows[c], 2 * W:]
        o_ref[0, rows[c], :] = (jax.nn.sigmoid(co) * hc).astype(BF16)


def _mlstm(layer, qkvc, gc, onorm, tri, i4, bd):
    B, S, _ = qkvc.shape
    ts = min(TS_MLSTM, S)
    return pl.pallas_call(
        _mlstm_body,
        grid=(B, S // ts),
        in_specs=[
            pl.BlockSpec((1, ts, 3 * MLSTM_W), lambda b, i: (b, i, 0)),
            pl.BlockSpec((1, ts, 3 * MLSTM_W), lambda b, i: (b, i, 0)),
            _layer_spec((1, MLSTM_W), layer),
            _const_spec((CHUNK, CHUNK)),
            _const_spec((CHUNK, MLSTM_W)),
            _const_spec((MLSTM_W, MLSTM_W)),
        ],
        out_specs=pl.BlockSpec((1, ts, MLSTM_W), lambda b, i: (b, i, 0)),
        out_shape=jax.ShapeDtypeStruct((B, S, MLSTM_W), BF16),
        scratch_shapes=[pltpu.VMEM((HEAD, MLSTM_W), F32), pltpu.VMEM((HEAD, MLSTM_W), F32),
                        pltpu.VMEM((8, MLSTM_W), F32)],
        compiler_params=pltpu.CompilerParams(
            dimension_semantics=("arbitrary", "arbitrary"), vmem_limit_bytes=VMEM_LIMIT),
        name="mlstm",
    )(qkvc, gc, onorm, tri, i4, bd)


TM_FFN = 512
FF_CHUNK = 1024


def _out_ffn_body(x_ref, ya_ref, yb_ref, yc_ref, wo_ref, n2_ref, w1_ref, w2_ref, o_ref, a_ref):
    x1 = (x_ref[...] + _dot(ya_ref[...], wo_ref[0:SWA_Q, :])
          + _dot(yb_ref[...], wo_ref[SWA_Q:SWA_Q + FOX_W, :])
          + _dot(yc_ref[...], wo_ref[SWA_Q + FOX_W:, :]))
    ms = jnp.mean(x1 * x1, axis=-1, keepdims=True)
    h2 = ((x1 * lax.rsqrt(ms + EPS)) * n2_ref[...]).astype(BF16)
    for j in range(D_FF // FF_CHUNK):
        cols = slice(j * FF_CHUNK, (j + 1) * FF_CHUNK)
        r = jnp.maximum(_dot(h2, w1_ref[:, cols]), 0.0)
        a_ref[:, cols] = (r * r).astype(BF16)
    o_ref[...] = x1 + _dot(a_ref[...], w2_ref[...])


def _out_ffn(layer, x2d, ya, yb, yc, wo, n2, w1, w2):
    T = x2d.shape[0]
    tm = min(TM_FFN, T)
    tok = lambda w: pl.BlockSpec((tm, w), lambda i: (i, 0))
    return pl.pallas_call(
        _out_ffn_body,
        grid=(T // tm,),
        in_specs=[
            tok(D_MODEL), tok(SWA_Q), tok(FOX_W), tok(MLSTM_W),
            _layer_spec((D_MODEL, D_MODEL), layer),
            _layer_spec((1, D_MODEL), layer),
            _layer_spec((D_MODEL, D_FF), layer),
            _layer_spec((D_FF, D_MODEL), layer),
        ],
        out_specs=tok(D_MODEL),
        out_shape=jax.ShapeDtypeStruct((T, D_MODEL), F32),
        scratch_shapes=[pltpu.VMEM((tm, D_FF), BF16)],
        compiler_params=pltpu.CompilerParams(
            dimension_semantics=("arbitrary",), vmem_limit_bytes=VMEM_LIMIT),
        name="out_ffn",
    )(x2d, ya, yb, yc, wo, n2, w1, w2)


def _rope_tables(S):
    inv = ROPE_THETA ** (-np.arange(HALF, dtype=np.float64) / HALF)
    ang = np.arange(S, dtype=np.float64)[:, None] * inv[None, :]
    cos = np.tile(np.cos(ang), (1, LANES // HALF))
    sin = np.tile(np.sin(ang), (1, LANES // HALF))
    sign = np.where(np.arange(LANES) < HEAD, -1.0, 1.0)
    return jnp.asarray(cos, F32), jnp.asarray(sin * sign[None, :], F32)


def _prep_w_in(w_in):
    select = lambda w, p: jnp.einsum("ldk,kn->ldn", w.astype(BF16), jnp.asarray(p, BF16),
                                     preferred_element_type=BF16)
    w_b = w_in[:, :, W_A:W_A + W_B].astype(BF16)
    return select(w_in[:, :, :W_A], _PERM_A), w_b, select(w_in[:, :, W_A + W_B:], _PERM_C)


def _prep_vecs(swa_q_norm, swa_k_norm, fox_q_norm, fox_k_norm, fox_f_bias, mlstm_i_bias, mlstm_f_bias):
    L = swa_q_norm.shape[0]

    def swa_lanes(g):
        return jnp.concatenate([g[:, :HALF], g[:, :HALF], g[:, HALF:], g[:, HALF:]], axis=-1)

    pad = lambda v: jnp.pad(v, ((0, 0), (0, MLSTM_W - v.shape[1])))
    return jnp.stack([
        pad(swa_lanes(swa_q_norm)), pad(swa_lanes(swa_k_norm)),
        pad(jnp.tile(fox_q_norm, (1, 2))), pad(jnp.tile(fox_k_norm, (1, 2))),
        pad(jnp.concatenate([jnp.tile(fox_f_bias, (1, 3)), mlstm_i_bias, mlstm_f_bias], axis=-1)),
        jnp.zeros((L, MLSTM_W), F32), jnp.zeros((L, MLSTM_W), F32), jnp.zeros((L, MLSTM_W), F32),
    ], axis=1).astype(F32)


def kernel(x, norm1, w_in, swa_q_norm, swa_k_norm, swa_sinks, fox_q_norm, fox_k_norm, fox_f_bias,
           mlstm_i_bias, mlstm_f_bias, mlstm_out_norm, w_out, norm2, w_ff1, w_ff2):
    B, S, D = x.shape
    L = norm1.shape[0]
    T = B * S
    cos, sin = _rope_tables(S)
    tri256, tri64, bd256 = (jnp.asarray(c, BF16) for c in (_TRI256, _TRI64, _BD256))
    i4 = jnp.asarray(_I4, F32)
    e3 = jnp.asarray(_gate_expand(), BF16)
    eye256 = jnp.eye(MXU_DIM, dtype=BF16)
    band = jnp.asarray(_swa_band_cap(), F32)

    w_a, w_b, w_c = _prep_w_in(w_in)
    wo = w_out.astype(BF16)
    w1 = w_ff1.astype(BF16)
    w2 = w_ff2.astype(BF16)
    vecs = _prep_vecs(swa_q_norm, swa_k_norm, fox_q_norm, fox_k_norm, fox_f_bias, mlstm_i_bias,
                      mlstm_f_bias)
    n1 = norm1.reshape(L, 1, D).astype(F32)
    n2 = norm2.reshape(L, 1, D).astype(F32)
    onorm = mlstm_out_norm.reshape(L, 1, MLSTM_W).astype(F32)
    sinks = swa_sinks.reshape(L * SWA_HEADS).astype(F32)

    for l in range(L):
        qa, kva, qb, kb, vb, qkvc, gc = _in_proj(l, x, n1, w_a, w_b, w_c, vecs, cos, sin, tri256, e3)
        ya = _swa(l, sinks, qa, kva, eye256, band)
        yb = _fox(qb, kb, vb, eye256)
        yc = _mlstm(l, qkvc, gc, onorm, tri64, i4, bd256)
        x = _out_ffn(l, x.reshape(T, D), ya.reshape(T, SWA_Q), yb.reshape(T, FOX_W),
                     yc.reshape(T, MLSTM_W), wo, n2, w1, w2).reshape(B, S, D)
    return x
```

```python
import functools
import math

import numpy as np
import jax
import jax.numpy as jnp
from jax import lax
from jax.experimental import pallas as pl
from jax.experimental.pallas import tpu as pltpu

F32 = jnp.float32
BF16 = jnp.bfloat16

D_MODEL = 1024
HEAD = 64
HALF = HEAD // 2
CHUNK = 64
SWA_HEADS = 8
SWA_KV_HEADS = 2
FOX_HEADS = 4
MLSTM_HEADS = 4
SWA_Q = SWA_HEADS * HEAD
SWA_KV = SWA_KV_HEADS * HEAD
FOX_W = FOX_HEADS * HEAD
MLSTM_W = MLSTM_HEADS * HEAD
D_FF = 4 * D_MODEL
QBLK = 128
ROPE_THETA = 10000.0
EPS = 1e-6
QK_SCALE = 1.0 / math.sqrt(HEAD)
LOG2E = math.log2(math.e)

LANES = 128
MXU_DIM = 256
VMEM_LIMIT = 56 * 1024 * 1024

W_A = SWA_Q + 2 * SWA_KV
W_B = 3 * FOX_W
SEC_C = 0
SEC_CO = SEC_C + 768
SEC_G = SEC_CO + 256
W_C = SEC_G + LANES
GATE_BF = 0
GATE_CI = 12
GATE_CF = 16
IN_WIDTH = W_A + W_B + FOX_HEADS + 4 * MLSTM_W + 2 * MLSTM_HEADS


def _perm_a():
    p = np.zeros((W_A, W_A), np.float32)
    lane = np.arange(LANES)
    quarter, i = lane // HALF, lane % HALF
    dim, is_b = i + HALF * (quarter // 2), quarter % 2
    for j in range(SWA_HEADS // 2):
        p[(j + 4 * is_b) * HEAD + dim, j * LANES + lane] = 1.0
    p[SWA_Q + is_b * HEAD + dim, SWA_Q + lane] = 1.0
    p[SWA_Q + SWA_KV + lane, SWA_Q + SWA_KV + lane] = 1.0
    return p


def _perm_c():
    src = IN_WIDTH - W_A - W_B
    o_bf, o_cqkv = 0, FOX_HEADS
    o_ci = o_cqkv + 3 * MLSTM_W
    o_cf, o_co = o_ci + MLSTM_HEADS, o_ci + 2 * MLSTM_HEADS
    p = np.zeros((src, W_C), np.float32)
    w = np.arange(3 * MLSTM_W)
    p[o_cqkv + w, SEC_C + w] = 1.0
    w = np.arange(MLSTM_W)
    p[o_co + w, SEC_CO + w] = 1.0
    h = np.arange(FOX_HEADS)
    for rep in range(3):
        p[o_bf + h, SEC_G + GATE_BF + rep * FOX_HEADS + h] = 1.0
    h = np.arange(MLSTM_HEADS)
    p[o_ci + h, SEC_G + GATE_CI + h] = 1.0
    p[o_cf + h, SEC_G + GATE_CF + h] = 1.0
    return p


def _gate_expand():
    e = np.zeros((LANES, 2 * MLSTM_W), np.float32)
    for h in range(MLSTM_HEADS):
        e[GATE_CI + h, h * HEAD:(h + 1) * HEAD] = 1.0
        e[GATE_CF + h, MLSTM_W + h * HEAD:MLSTM_W + (h + 1) * HEAD] = 1.0
    return np.concatenate([e, e, e], axis=0)


_PERM_A = _perm_a()
_PERM_C = _perm_c()


def _block_diag_ones(head_of_lane):
    return (head_of_lane[:, None] == head_of_lane[None, :]).astype(np.float32)


_TRI256 = np.tril(np.ones((MXU_DIM, MXU_DIM), np.float32))
_TRI64 = np.tril(np.ones((CHUNK, CHUNK), np.float32))
_w256 = np.arange(MLSTM_W)
_BD256 = _block_diag_ones(_w256 // HEAD)
_I4 = (np.arange(CHUNK)[:, None] == (_w256 % HEAD)[None, :]).astype(np.float32)


def _swa_band_cap():
    key_chunk = np.arange(2 * QBLK)[:, None] // CHUNK
    qry_chunk = (np.arange(MXU_DIM)[None, :] % QBLK) // CHUNK
    ahead = key_chunk - qry_chunk
    return np.where((ahead >= 0) & (ahead <= 2), np.inf, -np.inf).astype(np.float32)


def _const_spec(shape):
    nd = len(shape)
    return pl.BlockSpec(shape, lambda *_: (0,) * nd, pipeline_mode=pl.Buffered(1))


def _layer_spec(shape, layer):
    nd = len(shape)
    return pl.BlockSpec((None,) + tuple(shape), lambda *_: (layer,) + (0,) * nd,
                        pipeline_mode=pl.Buffered(1))


def _log_sigmoid(x):
    return jnp.minimum(x, 0.0) - jnp.log(1.0 + jnp.exp(-jnp.abs(x)))


def _split2(x):
    hi = x.astype(BF16)
    lo = (x - hi.astype(F32)).astype(BF16)
    return hi, lo


def _split3_by_lane(x, lane):
    hi = x.astype(BF16).astype(F32)
    r1 = x - hi
    mid = r1.astype(BF16).astype(F32)
    low = r1 - mid
    return jnp.where(lane < 4, hi, jnp.where(lane < 8, mid, low)).astype(BF16)


def _dot(a, b):
    return jnp.dot(a, b, preferred_element_type=F32)


def _dot_nt(a, b):
    return lax.dot_general(a, b, (((1,), (1,)), ((), ())), preferred_element_type=F32)


def _dot_tn(a, b):
    return lax.dot_general(a, b, (((0,), (0,)), ((), ())), preferred_element_type=F32)


TM_IN = 1024


def _in_proj_body(x_ref, n1_ref, wa_ref, wb_ref, wc_ref, vec_ref, cos_ref, sin_ref, tri_ref, e3_ref,
                  qa_ref, kva_ref, qb_ref, kb_ref, vb_ref, qkvc_ref, gc_ref, carry_ref):
    tm = x_ref.shape[1]

    @pl.when(pl.program_id(1) == 0)
    def _():
        carry_ref[...] = jnp.zeros_like(carry_ref)

    x = x_ref[0]
    ms = jnp.mean(x * x, axis=-1, keepdims=True)
    h = ((x * lax.rsqrt(ms + EPS)) * n1_ref[...]).astype(BF16)

    def proj(w_ref, c0, width):
        return _dot(h, w_ref[:, c0:c0 + width])

    lane_row = lax.broadcasted_iota(jnp.int32, (1, LANES), 1)
    first_swa = (lane_row // HALF) % 2 == 0
    first_nat = lane_row < HEAD

    def head_norm(z, first, gain):
        sq = z * z
        s_first = jnp.sum(jnp.where(first, sq, 0.0), axis=-1, keepdims=True)
        s_second = jnp.sum(jnp.where(first, 0.0, sq), axis=-1, keepdims=True)
        ss = jnp.where(first, s_first, s_second)
        return (z * lax.rsqrt(ss * (1.0 / HEAD) + EPS)) * gain

    cos = cos_ref[...]
    sin = sin_ref[...]

    def rope(z):
        return z * cos + pltpu.roll(z, HEAD, 1) * sin

    g_aq = vec_ref[0:1, 0:LANES]
    g_ak = vec_ref[1:2, 0:LANES]
    g_bq = vec_ref[2:3, 0:LANES]
    g_bk = vec_ref[3:4, 0:LANES]
    gate_bias = vec_ref[4:5, 0:LANES]

    lane = lax.broadcasted_iota(jnp.int32, (MXU_DIM, LANES), 1)
    zg = proj(wc_ref, SEC_G, LANES) + gate_bias
    log_sig = _log_sigmoid(zg)
    log_f = jnp.where(lane[0:1] < GATE_CI, log_sig, 0.0)
    f_terms = [_split3_by_lane(log_f[blk * MXU_DIM:(blk + 1) * MXU_DIM], lane)
               for blk in range(tm // MXU_DIM)]
    gz = jnp.where(lane[0:1] < GATE_CI, 0.0,
                   jnp.where(lane[0:1] < GATE_CF, zg,
                             jnp.where(lane[0:1] < GATE_CF + MLSTM_HEADS, log_sig, 0.0)))
    gz_hi = gz.astype(BF16)
    gz_r = gz - gz_hi.astype(F32)
    gz_mid = gz_r.astype(BF16)
    gz_terms = jnp.concatenate([gz_hi, gz_mid, (gz_r - gz_mid.astype(F32)).astype(BF16)], axis=1)

    za = proj(wa_ref, 0, W_A)
    for j in range(SWA_HEADS // 2):
        q = rope(head_norm(za[:, j * LANES:(j + 1) * LANES], first_swa, g_aq)) * (QK_SCALE * LOG2E)
        qa_ref[0, :, j * LANES:(j + 1) * LANES] = q.astype(BF16)
    k = rope(head_norm(za[:, SWA_Q:SWA_Q + LANES], first_swa, g_ak))
    kva_ref[0, :, 0:LANES] = k.astype(BF16)
    kva_ref[0, :, LANES:2 * LANES] = za[:, SWA_Q + SWA_KV:].astype(BF16)

    zb = proj(wb_ref, 0, W_B)
    for p in range(FOX_HEADS // 2):
        q = head_norm(zb[:, p * LANES:(p + 1) * LANES], first_nat, g_bq) * (QK_SCALE * LOG2E)
        qb_ref[0, :, p * LANES:(p + 1) * LANES] = q.astype(BF16)
        k = head_norm(zb[:, FOX_W + p * LANES:FOX_W + (p + 1) * LANES], first_nat, g_bk)
        kb_ref[0, :, 2 * p * LANES:(2 * p + 1) * LANES] = k.astype(BF16)
    vb_ref[0] = zb[:, 2 * FOX_W:].astype(BF16)

    zc = proj(wc_ref, SEC_C, 768)
    qkvc_ref[0, :, 0:MLSTM_W] = zc[:, 0:MLSTM_W].astype(BF16)
    qkvc_ref[0, :, MLSTM_W:2 * MLSTM_W] = (zc[:, MLSTM_W:2 * MLSTM_W] * QK_SCALE).astype(BF16)
    qkvc_ref[0, :, 2 * MLSTM_W:] = zc[:, 2 * MLSTM_W:].astype(BF16)
    gc_ref[0, :, 2 * MLSTM_W:] = proj(wc_ref, SEC_CO, MLSTM_W)

    gc_ref[0, :, 0:2 * MLSTM_W] = _dot(gz_terms, e3_ref[...])

    carry = carry_ref[0:1, :]
    for blk in range(tm // MXU_DIM):
        rows = slice(blk * MXU_DIM, (blk + 1) * MXU_DIM)
        cs = _dot(tri_ref[...], f_terms[blk])
        f_loc = cs + pltpu.roll(cs, LANES - 4, 1) + pltpu.roll(cs, LANES - 8, 1)
        f_cum = jnp.where(lane < 4, f_loc, 0.0) + carry
        carry = f_cum[MXU_DIM - 1:MXU_DIM, :]
        f_rep = (f_cum + pltpu.roll(f_cum, 4, 1) + pltpu.roll(f_cum, 8, 1)) * LOG2E
        aug = _split3_by_lane(f_rep, lane)
        kb_ref[0, rows, LANES:2 * LANES] = aug
        kb_ref[0, rows, 3 * LANES:4 * LANES] = aug
    carry_ref[...] = jnp.broadcast_to(carry, carry_ref.shape)


def _in_proj(layer, x, n1, w_a, w_b, w_c, vecs, cos, sin, tri, e3):
    B, S, _ = x.shape
    tm = min(TM_IN, S)
    grid = (B, S // tm)
    tok = lambda w: pl.BlockSpec((1, tm, w), lambda b, i: (b, i, 0))
    out_shapes = (
        jax.ShapeDtypeStruct((B, S, SWA_Q), BF16),
        jax.ShapeDtypeStruct((B, S, 2 * SWA_KV), BF16),
        jax.ShapeDtypeStruct((B, S, FOX_W), BF16),
        jax.ShapeDtypeStruct((B, S, 2 * MXU_DIM), BF16),
        jax.ShapeDtypeStruct((B, S, FOX_W), BF16),
        jax.ShapeDtypeStruct((B, S, 3 * MLSTM_W), BF16),
        jax.ShapeDtypeStruct((B, S, 3 * MLSTM_W), F32),
    )
    return pl.pallas_call(
        _in_proj_body,
        grid=grid,
        in_specs=[
            tok(D_MODEL),
            _layer_spec((1, D_MODEL), layer),
            _layer_spec((D_MODEL, W_A), layer),
            _layer_spec((D_MODEL, W_B), layer),
            _layer_spec((D_MODEL, W_C), layer),
            _layer_spec((8, MLSTM_W), layer),
            pl.BlockSpec((tm, LANES), lambda b, i: (i, 0)),
            pl.BlockSpec((tm, LANES), lambda b, i: (i, 0)),
            _const_spec((MXU_DIM, MXU_DIM)),
            _const_spec((3 * LANES, 2 * MLSTM_W)),
        ],
        out_specs=[tok(s.shape[-1]) for s in out_shapes],
        out_shape=out_shapes,
        scratch_shapes=[pltpu.VMEM((8, LANES), F32)],
        compiler_params=pltpu.CompilerParams(
            dimension_semantics=("arbitrary", "arbitrary"), vmem_limit_bytes=VMEM_LIMIT),
        name="in_proj",
    )(x, n1, w_a, w_b, w_c, vecs, cos, sin, tri, e3)


TS_SWA = 1024
SWA_DEPTH = 8
VT_ROWS = HEAD + 16


def _swa_body(sink_base, sink_ref, q_ref, kvc_ref, kvp_ref, eye_ref, band_ref, o_ref):
    step = pl.program_id(1)
    nblk = q_ref.shape[1] // QBLK
    per_group = SWA_HEADS // SWA_KV_HEADS
    eye = eye_ref[0:LANES, 0:LANES]

    band_cap = band_ref[...]
    key = lax.broadcasted_iota(jnp.int32, (2 * QBLK, MXU_DIM), 0)
    first_cap = jnp.where((key >= QBLK) | (step > 0), band_cap, -jnp.inf)
    lane = lax.broadcasted_iota(jnp.int32, (QBLK, LANES), 1)
    lane_a = (lane // HALF) % 2 == 0
    first_head = lax.broadcasted_iota(jnp.int32, (1, MXU_DIM), 1) < QBLK
    zero = jnp.zeros((QBLK, LANES), BF16)
    ones = jnp.ones((VT_ROWS - HEAD, 2 * QBLK), BF16)
    kv_all = jnp.concatenate([kvp_ref[0], kvc_ref[0]], axis=0)
    rows = [slice(blk * QBLK, (blk + 1) * QBLK) for blk in range(nblk)]
    bands = [kv_all[blk * QBLK:(blk + 2) * QBLK] for blk in range(nblk)]

    vt_aug = []
    for blk in range(nblk):
        v_t = _dot_nt(eye, bands[blk][:, LANES:]).astype(BF16)
        vt_aug.append([jnp.concatenate([v_t[g * HEAD:(g + 1) * HEAD], ones], axis=0)
                       for g in range(SWA_KV_HEADS)])

    units = [(blk, g, part) for blk in range(nblk) for g in range(SWA_KV_HEADS) for part in range(2)]

    def score(blk, g, part):
        keep = lane_a if g == 0 else jnp.logical_not(lane_a)
        q_cols = [q_ref[0, rows[blk], j * LANES:(j + 1) * LANES] for j in (2 * part, 2 * part + 1)]
        q_g = jnp.concatenate([jnp.where(keep, q2, zero) for q2 in q_cols], axis=0)
        return _dot_nt(bands[blk][:, 0:LANES], q_g)

    def fold(s, blk, g, part):
        s = jnp.minimum(s, band_cap if blk > 0 else first_cap)
        head = sink_base + g * per_group + 2 * part
        sink = jnp.where(first_head, sink_ref[head], sink_ref[head + 1]) * LOG2E
        m = jnp.maximum(jnp.max(s, axis=0, keepdims=True), sink)
        p = jnp.exp2(s - m)
        o_aug = _dot(vt_aug[blk][g], p.astype(BF16))
        den = o_aug[HEAD:HEAD + 1] + jnp.exp2(sink - m)
        return (o_aug[0:HEAD] / den).astype(BF16)

    def write_block(blk, outs):
        for g in range(SWA_KV_HEADS):
            for part in range(2):
                o = outs[(blk, g, part)]
                pair_t = jnp.concatenate([o[:, 0:QBLK], o[:, QBLK:]], axis=0)
                j = 2 * g + part
                o_ref[0, rows[blk], j * LANES:(j + 1) * LANES] = _dot_nt(eye, pair_t).astype(BF16)

    outs = {}
    queue = [score(*unit) for unit in units[:SWA_DEPTH]]
    for n, unit in enumerate(units):
        s_cur = queue.pop(0)
        if n + SWA_DEPTH < len(units):
            queue.append(score(*units[n + SWA_DEPTH]))
        outs[unit] = fold(s_cur, *unit)
        blk = unit[0]
        if unit[1:] == (SWA_KV_HEADS - 1, 1) and blk > 0:
            write_block(blk - 1, outs)
    write_block(nblk - 1, outs)


def _swa(layer, sinks, qa, kva, eye, band):
    B, S, _ = qa.shape
    ts = min(TS_SWA, S)
    per_step = ts // QBLK
    return pl.pallas_call(
        functools.partial(_swa_body, layer * SWA_HEADS),
        grid=(B, S // ts),
        in_specs=[
            pl.BlockSpec(memory_space=pltpu.SMEM),
            pl.BlockSpec((1, ts, SWA_Q), lambda b, i: (b, i, 0)),
            pl.BlockSpec((1, ts, 2 * SWA_KV), lambda b, i: (b, i, 0)),
            pl.BlockSpec((1, QBLK, 2 * SWA_KV), lambda b, i: (b, jnp.maximum(i * per_step - 1, 0), 0)),
            _const_spec((MXU_DIM, MXU_DIM)),
            _const_spec((2 * QBLK, MXU_DIM)),
        ],
        out_specs=pl.BlockSpec((1, ts, SWA_Q), lambda b, i: (b, i, 0)),
        out_shape=jax.ShapeDtypeStruct((B, S, SWA_Q), BF16),
        compiler_params=pltpu.CompilerParams(
            dimension_semantics=("arbitrary", "arbitrary"), vmem_limit_bytes=VMEM_LIMIT),
        name="swa",
    )(sinks, qa, kva, kva, eye, band)


TQ_FOX = 1024
TK_FOX = 256
NEG_BIG = -1e30


FOX_UNROLL = 4
FOX_DEPTH = 8


def _fox_body(q_ref, k_ref, v_ref, eye_ref, o_ref, vt_ref, qt_ref, acc_ref):
    tq = q_ref.shape[1]
    tk = TK_FOX
    S = k_ref.shape[1]
    pair = pl.program_id(1)
    qi = pl.program_id(2)
    eye = eye_ref[...]
    n_tiles = 2 * tq // MXU_DIM
    tiles_per_head = tq // MXU_DIM

    @pl.when(qi == 0)
    def _():
        ones = jnp.ones((VT_ROWS - HEAD, tk), BF16)
        for j in range(S // tk):
            blk = slice(j * tk, (j + 1) * tk)
            v_t = _dot_nt(eye[0:LANES, 0:LANES], v_ref[0, blk, :]).astype(BF16)
            vt_ref[:, blk] = jnp.concatenate([v_t[0:HEAD], ones, v_t[HEAD:], ones], axis=0)

    lane = lax.broadcasted_iota(jnp.int32, (tq, LANES), 1)
    lane_a = lane < HEAD
    q2 = q_ref[0]
    zero = jnp.zeros_like(q2)

    def minus_one_at(head):
        hit = (lane == head) | (lane == head + 4) | (lane == head + 8)
        return jnp.where(hit, -1.0, 0.0).astype(BF16)

    q_aug = [jnp.concatenate([jnp.where(lane_a, q2, zero), minus_one_at(2 * pair)], axis=1),
             jnp.concatenate([jnp.where(lane_a, zero, q2), minus_one_at(2 * pair + 1)], axis=1)]
    for t in range(n_tiles):
        hd, part = divmod(t, tiles_per_head)
        rows = q_aug[hd][part * MXU_DIM:(part + 1) * MXU_DIM]
        qt_ref[:, t * MXU_DIM:(t + 1) * MXU_DIM] = _dot_nt(eye, rows).astype(BF16)
    acc_ref[...] = jnp.zeros_like(acc_ref)

    key = lax.broadcasted_iota(jnp.int32, (tk // 2, MXU_DIM), 0)
    qry = lax.broadcasted_iota(jnp.int32, (tk // 2, MXU_DIM), 1)

    def fold(s, t, vt_blk, m, triangular):
        hd = t // tiles_per_head
        vt_h = vt_blk[hd * VT_ROWS:(hd + 1) * VT_ROWS]
        half = tk // 2
        for h in range(2):
            sh = s[h * half:(h + 1) * half]
            if triangular:
                valid = key + h * half <= qry
                sh = jnp.where(valid, sh, NEG_BIG)
            m_new = jnp.maximum(m, jnp.max(sh, axis=0, keepdims=True))
            alpha = jnp.exp2(m - m_new)
            p = jnp.exp2(sh - m_new)
            if triangular:
                p = jnp.where(valid, p, 0.0)
            acc_ref[t] = alpha * acc_ref[t] + _dot(vt_h[:, h * half:(h + 1) * half], p.astype(BF16))
            m = m_new
        return m

    def region(j0, ms, plans):
        k_blks, vt_blks, todo = [], [], []
        for b, plan in enumerate(plans):
            r0 = pl.multiple_of((j0 + b) * tk, tk)
            k_blks.append(k_ref[0, pl.ds(r0, tk), :])
            vt_blks.append(vt_ref[:, pl.ds(r0, tk)])
            todo += [(b, t) for t in range(n_tiles) if plan[t] is not None]
        score = lambda b, t: _dot(k_blks[b], qt_ref[:, t * MXU_DIM:(t + 1) * MXU_DIM])
        ms = list(ms)
        queue = [score(*unit) for unit in todo[:FOX_DEPTH]]
        for n, (b, t) in enumerate(todo):
            s_cur = queue.pop(0)
            if n + FOX_DEPTH < len(todo):
                queue.append(score(*todo[n + FOX_DEPTH]))
            ms[t] = fold(s_cur, t, vt_blks[b], ms[t], plans[b][t])
        return tuple(ms)

    blocks_per_tile = tq // tk
    full = (False,) * n_tiles
    n_full = blocks_per_tile * qi
    m_init = jnp.full((1, MXU_DIM), NEG_BIG, F32)
    ms = lax.fori_loop(0, n_full // FOX_UNROLL,
                       lambda g, c: region(g * FOX_UNROLL, c, (full,) * FOX_UNROLL), (m_init,) * n_tiles)
    if FOX_UNROLL > blocks_per_tile:
        done = (n_full // FOX_UNROLL) * FOX_UNROLL
        ms = lax.fori_loop(0, (n_full - done) // blocks_per_tile,
                           lambda g, c: region(done + g * blocks_per_tile, c, (full,) * blocks_per_tile), ms)
    diag = [tuple(None if t % tiles_per_head < d else t % tiles_per_head == d for t in range(n_tiles))
            for d in range(blocks_per_tile)]
    ms = region(n_full, ms, diag)

    for hd in range(2):
        cols = []
        for part in range(tiles_per_head):
            acc = acc_ref[hd * tiles_per_head + part]
            cols.append(acc[0:HEAD] / acc[HEAD:HEAD + 1])
        o_h = jnp.concatenate(cols, axis=1)
        if hd == 0:
            o_t = o_h
        else:
            o_t = jnp.concatenate([o_t, o_h], axis=0).astype(BF16)
    for c in range(tq // MXU_DIM):
        rows = slice(c * MXU_DIM, (c + 1) * MXU_DIM)
        o_ref[0, rows, :] = _dot_nt(eye, o_t[:, rows]).astype(BF16)


def _fox(qb, kb, vb, eye):
    B, S, _ = qb.shape
    tq = min(TQ_FOX, S)
    return pl.pallas_call(
        _fox_body,
        grid=(B, FOX_HEADS // 2, S // tq),
        in_specs=[
            pl.BlockSpec((1, tq, LANES), lambda b, p, i: (b, i, p)),
            pl.BlockSpec((1, S, MXU_DIM), lambda b, p, i: (b, 0, p)),
            pl.BlockSpec((1, S, LANES), lambda b, p, i: (b, 0, p)),
            _const_spec((MXU_DIM, MXU_DIM)),
        ],
        out_specs=pl.BlockSpec((1, tq, LANES), lambda b, p, i: (b, i, p)),
        out_shape=jax.ShapeDtypeStruct((B, S, FOX_W), BF16),
        scratch_shapes=[
            pltpu.VMEM((2 * VT_ROWS, S), BF16),
            pltpu.VMEM((MXU_DIM, 2 * tq), BF16),
            pltpu.VMEM((2 * tq // MXU_DIM, VT_ROWS, MXU_DIM), F32),
        ],
        compiler_params=pltpu.CompilerParams(
            dimension_semantics=("arbitrary", "arbitrary", "arbitrary"),
            vmem_limit_bytes=VMEM_LIMIT),
        name="fox",
    )(qb, kb, vb, eye)


TS_MLSTM = 1024


def _mlstm_body(qkv_ref, g_ref, onorm_ref, tri_ref, i4_ref, bd_ref, o_ref, c_ref, n_ref, m_ref):
    ts = qkv_ref.shape[1]
    W = MLSTM_W

    @pl.when(pl.program_id(1) == 0)
    def _():
        c_ref[...] = jnp.zeros_like(c_ref)
        n_ref[...] = jnp.zeros_like(n_ref)
        m_ref[...] = jnp.zeros_like(m_ref)

    row = lax.broadcasted_iota(jnp.int32, (CHUNK, W), 0)
    lane = lax.broadcasted_iota(jnp.int32, (CHUNK, W), 1)
    causal = (lane % HEAD) <= row
    lane_head = lane // HEAD
    bd = bd_ref[...]
    bd2 = jnp.concatenate([bd, bd], axis=1)
    i4_bf = i4_ref[...].astype(BF16)

    def block_diag(x):
        tiled = jnp.concatenate([x] * MLSTM_HEADS, axis=0)
        return tiled * (bd if x.shape[1] == W else bd2)

    def seg_max(d):
        out = jnp.zeros_like(d)
        for hd in range(MLSTM_HEADS):
            sel = lane_head == hd
            mx = jnp.max(jnp.where(sel, d, -jnp.inf), axis=-1, keepdims=True)
            out = jnp.where(sel, mx, out)
        return out

    chunks = range(ts // CHUNK)
    rows = [pl.ds(c * CHUNK, CHUNK) for c in chunks]

    b, g, u, dm, dm_max, a_max, wa = [], [], [], [], [], [], []
    for c in chunks:
        ig = g_ref[0, rows[c], 0:W]
        lf_hi, lf_lo = _split2(g_ref[0, rows[c], W:2 * W])
        b_c = _dot(tri_ref[...], lf_hi) + _dot(tri_ref[...], lf_lo)
        g_c = b_c[CHUNK - 1:CHUNK, :]
        u_c = b_c - ig
        u_row = jnp.sum(u_c * i4_ref[...], axis=0, keepdims=True)
        dm_c = jnp.where(causal, b_c - u_row, -jnp.inf)
        a_c = g_c - u_c
        am_c = jnp.max(a_c, axis=0, keepdims=True)
        b.append(b_c); g.append(g_c); u.append(u_c); dm.append(dm_c)
        dm_max.append(seg_max(dm_c)); a_max.append(am_c); wa.append(jnp.exp(a_c - am_c))

    m_prev, s_old, s_loc = [m_ref[0:1, :]], [], []
    for c in chunks:
        gm = g[c] + m_prev[c]
        m_new = jnp.maximum(gm, a_max[c])
        s_old.append(jnp.exp(gm - m_new))
        s_loc.append(jnp.exp(a_max[c] - m_new))
        m_prev.append(m_new)

    m_t, w_inter, sm_hi, sm_lo, k_t = [], [], [], [], []
    for c in chunks:
        inter = b[c] + m_prev[c]
        mt_c = jnp.maximum(inter, dm_max[c])
        decay = jnp.exp(dm[c] - mt_c)
        q = qkv_ref[0, rows[c], 0:W]
        k = qkv_ref[0, rows[c], W:2 * W]
        qk = _dot_nt(jnp.concatenate([q, i4_bf], axis=0), block_diag(k))
        hi, lo = _split2(qk[0:CHUNK] * decay)
        m_t.append(mt_c); w_inter.append(jnp.exp(inter - mt_c)); sm_hi.append(hi); sm_lo.append(lo)
        k_t.append(qk[CHUNK:].astype(BF16))

    c_cur, n_cur = c_ref[...], n_ref[...]
    state_bf = []
    for c in chunks:
        state_bf.append(jnp.concatenate([c_cur.astype(BF16), n_cur.astype(BF16)], axis=1))
        v = qkv_ref[0, rows[c], 2 * W:]
        weighted = jnp.concatenate([(wa[c] * v.astype(F32)).astype(BF16), wa[c].astype(BF16)], axis=1)
        loc = _dot(k_t[c], block_diag(weighted))
        c_cur = s_old[c] * c_cur + s_loc[c] * loc[:, 0:W]
        n_cur = s_old[c] * n_cur + s_loc[c] * loc[:, W:]
    c_ref[...] = c_cur
    n_ref[...] = n_cur
    m_ref[...] = jnp.broadcast_to(m_prev[-1], m_ref.shape)

    hval = []
    for c in chunks:
        q = qkv_ref[0, rows[c], 0:W]
        v = qkv_ref[0, rows[c], 2 * W:]
        inter_cn = _dot(q, block_diag(state_bf[c]))
        intra = _dot(sm_hi[c], jnp.concatenate([block_diag(v), bd], axis=1))
        num = w_inter[c] * inter_cn[:, 0:W] + intra[:, 0:W]
        den = w_inter[c] * inter_cn[:, W:] + intra[:, W:] + _dot(sm_lo[c], bd)
        hval.append(num / jnp.maximum(jnp.abs(den), jnp.exp(-m_t[c])))

    for c in chunks:
        hs_hi, hs_lo = _split2(hval[c] * hval[c])
        ss = _dot(hs_hi, bd) + _dot(hs_lo, bd)
        hc = (hval[c] * lax.rsqrt(ss * (1.0 / HEAD) + EPS)) * onorm_ref[...]
        co = g_ref[0, rows[c], 2 * W:]
        o_ref[0, rows[c], :] = (jax.nn.sigmoid(co) * hc).astype(BF16)


def _mlstm(layer, qkvc, gc, onorm, tri, i4, bd):
    B, S, _ = qkvc.shape
    ts = min(TS_MLSTM, S)
    return pl.pallas_call(
        _mlstm_body,
        grid=(B, S // ts),
        in_specs=[
            pl.BlockSpec((1, ts, 3 * MLSTM_W), lambda b, i: (b, i, 0)),
            pl.BlockSpec((1, ts, 3 * MLSTM_W), lambda b, i: (b, i, 0)),
            _layer_spec((1, MLSTM_W), layer),
            _const_spec((CHUNK, CHUNK)),
            _const_spec((CHUNK, MLSTM_W)),
            _const_spec((MLSTM_W, MLSTM_W)),
        ],
        out_specs=pl.BlockSpec((1, ts, MLSTM_W), lambda b, i: (b, i, 0)),
        out_shape=jax.ShapeDtypeStruct((B, S, MLSTM_W), BF16),
        scratch_shapes=[pltpu.VMEM((HEAD, MLSTM_W), F32), pltpu.VMEM((HEAD, MLSTM_W), F32),
                        pltpu.VMEM((8, MLSTM_W), F32)],
        compiler_params=pltpu.CompilerParams(
            dimension_semantics=("arbitrary", "arbitrary"), vmem_limit_bytes=VMEM_LIMIT),
        name="mlstm",
    )(qkvc, gc, onorm, tri, i4, bd)


TM_FFN = 512
FF_CHUNK = 1024


def _out_ffn_body(x_ref, ya_ref, yb_ref, yc_ref, wo_ref, n2_ref, w1_ref, w2_ref, o_ref, a_ref):
    x1 = (x_ref[...] + _dot(ya_ref[...], wo_ref[0:SWA_Q, :])
          + _dot(yb_ref[...], wo_ref[SWA_Q:SWA_Q + FOX_W, :])
          + _dot(yc_ref[...], wo_ref[SWA_Q + FOX_W:, :]))
    ms = jnp.mean(x1 * x1, axis=-1, keepdims=True)
    h2 = ((x1 * lax.rsqrt(ms + EPS)) * n2_ref[...]).astype(BF16)
    for j in range(D_FF // FF_CHUNK):
        cols = slice(j * FF_CHUNK, (j + 1) * FF_CHUNK)
        r = jnp.maximum(_dot(h2, w1_ref[:, cols]), 0.0)
        a_ref[:, cols] = (r * r).astype(BF16)
    o_ref[...] = x1 + _dot(a_ref[...], w2_ref[...])


def _out_ffn(layer, x2d, ya, yb, yc, wo, n2, w1, w2):
    T = x2d.shape[0]
    tm = min(TM_FFN, T)
    tok = lambda w: pl.BlockSpec((tm, w), lambda i: (i, 0))
    return pl.pallas_call(
        _out_ffn_body,
        grid=(T // tm,),
        in_specs=[
            tok(D_MODEL), tok(SWA_Q), tok(FOX_W), tok(MLSTM_W),
            _layer_spec((D_MODEL, D_MODEL), layer),
            _layer_spec((1, D_MODEL), layer),
            _layer_spec((D_MODEL, D_FF), layer),
            _layer_spec((D_FF, D_MODEL), layer),
        ],
        out_specs=tok(D_MODEL),
        out_shape=jax.ShapeDtypeStruct((T, D_MODEL), F32),
        scratch_shapes=[pltpu.VMEM((tm, D_FF), BF16)],
        compiler_params=pltpu.CompilerParams(
            dimension_semantics=("arbitrary",), vmem_limit_bytes=VMEM_LIMIT),
        name="out_ffn",
    )(x2d, ya, yb, yc, wo, n2, w1, w2)


def _rope_tables(S):
    inv = ROPE_THETA ** (-np.arange(HALF, dtype=np.float64) / HALF)
    ang = np.arange(S, dtype=np.float64)[:, None] * inv[None, :]
    cos = np.tile(np.cos(ang), (1, LANES // HALF))
    sin = np.tile(np.sin(ang), (1, LANES // HALF))
    sign = np.where(np.arange(LANES) < HEAD, -1.0, 1.0)
    return jnp.asarray(cos, F32), jnp.asarray(sin * sign[None, :], F32)


def _prep_w_in(w_in):
    select = lambda w, p: jnp.einsum("ldk,kn->ldn", w.astype(BF16), jnp.asarray(p, BF16),
                                     preferred_element_type=BF16)
    w_b = w_in[:, :, W_A:W_A + W_B].astype(BF16)
    return select(w_in[:, :, :W_A], _PERM_A), w_b, select(w_in[:, :, W_A + W_B:], _PERM_C)


def _prep_vecs(swa_q_norm, swa_k_norm, fox_q_norm, fox_k_norm, fox_f_bias, mlstm_i_bias, mlstm_f_bias):
    L = swa_q_norm.shape[0]

    def swa_lanes(g):
        return jnp.concatenate([g[:, :HALF], g[:, :HALF], g[:, HALF:], g[:, HALF:]], axis=-1)

    pad = lambda v: jnp.pad(v, ((0, 0), (0, MLSTM_W - v.shape[1])))
    return jnp.stack([
        pad(swa_lanes(swa_q_norm)), pad(swa_lanes(swa_k_norm)),
        pad(jnp.tile(fox_q_norm, (1, 2))), pad(jnp.tile(fox_k_norm, (1, 2))),
        pad(jnp.concatenate([jnp.tile(fox_f_bias, (1, 3)), mlstm_i_bias, mlstm_f_bias], axis=-1)),
        jnp.zeros((L, MLSTM_W), F32), jnp.zeros((L, MLSTM_W), F32), jnp.zeros((L, MLSTM_W), F32),
    ], axis=1).astype(F32)


def kernel(x, norm1, w_in, swa_q_norm, swa_k_norm, swa_sinks, fox_q_norm, fox_k_norm, fox_f_bias,
           mlstm_i_bias, mlstm_f_bias, mlstm_out_norm, w_out, norm2, w_ff1, w_ff2):
    B, S, D = x.shape
    L = norm1.shape[0]
    T = B * S
    cos, sin = _rope_tables(S)
    tri256, tri64, bd256 = (jnp.asarray(c, BF16) for c in (_TRI256, _TRI64, _BD256))
    i4 = jnp.asarray(_I4, F32)
    e3 = jnp.asarray(_gate_expand(), BF16)
    eye256 = jnp.eye(MXU_DIM, dtype=BF16)
    band = jnp.asarray(_swa_band_cap(), F32)

    w_a, w_b, w_c = _prep_w_in(w_in)
    wo = w_out.astype(BF16)
    w1 = w_ff1.astype(BF16)
    w2 = w_ff2.astype(BF16)
    vecs = _prep_vecs(swa_q_norm, swa_k_norm, fox_q_norm, fox_k_norm, fox_f_bias, mlstm_i_bias,
                      mlstm_f_bias)
    n1 = norm1.reshape(L, 1, D).astype(F32)
    n2 = norm2.reshape(L, 1, D).astype(F32)
    onorm = mlstm_out_norm.reshape(L, 1, MLSTM_W).astype(F32)
    sinks = swa_sinks.reshape(L * SWA_HEADS).astype(F32)

    for l in range(L):
        qa, kva, qb, kb, vb, qkvc, gc = _in_proj(l, x, n1, w_a, w_b, w_c, vecs, cos, sin, tri256, e3)
        ya = _swa(l, sinks, qa, kva, eye256, band)
        yb = _fox(qb, kb, vb, eye256)
        yc = _mlstm(l, qkvc, gc, onorm, tri64, i4, bd256)
        x = _out_ffn(l, x.reshape(T, D), ya.reshape(T, SWA_Q), yb.reshape(T, FOX_W),
                     yc.reshape(T, MLSTM_W), wo, n2, w1, w2).reshape(B, S, D)
    return x
```

```python
import functools
import math

import numpy as np
import jax
import jax.numpy as jnp
from jax import lax
from jax.experimental import pallas as pl
from jax.experimental.pallas import tpu as pltpu

F32 = jnp.float32
BF16 = jnp.bfloat16

D_MODEL = 1024
HEAD = 64
HALF = HEAD // 2
CHUNK = 64
SWA_HEADS = 8
SWA_KV_HEADS = 2
FOX_HEADS = 4
MLSTM_HEADS = 4
SWA_Q = SWA_HEADS * HEAD
SWA_KV = SWA_KV_HEADS * HEAD
FOX_W = FOX_HEADS * HEAD
MLSTM_W = MLSTM_HEADS * HEAD
D_FF = 4 * D_MODEL
QBLK = 128
ROPE_THETA = 10000.0
EPS = 1e-6
QK_SCALE = 1.0 / math.sqrt(HEAD)
LOG2E = math.log2(math.e)

LANES = 128
MXU_DIM = 256
VMEM_LIMIT = 56 * 1024 * 1024

W_A = SWA_Q + 2 * SWA_KV
W_B = 3 * FOX_W
SEC_C = 0
SEC_CO = SEC_C + 768
SEC_G = SEC_CO + 256
W_C = SEC_G + LANES
GATE_BF = 0
GATE_CI = 12
GATE_CF = 16
IN_WIDTH = W_A + W_B + FOX_HEADS + 4 * MLSTM_W + 2 * MLSTM_HEADS


def _perm_a():
    p = np.zeros((W_A, W_A), np.float32)
    lane = np.arange(LANES)
    quarter, i = lane // HALF, lane % HALF
    dim, is_b = i + HALF * (quarter // 2), quarter % 2
    for j in range(SWA_HEADS // 2):
        p[(j + 4 * is_b) * HEAD + dim, j * LANES + lane] = 1.0
    p[SWA_Q + is_b * HEAD + dim, SWA_Q + lane] = 1.0
    p[SWA_Q + SWA_KV + lane, SWA_Q + SWA_KV + lane] = 1.0
    return p


def _perm_c():
    src = IN_WIDTH - W_A - W_B
    o_bf, o_cqkv = 0, FOX_HEADS
    o_ci = o_cqkv + 3 * MLSTM_W
    o_cf, o_co = o_ci + MLSTM_HEADS, o_ci + 2 * MLSTM_HEADS
    p = np.zeros((src, W_C), np.float32)
    w = np.arange(3 * MLSTM_W)
    p[o_cqkv + w, SEC_C + w] = 1.0
    w = np.arange(MLSTM_W)
    p[o_co + w, SEC_CO + w] = 1.0
    h = np.arange(FOX_HEADS)
    for rep in range(3):
        p[o_bf + h, SEC_G + GATE_BF + rep * FOX_HEADS + h] = 1.0
    h = np.arange(MLSTM_HEADS)
    p[o_ci + h, SEC_G + GATE_CI + h] = 1.0
    p[o_cf + h, SEC_G + GATE_CF + h] = 1.0
    return p


def _gate_expand():
    e = np.zeros((LANES, 2 * MLSTM_W), np.float32)
    for h in range(MLSTM_HEADS):
        e[GATE_CI + h, h * HEAD:(h + 1) * HEAD] = 1.0
        e[GATE_CF + h, MLSTM_W + h * HEAD:MLSTM_W + (h + 1) * HEAD] = 1.0
    return np.concatenate([e, e, e], axis=0)


_PERM_A = _perm_a()
_PERM_C = _perm_c()


def _block_diag_ones(head_of_lane):
    return (head_of_lane[:, None] == head_of_lane[None, :]).astype(np.float32)


_TRI256 = np.tril(np.ones((MXU_DIM, MXU_DIM), np.float32))
_TRI64 = np.tril(np.ones((CHUNK, CHUNK), np.float32))
_w256 = np.arange(MLSTM_W)
_BD256 = _block_diag_ones(_w256 // HEAD)
_I4 = (np.arange(CHUNK)[:, None] == (_w256 % HEAD)[None, :]).astype(np.float32)


def _swa_band_cap():
    key_chunk = np.arange(2 * QBLK)[:, None] // CHUNK
    qry_chunk = (np.arange(MXU_DIM)[None, :] % QBLK) // CHUNK
    ahead = key_chunk - qry_chunk
    return np.where((ahead >= 0) & (ahead <= 2), np.inf, -np.inf).astype(np.float32)


def _const_spec(shape):
    nd = len(shape)
    return pl.BlockSpec(shape, lambda *_: (0,) * nd, pipeline_mode=pl.Buffered(1))


def _layer_spec(shape, layer):
    nd = len(shape)
    return pl.BlockSpec((None,) + tuple(shape), lambda *_: (layer,) + (0,) * nd,
                        pipeline_mode=pl.Buffered(1))


def _log_sigmoid(x):
    return jnp.minimum(x, 0.0) - jnp.log(1.0 + jnp.exp(-jnp.abs(x)))


def _split2(x):
    hi = x.astype(BF16)
    lo = (x - hi.astype(F32)).astype(BF16)
    return hi, lo


def _split3_by_lane(x, lane):
    hi = x.astype(BF16).astype(F32)
    r1 = x - hi
    mid = r1.astype(BF16).astype(F32)
    low = r1 - mid
    return jnp.where(lane < 4, hi, jnp.where(lane < 8, mid, low)).astype(BF16)


def _dot(a, b):
    return jnp.dot(a, b, preferred_element_type=F32)


def _dot_nt(a, b):
    return lax.dot_general(a, b, (((1,), (1,)), ((), ())), preferred_element_type=F32)


def _dot_tn(a, b):
    return lax.dot_general(a, b, (((0,), (0,)), ((), ())), preferred_element_type=F32)


TM_IN = 1024


def _in_proj_body(x_ref, n1_ref, wa_ref, wb_ref, wc_ref, vec_ref, cos_ref, sin_ref, tri_ref, e3_ref,
                  qa_ref, kva_ref, qb_ref, kb_ref, vb_ref, qkvc_ref, gc_ref, carry_ref):
    tm = x_ref.shape[1]

    @pl.when(pl.program_id(1) == 0)
    def _():
        carry_ref[...] = jnp.zeros_like(carry_ref)

    x = x_ref[0]
    ms = jnp.mean(x * x, axis=-1, keepdims=True)
    h = ((x * lax.rsqrt(ms + EPS)) * n1_ref[...]).astype(BF16)

    def proj(w_ref, c0, width):
        return _dot(h, w_ref[:, c0:c0 + width])

    lane_row = lax.broadcasted_iota(jnp.int32, (1, LANES), 1)
    first_swa = (lane_row // HALF) % 2 == 0
    first_nat = lane_row < HEAD

    def head_norm(z, first, gain):
        sq = z * z
        s_first = jnp.sum(jnp.where(first, sq, 0.0), axis=-1, keepdims=True)
        s_second = jnp.sum(jnp.where(first, 0.0, sq), axis=-1, keepdims=True)
        ss = jnp.where(first, s_first, s_second)
        return (z * lax.rsqrt(ss * (1.0 / HEAD) + EPS)) * gain

    cos = cos_ref[...]
    sin = sin_ref[...]

    def rope(z):
        return z * cos + pltpu.roll(z, HEAD, 1) * sin

    g_aq = vec_ref[0:1, 0:LANES]
    g_ak = vec_ref[1:2, 0:LANES]
    g_bq = vec_ref[2:3, 0:LANES]
    g_bk = vec_ref[3:4, 0:LANES]
    gate_bias = vec_ref[4:5, 0:LANES]

    lane = lax.broadcasted_iota(jnp.int32, (MXU_DIM, LANES), 1)
    zg = proj(wc_ref, SEC_G, LANES) + gate_bias
    log_sig = _log_sigmoid(zg)
    log_f = jnp.where(lane[0:1] < GATE_CI, log_sig, 0.0)
    f_terms = [_split3_by_lane(log_f[blk * MXU_DIM:(blk + 1) * MXU_DIM], lane)
               for blk in range(tm // MXU_DIM)]
    gz = jnp.where(lane[0:1] < GATE_CI, 0.0,
                   jnp.where(lane[0:1] < GATE_CF, zg,
                             jnp.where(lane[0:1] < GATE_CF + MLSTM_HEADS, log_sig, 0.0)))
    gz_hi = gz.astype(BF16)
    gz_r = gz - gz_hi.astype(F32)
    gz_mid = gz_r.astype(BF16)
    gz_terms = jnp.concatenate([gz_hi, gz_mid, (gz_r - gz_mid.astype(F32)).astype(BF16)], axis=1)

    za = proj(wa_ref, 0, W_A)
    for j in range(SWA_HEADS // 2):
        q = rope(head_norm(za[:, j * LANES:(j + 1) * LANES], first_swa, g_aq)) * (QK_SCALE * LOG2E)
        qa_ref[0, :, j * LANES:(j + 1) * LANES] = q.astype(BF16)
    k = rope(head_norm(za[:, SWA_Q:SWA_Q + LANES], first_swa, g_ak))
    kva_ref[0, :, 0:LANES] = k.astype(BF16)
    kva_ref[0, :, LANES:2 * LANES] = za[:, SWA_Q + SWA_KV:].astype(BF16)

    zb = proj(wb_ref, 0, W_B)
    for p in range(FOX_HEADS // 2):
        q = head_norm(zb[:, p * LANES:(p + 1) * LANES], first_nat, g_bq) * (QK_SCALE * LOG2E)
        qb_ref[0, :, p * LANES:(p + 1) * LANES] = q.astype(BF16)
        k = head_norm(zb[:, FOX_W + p * LANES:FOX_W + (p + 1) * LANES], first_nat, g_bk)
        kb_ref[0, :, 2 * p * LANES:(2 * p + 1) * LANES] = k.astype(BF16)
    vb_ref[0] = zb[:, 2 * FOX_W:].astype(BF16)

    zc = proj(wc_ref, SEC_C, 768)
    qkvc_ref[0, :, 0:MLSTM_W] = zc[:, 0:MLSTM_W].astype(BF16)
    qkvc_ref[0, :, MLSTM_W:2 * MLSTM_W] = (zc[:, MLSTM_W:2 * MLSTM_W] * QK_SCALE).astype(BF16)
    qkvc_ref[0, :, 2 * MLSTM_W:] = zc[:, 2 * MLSTM_W:].astype(BF16)
    gc_ref[0, :, 2 * MLSTM_W:] = proj(wc_ref, SEC_CO, MLSTM_W)

    gc_ref[0, :, 0:2 * MLSTM_W] = _dot(gz_terms, e3_ref[...])

    carry = carry_ref[0:1, :]
    for blk in range(tm // MXU_DIM):
        rows = slice(blk * MXU_DIM, (blk + 1) * MXU_DIM)
        cs = _dot(tri_ref[...], f_terms[blk])
        f_loc = cs + pltpu.roll(cs, LANES - 4, 1) + pltpu.roll(cs, LANES - 8, 1)
        f_cum = jnp.where(lane < 4, f_loc, 0.0) + carry
        carry = f_cum[MXU_DIM - 1:MXU_DIM, :]
        f_rep = (f_cum + pltpu.roll(f_cum, 4, 1) + pltpu.roll(f_cum, 8, 1)) * LOG2E
        aug = _split3_by_lane(f_rep, lane)
        kb_ref[0, rows, LANES:2 * LANES] = aug
        kb_ref[0, rows, 3 * LANES:4 * LANES] = aug
    carry_ref[...] = jnp.broadcast_to(carry, carry_ref.shape)


def _in_proj(layer, x, n1, w_a, w_b, w_c, vecs, cos, sin, tri, e3):
    B, S, _ = x.shape
    tm = min(TM_IN, S)
    grid = (B, S // tm)
    tok = lambda w: pl.BlockSpec((1, tm, w), lambda b, i: (b, i, 0))
    out_shapes = (
        jax.ShapeDtypeStruct((B, S, SWA_Q), BF16),
        jax.ShapeDtypeStruct((B, S, 2 * SWA_KV), BF16),
        jax.ShapeDtypeStruct((B, S, FOX_W), BF16),
        jax.ShapeDtypeStruct((B, S, 2 * MXU_DIM), BF16),
        jax.ShapeDtypeStruct((B, S, FOX_W), BF16),
        jax.ShapeDtypeStruct((B, S, 3 * MLSTM_W), BF16),
        jax.ShapeDtypeStruct((B, S, 3 * MLSTM_W), F32),
    )
    return pl.pallas_call(
        _in_proj_body,
        grid=grid,
        in_specs=[
            tok(D_MODEL),
            _layer_spec((1, D_MODEL), layer),
            _layer_spec((D_MODEL, W_A), layer),
            _layer_spec((D_MODEL, W_B), layer),
            _layer_spec((D_MODEL, W_C), layer),
            _layer_spec((8, MLSTM_W), layer),
            pl.BlockSpec((tm, LANES), lambda b, i: (i, 0)),
            pl.BlockSpec((tm, LANES), lambda b, i: (i, 0)),
            _const_spec((MXU_DIM, MXU_DIM)),
            _const_spec((3 * LANES, 2 * MLSTM_W)),
        ],
        out_specs=[tok(s.shape[-1]) for s in out_shapes],
        out_shape=out_shapes,
        scratch_shapes=[pltpu.VMEM((8, LANES), F32)],
        compiler_params=pltpu.CompilerParams(
            dimension_semantics=("arbitrary", "arbitrary"), vmem_limit_bytes=VMEM_LIMIT),
        name="in_proj",
    )(x, n1, w_a, w_b, w_c, vecs, cos, sin, tri, e3)


TS_SWA = 1024
SWA_DEPTH = 8
VT_ROWS = HEAD + 16


def _swa_body(sink_base, sink_ref, q_ref, kvc_ref, kvp_ref, eye_ref, band_ref, o_ref):
    step = pl.program_id(1)
    nblk = q_ref.shape[1] // QBLK
    per_group = SWA_HEADS // SWA_KV_HEADS
    eye = eye_ref[0:LANES, 0:LANES]

    band_cap = band_ref[...]
    key = lax.broadcasted_iota(jnp.int32, (2 * QBLK, MXU_DIM), 0)
    first_cap = jnp.where((key >= QBLK) | (step > 0), band_cap, -jnp.inf)
    lane = lax.broadcasted_iota(jnp.int32, (QBLK, LANES), 1)
    lane_a = (lane // HALF) % 2 == 0
    first_head = lax.broadcasted_iota(jnp.int32, (1, MXU_DIM), 1) < QBLK
    zero = jnp.zeros((QBLK, LANES), BF16)
    ones = jnp.ones((VT_ROWS - HEAD, 2 * QBLK), BF16)
    kv_all = jnp.concatenate([kvp_ref[0], kvc_ref[0]], axis=0)
    rows = [slice(blk * QBLK, (blk + 1) * QBLK) for blk in range(nblk)]
    bands = [kv_all[blk * QBLK:(blk + 2) * QBLK] for blk in range(nblk)]

    vt_aug = []
    for blk in range(nblk):
        v_t = _dot_nt(eye, bands[blk][:, LANES:]).astype(BF16)
        vt_aug.append([jnp.concatenate([v_t[g * HEAD:(g + 1) * HEAD], ones], axis=0)
                       for g in range(SWA_KV_HEADS)])

    units = [(blk, g, part) for blk in range(nblk) for g in range(SWA_KV_HEADS) for part in range(2)]

    def score(blk, g, part):
        keep = lane_a if g == 0 else jnp.logical_not(lane_a)
        q_cols = [q_ref[0, rows[blk], j * LANES:(j + 1) * LANES] for j in (2 * part, 2 * part + 1)]
        q_g = jnp.concatenate([jnp.where(keep, q2, zero) for q2 in q_cols], axis=0)
        return _dot_nt(bands[blk][:, 0:LANES], q_g)

    def fold(s, blk, g, part):
        s = jnp.minimum(s, band_cap if blk > 0 else first_cap)
        head = sink_base + g * per_group + 2 * part
        sink = jnp.where(first_head, sink_ref[head], sink_ref[head + 1]) * LOG2E
        m = jnp.maximum(jnp.max(s, axis=0, keepdims=True), sink)
        p = jnp.exp2(s - m)
        o_aug = _dot(vt_aug[blk][g], p.astype(BF16))
        den = o_aug[HEAD:HEAD + 1] + jnp.exp2(sink - m)
        return (o_aug[0:HEAD] / den).astype(BF16)

    def write_block(blk, outs):
        for g in range(SWA_KV_HEADS):
            for part in range(2):
                o = outs[(blk, g, part)]
                pair_t = jnp.concatenate([o[:, 0:QBLK], o[:, QBLK:]], axis=0)
                j = 2 * g + part
                o_ref[0, rows[blk], j * LANES:(j + 1) * LANES] = _dot_nt(eye, pair_t).astype(BF16)

    outs = {}
    queue = [score(*unit) for unit in units[:SWA_DEPTH]]
    for n, unit in enumerate(units):
        s_cur = queue.pop(0)
        if n + SWA_DEPTH < len(units):
            queue.append(score(*units[n + SWA_DEPTH]))
        outs[unit] = fold(s_cur, *unit)
        blk = unit[0]
        if unit[1:] == (SWA_KV_HEADS - 1, 1) and blk > 0:
            write_block(blk - 1, outs)
    write_block(nblk - 1, outs)


def _swa(layer, sinks, qa, kva, eye, band):
    B, S, _ = qa.shape
    ts = min(TS_SWA, S)
    per_step = ts // QBLK
    return pl.pallas_call(
        functools.partial(_swa_body, layer * SWA_HEADS),
        grid=(B, S // ts),
        in_specs=[
            pl.BlockSpec(memory_space=pltpu.SMEM),
            pl.BlockSpec((1, ts, SWA_Q), lambda b, i: (b, i, 0)),
            pl.BlockSpec((1, ts, 2 * SWA_KV), lambda b, i: (b, i, 0)),
            pl.BlockSpec((1, QBLK, 2 * SWA_KV), lambda b, i: (b, jnp.maximum(i * per_step - 1, 0), 0)),
            _const_spec((MXU_DIM, MXU_DIM)),
            _const_spec((2 * QBLK, MXU_DIM)),
        ],
        out_specs=pl.BlockSpec((1, ts, SWA_Q), lambda b, i: (b, i, 0)),
        out_shape=jax.ShapeDtypeStruct((B, S, SWA_Q), BF16),
        compiler_params=pltpu.CompilerParams(
            dimension_semantics=("arbitrary", "arbitrary"), vmem_limit_bytes=VMEM_LIMIT),
        name="swa",
    )(sinks, qa, kva, kva, eye, band)


TQ_FOX = 1024
TK_FOX = 256
NEG_BIG = -1e30


FOX_UNROLL = 4
FOX_DEPTH = 8


def _fox_body(q_ref, k_ref, v_ref, eye_ref, o_ref, vt_ref, qt_ref, acc_ref):
    tq = q_ref.shape[1]
    tk = TK_FOX
    S = k_ref.shape[1]
    pair = pl.program_id(1)
    qi = pl.program_id(2)
    eye = eye_ref[...]
    n_tiles = 2 * tq // MXU_DIM
    tiles_per_head = tq // MXU_DIM

    @pl.when(qi == 0)
    def _():
        ones = jnp.ones((VT_ROWS - HEAD, tk), BF16)
        for j in range(S // tk):
            blk = slice(j * tk, (j + 1) * tk)
            v_t = _dot_nt(eye[0:LANES, 0:LANES], v_ref[0, blk, :]).astype(BF16)
            vt_ref[:, blk] = jnp.concatenate([v_t[0:HEAD], ones, v_t[HEAD:], ones], axis=0)

    lane = lax.broadcasted_iota(jnp.int32, (tq, LANES), 1)
    lane_a = lane < HEAD
    q2 = q_ref[0]
    zero = jnp.zeros_like(q2)

    def minus_one_at(head):
        hit = (lane == head) | (lane == head + 4) | (lane == head + 8)
        return jnp.where(hit, -1.0, 0.0).astype(BF16)

    q_aug = [jnp.concatenate([jnp.where(lane_a, q2, zero), minus_one_at(2 * pair)], axis=1),
             jnp.concatenate([jnp.where(lane_a, zero, q2), minus_one_at(2 * pair + 1)], axis=1)]
    for t in range(n_tiles):
        hd, part = divmod(t, tiles_per_head)
        rows = q_aug[hd][part * MXU_DIM:(part + 1) * MXU_DIM]
        qt_ref[:, t * MXU_DIM:(t + 1) * MXU_DIM] = _dot_nt(eye, rows).astype(BF16)
    acc_ref[...] = jnp.zeros_like(acc_ref)

    key = lax.broadcasted_iota(jnp.int32, (tk // 2, MXU_DIM), 0)
    qry = lax.broadcasted_iota(jnp.int32, (tk // 2, MXU_DIM), 1)

    def fold(s, t, vt_blk, m, triangular):
        hd = t // tiles_per_head
        vt_h = vt_blk[hd * VT_ROWS:(hd + 1) * VT_ROWS]
        half = tk // 2
        halves = [s[h * half:(h + 1) * half] for h in range(2)]
        if triangular:
            valid = [key + h * half <= qry for h in range(2)]
            halves = [jnp.where(valid[h], halves[h], NEG_BIG) for h in range(2)]
        m_new = jnp.maximum(m, jnp.maximum(jnp.max(halves[0], axis=0, keepdims=True),
                                           jnp.max(halves[1], axis=0, keepdims=True)))
        alpha = jnp.exp2(m - m_new)
        ps = [jnp.exp2(halves[h] - m_new) for h in range(2)]
        if triangular:
            ps = [jnp.where(valid[h], ps[h], 0.0) for h in range(2)]
        p = jnp.concatenate([ps[0].astype(BF16), ps[1].astype(BF16)], axis=0)
        acc_ref[t] = alpha * acc_ref[t] + _dot(vt_h, p)
        return m_new

    def region(j0, ms, plans):
        k_blks, vt_blks, todo = [], [], []
        for b, plan in enumerate(plans):
            r0 = pl.multiple_of((j0 + b) * tk, tk)
            k_blks.append(k_ref[0, pl.ds(r0, tk), :])
            vt_blks.append(vt_ref[:, pl.ds(r0, tk)])
            todo += [(b, t) for t in range(n_tiles) if plan[t] is not None]
        score = lambda b, t: _dot(k_blks[b], qt_ref[:, t * MXU_DIM:(t + 1) * MXU_DIM])
        ms = list(ms)
        queue = [score(*unit) for unit in todo[:FOX_DEPTH]]
        for n, (b, t) in enumerate(todo):
            s_cur = queue.pop(0)
            if n + FOX_DEPTH < len(todo):
                queue.append(score(*todo[n + FOX_DEPTH]))
            ms[t] = fold(s_cur, t, vt_blks[b], ms[t], plans[b][t])
        return tuple(ms)

    blocks_per_tile = tq // tk
    full = (False,) * n_tiles
    n_full = blocks_per_tile * qi
    m_init = jnp.full((1, MXU_DIM), NEG_BIG, F32)
    ms = lax.fori_loop(0, n_full // FOX_UNROLL,
                       lambda g, c: region(g * FOX_UNROLL, c, (full,) * FOX_UNROLL), (m_init,) * n_tiles)
    if FOX_UNROLL > blocks_per_tile:
        done = (n_full // FOX_UNROLL) * FOX_UNROLL
        ms = lax.fori_loop(0, (n_full - done) // blocks_per_tile,
                           lambda g, c: region(done + g * blocks_per_tile, c, (full,) * blocks_per_tile), ms)
    diag = [tuple(None if t % tiles_per_head < d else t % tiles_per_head == d for t in range(n_tiles))
            for d in range(blocks_per_tile)]
    ms = region(n_full, ms, diag)

    for hd in range(2):
        cols = []
        for part in range(tiles_per_head):
            acc = acc_ref[hd * tiles_per_head + part]
            cols.append(acc[0:HEAD] / acc[HEAD:HEAD + 1])
        o_h = jnp.concatenate(cols, axis=1)
        if hd == 0:
            o_t = o_h
        else:
            o_t = jnp.concatenate([o_t, o_h], axis=0).astype(BF16)
    for c in range(tq // MXU_DIM):
        rows = slice(c * MXU_DIM, (c + 1) * MXU_DIM)
        o_ref[0, rows, :] = _dot_nt(eye, o_t[:, rows]).astype(BF16)


def _fox(qb, kb, vb, eye):
    B, S, _ = qb.shape
    tq = min(TQ_FOX, S)
    return pl.pallas_call(
        _fox_body,
        grid=(B, FOX_HEADS // 2, S // tq),
        in_specs=[
            pl.BlockSpec((1, tq, LANES), lambda b, p, i: (b, i, p)),
            pl.BlockSpec((1, S, MXU_DIM), lambda b, p, i: (b, 0, p)),
            pl.BlockSpec((1, S, LANES), lambda b, p, i: (b, 0, p)),
            _const_spec((MXU_DIM, MXU_DIM)),
        ],
        out_specs=pl.BlockSpec((1, tq, LANES), lambda b, p, i: (b, i, p)),
        out_shape=jax.ShapeDtypeStruct((B, S, FOX_W), BF16),
        scratch_shapes=[
            pltpu.VMEM((2 * VT_ROWS, S), BF16),
            pltpu.VMEM((MXU_DIM, 2 * tq), BF16),
            pltpu.VMEM((2 * tq // MXU_DIM, VT_ROWS, MXU_DIM), F32),
        ],
        compiler_params=pltpu.CompilerParams(
            dimension_semantics=("arbitrary", "arbitrary", "arbitrary"),
            vmem_limit_bytes=VMEM_LIMIT),
        name="fox",
    )(qb, kb, vb, eye)


TS_MLSTM = 1024


def _mlstm_body(qkv_ref, g_ref, onorm_ref, tri_ref, i4_ref, bd_ref, o_ref, c_ref, n_ref, m_ref):
    ts = qkv_ref.shape[1]
    W = MLSTM_W

    @pl.when(pl.program_id(1) == 0)
    def _():
        c_ref[...] = jnp.zeros_like(c_ref)
        n_ref[...] = jnp.zeros_like(n_ref)
        m_ref[...] = jnp.zeros_like(m_ref)

    row = lax.broadcasted_iota(jnp.int32, (CHUNK, W), 0)
    lane = lax.broadcasted_iota(jnp.int32, (CHUNK, W), 1)
    causal = (lane % HEAD) <= row
    lane_head = lane // HEAD
    bd = bd_ref[...]
    bd2 = jnp.concatenate([bd, bd], axis=1)
    i4_bf = i4_ref[...].astype(BF16)

    def block_diag(x):
        tiled = jnp.concatenate([x] * MLSTM_HEADS, axis=0)
        return tiled * (bd if x.shape[1] == W else bd2)

    def seg_max(d):
        out = jnp.zeros_like(d)
        for hd in range(MLSTM_HEADS):
            sel = lane_head == hd
            mx = jnp.max(jnp.where(sel, d, -jnp.inf), axis=-1, keepdims=True)
            out = jnp.where(sel, mx, out)
        return out

    chunks = range(ts // CHUNK)
    rows = [pl.ds(c * CHUNK, CHUNK) for c in chunks]

    b, g, u, dm, dm_max, a_max, wa = [], [], [], [], [], [], []
    for c in chunks:
        ig = g_ref[0, rows[c], 0:W]
        lf_hi, lf_lo = _split2(g_ref[0, rows[c], W:2 * W])
        b_c = _dot(tri_ref[...], lf_hi) + _dot(tri_ref[...], lf_lo)
        g_c = b_c[CHUNK - 1:CHUNK, :]
        u_c = b_c - ig
        u_row = jnp.sum(u_c * i4_ref[...], axis=0, keepdims=True)
        dm_c = jnp.where(causal, b_c - u_row, -jnp.inf)
        a_c = g_c - u_c
        am_c = jnp.max(a_c, axis=0, keepdims=True)
        b.append(b_c); g.append(g_c); u.append(u_c); dm.append(dm_c)
        dm_max.append(seg_max(dm_c)); a_max.append(am_c); wa.append(jnp.exp(a_c - am_c))

    m_prev, s_old, s_loc = [m_ref[0:1, :]], [], []
    for c in chunks:
        gm = g[c] + m_prev[c]
        m_new = jnp.maximum(gm, a_max[c])
        s_old.append(jnp.exp(gm - m_new))
        s_loc.append(jnp.exp(a_max[c] - m_new))
        m_prev.append(m_new)

    m_t, w_inter, sm_hi, sm_lo, k_t = [], [], [], [], []
    for c in chunks:
        inter = b[c] + m_prev[c]
        mt_c = jnp.maximum(inter, dm_max[c])
        decay = jnp.exp(dm[c] - mt_c)
        q = qkv_ref[0, rows[c], 0:W]
        k = qkv_ref[0, rows[c], W:2 * W]
        qk = _dot_nt(jnp.concatenate([q, i4_bf], axis=0), block_diag(k))
        hi, lo = _split2(qk[0:CHUNK] * decay)
        m_t.append(mt_c); w_inter.append(jnp.exp(inter - mt_c)); sm_hi.append(hi); sm_lo.append(lo)
        k_t.append(qk[CHUNK:].astype(BF16))

    c_cur, n_cur = c_ref[...], n_ref[...]
    state_bf = []
    for c in chunks:
        state_bf.append(jnp.concatenate([c_cur.astype(BF16), n_cur.astype(BF16)], axis=1))
        v = qkv_ref[0, rows[c], 2 * W:]
        weighted = jnp.concatenate([(wa[c] * v.astype(F32)).astype(BF16), wa[c].astype(BF16)], axis=1)
        loc = _dot(k_t[c], block_diag(weighted))
        c_cur = s_old[c] * c_cur + s_loc[c] * loc[:, 0:W]
        n_cur = s_old[c] * n_cur + s_loc[c] * loc[:, W:]
    c_ref[...] = c_cur
    n_ref[...] = n_cur
    m_ref[...] = jnp.broadcast_to(m_prev[-1], m_ref.shape)

    hval = []
    for c in chunks:
        q = qkv_ref[0, rows[c], 0:W]
        v = qkv_ref[0, rows[c], 2 * W:]
        inter_cn = _dot(q, block_diag(state_bf[c]))
        intra = _dot(sm_hi[c], jnp.concatenate([block_diag(v), bd], axis=1))
        num = w_inter[c] * inter_cn[:, 0:W] + intra[:, 0:W]
        den = w_inter[c] * inter_cn[:, W:] + intra[:, W:] + _dot(sm_lo[c], bd)
        hval.append(num / jnp.maximum(jnp.abs(den), jnp.exp(-m_t[c])))

    for c in chunks:
        hs_hi, hs_lo = _split2(hval[c] * hval[c])
        ss = _dot(hs_hi, bd) + _dot(hs_lo, bd)
        hc = (hval[c] * lax.rsqrt(ss * (1.0 / HEAD) + EPS)) * onorm_ref[...]
        co = g_ref[0, rows[c], 2 * W:]
        o_ref[0, rows[c], :] = (jax.nn.sigmoid(co) * hc).astype(BF16)


def _mlstm(layer, qkvc, gc, onorm, tri, i4, bd):
    B, S, _ = qkvc.shape
    ts = min(TS_MLSTM, S)
    return pl.pallas_call(
        _mlstm_body,
        grid=(B, S // ts),
        in_specs=[
            pl.BlockSpec((1, ts, 3 * MLSTM_W), lambda b, i: (b, i, 0)),
            pl.BlockSpec((1, ts, 3 * MLSTM_W), lambda b, i: (b, i, 0)),
            _layer_spec((1, MLSTM_W), layer),
            _const_spec((CHUNK, CHUNK)),
            _const_spec((CHUNK, MLSTM_W)),
            _const_spec((MLSTM_W, MLSTM_W)),
        ],
        out_specs=pl.BlockSpec((1, ts, MLSTM_W), lambda b, i: (b, i, 0)),
        out_shape=jax.ShapeDtypeStruct((B, S, MLSTM_W), BF16),
        scratch_shapes=[pltpu.VMEM((HEAD, MLSTM_W), F32), pltpu.VMEM((HEAD, MLSTM_W), F32),
                        pltpu.VMEM((8, MLSTM_W), F32)],
        compiler_params=pltpu.CompilerParams(
            dimension_semantics=("arbitrary", "arbitrary"), vmem_limit_bytes=VMEM_LIMIT),
        name="mlstm",
    )(qkvc, gc, onorm, tri, i4, bd)


TM_FFN = 512
FF_CHUNK = 1024


def _out_ffn_body(x_ref, ya_ref, yb_ref, yc_ref, wo_ref, n2_ref, w1_ref, w2_ref, o_ref, a_ref):
    x1 = (x_ref[...] + _dot(ya_ref[...], wo_ref[0:SWA_Q, :])
          + _dot(yb_ref[...], wo_ref[SWA_Q:SWA_Q + FOX_W, :])
          + _dot(yc_ref[...], wo_ref[SWA_Q + FOX_W:, :]))
    ms = jnp.mean(x1 * x1, axis=-1, keepdims=True)
    h2 = ((x1 * lax.rsqrt(ms + EPS)) * n2_ref[...]).astype(BF16)
    for j in range(D_FF // FF_CHUNK):
        cols = slice(j * FF_CHUNK, (j + 1) * FF_CHUNK)
        r = jnp.maximum(_dot(h2, w1_ref[:, cols]), 0.0)
        a_ref[:, cols] = (r * r).astype(BF16)
    o_ref[...] = x1 + _dot(a_ref[...], w2_ref[...])


def _out_ffn(layer, x2d, ya, yb, yc, wo, n2, w1, w2):
    T = x2d.shape[0]
    tm = min(TM_FFN, T)
    tok = lambda w: pl.BlockSpec((tm, w), lambda i: (i, 0))
    return pl.pallas_call(
        _out_ffn_body,
        grid=(T // tm,),
        in_specs=[
            tok(D_MODEL), tok(SWA_Q), tok(FOX_W), tok(MLSTM_W),
            _layer_spec((D_MODEL, D_MODEL), layer),
            _layer_spec((1, D_MODEL), layer),
            _layer_spec((D_MODEL, D_FF), layer),
            _layer_spec((D_FF, D_MODEL), layer),
        ],
        out_specs=tok(D_MODEL),
        out_shape=jax.ShapeDtypeStruct((T, D_MODEL), F32),
        scratch_shapes=[pltpu.VMEM((tm, D_FF), BF16)],
        compiler_params=pltpu.CompilerParams(
            dimension_semantics=("arbitrary",), vmem_limit_bytes=VMEM_LIMIT),
        name="out_ffn",
    )(x2d, ya, yb, yc, wo, n2, w1, w2)


def _rope_tables(S):
    inv = ROPE_THETA ** (-np.arange(HALF, dtype=np.float64) / HALF)
    ang = np.arange(S, dtype=np.float64)[:, None] * inv[None, :]
    cos = np.tile(np.cos(ang), (1, LANES // HALF))
    sin = np.tile(np.sin(ang), (1, LANES // HALF))
    sign = np.where(np.arange(LANES) < HEAD, -1.0, 1.0)
    return jnp.asarray(cos, F32), jnp.asarray(sin * sign[None, :], F32)


def _prep_w_in(w_in):
    select = lambda w, p: jnp.einsum("ldk,kn->ldn", w.astype(BF16), jnp.asarray(p, BF16),
                                     preferred_element_type=BF16)
    w_b = w_in[:, :, W_A:W_A + W_B].astype(BF16)
    return select(w_in[:, :, :W_A], _PERM_A), w_b, select(w_in[:, :, W_A + W_B:], _PERM_C)


def _prep_vecs(swa_q_norm, swa_k_norm, fox_q_norm, fox_k_norm, fox_f_bias, mlstm_i_bias, mlstm_f_bias):
    L = swa_q_norm.shape[0]

    def swa_lanes(g):
        return jnp.concatenate([g[:, :HALF], g[:, :HALF], g[:, HALF:], g[:, HALF:]], axis=-1)

    pad = lambda v: jnp.pad(v, ((0, 0), (0, MLSTM_W - v.shape[1])))
    return jnp.stack([
        pad(swa_lanes(swa_q_norm)), pad(swa_lanes(swa_k_norm)),
        pad(jnp.tile(fox_q_norm, (1, 2))), pad(jnp.tile(fox_k_norm, (1, 2))),
        pad(jnp.concatenate([jnp.tile(fox_f_bias, (1, 3)), mlstm_i_bias, mlstm_f_bias], axis=-1)),
        jnp.zeros((L, MLSTM_W), F32), jnp.zeros((L, MLSTM_W), F32), jnp.zeros((L, MLSTM_W), F32),
    ], axis=1).astype(F32)


def kernel(x, norm1, w_in, swa_q_norm, swa_k_norm, swa_sinks, fox_q_norm, fox_k_norm, fox_f_bias,
           mlstm_i_bias, mlstm_f_bias, mlstm_out_norm, w_out, norm2, w_ff1, w_ff2):
    B, S, D = x.shape
    L = norm1.shape[0]
    T = B * S
    cos, sin = _rope_tables(S)
    tri256, tri64, bd256 = (jnp.asarray(c, BF16) for c in (_TRI256, _TRI64, _BD256))
    i4 = jnp.asarray(_I4, F32)
    e3 = jnp.asarray(_gate_expand(), BF16)
    eye256 = jnp.eye(MXU_DIM, dtype=BF16)
    band = jnp.asarray(_swa_band_cap(), F32)

    w_a, w_b, w_c = _prep_w_in(w_in)
    wo = w_out.astype(BF16)
    w1 = w_ff1.astype(BF16)
    w2 = w_ff2.astype(BF16)
    vecs = _prep_vecs(swa_q_norm, swa_k_norm, fox_q_norm, fox_k_norm, fox_f_bias, mlstm_i_bias,
                      mlstm_f_bias)
    n1 = norm1.reshape(L, 1, D).astype(F32)
    n2 = norm2.reshape(L, 1, D).astype(F32)
    onorm = mlstm_out_norm.reshape(L, 1, MLSTM_W).astype(F32)
    sinks = swa_sinks.reshape(L * SWA_HEADS).astype(F32)

    for l in range(L):
        qa, kva, qb, kb, vb, qkvc, gc = _in_proj(l, x, n1, w_a, w_b, w_c, vecs, cos, sin, tri256, e3)
        ya = _swa(l, sinks, qa, kva, eye256, band)
        yb = _fox(qb, kb, vb, eye256)
        yc = _mlstm(l, qkvc, gc, onorm, tri64, i4, bd256)
        x = _out_ffn(l, x.reshape(T, D), ya.reshape(T, SWA_Q), yb.reshape(T, FOX_W),
                     yc.reshape(T, MLSTM_W), wo, n2, w1, w2).reshape(B, S, D)
    return x
```

```python
import functools
import math

import numpy as np
import jax
import jax.numpy as jnp
from jax import lax
from jax.experimental import pallas as pl
from jax.experimental.pallas import tpu as pltpu

F32 = jnp.float32
BF16 = jnp.bfloat16

D_MODEL = 1024
HEAD = 64
HALF = HEAD // 2
CHUNK = 64
SWA_HEADS = 8
SWA_KV_HEADS = 2
FOX_HEADS = 4
MLSTM_HEADS = 4
SWA_Q = SWA_HEADS * HEAD
SWA_KV = SWA_KV_HEADS * HEAD
FOX_W = FOX_HEADS * HEAD
MLSTM_W = MLSTM_HEADS * HEAD
D_FF = 4 * D_MODEL
QBLK = 128
ROPE_THETA = 10000.0
EPS = 1e-6
QK_SCALE = 1.0 / math.sqrt(HEAD)
LOG2E = math.log2(math.e)

LANES = 128
MXU_DIM = 256
VMEM_LIMIT = 56 * 1024 * 1024

W_A = SWA_Q + 2 * SWA_KV
W_B = 3 * FOX_W
SEC_C = 0
SEC_CO = SEC_C + 768
SEC_G = SEC_CO + 256
W_C = SEC_G + LANES
GATE_BF = 0
GATE_CI = 12
GATE_CF = 16
IN_WIDTH = W_A + W_B + FOX_HEADS + 4 * MLSTM_W + 2 * MLSTM_HEADS


def _perm_a():
    p = np.zeros((W_A, W_A), np.float32)
    lane = np.arange(LANES)
    quarter, i = lane // HALF, lane % HALF
    dim, is_b = i + HALF * (quarter // 2), quarter % 2
    for j in range(SWA_HEADS // 2):
        p[(j + 4 * is_b) * HEAD + dim, j * LANES + lane] = 1.0
    p[SWA_Q + is_b * HEAD + dim, SWA_Q + lane] = 1.0
    p[SWA_Q + SWA_KV + lane, SWA_Q + SWA_KV + lane] = 1.0
    return p


def _perm_c():
    src = IN_WIDTH - W_A - W_B
    o_bf, o_cqkv = 0, FOX_HEADS
    o_ci = o_cqkv + 3 * MLSTM_W
    o_cf, o_co = o_ci + MLSTM_HEADS, o_ci + 2 * MLSTM_HEADS
    p = np.zeros((src, W_C), np.float32)
    w = np.arange(3 * MLSTM_W)
    p[o_cqkv + w, SEC_C + w] = 1.0
    w = np.arange(MLSTM_W)
    p[o_co + w, SEC_CO + w] = 1.0
    h = np.arange(FOX_HEADS)
    for rep in range(3):
        p[o_bf + h, SEC_G + GATE_BF + rep * FOX_HEADS + h] = 1.0
    h = np.arange(MLSTM_HEADS)
    p[o_ci + h, SEC_G + GATE_CI + h] = 1.0
    p[o_cf + h, SEC_G + GATE_CF + h] = 1.0
    return p


def _gate_expand():
    e = np.zeros((LANES, 2 * MLSTM_W), np.float32)
    for h in range(MLSTM_HEADS):
        e[GATE_CI + h, h * HEAD:(h + 1) * HEAD] = 1.0
        e[GATE_CF + h, MLSTM_W + h * HEAD:MLSTM_W + (h + 1) * HEAD] = 1.0
    return np.concatenate([e, e, e], axis=0)


_PERM_A = _perm_a()
_PERM_C = _perm_c()


def _block_diag_ones(head_of_lane):
    return (head_of_lane[:, None] == head_of_lane[None, :]).astype(np.float32)


_TRI256 = np.tril(np.ones((MXU_DIM, MXU_DIM), np.float32))
_TRI64 = np.tril(np.ones((CHUNK, CHUNK), np.float32))
_w256 = np.arange(MLSTM_W)
_BD256 = _block_diag_ones(_w256 // HEAD)
_I4 = (np.arange(CHUNK)[:, None] == (_w256 % HEAD)[None, :]).astype(np.float32)


def _swa_band_cap():
    key_chunk = np.arange(2 * QBLK)[:, None] // CHUNK
    qry_chunk = (np.arange(MXU_DIM)[None, :] % QBLK) // CHUNK
    ahead = key_chunk - qry_chunk
    return np.where((ahead >= 0) & (ahead <= 2), np.inf, -np.inf).astype(np.float32)


def _const_spec(shape):
    nd = len(shape)
    return pl.BlockSpec(shape, lambda *_: (0,) * nd, pipeline_mode=pl.Buffered(1))


def _layer_spec(shape, layer):
    nd = len(shape)
    return pl.BlockSpec((None,) + tuple(shape), lambda *_: (layer,) + (0,) * nd,
                        pipeline_mode=pl.Buffered(1))


def _log_sigmoid(x):
    return jnp.minimum(x, 0.0) - jnp.log(1.0 + jnp.exp(-jnp.abs(x)))


def _split2(x):
    hi = x.astype(BF16)
    lo = (x - hi.astype(F32)).astype(BF16)
    return hi, lo


def _split3_by_lane(x, lane):
    hi = x.astype(BF16).astype(F32)
    r1 = x - hi
    mid = r1.astype(BF16).astype(F32)
    low = r1 - mid
    return jnp.where(lane < 4, hi, jnp.where(lane < 8, mid, low)).astype(BF16)


def _dot(a, b):
    return jnp.dot(a, b, preferred_element_type=F32)


def _dot_nt(a, b):
    return lax.dot_general(a, b, (((1,), (1,)), ((), ())), preferred_element_type=F32)


def _dot_tn(a, b):
    return lax.dot_general(a, b, (((0,), (0,)), ((), ())), preferred_element_type=F32)


TM_IN = 1024


def _in_proj_body(x_ref, n1_ref, wa_ref, wb_ref, wc_ref, vec_ref, cos_ref, sin_ref, tri_ref, e3_ref,
                  wff1_ref, wff2_ref,
                  qa_ref, kva_ref, qb_ref, kb_ref, vb_ref, qkvc_ref, gc_ref, wff1_bf_ref, wff2_bf_ref,
                  carry_ref):
    tm = x_ref.shape[1]
    wff1_bf_ref[...] = wff1_ref[...].astype(BF16)
    wff2_bf_ref[...] = wff2_ref[...].astype(BF16)

    @pl.when(pl.program_id(1) == 0)
    def _():
        carry_ref[...] = jnp.zeros_like(carry_ref)

    x = x_ref[0]
    ms = jnp.mean(x * x, axis=-1, keepdims=True)
    h = ((x * lax.rsqrt(ms + EPS)) * n1_ref[...]).astype(BF16)

    def proj(w_ref, c0, width):
        return _dot(h, w_ref[:, c0:c0 + width])

    lane_row = lax.broadcasted_iota(jnp.int32, (1, LANES), 1)
    first_swa = (lane_row // HALF) % 2 == 0
    first_nat = lane_row < HEAD

    def head_norm(z, first, gain):
        sq = z * z
        s_first = jnp.sum(jnp.where(first, sq, 0.0), axis=-1, keepdims=True)
        s_second = jnp.sum(jnp.where(first, 0.0, sq), axis=-1, keepdims=True)
        ss = jnp.where(first, s_first, s_second)
        return (z * lax.rsqrt(ss * (1.0 / HEAD) + EPS)) * gain

    cos = cos_ref[...]
    sin = sin_ref[...]

    def rope(z):
        return z * cos + pltpu.roll(z, HEAD, 1) * sin

    g_aq = vec_ref[0:1, 0:LANES]
    g_ak = vec_ref[1:2, 0:LANES]
    g_bq = vec_ref[2:3, 0:LANES]
    g_bk = vec_ref[3:4, 0:LANES]
    gate_bias = vec_ref[4:5, 0:LANES]

    lane = lax.broadcasted_iota(jnp.int32, (MXU_DIM, LANES), 1)
    zg = proj(wc_ref, SEC_G, LANES) + gate_bias
    log_sig = _log_sigmoid(zg)
    log_f = jnp.where(lane[0:1] < GATE_CI, log_sig, 0.0)
    f_terms = [_split3_by_lane(log_f[blk * MXU_DIM:(blk + 1) * MXU_DIM], lane)
               for blk in range(tm // MXU_DIM)]
    gz = jnp.where(lane[0:1] < GATE_CI, 0.0,
                   jnp.where(lane[0:1] < GATE_CF, zg,
                             jnp.where(lane[0:1] < GATE_CF + MLSTM_HEADS, log_sig, 0.0)))
    gz_hi = gz.astype(BF16)
    gz_r = gz - gz_hi.astype(F32)
    gz_mid = gz_r.astype(BF16)
    gz_terms = jnp.concatenate([gz_hi, gz_mid, (gz_r - gz_mid.astype(F32)).astype(BF16)], axis=1)

    za = proj(wa_ref, 0, W_A)
    for j in range(SWA_HEADS // 2):
        q = rope(head_norm(za[:, j * LANES:(j + 1) * LANES], first_swa, g_aq)) * (QK_SCALE * LOG2E)
        qa_ref[0, :, j * LANES:(j + 1) * LANES] = q.astype(BF16)
    k = rope(head_norm(za[:, SWA_Q:SWA_Q + LANES], first_swa, g_ak))
    kva_ref[0, :, 0:LANES] = k.astype(BF16)
    kva_ref[0, :, LANES:2 * LANES] = za[:, SWA_Q + SWA_KV:].astype(BF16)

    zb = proj(wb_ref, 0, W_B)
    for p in range(FOX_HEADS // 2):
        q = head_norm(zb[:, p * LANES:(p + 1) * LANES], first_nat, g_bq) * (QK_SCALE * LOG2E)
        qb_ref[0, :, p * LANES:(p + 1) * LANES] = q.astype(BF16)
        k = head_norm(zb[:, FOX_W + p * LANES:FOX_W + (p + 1) * LANES], first_nat, g_bk)
        kb_ref[0, :, 2 * p * LANES:(2 * p + 1) * LANES] = k.astype(BF16)
    vb_ref[0] = zb[:, 2 * FOX_W:].astype(BF16)

    zc = proj(wc_ref, SEC_C, 768)
    qkvc_ref[0, :, 0:MLSTM_W] = zc[:, 0:MLSTM_W].astype(BF16)
    qkvc_ref[0, :, MLSTM_W:2 * MLSTM_W] = (zc[:, MLSTM_W:2 * MLSTM_W] * QK_SCALE).astype(BF16)
    qkvc_ref[0, :, 2 * MLSTM_W:] = zc[:, 2 * MLSTM_W:].astype(BF16)
    gc_ref[0, :, 2 * MLSTM_W:] = proj(wc_ref, SEC_CO, MLSTM_W)

    gc_ref[0, :, 0:2 * MLSTM_W] = _dot(gz_terms, e3_ref[...])

    carry = carry_ref[0:1, :]
    for blk in range(tm // MXU_DIM):
        rows = slice(blk * MXU_DIM, (blk + 1) * MXU_DIM)
        cs = _dot(tri_ref[...], f_terms[blk])
        f_loc = cs + pltpu.roll(cs, LANES - 4, 1) + pltpu.roll(cs, LANES - 8, 1)
        f_cum = jnp.where(lane < 4, f_loc, 0.0) + carry
        carry = f_cum[MXU_DIM - 1:MXU_DIM, :]
        f_rep = (f_cum + pltpu.roll(f_cum, 4, 1) + pltpu.roll(f_cum, 8, 1)) * LOG2E
        aug = _split3_by_lane(f_rep, lane)
        kb_ref[0, rows, LANES:2 * LANES] = aug
        kb_ref[0, rows, 3 * LANES:4 * LANES] = aug
    carry_ref[...] = jnp.broadcast_to(carry, carry_ref.shape)


def _in_proj(layer, x, n1, w_a, w_b, w_c, vecs, cos, sin, tri, e3, w_ff1, w_ff2):
    B, S, _ = x.shape
    tm = min(TM_IN, S)
    n_i = S // tm
    grid = (B, n_i)
    steps = B * n_i
    assert D_MODEL % (16 * steps) == 0, "FFN weight slabs must be whole bf16 sublane tiles"
    r1, r2 = D_MODEL // steps, D_FF // steps
    tok = lambda w: pl.BlockSpec((1, tm, w), lambda b, i: (b, i, 0))
    out_shapes = (
        jax.ShapeDtypeStruct((B, S, SWA_Q), BF16),
        jax.ShapeDtypeStruct((B, S, 2 * SWA_KV), BF16),
        jax.ShapeDtypeStruct((B, S, FOX_W), BF16),
        jax.ShapeDtypeStruct((B, S, 2 * MXU_DIM), BF16),
        jax.ShapeDtypeStruct((B, S, FOX_W), BF16),
        jax.ShapeDtypeStruct((B, S, 3 * MLSTM_W), BF16),
        jax.ShapeDtypeStruct((B, S, 3 * MLSTM_W), F32),
        jax.ShapeDtypeStruct((D_MODEL, D_FF), BF16),
        jax.ShapeDtypeStruct((D_FF, D_MODEL), BF16),
    )
    slab_out = [pl.BlockSpec((r1, D_FF), lambda b, i: (b * n_i + i, 0)),
                pl.BlockSpec((r2, D_MODEL), lambda b, i: (b * n_i + i, 0))]
    return pl.pallas_call(
        _in_proj_body,
        grid=grid,
        in_specs=[
            tok(D_MODEL),
            _layer_spec((1, D_MODEL), layer),
            _layer_spec((D_MODEL, W_A), layer),
            _layer_spec((D_MODEL, W_B), layer),
            _layer_spec((D_MODEL, W_C), layer),
            _layer_spec((8, MLSTM_W), layer),
            pl.BlockSpec((tm, LANES), lambda b, i: (i, 0)),
            pl.BlockSpec((tm, LANES), lambda b, i: (i, 0)),
            _const_spec((MXU_DIM, MXU_DIM)),
            _const_spec((3 * LANES, 2 * MLSTM_W)),
            pl.BlockSpec((None, r1, D_FF), lambda b, i: (layer, b * n_i + i, 0)),
            pl.BlockSpec((None, r2, D_MODEL), lambda b, i: (layer, b * n_i + i, 0)),
        ],
        out_specs=[tok(s.shape[-1]) for s in out_shapes[:-2]] + slab_out,
        out_shape=out_shapes,
        scratch_shapes=[pltpu.VMEM((8, LANES), F32)],
        compiler_params=pltpu.CompilerParams(
            dimension_semantics=("arbitrary", "arbitrary"), vmem_limit_bytes=VMEM_LIMIT),
        name="in_proj",
    )(x, n1, w_a, w_b, w_c, vecs, cos, sin, tri, e3, w_ff1, w_ff2)


TS_SWA = 1024
SWA_DEPTH = 8
VT_ROWS = HEAD + 16


def _swa_body(sink_base, sink_ref, q_ref, kvc_ref, kvp_ref, eye_ref, band_ref, o_ref):
    step = pl.program_id(1)
    nblk = q_ref.shape[1] // QBLK
    per_group = SWA_HEADS // SWA_KV_HEADS
    eye = eye_ref[0:LANES, 0:LANES]

    band_cap = band_ref[...]
    key = lax.broadcasted_iota(jnp.int32, (2 * QBLK, MXU_DIM), 0)
    first_cap = jnp.where((key >= QBLK) | (step > 0), band_cap, -jnp.inf)
    lane = lax.broadcasted_iota(jnp.int32, (QBLK, LANES), 1)
    lane_a = (lane // HALF) % 2 == 0
    first_head = lax.broadcasted_iota(jnp.int32, (1, MXU_DIM), 1) < QBLK
    zero = jnp.zeros((QBLK, LANES), BF16)
    ones = jnp.ones((VT_ROWS - HEAD, 2 * QBLK), BF16)
    kv_all = jnp.concatenate([kvp_ref[0], kvc_ref[0]], axis=0)
    rows = [slice(blk * QBLK, (blk + 1) * QBLK) for blk in range(nblk)]
    bands = [kv_all[blk * QBLK:(blk + 2) * QBLK] for blk in range(nblk)]

    vt_aug = []
    for blk in range(nblk):
        v_t = _dot_nt(eye, bands[blk][:, LANES:]).astype(BF16)
        vt_aug.append([jnp.concatenate([v_t[g * HEAD:(g + 1) * HEAD], ones], axis=0)
                       for g in range(SWA_KV_HEADS)])

    units = [(blk, g, part) for blk in range(nblk) for g in range(SWA_KV_HEADS) for part in range(2)]

    def score(blk, g, part):
        keep = lane_a if g == 0 else jnp.logical_not(lane_a)
        q_cols = [q_ref[0, rows[blk], j * LANES:(j + 1) * LANES] for j in (2 * part, 2 * part + 1)]
        q_g = jnp.concatenate([jnp.where(keep, q2, zero) for q2 in q_cols], axis=0)
        return _dot_nt(bands[blk][:, 0:LANES], q_g)

    def fold(s, blk, g, part):
        s = jnp.minimum(s, band_cap if blk > 0 else first_cap)
        head = sink_base + g * per_group + 2 * part
        sink = jnp.where(first_head, sink_ref[head], sink_ref[head + 1]) * LOG2E
        m = jnp.maximum(jnp.max(s, axis=0, keepdims=True), sink)
        p = jnp.exp2(s - m)
        o_aug = _dot(vt_aug[blk][g], p.astype(BF16))
        den = o_aug[HEAD:HEAD + 1] + jnp.exp2(sink - m)
        return (o_aug[0:HEAD] / den).astype(BF16)

    def write_block(blk, outs):
        for g in range(SWA_KV_HEADS):
            for part in range(2):
                o = outs[(blk, g, part)]
                pair_t = jnp.concatenate([o[:, 0:QBLK], o[:, QBLK:]], axis=0)
                j = 2 * g + part
                o_ref[0, rows[blk], j * LANES:(j + 1) * LANES] = _dot_nt(eye, pair_t).astype(BF16)

    outs = {}
    queue = [score(*unit) for unit in units[:SWA_DEPTH]]
    for n, unit in enumerate(units):
        s_cur = queue.pop(0)
        if n + SWA_DEPTH < len(units):
            queue.append(score(*units[n + SWA_DEPTH]))
        outs[unit] = fold(s_cur, *unit)
        blk = unit[0]
        if unit[1:] == (SWA_KV_HEADS - 1, 1) and blk > 0:
            write_block(blk - 1, outs)
    write_block(nblk - 1, outs)


def _swa(layer, sinks, qa, kva, eye, band):
    B, S, _ = qa.shape
    ts = min(TS_SWA, S)
    per_step = ts // QBLK
    return pl.pallas_call(
        functools.partial(_swa_body, layer * SWA_HEADS),
        grid=(B, S // ts),
        in_specs=[
            pl.BlockSpec(memory_space=pltpu.SMEM),
            pl.BlockSpec((1, ts, SWA_Q), lambda b, i: (b, i, 0)),
            pl.BlockSpec((1, ts, 2 * SWA_KV), lambda b, i: (b, i, 0)),
            pl.BlockSpec((1, QBLK, 2 * SWA_KV), lambda b, i: (b, jnp.maximum(i * per_step - 1, 0), 0)),
            _const_spec((MXU_DIM, MXU_DIM)),
            _const_spec((2 * QBLK, MXU_DIM)),
        ],
        out_specs=pl.BlockSpec((1, ts, SWA_Q), lambda b, i: (b, i, 0)),
        out_shape=jax.ShapeDtypeStruct((B, S, SWA_Q), BF16),
        compiler_params=pltpu.CompilerParams(
            dimension_semantics=("arbitrary", "arbitrary"), vmem_limit_bytes=VMEM_LIMIT),
        name="swa",
    )(sinks, qa, kva, kva, eye, band)


TQ_FOX = 1024
TK_FOX = 256
NEG_BIG = -1e30


FOX_UNROLL = 4
FOX_DEPTH = 8


def _fox_body(q_ref, k_ref, v_ref, eye_ref, o_ref, vt_ref, qt_ref, acc_ref):
    tq = q_ref.shape[1]
    tk = TK_FOX
    S = k_ref.shape[1]
    pair = pl.program_id(1)
    qi = pl.program_id(2)
    eye = eye_ref[...]
    n_tiles = 2 * tq // MXU_DIM
    tiles_per_head = tq // MXU_DIM

    @pl.when(qi == 0)
    def _():
        ones = jnp.ones((VT_ROWS - HEAD, tk), BF16)
        for j in range(S // tk):
            blk = slice(j * tk, (j + 1) * tk)
            v_t = _dot_nt(eye[0:LANES, 0:LANES], v_ref[0, blk, :]).astype(BF16)
            vt_ref[:, blk] = jnp.concatenate([v_t[0:HEAD], ones, v_t[HEAD:], ones], axis=0)

    lane = lax.broadcasted_iota(jnp.int32, (tq, LANES), 1)
    lane_a = lane < HEAD
    q2 = q_ref[0]
    zero = jnp.zeros_like(q2)

    def minus_one_at(head):
        hit = (lane == head) | (lane == head + 4) | (lane == head + 8)
        return jnp.where(hit, -1.0, 0.0).astype(BF16)

    q_aug = [jnp.concatenate([jnp.where(lane_a, q2, zero), minus_one_at(2 * pair)], axis=1),
             jnp.concatenate([jnp.where(lane_a, zero, q2), minus_one_at(2 * pair + 1)], axis=1)]
    for t in range(n_tiles):
        hd, part = divmod(t, tiles_per_head)
        rows = q_aug[hd][part * MXU_DIM:(part + 1) * MXU_DIM]
        qt_ref[:, t * MXU_DIM:(t + 1) * MXU_DIM] = _dot_nt(eye, rows).astype(BF16)
    acc_ref[...] = jnp.zeros_like(acc_ref)

    key = lax.broadcasted_iota(jnp.int32, (tk // 2, MXU_DIM), 0)
    qry = lax.broadcasted_iota(jnp.int32, (tk // 2, MXU_DIM), 1)

    def fold(s, t, vt_blk, m, triangular):
        hd = t // tiles_per_head
        vt_h = vt_blk[hd * VT_ROWS:(hd + 1) * VT_ROWS]
        half = tk // 2
        for h in range(2):
            sh = s[h * half:(h + 1) * half]
            if triangular:
                valid = key + h * half <= qry
                sh = jnp.where(valid, sh, NEG_BIG)
            m_new = jnp.maximum(m, jnp.max(sh, axis=0, keepdims=True))
            alpha = jnp.exp2(m - m_new)
            p = jnp.exp2(sh - m_new)
            if triangular:
                p = jnp.where(valid, p, 0.0)
            acc_ref[t] = alpha * acc_ref[t] + _dot(vt_h[:, h * half:(h + 1) * half], p.astype(BF16))
            m = m_new
        return m

    def region(j0, ms, plans):
        k_blks, vt_blks, todo = [], [], []
        for b, plan in enumerate(plans):
            r0 = pl.multiple_of((j0 + b) * tk, tk)
            k_blks.append(k_ref[0, pl.ds(r0, tk), :])
            vt_blks.append(vt_ref[:, pl.ds(r0, tk)])
            todo += [(b, t) for t in range(n_tiles) if plan[t] is not None]
        score = lambda b, t: _dot(k_blks[b], qt_ref[:, t * MXU_DIM:(t + 1) * MXU_DIM])
        ms = list(ms)
        queue = [score(*unit) for unit in todo[:FOX_DEPTH]]
        for n, (b, t) in enumerate(todo):
            s_cur = queue.pop(0)
            if n + FOX_DEPTH < len(todo):
                queue.append(score(*todo[n + FOX_DEPTH]))
            ms[t] = fold(s_cur, t, vt_blks[b], ms[t], plans[b][t])
        return tuple(ms)

    blocks_per_tile = tq // tk
    full = (False,) * n_tiles
    n_full = blocks_per_tile * qi
    m_init = jnp.full((1, MXU_DIM), NEG_BIG, F32)
    ms = lax.fori_loop(0, n_full // FOX_UNROLL,
                       lambda g, c: region(g * FOX_UNROLL, c, (full,) * FOX_UNROLL), (m_init,) * n_tiles)
    if FOX_UNROLL > blocks_per_tile:
        done = (n_full // FOX_UNROLL) * FOX_UNROLL
        ms = lax.fori_loop(0, (n_full - done) // blocks_per_tile,
                           lambda g, c: region(done + g * blocks_per_tile, c, (full,) * blocks_per_tile), ms)
    diag = [tuple(None if t % tiles_per_head < d else t % tiles_per_head == d for t in range(n_tiles))
            for d in range(blocks_per_tile)]
    ms = region(n_full, ms, diag)

    for hd in range(2):
        cols = []
        for part in range(tiles_per_head):
            acc = acc_ref[hd * tiles_per_head + part]
            cols.append(acc[0:HEAD] / acc[HEAD:HEAD + 1])
        o_h = jnp.concatenate(cols, axis=1)
        if hd == 0:
            o_t = o_h
        else:
            o_t = jnp.concatenate([o_t, o_h], axis=0).astype(BF16)
    for c in range(tq // MXU_DIM):
        rows = slice(c * MXU_DIM, (c + 1) * MXU_DIM)
        o_ref[0, rows, :] = _dot_nt(eye, o_t[:, rows]).astype(BF16)


def _fox(qb, kb, vb, eye):
    B, S, _ = qb.shape
    tq = min(TQ_FOX, S)
    return pl.pallas_call(
        _fox_body,
        grid=(B, FOX_HEADS // 2, S // tq),
        in_specs=[
            pl.BlockSpec((1, tq, LANES), lambda b, p, i: (b, i, p)),
            pl.BlockSpec((1, S, MXU_DIM), lambda b, p, i: (b, 0, p)),
            pl.BlockSpec((1, S, LANES), lambda b, p, i: (b, 0, p)),
            _const_spec((MXU_DIM, MXU_DIM)),
        ],
        out_specs=pl.BlockSpec((1, tq, LANES), lambda b, p, i: (b, i, p)),
        out_shape=jax.ShapeDtypeStruct((B, S, FOX_W), BF16),
        scratch_shapes=[
            pltpu.VMEM((2 * VT_ROWS, S), BF16),
            pltpu.VMEM((MXU_DIM, 2 * tq), BF16),
            pltpu.VMEM((2 * tq // MXU_DIM, VT_ROWS, MXU_DIM), F32),
        ],
        compiler_params=pltpu.CompilerParams(
            dimension_semantics=("arbitrary", "arbitrary", "arbitrary"),
            vmem_limit_bytes=VMEM_LIMIT),
        name="fox",
    )(qb, kb, vb, eye)


TS_MLSTM = 1024


def _mlstm_body(qkv_ref, g_ref, onorm_ref, tri_ref, i4_ref, bd_ref, o_ref, c_ref, n_ref, m_ref):
    ts = qkv_ref.shape[1]
    W = MLSTM_W

    @pl.when(pl.program_id(1) == 0)
    def _():
        c_ref[...] = jnp.zeros_like(c_ref)
        n_ref[...] = jnp.zeros_like(n_ref)
        m_ref[...] = jnp.zeros_like(m_ref)

    row = lax.broadcasted_iota(jnp.int32, (CHUNK, W), 0)
    lane = lax.broadcasted_iota(jnp.int32, (CHUNK, W), 1)
    causal = (lane % HEAD) <= row
    lane_head = lane // HEAD
    bd = bd_ref[...]
    bd2 = jnp.concatenate([bd, bd], axis=1)
    i4_bf = i4_ref[...].astype(BF16)

    def block_diag(x):
        tiled = jnp.concatenate([x] * MLSTM_HEADS, axis=0)
        return tiled * (bd if x.shape[1] == W else bd2)

    def seg_max(d):
        out = jnp.zeros_like(d)
        for hd in range(MLSTM_HEADS):
            sel = lane_head == hd
            mx = jnp.max(jnp.where(sel, d, -jnp.inf), axis=-1, keepdims=True)
            out = jnp.where(sel, mx, out)
        return out

    chunks = range(ts // CHUNK)
    rows = [pl.ds(c * CHUNK, CHUNK) for c in chunks]

    b, g, u, dm, dm_max, a_max, wa = [], [], [], [], [], [], []
    for c in chunks:
        ig = g_ref[0, rows[c], 0:W]
        lf_hi, lf_lo = _split2(g_ref[0, rows[c], W:2 * W])
        b_c = _dot(tri_ref[...], lf_hi) + _dot(tri_ref[...], lf_lo)
        g_c = b_c[CHUNK - 1:CHUNK, :]
        u_c = b_c - ig
        u_row = jnp.sum(u_c * i4_ref[...], axis=0, keepdims=True)
        dm_c = jnp.where(causal, b_c - u_row, -jnp.inf)
        a_c = g_c - u_c
        am_c = jnp.max(a_c, axis=0, keepdims=True)
        b.append(b_c); g.append(g_c); u.append(u_c); dm.append(dm_c)
        dm_max.append(seg_max(dm_c)); a_max.append(am_c); wa.append(jnp.exp(a_c - am_c))

    m_prev, s_old, s_loc = [m_ref[0:1, :]], [], []
    for c in chunks:
        gm = g[c] + m_prev[c]
        m_new = jnp.maximum(gm, a_max[c])
        s_old.append(jnp.exp(gm - m_new))
        s_loc.append(jnp.exp(a_max[c] - m_new))
        m_prev.append(m_new)

    m_t, w_inter, sm_hi, sm_lo, k_t = [], [], [], [], []
    for c in chunks:
        inter = b[c] + m_prev[c]
        mt_c = jnp.maximum(inter, dm_max[c])
        decay = jnp.exp(dm[c] - mt_c)
        q = qkv_ref[0, rows[c], 0:W]
        k = qkv_ref[0, rows[c], W:2 * W]
        qk = _dot_nt(jnp.concatenate([q, i4_bf], axis=0), block_diag(k))
        hi, lo = _split2(qk[0:CHUNK] * decay)
        m_t.append(mt_c); w_inter.append(jnp.exp(inter - mt_c)); sm_hi.append(hi); sm_lo.append(lo)
        k_t.append(qk[CHUNK:].astype(BF16))

    c_cur, n_cur = c_ref[...], n_ref[...]
    state_bf = []
    for c in chunks:
        state_bf.append(jnp.concatenate([c_cur.astype(BF16), n_cur.astype(BF16)], axis=1))
        v = qkv_ref[0, rows[c], 2 * W:]
        weighted = jnp.concatenate([(wa[c] * v.astype(F32)).astype(BF16), wa[c].astype(BF16)], axis=1)
        loc = _dot(k_t[c], block_diag(weighted))
        c_cur = s_old[c] * c_cur + s_loc[c] * loc[:, 0:W]
        n_cur = s_old[c] * n_cur + s_loc[c] * loc[:, W:]
    c_ref[...] = c_cur
    n_ref[...] = n_cur
    m_ref[...] = jnp.broadcast_to(m_prev[-1], m_ref.shape)

    hval = []
    for c in chunks:
        q = qkv_ref[0, rows[c], 0:W]
        v = qkv_ref[0, rows[c], 2 * W:]
        inter_cn = _dot(q, block_diag(state_bf[c]))
        intra = _dot(sm_hi[c], jnp.concatenate([block_diag(v), bd], axis=1))
        num = w_inter[c] * inter_cn[:, 0:W] + intra[:, 0:W]
        den = w_inter[c] * inter_cn[:, W:] + intra[:, W:] + _dot(sm_lo[c], bd)
        hval.append(num / jnp.maximum(jnp.abs(den), jnp.exp(-m_t[c])))

    for c in chunks:
        hs_hi, hs_lo = _split2(hval[c] * hval[c])
        ss = _dot(hs_hi, bd) + _dot(hs_lo, bd)
        hc = (hval[c] * lax.rsqrt(ss * (1.0 / HEAD) + EPS)) * onorm_ref[...]
        co = g_ref[0, rows[c], 2 * W:]
        o_ref[0, rows[c], :] = (jax.nn.sigmoid(co) * hc).astype(BF16)


def _mlstm(layer, qkvc, gc, onorm, tri, i4, bd):
    B, S, _ = qkvc.shape
    ts = min(TS_MLSTM, S)
    return pl.pallas_call(
        _mlstm_body,
        grid=(B, S // ts),
        in_specs=[
            pl.BlockSpec((1, ts, 3 * MLSTM_W), lambda b, i: (b, i, 0)),
            pl.BlockSpec((1, ts, 3 * MLSTM_W), lambda b, i: (b, i, 0)),
            _layer_spec((1, MLSTM_W), layer),
            _const_spec((CHUNK, CHUNK)),
            _const_spec((CHUNK, MLSTM_W)),
            _const_spec((MLSTM_W, MLSTM_W)),
        ],
        out_specs=pl.BlockSpec((1, ts, MLSTM_W), lambda b, i: (b, i, 0)),
        out_shape=jax.ShapeDtypeStruct((B, S, MLSTM_W), BF16),
        scratch_shapes=[pltpu.VMEM((HEAD, MLSTM_W), F32), pltpu.VMEM((HEAD, MLSTM_W), F32),
                        pltpu.VMEM((8, MLSTM_W), F32)],
        compiler_params=pltpu.CompilerParams(
            dimension_semantics=("arbitrary", "arbitrary"), vmem_limit_bytes=VMEM_LIMIT),
        name="mlstm",
    )(qkvc, gc, onorm, tri, i4, bd)


TM_FFN = 512
FF_CHUNK = 1024


def _out_ffn_body(x_ref, ya_ref, yb_ref, yc_ref, wo_ref, n2_ref, w1_ref, w2_ref, o_ref, a_ref):
    x1 = (x_ref[...] + _dot(ya_ref[...], wo_ref[0:SWA_Q, :])
          + _dot(yb_ref[...], wo_ref[SWA_Q:SWA_Q + FOX_W, :])
          + _dot(yc_ref[...], wo_ref[SWA_Q + FOX_W:, :]))
    ms = jnp.mean(x1 * x1, axis=-1, keepdims=True)
    h2 = ((x1 * lax.rsqrt(ms + EPS)) * n2_ref[...]).astype(BF16)
    for j in range(D_FF // FF_CHUNK):
        cols = slice(j * FF_CHUNK, (j + 1) * FF_CHUNK)
        r = jnp.maximum(_dot(h2, w1_ref[:, cols]), 0.0)
        a_ref[:, cols] = (r * r).astype(BF16)
    o_ref[...] = x1 + _dot(a_ref[...], w2_ref[...])


def _out_ffn(layer, x2d, ya, yb, yc, wo, n2, w1, w2):
    T = x2d.shape[0]
    tm = min(TM_FFN, T)
    tok = lambda w: pl.BlockSpec((tm, w), lambda i: (i, 0))
    return pl.pallas_call(
        _out_ffn_body,
        grid=(T // tm,),
        in_specs=[
            tok(D_MODEL), tok(SWA_Q), tok(FOX_W), tok(MLSTM_W),
            _layer_spec((D_MODEL, D_MODEL), layer),
            _layer_spec((1, D_MODEL), layer),
            _const_spec((D_MODEL, D_FF)),
            _const_spec((D_FF, D_MODEL)),
        ],
        out_specs=tok(D_MODEL),
        out_shape=jax.ShapeDtypeStruct((T, D_MODEL), F32),
        scratch_shapes=[pltpu.VMEM((tm, D_FF), BF16)],
        compiler_params=pltpu.CompilerParams(
            dimension_semantics=("arbitrary",), vmem_limit_bytes=VMEM_LIMIT),
        name="out_ffn",
    )(x2d, ya, yb, yc, wo, n2, w1, w2)


def _rope_tables(S):
    inv = ROPE_THETA ** (-np.arange(HALF, dtype=np.float64) / HALF)
    ang = np.arange(S, dtype=np.float64)[:, None] * inv[None, :]
    cos = np.tile(np.cos(ang), (1, LANES // HALF))
    sin = np.tile(np.sin(ang), (1, LANES // HALF))
    sign = np.where(np.arange(LANES) < HEAD, -1.0, 1.0)
    return jnp.asarray(cos, F32), jnp.asarray(sin * sign[None, :], F32)


def _prep_w_in(w_in):
    select = lambda w, p: jnp.einsum("ldk,kn->ldn", w.astype(BF16), jnp.asarray(p, BF16),
                                     preferred_element_type=BF16)
    w_b = w_in[:, :, W_A:W_A + W_B].astype(BF16)
    return select(w_in[:, :, :W_A], _PERM_A), w_b, select(w_in[:, :, W_A + W_B:], _PERM_C)


def _prep_vecs(swa_q_norm, swa_k_norm, fox_q_norm, fox_k_norm, fox_f_bias, mlstm_i_bias, mlstm_f_bias):
    L = swa_q_norm.shape[0]

    def swa_lanes(g):
        return jnp.concatenate([g[:, :HALF], g[:, :HALF], g[:, HALF:], g[:, HALF:]], axis=-1)

    pad = lambda v: jnp.pad(v, ((0, 0), (0, MLSTM_W - v.shape[1])))
    return jnp.stack([
        pad(swa_lanes(swa_q_norm)), pad(swa_lanes(swa_k_norm)),
        pad(jnp.tile(fox_q_norm, (1, 2))), pad(jnp.tile(fox_k_norm, (1, 2))),
        pad(jnp.concatenate([jnp.tile(fox_f_bias, (1, 3)), mlstm_i_bias, mlstm_f_bias], axis=-1)),
        jnp.zeros((L, MLSTM_W), F32), jnp.zeros((L, MLSTM_W), F32), jnp.zeros((L, MLSTM_W), F32),
    ], axis=1).astype(F32)


def kernel(x, norm1, w_in, swa_q_norm, swa_k_norm, swa_sinks, fox_q_norm, fox_k_norm, fox_f_bias,
           mlstm_i_bias, mlstm_f_bias, mlstm_out_norm, w_out, norm2, w_ff1, w_ff2):
    B, S, D = x.shape
    L = norm1.shape[0]
    T = B * S
    cos, sin = _rope_tables(S)
    tri256, tri64, bd256 = (jnp.asarray(c, BF16) for c in (_TRI256, _TRI64, _BD256))
    i4 = jnp.asarray(_I4, F32)
    e3 = jnp.asarray(_gate_expand(), BF16)
    eye256 = jnp.eye(MXU_DIM, dtype=BF16)
    band = jnp.asarray(_swa_band_cap(), F32)

    w_a, w_b, w_c = _prep_w_in(w_in)
    wo = w_out.astype(BF16)
    vecs = _prep_vecs(swa_q_norm, swa_k_norm, fox_q_norm, fox_k_norm, fox_f_bias, mlstm_i_bias,
                      mlstm_f_bias)
    n1 = norm1.reshape(L, 1, D).astype(F32)
    n2 = norm2.reshape(L, 1, D).astype(F32)
    onorm = mlstm_out_norm.reshape(L, 1, MLSTM_W).astype(F32)
    sinks = swa_sinks.reshape(L * SWA_HEADS).astype(F32)

    for l in range(L):
        qa, kva, qb, kb, vb, qkvc, gc, w1, w2 = _in_proj(
            l, x, n1, w_a, w_b, w_c, vecs, cos, sin, tri256, e3, w_ff1, w_ff2)
        ya = _swa(l, sinks, qa, kva, eye256, band)
        yb = _fox(qb, kb, vb, eye256)
        yc = _mlstm(l, qkvc, gc, onorm, tri64, i4, bd256)
        x = _out_ffn(l, x.reshape(T, D), ya.reshape(T, SWA_Q), yb.reshape(T, FOX_W),
                     yc.reshape(T, MLSTM_W), wo, n2, w1, w2).reshape(B, S, D)
    return x
```

```python
import functools
import math

import numpy as np
import jax
import jax.numpy as jnp
from jax import lax
from jax.experimental import pallas as pl
from jax.experimental.pallas import tpu as pltpu

F32 = jnp.float32
BF16 = jnp.bfloat16

D_MODEL = 1024
HEAD = 64
HALF = HEAD // 2
CHUNK = 64
SWA_HEADS = 8
SWA_KV_HEADS = 2
FOX_HEADS = 4
MLSTM_HEADS = 4
SWA_Q = SWA_HEADS * HEAD
SWA_KV = SWA_KV_HEADS * HEAD
FOX_W = FOX_HEADS * HEAD
MLSTM_W = MLSTM_HEADS * HEAD
D_FF = 4 * D_MODEL
QBLK = 128
ROPE_THETA = 10000.0
EPS = 1e-6
QK_SCALE = 1.0 / math.sqrt(HEAD)
LOG2E = math.log2(math.e)

LANES = 128
MXU_DIM = 256
VMEM_LIMIT = 56 * 1024 * 1024

W_A = SWA_Q + 2 * SWA_KV
W_B = 3 * FOX_W
SEC_C = 0
SEC_CO = SEC_C + 768
SEC_G = SEC_CO + 256
W_C = SEC_G + LANES
GATE_BF = 0
GATE_CI = 12
GATE_CF = 16
IN_WIDTH = W_A + W_B + FOX_HEADS + 4 * MLSTM_W + 2 * MLSTM_HEADS


def _perm_a():
    p = np.zeros((W_A, W_A), np.float32)
    lane = np.arange(LANES)
    quarter, i = lane // HALF, lane % HALF
    dim, is_b = i + HALF * (quarter // 2), quarter % 2
    for j in range(SWA_HEADS // 2):
        p[(j + 4 * is_b) * HEAD + dim, j * LANES + lane] = 1.0
    p[SWA_Q + is_b * HEAD + dim, SWA_Q + lane] = 1.0
    p[SWA_Q + SWA_KV + lane, SWA_Q + SWA_KV + lane] = 1.0
    return p


def _perm_c():
    src = IN_WIDTH - W_A - W_B
    o_bf, o_cqkv = 0, FOX_HEADS
    o_ci = o_cqkv + 3 * MLSTM_W
    o_cf, o_co = o_ci + MLSTM_HEADS, o_ci + 2 * MLSTM_HEADS
    p = np.zeros((src, W_C), np.float32)
    w = np.arange(3 * MLSTM_W)
    p[o_cqkv + w, SEC_C + w] = 1.0
    w = np.arange(MLSTM_W)
    p[o_co + w, SEC_CO + w] = 1.0
    h = np.arange(FOX_HEADS)
    for rep in range(3):
        p[o_bf + h, SEC_G + GATE_BF + rep * FOX_HEADS + h] = 1.0
    h = np.arange(MLSTM_HEADS)
    p[o_ci + h, SEC_G + GATE_CI + h] = 1.0
    p[o_cf + h, SEC_G + GATE_CF + h] = 1.0
    return p


def _gate_expand():
    e = np.zeros((LANES, 2 * MLSTM_W), np.float32)
    for h in range(MLSTM_HEADS):
        e[GATE_CI + h, h * HEAD:(h + 1) * HEAD] = 1.0
        e[GATE_CF + h, MLSTM_W + h * HEAD:MLSTM_W + (h + 1) * HEAD] = 1.0
    return np.concatenate([e, e, e], axis=0)


_PERM_A = _perm_a()
_PERM_C = _perm_c()


def _block_diag_ones(head_of_lane):
    return (head_of_lane[:, None] == head_of_lane[None, :]).astype(np.float32)


_TRI256 = np.tril(np.ones((MXU_DIM, MXU_DIM), np.float32))
_TRI64 = np.tril(np.ones((CHUNK, CHUNK), np.float32))
_w256 = np.arange(MLSTM_W)
_BD256 = _block_diag_ones(_w256 // HEAD)
_I4 = (np.arange(CHUNK)[:, None] == (_w256 % HEAD)[None, :]).astype(np.float32)


def _swa_band_cap():
    key_chunk = np.arange(2 * QBLK)[:, None] // CHUNK
    qry_chunk = (np.arange(MXU_DIM)[None, :] % QBLK) // CHUNK
    ahead = key_chunk - qry_chunk
    return np.where((ahead >= 0) & (ahead <= 2), np.inf, -np.inf).astype(np.float32)


def _const_spec(shape):
    nd = len(shape)
    return pl.BlockSpec(shape, lambda *_: (0,) * nd, pipeline_mode=pl.Buffered(1))


def _layer_spec(shape, layer):
    nd = len(shape)
    return pl.BlockSpec((None,) + tuple(shape), lambda *_: (layer,) + (0,) * nd,
                        pipeline_mode=pl.Buffered(1))


def _log_sigmoid(x):
    return jnp.minimum(x, 0.0) - jnp.log(1.0 + jnp.exp(-jnp.abs(x)))


def _split2(x):
    hi = x.astype(BF16)
    lo = (x - hi.astype(F32)).astype(BF16)
    return hi, lo


def _split3_by_lane(x, lane):
    hi = x.astype(BF16).astype(F32)
    r1 = x - hi
    mid = r1.astype(BF16).astype(F32)
    low = r1 - mid
    return jnp.where(lane < 4, hi, jnp.where(lane < 8, mid, low)).astype(BF16)


def _dot(a, b):
    return jnp.dot(a, b, preferred_element_type=F32)


def _dot_nt(a, b):
    return lax.dot_general(a, b, (((1,), (1,)), ((), ())), preferred_element_type=F32)


def _dot_tn(a, b):
    return lax.dot_general(a, b, (((0,), (0,)), ((), ())), preferred_element_type=F32)


TM_IN = 1024


def _in_proj_body(x_ref, n1_ref, wa_ref, wb_ref, wc_ref, vec_ref, cos_ref, sin_ref, tri_ref, e3_ref,
                  wff1_ref, wff2_ref,
                  qa_ref, kva_ref, qb_ref, kb_ref, vb_ref, qkvc_ref, gc_ref, wff1_bf_ref, wff2_bf_ref,
                  carry_ref):
    tm = x_ref.shape[1]
    wff1_bf_ref[...] = wff1_ref[...].astype(BF16)
    wff2_bf_ref[...] = wff2_ref[...].astype(BF16)

    @pl.when(pl.program_id(1) == 0)
    def _():
        carry_ref[...] = jnp.zeros_like(carry_ref)

    x = x_ref[0]
    ms = jnp.mean(x * x, axis=-1, keepdims=True)
    h = ((x * lax.rsqrt(ms + EPS)) * n1_ref[...]).astype(BF16)

    def proj(w_ref, c0, width):
        return _dot(h, w_ref[:, c0:c0 + width])

    lane_row = lax.broadcasted_iota(jnp.int32, (1, LANES), 1)
    first_swa = (lane_row // HALF) % 2 == 0
    first_nat = lane_row < HEAD

    def head_norm(z, first, gain):
        sq = z * z
        s_first = jnp.sum(jnp.where(first, sq, 0.0), axis=-1, keepdims=True)
        s_second = jnp.sum(jnp.where(first, 0.0, sq), axis=-1, keepdims=True)
        ss = jnp.where(first, s_first, s_second)
        return (z * lax.rsqrt(ss * (1.0 / HEAD) + EPS)) * gain

    cos = cos_ref[...]
    sin = sin_ref[...]

    def rope(z):
        return z * cos + pltpu.roll(z, HEAD, 1) * sin

    g_aq = vec_ref[0:1, 0:LANES]
    g_ak = vec_ref[1:2, 0:LANES]
    g_bq = vec_ref[2:3, 0:LANES]
    g_bk = vec_ref[3:4, 0:LANES]
    gate_bias = vec_ref[4:5, 0:LANES]

    lane = lax.broadcasted_iota(jnp.int32, (MXU_DIM, LANES), 1)
    zg = proj(wc_ref, SEC_G, LANES) + gate_bias
    log_sig = _log_sigmoid(zg)
    log_f = jnp.where(lane[0:1] < GATE_CI, log_sig, 0.0)
    f_terms = [_split3_by_lane(log_f[blk * MXU_DIM:(blk + 1) * MXU_DIM], lane)
               for blk in range(tm // MXU_DIM)]
    gz = jnp.where(lane[0:1] < GATE_CI, 0.0,
                   jnp.where(lane[0:1] < GATE_CF, zg,
                             jnp.where(lane[0:1] < GATE_CF + MLSTM_HEADS, log_sig, 0.0)))
    gz_hi = gz.astype(BF16)
    gz_r = gz - gz_hi.astype(F32)
    gz_mid = gz_r.astype(BF16)
    gz_terms = jnp.concatenate([gz_hi, gz_mid, (gz_r - gz_mid.astype(F32)).astype(BF16)], axis=1)

    za = proj(wa_ref, 0, W_A)
    for j in range(SWA_HEADS // 2):
        q = rope(head_norm(za[:, j * LANES:(j + 1) * LANES], first_swa, g_aq)) * (QK_SCALE * LOG2E)
        qa_ref[0, :, j * LANES:(j + 1) * LANES] = q.astype(BF16)
    k = rope(head_norm(za[:, SWA_Q:SWA_Q + LANES], first_swa, g_ak))
    kva_ref[0, :, 0:LANES] = k.astype(BF16)
    kva_ref[0, :, LANES:2 * LANES] = za[:, SWA_Q + SWA_KV:].astype(BF16)

    zb = proj(wb_ref, 0, W_B)
    for p in range(FOX_HEADS // 2):
        q = head_norm(zb[:, p * LANES:(p + 1) * LANES], first_nat, g_bq) * (QK_SCALE * LOG2E)
        qb_ref[0, :, p * LANES:(p + 1) * LANES] = q.astype(BF16)
        k = head_norm(zb[:, FOX_W + p * LANES:FOX_W + (p + 1) * LANES], first_nat, g_bk)
        kb_ref[0, :, 2 * p * LANES:(2 * p + 1) * LANES] = k.astype(BF16)
    vb_ref[0] = zb[:, 2 * FOX_W:].astype(BF16)

    zc = proj(wc_ref, SEC_C, 768)
    qkvc_ref[0, :, 0:MLSTM_W] = zc[:, 0:MLSTM_W].astype(BF16)
    qkvc_ref[0, :, MLSTM_W:2 * MLSTM_W] = (zc[:, MLSTM_W:2 * MLSTM_W] * QK_SCALE).astype(BF16)
    qkvc_ref[0, :, 2 * MLSTM_W:] = zc[:, 2 * MLSTM_W:].astype(BF16)
    gc_ref[0, :, 2 * MLSTM_W:] = proj(wc_ref, SEC_CO, MLSTM_W)

    gc_ref[0, :, 0:2 * MLSTM_W] = _dot(gz_terms, e3_ref[...])

    carry = carry_ref[0:1, :]
    for blk in range(tm // MXU_DIM):
        rows = slice(blk * MXU_DIM, (blk + 1) * MXU_DIM)
        cs = _dot(tri_ref[...], f_terms[blk])
        f_loc = cs + pltpu.roll(cs, LANES - 4, 1) + pltpu.roll(cs, LANES - 8, 1)
        f_cum = jnp.where(lane < 4, f_loc, 0.0) + carry
        carry = f_cum[MXU_DIM - 1:MXU_DIM, :]
        f_rep = (f_cum + pltpu.roll(f_cum, 4, 1) + pltpu.roll(f_cum, 8, 1)) * LOG2E
        aug = _split3_by_lane(f_rep, lane)
        kb_ref[0, rows, LANES:2 * LANES] = aug
        kb_ref[0, rows, 3 * LANES:4 * LANES] = aug
    carry_ref[...] = jnp.broadcast_to(carry, carry_ref.shape)


def _in_proj(layer, x, n1, w_a, w_b, w_c, vecs, cos, sin, tri, e3, w_ff1, w_ff2):
    B, S, _ = x.shape
    tm = min(TM_IN, S)
    n_i = S // tm
    grid = (B, n_i)
    steps = B * n_i
    assert D_MODEL % (16 * steps) == 0, "FFN weight slabs must be whole bf16 sublane tiles"
    r1, r2 = D_MODEL // steps, D_FF // steps
    tok = lambda w: pl.BlockSpec((1, tm, w), lambda b, i: (b, i, 0))
    out_shapes = (
        jax.ShapeDtypeStruct((B, S, SWA_Q), BF16),
        jax.ShapeDtypeStruct((B, S, 2 * SWA_KV), BF16),
        jax.ShapeDtypeStruct((B, S, FOX_W), BF16),
        jax.ShapeDtypeStruct((B, S, 2 * MXU_DIM), BF16),
        jax.ShapeDtypeStruct((B, S, FOX_W), BF16),
        jax.ShapeDtypeStruct((B, S, 3 * MLSTM_W), BF16),
        jax.ShapeDtypeStruct((B, S, 3 * MLSTM_W), F32),
        jax.ShapeDtypeStruct((D_MODEL, D_FF), BF16),
        jax.ShapeDtypeStruct((D_FF, D_MODEL), BF16),
    )
    slab_out = [pl.BlockSpec((r1, D_FF), lambda b, i: (b * n_i + i, 0)),
                pl.BlockSpec((r2, D_MODEL), lambda b, i: (b * n_i + i, 0))]
    return pl.pallas_call(
        _in_proj_body,
        grid=grid,
        in_specs=[
            tok(D_MODEL),
            _layer_spec((1, D_MODEL), layer),
            _layer_spec((D_MODEL, W_A), layer),
            _layer_spec((D_MODEL, W_B), layer),
            _layer_spec((D_MODEL, W_C), layer),
            _layer_spec((8, MLSTM_W), layer),
            pl.BlockSpec((tm, LANES), lambda b, i: (i, 0)),
            pl.BlockSpec((tm, LANES), lambda b, i: (i, 0)),
            _const_spec((MXU_DIM, MXU_DIM)),
            _const_spec((3 * LANES, 2 * MLSTM_W)),
            pl.BlockSpec((None, r1, D_FF), lambda b, i: (layer, b * n_i + i, 0)),
            pl.BlockSpec((None, r2, D_MODEL), lambda b, i: (layer, b * n_i + i, 0)),
        ],
        out_specs=[tok(s.shape[-1]) for s in out_shapes[:-2]] + slab_out,
        out_shape=out_shapes,
        scratch_shapes=[pltpu.VMEM((8, LANES), F32)],
        compiler_params=pltpu.CompilerParams(
            dimension_semantics=("arbitrary", "arbitrary"), vmem_limit_bytes=VMEM_LIMIT),
        name="in_proj",
    )(x, n1, w_a, w_b, w_c, vecs, cos, sin, tri, e3, w_ff1, w_ff2)


TS_SWA = 1024
SWA_DEPTH = 8
VT_ROWS = HEAD + 16


def _swa_body(sink_base, sink_ref, q_ref, kvc_ref, kvp_ref, eye_ref, band_ref, o_ref):
    step = pl.program_id(1)
    nblk = q_ref.shape[1] // QBLK
    per_group = SWA_HEADS // SWA_KV_HEADS
    eye = eye_ref[0:LANES, 0:LANES]

    band_cap = band_ref[...]
    key = lax.broadcasted_iota(jnp.int32, (2 * QBLK, MXU_DIM), 0)
    first_cap = jnp.where((key >= QBLK) | (step > 0), band_cap, -jnp.inf)
    lane = lax.broadcasted_iota(jnp.int32, (QBLK, LANES), 1)
    lane_a = (lane // HALF) % 2 == 0
    first_head = lax.broadcasted_iota(jnp.int32, (1, MXU_DIM), 1) < QBLK
    zero = jnp.zeros((QBLK, LANES), BF16)
    ones = jnp.ones((VT_ROWS - HEAD, 2 * QBLK), BF16)
    kv_all = jnp.concatenate([kvp_ref[0], kvc_ref[0]], axis=0)
    rows = [slice(blk * QBLK, (blk + 1) * QBLK) for blk in range(nblk)]
    bands = [kv_all[blk * QBLK:(blk + 2) * QBLK] for blk in range(nblk)]

    vt_aug = []
    for blk in range(nblk):
        v_t = _dot_nt(eye, bands[blk][:, LANES:]).astype(BF16)
        vt_aug.append([jnp.concatenate([v_t[g * HEAD:(g + 1) * HEAD], ones], axis=0)
                       for g in range(SWA_KV_HEADS)])

    units = [(blk, g, part) for blk in range(nblk) for g in range(SWA_KV_HEADS) for part in range(2)]

    def score(blk, g, part):
        keep = lane_a if g == 0 else jnp.logical_not(lane_a)
        q_cols = [q_ref[0, rows[blk], j * LANES:(j + 1) * LANES] for j in (2 * part, 2 * part + 1)]
        q_g = jnp.concatenate([jnp.where(keep, q2, zero) for q2 in q_cols], axis=0)
        return _dot_nt(bands[blk][:, 0:LANES], q_g)

    def fold(s, blk, g, part):
        s = jnp.minimum(s, band_cap if blk > 0 else first_cap)
        head = sink_base + g * per_group + 2 * part
        sink = jnp.where(first_head, sink_ref[head], sink_ref[head + 1]) * LOG2E
        m = jnp.maximum(jnp.max(s, axis=0, keepdims=True), sink)
        p = jnp.exp2(s - m)
        o_aug = _dot(vt_aug[blk][g], p.astype(BF16))
        den = o_aug[HEAD:HEAD + 1] + jnp.exp2(sink - m)
        return (o_aug[0:HEAD] / den).astype(BF16)

    def write_block(blk, outs):
        for g in range(SWA_KV_HEADS):
            for part in range(2):
                o = outs[(blk, g, part)]
                pair_t = jnp.concatenate([o[:, 0:QBLK], o[:, QBLK:]], axis=0)
                j = 2 * g + part
                o_ref[0, rows[blk], j * LANES:(j + 1) * LANES] = _dot_nt(eye, pair_t).astype(BF16)

    outs = {}
    queue = [score(*unit) for unit in units[:SWA_DEPTH]]
    for n, unit in enumerate(units):
        s_cur = queue.pop(0)
        if n + SWA_DEPTH < len(units):
            queue.append(score(*units[n + SWA_DEPTH]))
        outs[unit] = fold(s_cur, *unit)
        blk = unit[0]
        if unit[1:] == (SWA_KV_HEADS - 1, 1) and blk > 0:
            write_block(blk - 1, outs)
    write_block(nblk - 1, outs)


def _swa(layer, sinks, qa, kva, eye, band):
    B, S, _ = qa.shape
    ts = min(TS_SWA, S)
    per_step = ts // QBLK
    return pl.pallas_call(
        functools.partial(_swa_body, layer * SWA_HEADS),
        grid=(B, S // ts),
        in_specs=[
            pl.BlockSpec(memory_space=pltpu.SMEM),
            pl.BlockSpec((1, ts, SWA_Q), lambda b, i: (b, i, 0)),
            pl.BlockSpec((1, ts, 2 * SWA_KV), lambda b, i: (b, i, 0)),
            pl.BlockSpec((1, QBLK, 2 * SWA_KV), lambda b, i: (b, jnp.maximum(i * per_step - 1, 0), 0)),
            _const_spec((MXU_DIM, MXU_DIM)),
            _const_spec((2 * QBLK, MXU_DIM)),
        ],
        out_specs=pl.BlockSpec((1, ts, SWA_Q), lambda b, i: (b, i, 0)),
        out_shape=jax.ShapeDtypeStruct((B, S, SWA_Q), BF16),
        compiler_params=pltpu.CompilerParams(
            dimension_semantics=("arbitrary", "arbitrary"), vmem_limit_bytes=VMEM_LIMIT),
        name="swa",
    )(sinks, qa, kva, kva, eye, band)


TQ_FOX = 1024
TK_FOX = 256
NEG_BIG = -1e30


FOX_UNROLL = 4
FOX_DEPTH = 8


def _fox_body(q_ref, k_ref, v_ref, eye_ref, o_ref, vt_ref, qt_ref, acc_ref):
    tq = q_ref.shape[1]
    tk = TK_FOX
    S = k_ref.shape[1]
    pair = pl.program_id(1)
    qi = pl.program_id(2)
    eye = eye_ref[...]
    n_tiles = 2 * tq // MXU_DIM
    tiles_per_head = tq // MXU_DIM

    @pl.when(qi == 0)
    def _():
        ones = jnp.ones((VT_ROWS - HEAD, tk), BF16)
        for j in range(S // tk):
            blk = slice(j * tk, (j + 1) * tk)
            v_t = _dot_nt(eye[0:LANES, 0:LANES], v_ref[0, blk, :]).astype(BF16)
            vt_ref[:, blk] = jnp.concatenate([v_t[0:HEAD], ones, v_t[HEAD:], ones], axis=0)

    lane = lax.broadcasted_iota(jnp.int32, (tq, LANES), 1)
    lane_a = lane < HEAD
    q2 = q_ref[0]
    zero = jnp.zeros_like(q2)

    def minus_one_at(head):
        hit = (lane == head) | (lane == head + 4) | (lane == head + 8)
        return jnp.where(hit, -1.0, 0.0).astype(BF16)

    q_aug = [jnp.concatenate([jnp.where(lane_a, q2, zero), minus_one_at(2 * pair)], axis=1),
             jnp.concatenate([jnp.where(lane_a, zero, q2), minus_one_at(2 * pair + 1)], axis=1)]
    for t in range(n_tiles):
        hd, part = divmod(t, tiles_per_head)
        rows = q_aug[hd][part * MXU_DIM:(part + 1) * MXU_DIM]
        qt_ref[:, t * MXU_DIM:(t + 1) * MXU_DIM] = _dot_nt(eye, rows).astype(BF16)
    acc_ref[...] = jnp.zeros_like(acc_ref)

    key = lax.broadcasted_iota(jnp.int32, (tk // 2, MXU_DIM), 0)
    qry = lax.broadcasted_iota(jnp.int32, (tk // 2, MXU_DIM), 1)

    def fold(s, t, vt_blk, m, triangular):
        hd = t // tiles_per_head
        vt_h = vt_blk[hd * VT_ROWS:(hd + 1) * VT_ROWS]
        half = tk // 2
        for h in range(2):
            sh = s[h * half:(h + 1) * half]
            if triangular:
                valid = key + h * half <= qry
                sh = jnp.where(valid, sh, NEG_BIG)
            m_new = jnp.maximum(m, jnp.max(sh, axis=0, keepdims=True))
            alpha = jnp.exp2(m - m_new)
            p = jnp.exp2(sh - m_new)
            if triangular:
                p = jnp.where(valid, p, 0.0)
            acc_ref[t] = alpha * acc_ref[t] + _dot(vt_h[:, h * half:(h + 1) * half], p.astype(BF16))
            m = m_new
        return m

    def region(j0, ms, plans):
        k_blks, vt_blks, todo = [], [], []
        for b, plan in enumerate(plans):
            r0 = pl.multiple_of((j0 + b) * tk, tk)
            k_blks.append(k_ref[0, pl.ds(r0, tk), :])
            vt_blks.append(vt_ref[:, pl.ds(r0, tk)])
            todo += [(b, t) for t in range(n_tiles) if plan[t] is not None]
        score = lambda b, t: _dot(k_blks[b], qt_ref[:, t * MXU_DIM:(t + 1) * MXU_DIM])
        ms = list(ms)
        queue = [score(*unit) for unit in todo[:FOX_DEPTH]]
        for n, (b, t) in enumerate(todo):
            s_cur = queue.pop(0)
            if n + FOX_DEPTH < len(todo):
                queue.append(score(*todo[n + FOX_DEPTH]))
            ms[t] = fold(s_cur, t, vt_blks[b], ms[t], plans[b][t])
        return tuple(ms)

    blocks_per_tile = tq // tk
    full = (False,) * n_tiles
    n_full = blocks_per_tile * qi
    m_init = jnp.full((1, MXU_DIM), NEG_BIG, F32)
    ms = lax.fori_loop(0, n_full // FOX_UNROLL,
                       lambda g, c: region(g * FOX_UNROLL, c, (full,) * FOX_UNROLL), (m_init,) * n_tiles)
    if FOX_UNROLL > blocks_per_tile:
        done = (n_full // FOX_UNROLL) * FOX_UNROLL
        ms = lax.fori_loop(0, (n_full - done) // blocks_per_tile,
                           lambda g, c: region(done + g * blocks_per_tile, c, (full,) * blocks_per_tile), ms)
    diag = [tuple(None if t % tiles_per_head < d else t % tiles_per_head == d for t in range(n_tiles))
            for d in range(blocks_per_tile)]
    ms = region(n_full, ms, diag)

    for hd in range(2):
        cols = []
        for part in range(tiles_per_head):
            acc = acc_ref[hd * tiles_per_head + part]
            cols.append(acc[0:HEAD] / acc[HEAD:HEAD + 1])
        o_h = jnp.concatenate(cols, axis=1)
        if hd == 0:
            o_t = o_h
        else:
            o_t = jnp.concatenate([o_t, o_h], axis=0).astype(BF16)
    for c in range(tq // MXU_DIM):
        rows = slice(c * MXU_DIM, (c + 1) * MXU_DIM)
        o_ref[0, rows, :] = _dot_nt(eye, o_t[:, rows]).astype(BF16)


def _fox(qb, kb, vb, eye):
    B, S, _ = qb.shape
    tq = min(TQ_FOX, S)
    return pl.pallas_call(
        _fox_body,
        grid=(B, FOX_HEADS // 2, S // tq),
        in_specs=[
            pl.BlockSpec((1, tq, LANES), lambda b, p, i: (b, i, p)),
            pl.BlockSpec((1, S, MXU_DIM), lambda b, p, i: (b, 0, p)),
            pl.BlockSpec((1, S, LANES), lambda b, p, i: (b, 0, p)),
            _const_spec((MXU_DIM, MXU_DIM)),
        ],
        out_specs=pl.BlockSpec((1, tq, LANES), lambda b, p, i: (b, i, p)),
        out_shape=jax.ShapeDtypeStruct((B, S, FOX_W), BF16),
        scratch_shapes=[
            pltpu.VMEM((2 * VT_ROWS, S), BF16),
            pltpu.VMEM((MXU_DIM, 2 * tq), BF16),
            pltpu.VMEM((2 * tq // MXU_DIM, VT_ROWS, MXU_DIM), F32),
        ],
        compiler_params=pltpu.CompilerParams(
            dimension_semantics=("arbitrary", "arbitrary", "arbitrary"),
            vmem_limit_bytes=VMEM_LIMIT),
        name="fox",
    )(qb, kb, vb, eye)


TS_MLSTM = 1024


def _mlstm_body(qkv_ref, g_ref, onorm_ref, tri_ref, i4_ref, bd_ref, o_ref, c_ref, n_ref, m_ref):
    ts = qkv_ref.shape[1]
    W = MLSTM_W

    @pl.when(pl.program_id(1) == 0)
    def _():
        c_ref[...] = jnp.zeros_like(c_ref)
        n_ref[...] = jnp.zeros_like(n_ref)
        m_ref[...] = jnp.zeros_like(m_ref)

    row = lax.broadcasted_iota(jnp.int32, (CHUNK, W), 0)
    lane = lax.broadcasted_iota(jnp.int32, (CHUNK, W), 1)
    causal = (lane % HEAD) <= row
    lane_head = lane // HEAD
    bd = bd_ref[...]
    bd2 = jnp.concatenate([bd, bd], axis=1)
    i4_bf = i4_ref[...].astype(BF16)

    def block_diag(x):
        tiled = jnp.concatenate([x] * MLSTM_HEADS, axis=0)
        return tiled * (bd if x.shape[1] == W else bd2)

    def seg_max(d):
        out = jnp.zeros_like(d)
        for hd in range(MLSTM_HEADS):
            sel = lane_head == hd
            mx = jnp.max(jnp.where(sel, d, -jnp.inf), axis=-1, keepdims=True)
            out = jnp.where(sel, mx, out)
        return out

    chunks = range(ts // CHUNK)
    rows = [pl.ds(c * CHUNK, CHUNK) for c in chunks]

    b, g, u, dm, dm_max, a_max, wa = [], [], [], [], [], [], []
    for c in chunks:
        ig = g_ref[0, rows[c], 0:W]
        lf_hi, lf_lo = _split2(g_ref[0, rows[c], W:2 * W])
        b_c = _dot(tri_ref[...], lf_hi) + _dot(tri_ref[...], lf_lo)
        g_c = b_c[CHUNK - 1:CHUNK, :]
        u_c = b_c - ig
        u_row = jnp.sum(u_c * i4_ref[...], axis=0, keepdims=True)
        dm_c = jnp.where(causal, b_c - u_row, -jnp.inf)
        a_c = g_c - u_c
        am_c = jnp.max(a_c, axis=0, keepdims=True)
        b.append(b_c); g.append(g_c); u.append(u_c); dm.append(dm_c)
        dm_max.append(seg_max(dm_c)); a_max.append(am_c); wa.append(jnp.exp(a_c - am_c))

    m_prev, s_old, s_loc = [m_ref[0:1, :]], [], []
    for c in chunks:
        gm = g[c] + m_prev[c]
        m_new = jnp.maximum(gm, a_max[c])
        s_old.append(jnp.exp(gm - m_new))
        s_loc.append(jnp.exp(a_max[c] - m_new))
        m_prev.append(m_new)

    m_t, w_inter, sm_hi, sm_lo, k_t = [], [], [], [], []
    for c in chunks:
        inter = b[c] + m_prev[c]
        mt_c = jnp.maximum(inter, dm_max[c])
        decay = jnp.exp(dm[c] - mt_c)
        q = qkv_ref[0, rows[c], 0:W]
        k = qkv_ref[0, rows[c], W:2 * W]
        qk = _dot_nt(jnp.concatenate([q, i4_bf], axis=0), block_diag(k))
        hi, lo = _split2(qk[0:CHUNK] * decay)
        m_t.append(mt_c); w_inter.append(jnp.exp(inter - mt_c)); sm_hi.append(hi); sm_lo.append(lo)
        k_t.append(qk[CHUNK:].astype(BF16))

    c_cur, n_cur = c_ref[...], n_ref[...]
    state_bf = []
    for c in chunks:
        state_bf.append(jnp.concatenate([c_cur.astype(BF16), n_cur.astype(BF16)], axis=1))
        v = qkv_ref[0, rows[c], 2 * W:]
        weighted = jnp.concatenate([(wa[c] * v.astype(F32)).astype(BF16), wa[c].astype(BF16)], axis=1)
        loc = _dot(k_t[c], block_diag(weighted))
        c_cur = s_old[c] * c_cur + s_loc[c] * loc[:, 0:W]
        n_cur = s_old[c] * n_cur + s_loc[c] * loc[:, W:]
    c_ref[...] = c_cur
    n_ref[...] = n_cur
    m_ref[...] = jnp.broadcast_to(m_prev[-1], m_ref.shape)

    hval = []
    for c in chunks:
        q = qkv_ref[0, rows[c], 0:W]
        v = qkv_ref[0, rows[c], 2 * W:]
        inter_cn = _dot(q, block_diag(state_bf[c]))
        intra = _dot(sm_hi[c], jnp.concatenate([block_diag(v), bd], axis=1))
        num = w_inter[c] * inter_cn[:, 0:W] + intra[:, 0:W]
        den = w_inter[c] * inter_cn[:, W:] + intra[:, W:] + _dot(sm_lo[c], bd)
        hval.append(num / jnp.maximum(jnp.abs(den), jnp.exp(-m_t[c])))

    for c in chunks:
        hs_hi, hs_lo = _split2(hval[c] * hval[c])
        ss = _dot(hs_hi, bd) + _dot(hs_lo, bd)
        hc = (hval[c] * lax.rsqrt(ss * (1.0 / HEAD) + EPS)) * onorm_ref[...]
        co = g_ref[0, rows[c], 2 * W:]
        o_ref[0, rows[c], :] = (jax.nn.sigmoid(co) * hc).astype(BF16)


def _mlstm(layer, qkvc, gc, onorm, tri, i4, bd):
    B, S, _ = qkvc.shape
    ts = min(TS_MLSTM, S)
    return pl.pallas_call(
        _mlstm_body,
        grid=(B, S // ts),
        in_specs=[
            pl.BlockSpec((1, ts, 3 * MLSTM_W), lambda b, i: (b, i, 0)),
            pl.BlockSpec((1, ts, 3 * MLSTM_W), lambda b, i: (b, i, 0)),
            _layer_spec((1, MLSTM_W), layer),
            _const_spec((CHUNK, CHUNK)),
            _const_spec((CHUNK, MLSTM_W)),
            _const_spec((MLSTM_W, MLSTM_W)),
        ],
        out_specs=pl.BlockSpec((1, ts, MLSTM_W), lambda b, i: (b, i, 0)),
        out_shape=jax.ShapeDtypeStruct((B, S, MLSTM_W), BF16),
        scratch_shapes=[pltpu.VMEM((HEAD, MLSTM_W), F32), pltpu.VMEM((HEAD, MLSTM_W), F32),
                        pltpu.VMEM((8, MLSTM_W), F32)],
        compiler_params=pltpu.CompilerParams(
            dimension_semantics=("arbitrary", "arbitrary"), vmem_limit_bytes=VMEM_LIMIT),
        name="mlstm",
    )(qkvc, gc, onorm, tri, i4, bd)


TM_FFN = 512
FF_CHUNK = 1024


def _out_ffn_body(x_ref, ya_ref, yb_ref, yc_ref, wo_ref, n2_ref, w1_ref, w2_ref, o_ref, a_ref):
    x1 = (x_ref[...] + _dot(ya_ref[...], wo_ref[0:SWA_Q, :])
          + _dot(yb_ref[...], wo_ref[SWA_Q:SWA_Q + FOX_W, :])
          + _dot(yc_ref[...], wo_ref[SWA_Q + FOX_W:, :]))
    ms = jnp.mean(x1 * x1, axis=-1, keepdims=True)
    h2 = ((x1 * lax.rsqrt(ms + EPS)) * n2_ref[...]).astype(BF16)
    for j in range(D_FF // FF_CHUNK):
        cols = slice(j * FF_CHUNK, (j + 1) * FF_CHUNK)
        r = jnp.maximum(_dot(h2, w1_ref[:, cols]), 0.0)
        a_ref[:, cols] = (r * r).astype(BF16)
    o_ref[...] = x1 + _dot(a_ref[...], w2_ref[...])


def _out_ffn(layer, x2d, ya, yb, yc, wo, n2, w1, w2):
    T = x2d.shape[0]
    tm = min(TM_FFN, T)
    tok = lambda w: pl.BlockSpec((tm, w), lambda i: (i, 0))
    return pl.pallas_call(
        _out_ffn_body,
        grid=(T // tm,),
        in_specs=[
            tok(D_MODEL), tok(SWA_Q), tok(FOX_W), tok(MLSTM_W),
            _layer_spec((D_MODEL, D_MODEL), layer),
            _layer_spec((1, D_MODEL), layer),
            _const_spec((D_MODEL, D_FF)),
            _const_spec((D_FF, D_MODEL)),
        ],
        out_specs=tok(D_MODEL),
        out_shape=jax.ShapeDtypeStruct((T, D_MODEL), F32),
        scratch_shapes=[pltpu.VMEM((tm, D_FF), BF16)],
        compiler_params=pltpu.CompilerParams(
            dimension_semantics=("arbitrary",), vmem_limit_bytes=VMEM_LIMIT),
        name="out_ffn",
    )(x2d, ya, yb, yc, wo, n2, w1, w2)


def _rope_tables(S):
    inv = ROPE_THETA ** (-np.arange(HALF, dtype=np.float64) / HALF)
    ang = np.arange(S, dtype=np.float64)[:, None] * inv[None, :]
    cos = np.tile(np.cos(ang), (1, LANES // HALF))
    sin = np.tile(np.sin(ang), (1, LANES // HALF))
    sign = np.where(np.arange(LANES) < HEAD, -1.0, 1.0)
    return jnp.asarray(cos, F32), jnp.asarray(sin * sign[None, :], F32)


PREP_ROWS = 256
C_BULK = (IN_WIDTH // LANES) * LANES - (W_A + W_B)


def _prep_w_in_body(w_ref, pa_ref, pc_ref, pt_ref, wa_ref, wb_ref, wc_ref):
    c0 = W_A + W_B
    wa_ref[...] = _dot(w_ref[:, 0:W_A].astype(BF16), pa_ref[...]).astype(BF16)
    wb_ref[...] = w_ref[:, W_A:c0].astype(BF16)
    tail = w_ref[:, c0 + C_BULK:].astype(BF16)
    tail = jnp.concatenate([tail, jnp.zeros((tail.shape[0], LANES - tail.shape[1]), BF16)], axis=1)
    wc_ref[...] = (_dot(w_ref[:, c0:c0 + C_BULK].astype(BF16), pc_ref[...])
                   + _dot(tail, pt_ref[...])).astype(BF16)


def _prep_w_in(w_in):
    L = w_in.shape[0]
    perm_tail = np.zeros((LANES, W_C), np.float32)
    perm_tail[:IN_WIDTH - (W_A + W_B) - C_BULK] = _PERM_C[C_BULK:]
    rows = lambda w: pl.BlockSpec((None, PREP_ROWS, w), lambda l, r: (l, r, 0))
    return pl.pallas_call(
        _prep_w_in_body,
        grid=(L, D_MODEL // PREP_ROWS),
        in_specs=[rows(IN_WIDTH), _const_spec((W_A, W_A)), _const_spec((C_BULK, W_C)),
                  _const_spec((LANES, W_C))],
        out_specs=[rows(W_A), rows(W_B), rows(W_C)],
        out_shape=[jax.ShapeDtypeStruct((L, D_MODEL, w), BF16) for w in (W_A, W_B, W_C)],
        compiler_params=pltpu.CompilerParams(
            dimension_semantics=("arbitrary", "arbitrary"), vmem_limit_bytes=VMEM_LIMIT),
        name="prep_w_in",
    )(w_in, jnp.asarray(_PERM_A, BF16), jnp.asarray(_PERM_C[:C_BULK], BF16),
      jnp.asarray(perm_tail, BF16))


def _prep_vecs(swa_q_norm, swa_k_norm, fox_q_norm, fox_k_norm, fox_f_bias, mlstm_i_bias, mlstm_f_bias):
    L = swa_q_norm.shape[0]

    def swa_lanes(g):
        return jnp.concatenate([g[:, :HALF], g[:, :HALF], g[:, HALF:], g[:, HALF:]], axis=-1)

    pad = lambda v: jnp.pad(v, ((0, 0), (0, MLSTM_W - v.shape[1])))
    return jnp.stack([
        pad(swa_lanes(swa_q_norm)), pad(swa_lanes(swa_k_norm)),
        pad(jnp.tile(fox_q_norm, (1, 2))), pad(jnp.tile(fox_k_norm, (1, 2))),
        pad(jnp.concatenate([jnp.tile(fox_f_bias, (1, 3)), mlstm_i_bias, mlstm_f_bias], axis=-1)),
        jnp.zeros((L, MLSTM_W), F32), jnp.zeros((L, MLSTM_W), F32), jnp.zeros((L, MLSTM_W), F32),
    ], axis=1).astype(F32)


def kernel(x, norm1, w_in, swa_q_norm, swa_k_norm, swa_sinks, fox_q_norm, fox_k_norm, fox_f_bias,
           mlstm_i_bias, mlstm_f_bias, mlstm_out_norm, w_out, norm2, w_ff1, w_ff2):
    B, S, D = x.shape
    L = norm1.shape[0]
    T = B * S
    cos, sin = _rope_tables(S)
    tri256, tri64, bd256 = (jnp.asarray(c, BF16) for c in (_TRI256, _TRI64, _BD256))
    i4 = jnp.asarray(_I4, F32)
    e3 = jnp.asarray(_gate_expand(), BF16)
    eye256 = jnp.eye(MXU_DIM, dtype=BF16)
    band = jnp.asarray(_swa_band_cap(), F32)

    w_a, w_b, w_c = _prep_w_in(w_in)
    wo = w_out.astype(BF16)
    vecs = _prep_vecs(swa_q_norm, swa_k_norm, fox_q_norm, fox_k_norm, fox_f_bias, mlstm_i_bias,
                      mlstm_f_bias)
    n1 = norm1.reshape(L, 1, D).astype(F32)
    n2 = norm2.reshape(L, 1, D).astype(F32)
    onorm = mlstm_out_norm.reshape(L, 1, MLSTM_W).astype(F32)
    sinks = swa_sinks.reshape(L * SWA_HEADS).astype(F32)

    for l in range(L):
        qa, kva, qb, kb, vb, qkvc, gc, w1, w2 = _in_proj(
            l, x, n1, w_a, w_b, w_c, vecs, cos, sin, tri256, e3, w_ff1, w_ff2)
        ya = _swa(l, sinks, qa, kva, eye256, band)
        yb = _fox(qb, kb, vb, eye256)
        yc = _mlstm(l, qkvc, gc, onorm, tri64, i4, bd256)
        x = _out_ffn(l, x.reshape(T, D), ya.reshape(T, SWA_Q), yb.reshape(T, FOX_W),
                     yc.reshape(T, MLSTM_W), wo, n2, w1, w2).reshape(B, S, D)
    return x
```

```python
import functools
import math

import numpy as np
import jax
import jax.numpy as jnp
from jax import lax
from jax.experimental import pallas as pl
from jax.experimental.pallas import tpu as pltpu

F32 = jnp.float32
BF16 = jnp.bfloat16

D_MODEL = 1024
HEAD = 64
HALF = HEAD // 2
CHUNK = 64
SWA_HEADS = 8
SWA_KV_HEADS = 2
FOX_HEADS = 4
MLSTM_HEADS = 4
SWA_Q = SWA_HEADS * HEAD
SWA_KV = SWA_KV_HEADS * HEAD
FOX_W = FOX_HEADS * HEAD
MLSTM_W = MLSTM_HEADS * HEAD
D_FF = 4 * D_MODEL
QBLK = 128
ROPE_THETA = 10000.0
EPS = 1e-6
QK_SCALE = 1.0 / math.sqrt(HEAD)
LOG2E = math.log2(math.e)

LANES = 128
MXU_DIM = 256
VMEM_LIMIT = 56 * 1024 * 1024

W_A = SWA_Q + 2 * SWA_KV
W_B = 3 * FOX_W
SEC_C = 0
SEC_CO = SEC_C + 768
SEC_G = SEC_CO + 256
W_C = SEC_G + LANES
GATE_BF = 0
GATE_CI = 12
GATE_CF = 16
IN_WIDTH = W_A + W_B + FOX_HEADS + 4 * MLSTM_W + 2 * MLSTM_HEADS


def _perm_a():
    p = np.zeros((W_A, W_A), np.float32)
    lane = np.arange(LANES)
    quarter, i = lane // HALF, lane % HALF
    dim, is_b = i + HALF * (quarter // 2), quarter % 2
    for j in range(SWA_HEADS // 2):
        p[(j + 4 * is_b) * HEAD + dim, j * LANES + lane] = 1.0
    p[SWA_Q + is_b * HEAD + dim, SWA_Q + lane] = 1.0
    p[SWA_Q + SWA_KV + lane, SWA_Q + SWA_KV + lane] = 1.0
    return p


def _perm_c():
    src = IN_WIDTH - W_A - W_B
    o_bf, o_cqkv = 0, FOX_HEADS
    o_ci = o_cqkv + 3 * MLSTM_W
    o_cf, o_co = o_ci + MLSTM_HEADS, o_ci + 2 * MLSTM_HEADS
    p = np.zeros((src, W_C), np.float32)
    w = np.arange(3 * MLSTM_W)
    p[o_cqkv + w, SEC_C + w] = 1.0
    w = np.arange(MLSTM_W)
    p[o_co + w, SEC_CO + w] = 1.0
    h = np.arange(FOX_HEADS)
    for rep in range(3):
        p[o_bf + h, SEC_G + GATE_BF + rep * FOX_HEADS + h] = 1.0
    h = np.arange(MLSTM_HEADS)
    p[o_ci + h, SEC_G + GATE_CI + h] = 1.0
    p[o_cf + h, SEC_G + GATE_CF + h] = 1.0
    return p


def _gate_expand():
    e = np.zeros((LANES, 2 * MLSTM_W), np.float32)
    for h in range(MLSTM_HEADS):
        e[GATE_CI + h, h * HEAD:(h + 1) * HEAD] = 1.0
        e[GATE_CF + h, MLSTM_W + h * HEAD:MLSTM_W + (h + 1) * HEAD] = 1.0
    return np.concatenate([e, e, e], axis=0)


_PERM_A = _perm_a()
_PERM_C = _perm_c()


def _block_diag_ones(head_of_lane):
    return (head_of_lane[:, None] == head_of_lane[None, :]).astype(np.float32)


_TRI256 = np.tril(np.ones((MXU_DIM, MXU_DIM), np.float32))
_TRI64 = np.tril(np.ones((CHUNK, CHUNK), np.float32))
_w256 = np.arange(MLSTM_W)
_BD256 = _block_diag_ones(_w256 // HEAD)
_I4 = (np.arange(CHUNK)[:, None] == (_w256 % HEAD)[None, :]).astype(np.float32)


def _swa_band_cap():
    key_chunk = np.arange(2 * QBLK)[:, None] // CHUNK
    qry_chunk = (np.arange(MXU_DIM)[None, :] % QBLK) // CHUNK
    ahead = key_chunk - qry_chunk
    return np.where((ahead >= 0) & (ahead <= 2), np.inf, -np.inf).astype(np.float32)


def _const_spec(shape):
    nd = len(shape)
    return pl.BlockSpec(shape, lambda *_: (0,) * nd, pipeline_mode=pl.Buffered(1))


def _layer_spec(shape, layer):
    nd = len(shape)
    return pl.BlockSpec((None,) + tuple(shape), lambda *_: (layer,) + (0,) * nd,
                        pipeline_mode=pl.Buffered(1))


def _log_sigmoid(x):
    return jnp.minimum(x, 0.0) - jnp.log(1.0 + jnp.exp(-jnp.abs(x)))


def _split2(x):
    hi = x.astype(BF16)
    lo = (x - hi.astype(F32)).astype(BF16)
    return hi, lo


def _split3_by_lane(x, lane):
    hi = x.astype(BF16).astype(F32)
    r1 = x - hi
    mid = r1.astype(BF16).astype(F32)
    low = r1 - mid
    return jnp.where(lane < 4, hi, jnp.where(lane < 8, mid, low)).astype(BF16)


def _dot(a, b):
    return jnp.dot(a, b, preferred_element_type=F32)


def _dot_nt(a, b):
    return lax.dot_general(a, b, (((1,), (1,)), ((), ())), preferred_element_type=F32)


def _dot_tn(a, b):
    return lax.dot_general(a, b, (((0,), (0,)), ((), ())), preferred_element_type=F32)


TM_IN = 1024


def _in_proj_body(x_ref, n1_ref, wa_ref, wb_ref, wc_ref, vec_ref, cos_ref, sin_ref, tri_ref, e3_ref,
                  wff1_ref, wff2_ref,
                  qa_ref, kva_ref, qb_ref, kb_ref, vb_ref, qkvc_ref, gc_ref, wff1_bf_ref, wff2_bf_ref,
                  carry_ref):
    tm = x_ref.shape[1]
    wff1_bf_ref[...] = wff1_ref[...].astype(BF16)
    wff2_bf_ref[...] = wff2_ref[...].astype(BF16)

    @pl.when(pl.program_id(1) == 0)
    def _():
        carry_ref[...] = jnp.zeros_like(carry_ref)

    x = x_ref[0]
    ms = jnp.mean(x * x, axis=-1, keepdims=True)
    h = ((x * lax.rsqrt(ms + EPS)) * n1_ref[...]).astype(BF16)

    def proj(w_ref, c0, width):
        return _dot(h, w_ref[:, c0:c0 + width])

    lane_row = lax.broadcasted_iota(jnp.int32, (1, LANES), 1)
    first_swa = (lane_row // HALF) % 2 == 0
    first_nat = lane_row < HEAD

    def head_norm(z, first, gain):
        sq = z * z
        s_first = jnp.sum(jnp.where(first, sq, 0.0), axis=-1, keepdims=True)
        s_second = jnp.sum(jnp.where(first, 0.0, sq), axis=-1, keepdims=True)
        ss = jnp.where(first, s_first, s_second)
        return (z * lax.rsqrt(ss * (1.0 / HEAD) + EPS)) * gain

    cos = cos_ref[...]
    sin = sin_ref[...]

    def rope(z):
        return z * cos + pltpu.roll(z, HEAD, 1) * sin

    g_aq = vec_ref[0:1, 0:LANES]
    g_ak = vec_ref[1:2, 0:LANES]
    g_bq = vec_ref[2:3, 0:LANES]
    g_bk = vec_ref[3:4, 0:LANES]
    gate_bias = vec_ref[4:5, 0:LANES]

    lane = lax.broadcasted_iota(jnp.int32, (MXU_DIM, LANES), 1)
    zg = proj(wc_ref, SEC_G, LANES) + gate_bias
    log_sig = _log_sigmoid(zg)
    log_f = jnp.where(lane[0:1] < GATE_CI, log_sig, 0.0)
    f_terms = [_split3_by_lane(log_f[blk * MXU_DIM:(blk + 1) * MXU_DIM], lane)
               for blk in range(tm // MXU_DIM)]
    gz = jnp.where(lane[0:1] < GATE_CI, 0.0,
                   jnp.where(lane[0:1] < GATE_CF, zg,
                             jnp.where(lane[0:1] < GATE_CF + MLSTM_HEADS, log_sig, 0.0)))
    gz_hi = gz.astype(BF16)
    gz_r = gz - gz_hi.astype(F32)
    gz_mid = gz_r.astype(BF16)
    gz_terms = jnp.concatenate([gz_hi, gz_mid, (gz_r - gz_mid.astype(F32)).astype(BF16)], axis=1)

    za = proj(wa_ref, 0, W_A)
    for j in range(SWA_HEADS // 2):
        q = rope(head_norm(za[:, j * LANES:(j + 1) * LANES], first_swa, g_aq)) * (QK_SCALE * LOG2E)
        qa_ref[0, :, j * LANES:(j + 1) * LANES] = q.astype(BF16)
    k = rope(head_norm(za[:, SWA_Q:SWA_Q + LANES], first_swa, g_ak))
    kva_ref[0, :, 0:LANES] = k.astype(BF16)
    kva_ref[0, :, LANES:2 * LANES] = za[:, SWA_Q + SWA_KV:].astype(BF16)

    zb = proj(wb_ref, 0, W_B)
    for p in range(FOX_HEADS // 2):
        q = head_norm(zb[:, p * LANES:(p + 1) * LANES], first_nat, g_bq) * (QK_SCALE * LOG2E)
        qb_ref[0, :, p * LANES:(p + 1) * LANES] = q.astype(BF16)
        k = head_norm(zb[:, FOX_W + p * LANES:FOX_W + (p + 1) * LANES], first_nat, g_bk)
        kb_ref[0, :, 2 * p * LANES:(2 * p + 1) * LANES] = k.astype(BF16)
    vb_ref[0] = zb[:, 2 * FOX_W:].astype(BF16)

    zc = proj(wc_ref, SEC_C, 768)
    qkvc_ref[0, :, 0:MLSTM_W] = zc[:, 0:MLSTM_W].astype(BF16)
    qkvc_ref[0, :, MLSTM_W:2 * MLSTM_W] = (zc[:, MLSTM_W:2 * MLSTM_W] * QK_SCALE).astype(BF16)
    qkvc_ref[0, :, 2 * MLSTM_W:] = zc[:, 2 * MLSTM_W:].astype(BF16)
    gc_ref[0, :, 2 * MLSTM_W:] = proj(wc_ref, SEC_CO, MLSTM_W)

    gc_ref[0, :, 0:2 * MLSTM_W] = _dot(gz_terms, e3_ref[...])

    carry = carry_ref[0:1, :]
    for blk in range(tm // MXU_DIM):
        rows = slice(blk * MXU_DIM, (blk + 1) * MXU_DIM)
        cs = _dot(tri_ref[...], f_terms[blk])
        f_loc = cs + pltpu.roll(cs, LANES - 4, 1) + pltpu.roll(cs, LANES - 8, 1)
        f_cum = jnp.where(lane < 4, f_loc, 0.0) + carry
        carry = f_cum[MXU_DIM - 1:MXU_DIM, :]
        f_rep = (f_cum + pltpu.roll(f_cum, 4, 1) + pltpu.roll(f_cum, 8, 1)) * LOG2E
        aug = _split3_by_lane(f_rep, lane)
        kb_ref[0, rows, LANES:2 * LANES] = aug
        kb_ref[0, rows, 3 * LANES:4 * LANES] = aug
    carry_ref[...] = jnp.broadcast_to(carry, carry_ref.shape)


def _in_proj(layer, x, n1, w_a, w_b, w_c, vecs, cos, sin, tri, e3, w_ff1, w_ff2):
    B, S, _ = x.shape
    tm = min(TM_IN, S)
    n_i = S // tm
    grid = (B, n_i)
    steps = B * n_i
    assert D_MODEL % (16 * steps) == 0, "FFN weight slabs must be whole bf16 sublane tiles"
    r1, r2 = D_MODEL // steps, D_FF // steps
    tok = lambda w: pl.BlockSpec((1, tm, w), lambda b, i: (b, i, 0))
    out_shapes = (
        jax.ShapeDtypeStruct((B, S, SWA_Q), BF16),
        jax.ShapeDtypeStruct((B, S, 2 * SWA_KV), BF16),
        jax.ShapeDtypeStruct((B, S, FOX_W), BF16),
        jax.ShapeDtypeStruct((B, S, 2 * MXU_DIM), BF16),
        jax.ShapeDtypeStruct((B, S, FOX_W), BF16),
        jax.ShapeDtypeStruct((B, S, 3 * MLSTM_W), BF16),
        jax.ShapeDtypeStruct((B, S, 3 * MLSTM_W), F32),
        jax.ShapeDtypeStruct((D_MODEL, D_FF), BF16),
        jax.ShapeDtypeStruct((D_FF, D_MODEL), BF16),
    )
    slab_out = [pl.BlockSpec((r1, D_FF), lambda b, i: (b * n_i + i, 0)),
                pl.BlockSpec((r2, D_MODEL), lambda b, i: (b * n_i + i, 0))]
    return pl.pallas_call(
        _in_proj_body,
        grid=grid,
        in_specs=[
            tok(D_MODEL),
            _layer_spec((1, D_MODEL), layer),
            _layer_spec((D_MODEL, W_A), layer),
            _layer_spec((D_MODEL, W_B), layer),
            _layer_spec((D_MODEL, W_C), layer),
            _layer_spec((8, MLSTM_W), layer),
            pl.BlockSpec((tm, LANES), lambda b, i: (i, 0)),
            pl.BlockSpec((tm, LANES), lambda b, i: (i, 0)),
            _const_spec((MXU_DIM, MXU_DIM)),
            _const_spec((3 * LANES, 2 * MLSTM_W)),
            pl.BlockSpec((None, r1, D_FF), lambda b, i: (layer, b * n_i + i, 0)),
            pl.BlockSpec((None, r2, D_MODEL), lambda b, i: (layer, b * n_i + i, 0)),
        ],
        out_specs=[tok(s.shape[-1]) for s in out_shapes[:-2]] + slab_out,
        out_shape=out_shapes,
        scratch_shapes=[pltpu.VMEM((8, LANES), F32)],
        compiler_params=pltpu.CompilerParams(
            dimension_semantics=("arbitrary", "arbitrary"), vmem_limit_bytes=VMEM_LIMIT),
        name="in_proj",
    )(x, n1, w_a, w_b, w_c, vecs, cos, sin, tri, e3, w_ff1, w_ff2)


TS_SWA = 1024
SWA_DEPTH = 8
VT_ROWS = HEAD + 16


def _swa_body(sink_base, sink_ref, q_ref, kvc_ref, kvp_ref, eye_ref, band_ref, o_ref):
    step = pl.program_id(1)
    nblk = q_ref.shape[1] // QBLK
    per_group = SWA_HEADS // SWA_KV_HEADS
    eye = eye_ref[0:LANES, 0:LANES]

    band_cap = band_ref[...]
    key = lax.broadcasted_iota(jnp.int32, (2 * QBLK, MXU_DIM), 0)
    first_cap = jnp.where((key >= QBLK) | (step > 0), band_cap, -jnp.inf)
    lane = lax.broadcasted_iota(jnp.int32, (QBLK, LANES), 1)
    lane_a = (lane // HALF) % 2 == 0
    first_head = lax.broadcasted_iota(jnp.int32, (1, MXU_DIM), 1) < QBLK
    zero = jnp.zeros((QBLK, LANES), BF16)
    ones = jnp.ones((VT_ROWS - HEAD, 2 * QBLK), BF16)
    kv_all = jnp.concatenate([kvp_ref[0], kvc_ref[0]], axis=0)
    rows = [slice(blk * QBLK, (blk + 1) * QBLK) for blk in range(nblk)]
    bands = [kv_all[blk * QBLK:(blk + 2) * QBLK] for blk in range(nblk)]

    vt_aug = []
    for blk in range(nblk):
        v_t = _dot_nt(eye, bands[blk][:, LANES:]).astype(BF16)
        vt_aug.append([jnp.concatenate([v_t[g * HEAD:(g + 1) * HEAD], ones], axis=0)
                       for g in range(SWA_KV_HEADS)])

    units = [(blk, g, part) for blk in range(nblk) for g in range(SWA_KV_HEADS) for part in range(2)]

    def score(blk, g, part):
        keep = lane_a if g == 0 else jnp.logical_not(lane_a)
        q_cols = [q_ref[0, rows[blk], j * LANES:(j + 1) * LANES] for j in (2 * part, 2 * part + 1)]
        q_g = jnp.concatenate([jnp.where(keep, q2, zero) for q2 in q_cols], axis=0)
        return _dot_nt(bands[blk][:, 0:LANES], q_g)

    def fold(s, blk, g, part):
        s = jnp.minimum(s, band_cap if blk > 0 else first_cap)
        head = sink_base + g * per_group + 2 * part
        sink = jnp.where(first_head, sink_ref[head], sink_ref[head + 1]) * LOG2E
        m = jnp.maximum(jnp.max(s, axis=0, keepdims=True), sink)
        p = jnp.exp2(s - m)
        o_aug = _dot(vt_aug[blk][g], p.astype(BF16))
        den = o_aug[HEAD:HEAD + 1] + jnp.exp2(sink - m)
        return (o_aug[0:HEAD] / den).astype(BF16)

    def write_block(blk, outs):
        for g in range(SWA_KV_HEADS):
            for part in range(2):
                o = outs[(blk, g, part)]
                pair_t = jnp.concatenate([o[:, 0:QBLK], o[:, QBLK:]], axis=0)
                j = 2 * g + part
                o_ref[0, rows[blk], j * LANES:(j + 1) * LANES] = _dot_nt(eye, pair_t).astype(BF16)

    outs = {}
    queue = [score(*unit) for unit in units[:SWA_DEPTH]]
    for n, unit in enumerate(units):
        s_cur = queue.pop(0)
        if n + SWA_DEPTH < len(units):
            queue.append(score(*units[n + SWA_DEPTH]))
        outs[unit] = fold(s_cur, *unit)
        blk = unit[0]
        if unit[1:] == (SWA_KV_HEADS - 1, 1) and blk > 0:
            write_block(blk - 1, outs)
    write_block(nblk - 1, outs)


def _swa(layer, sinks, qa, kva, eye, band):
    B, S, _ = qa.shape
    ts = min(TS_SWA, S)
    per_step = ts // QBLK
    return pl.pallas_call(
        functools.partial(_swa_body, layer * SWA_HEADS),
        grid=(B, S // ts),
        in_specs=[
            pl.BlockSpec(memory_space=pltpu.SMEM),
            pl.BlockSpec((1, ts, SWA_Q), lambda b, i: (b, i, 0)),
            pl.BlockSpec((1, ts, 2 * SWA_KV), lambda b, i: (b, i, 0)),
            pl.BlockSpec((1, QBLK, 2 * SWA_KV), lambda b, i: (b, jnp.maximum(i * per_step - 1, 0), 0)),
            _const_spec((MXU_DIM, MXU_DIM)),
            _const_spec((2 * QBLK, MXU_DIM)),
        ],
        out_specs=pl.BlockSpec((1, ts, SWA_Q), lambda b, i: (b, i, 0)),
        out_shape=jax.ShapeDtypeStruct((B, S, SWA_Q), BF16),
        compiler_params=pltpu.CompilerParams(
            dimension_semantics=("arbitrary", "arbitrary"), vmem_limit_bytes=VMEM_LIMIT),
        name="swa",
    )(sinks, qa, kva, kva, eye, band)


TQ_FOX = 1024
TK_FOX = 256
NEG_BIG = -1e30


FOX_UNROLL = 4
FOX_DEPTH = 8


def _fox_body(q_ref, k_ref, v_ref, eye_ref, o_ref, vt_ref, qt_ref, acc_ref):
    tq = q_ref.shape[1]
    tk = TK_FOX
    S = k_ref.shape[1]
    pair = pl.program_id(1)
    qi = pl.program_id(2)
    eye = eye_ref[...]
    n_tiles = 2 * tq // MXU_DIM
    tiles_per_head = tq // MXU_DIM

    @pl.when(qi == 0)
    def _():
        ones = jnp.ones((VT_ROWS - HEAD, tk), BF16)
        for j in range(S // tk):
            blk = slice(j * tk, (j + 1) * tk)
            v_t = _dot_nt(eye[0:LANES, 0:LANES], v_ref[0, blk, :]).astype(BF16)
            vt_ref[:, blk] = jnp.concatenate([v_t[0:HEAD], ones, v_t[HEAD:], ones], axis=0)

    lane = lax.broadcasted_iota(jnp.int32, (tq, LANES), 1)
    lane_a = lane < HEAD
    q2 = q_ref[0]
    zero = jnp.zeros_like(q2)

    def minus_one_at(head):
        hit = (lane == head) | (lane == head + 4) | (lane == head + 8)
        return jnp.where(hit, -1.0, 0.0).astype(BF16)

    q_aug = [jnp.concatenate([jnp.where(lane_a, q2, zero), minus_one_at(2 * pair)], axis=1),
             jnp.concatenate([jnp.where(lane_a, zero, q2), minus_one_at(2 * pair + 1)], axis=1)]
    for t in range(n_tiles):
        hd, part = divmod(t, tiles_per_head)
        rows = q_aug[hd][part * MXU_DIM:(part + 1) * MXU_DIM]
        qt_ref[:, t * MXU_DIM:(t + 1) * MXU_DIM] = _dot_nt(eye, rows).astype(BF16)
    acc_ref[...] = jnp.zeros_like(acc_ref)

    key = lax.broadcasted_iota(jnp.int32, (tk // 2, MXU_DIM), 0)
    qry = lax.broadcasted_iota(jnp.int32, (tk // 2, MXU_DIM), 1)

    def fold(s, t, vt_blk, m, triangular):
        hd = t // tiles_per_head
        vt_h = vt_blk[hd * VT_ROWS:(hd + 1) * VT_ROWS]
        half = tk // 2
        for h in range(2):
            sh = s[h * half:(h + 1) * half]
            if triangular:
                valid = key + h * half <= qry
                sh = jnp.where(valid, sh, NEG_BIG)
            m_new = jnp.maximum(m, jnp.max(sh, axis=0, keepdims=True))
            alpha = jnp.exp2(m - m_new)
            p = jnp.exp2(sh - m_new)
            if triangular:
                p = jnp.where(valid, p, 0.0)
            acc_ref[t] = alpha * acc_ref[t] + _dot(vt_h[:, h * half:(h + 1) * half], p.astype(BF16))
            m = m_new
        return m

    def region(j0, ms, plans):
        k_blks, vt_blks, todo = [], [], []
        for b, plan in enumerate(plans):
            r0 = pl.multiple_of((j0 + b) * tk, tk)
            k_blks.append(k_ref[0, pl.ds(r0, tk), :])
            vt_blks.append(vt_ref[:, pl.ds(r0, tk)])
            todo += [(b, t) for t in range(n_tiles) if plan[t] is not None]
        score = lambda b, t: _dot(k_blks[b], qt_ref[:, t * MXU_DIM:(t + 1) * MXU_DIM])
        ms = list(ms)
        queue = [score(*unit) for unit in todo[:FOX_DEPTH]]
        for n, (b, t) in enumerate(todo):
            s_cur = queue.pop(0)
            if n + FOX_DEPTH < len(todo):
                queue.append(score(*todo[n + FOX_DEPTH]))
            ms[t] = fold(s_cur, t, vt_blks[b], ms[t], plans[b][t])
        return tuple(ms)

    blocks_per_tile = tq // tk
    full = (False,) * n_tiles
    n_full = blocks_per_tile * qi
    m_init = jnp.full((1, MXU_DIM), NEG_BIG, F32)
    ms = lax.fori_loop(0, n_full // FOX_UNROLL,
                       lambda g, c: region(g * FOX_UNROLL, c, (full,) * FOX_UNROLL), (m_init,) * n_tiles)
    if FOX_UNROLL > blocks_per_tile:
        done = (n_full // FOX_UNROLL) * FOX_UNROLL
        ms = lax.fori_loop(0, (n_full - done) // blocks_per_tile,
                           lambda g, c: region(done + g * blocks_per_tile, c, (full,) * blocks_per_tile), ms)
    diag = [tuple(None if t % tiles_per_head < d else t % tiles_per_head == d for t in range(n_tiles))
            for d in range(blocks_per_tile)]
    ms = region(n_full, ms, diag)

    for hd in range(2):
        cols = []
        for part in range(tiles_per_head):
            acc = acc_ref[hd * tiles_per_head + part]
            cols.append(acc[0:HEAD] / acc[HEAD:HEAD + 1])
        o_h = jnp.concatenate(cols, axis=1)
        if hd == 0:
            o_t = o_h
        else:
            o_t = jnp.concatenate([o_t, o_h], axis=0).astype(BF16)
    for c in range(tq // MXU_DIM):
        rows = slice(c * MXU_DIM, (c + 1) * MXU_DIM)
        o_ref[0, rows, :] = _dot_nt(eye, o_t[:, rows]).astype(BF16)


def _fox(qb, kb, vb, eye):
    B, S, _ = qb.shape
    tq = min(TQ_FOX, S)
    return pl.pallas_call(
        _fox_body,
        grid=(B, FOX_HEADS // 2, S // tq),
        in_specs=[
            pl.BlockSpec((1, tq, LANES), lambda b, p, i: (b, i, p)),
            pl.BlockSpec((1, S, MXU_DIM), lambda b, p, i: (b, 0, p)),
            pl.BlockSpec((1, S, LANES), lambda b, p, i: (b, 0, p)),
            _const_spec((MXU_DIM, MXU_DIM)),
        ],
        out_specs=pl.BlockSpec((1, tq, LANES), lambda b, p, i: (b, i, p)),
        out_shape=jax.ShapeDtypeStruct((B, S, FOX_W), BF16),
        scratch_shapes=[
            pltpu.VMEM((2 * VT_ROWS, S), BF16),
            pltpu.VMEM((MXU_DIM, 2 * tq), BF16),
            pltpu.VMEM((2 * tq // MXU_DIM, VT_ROWS, MXU_DIM), F32),
        ],
        compiler_params=pltpu.CompilerParams(
            dimension_semantics=("arbitrary", "arbitrary", "arbitrary"),
            vmem_limit_bytes=VMEM_LIMIT),
        name="fox",
    )(qb, kb, vb, eye)


TS_MLSTM = 1024


def _mlstm_body(qkv_ref, g_ref, onorm_ref, tri_ref, i4_ref, bd_ref, o_ref, c_ref, n_ref, m_ref):
    ts = qkv_ref.shape[1]
    W = MLSTM_W

    @pl.when(pl.program_id(1) == 0)
    def _():
        c_ref[...] = jnp.zeros_like(c_ref)
        n_ref[...] = jnp.zeros_like(n_ref)
        m_ref[...] = jnp.zeros_like(m_ref)

    row = lax.broadcasted_iota(jnp.int32, (CHUNK, W), 0)
    lane = lax.broadcasted_iota(jnp.int32, (CHUNK, W), 1)
    causal = (lane % HEAD) <= row
    lane_head = lane // HEAD
    bd = bd_ref[...]
    bd2 = jnp.concatenate([bd, bd], axis=1)
    i4_bf = i4_ref[...].astype(BF16)

    def block_diag(x):
        tiled = jnp.concatenate([x] * MLSTM_HEADS, axis=0)
        return tiled * (bd if x.shape[1] == W else bd2)

    def seg_max(d):
        out = jnp.zeros_like(d)
        for hd in range(MLSTM_HEADS):
            sel = lane_head == hd
            mx = jnp.max(jnp.where(sel, d, -jnp.inf), axis=-1, keepdims=True)
            out = jnp.where(sel, mx, out)
        return out

    chunks = range(ts // CHUNK)
    rows = [pl.ds(c * CHUNK, CHUNK) for c in chunks]

    b, g, u, dm, dm_max, a_max, wa = [], [], [], [], [], [], []
    for c in chunks:
        ig = g_ref[0, rows[c], 0:W]
        lf_hi, lf_lo = _split2(g_ref[0, rows[c], W:2 * W])
        b_c = _dot(tri_ref[...], lf_hi) + _dot(tri_ref[...], lf_lo)
        g_c = b_c[CHUNK - 1:CHUNK, :]
        u_c = b_c - ig
        u_row = jnp.sum(u_c * i4_ref[...], axis=0, keepdims=True)
        dm_c = jnp.where(causal, b_c - u_row, -jnp.inf)
        a_c = g_c - u_c
        am_c = jnp.max(a_c, axis=0, keepdims=True)
        b.append(b_c); g.append(g_c); u.append(u_c); dm.append(dm_c)
        dm_max.append(seg_max(dm_c)); a_max.append(am_c); wa.append(jnp.exp(a_c - am_c))

    m_prev, s_old, s_loc = [m_ref[0:1, :]], [], []
    for c in chunks:
        gm = g[c] + m_prev[c]
        m_new = jnp.maximum(gm, a_max[c])
        s_old.append(jnp.exp(gm - m_new))
        s_loc.append(jnp.exp(a_max[c] - m_new))
        m_prev.append(m_new)

    m_t, w_inter, sm_hi, sm_lo, k_t = [], [], [], [], []
    for c in chunks:
        inter = b[c] + m_prev[c]
        mt_c = jnp.maximum(inter, dm_max[c])
        decay = jnp.exp(dm[c] - mt_c)
        q = qkv_ref[0, rows[c], 0:W]
        k = qkv_ref[0, rows[c], W:2 * W]
        qk = _dot_nt(jnp.concatenate([q, i4_bf], axis=0), block_diag(k))
        hi, lo = _split2(qk[0:CHUNK] * decay)
        m_t.append(mt_c); w_inter.append(jnp.exp(inter - mt_c)); sm_hi.append(hi); sm_lo.append(lo)
        k_t.append(qk[CHUNK:].astype(BF16))

    c_cur, n_cur = c_ref[...], n_ref[...]
    state_bf = []
    for c in chunks:
        state_bf.append(jnp.concatenate([c_cur.astype(BF16), n_cur.astype(BF16)], axis=1))
        v = qkv_ref[0, rows[c], 2 * W:]
        weighted = jnp.concatenate([(wa[c] * v.astype(F32)).astype(BF16), wa[c].astype(BF16)], axis=1)
        loc = _dot(k_t[c], block_diag(weighted))
        c_cur = s_old[c] * c_cur + s_loc[c] * loc[:, 0:W]
        n_cur = s_old[c] * n_cur + s_loc[c] * loc[:, W:]
    c_ref[...] = c_cur
    n_ref[...] = n_cur
    m_ref[...] = jnp.broadcast_to(m_prev[-1], m_ref.shape)

    hval = []
    for c in chunks:
        q = qkv_ref[0, rows[c], 0:W]
        v = qkv_ref[0, rows[c], 2 * W:]
        inter_cn = _dot(q, block_diag(state_bf[c]))
        intra = _dot(sm_hi[c], jnp.concatenate([block_diag(v), bd], axis=1))
        num = w_inter[c] * inter_cn[:, 0:W] + intra[:, 0:W]
        den = w_inter[c] * inter_cn[:, W:] + intra[:, W:] + _dot(sm_lo[c], bd)
        hval.append(num / jnp.maximum(jnp.abs(den), jnp.exp(-m_t[c])))

    for c in chunks:
        hs_hi, hs_lo = _split2(hval[c] * hval[c])
        ss = _dot(hs_hi, bd) + _dot(hs_lo, bd)
        hc = (hval[c] * lax.rsqrt(ss * (1.0 / HEAD) + EPS)) * onorm_ref[...]
        co = g_ref[0, rows[c], 2 * W:]
        o_ref[0, rows[c], :] = (jax.nn.sigmoid(co) * hc).astype(BF16)


def _mlstm(layer, qkvc, gc, onorm, tri, i4, bd):
    B, S, _ = qkvc.shape
    ts = min(TS_MLSTM, S)
    return pl.pallas_call(
        _mlstm_body,
        grid=(B, S // ts),
        in_specs=[
            pl.BlockSpec((1, ts, 3 * MLSTM_W), lambda b, i: (b, i, 0)),
            pl.BlockSpec((1, ts, 3 * MLSTM_W), lambda b, i: (b, i, 0)),
            _layer_spec((1, MLSTM_W), layer),
            _const_spec((CHUNK, CHUNK)),
            _const_spec((CHUNK, MLSTM_W)),
            _const_spec((MLSTM_W, MLSTM_W)),
        ],
        out_specs=pl.BlockSpec((1, ts, MLSTM_W), lambda b, i: (b, i, 0)),
        out_shape=jax.ShapeDtypeStruct((B, S, MLSTM_W), BF16),
        scratch_shapes=[pltpu.VMEM((HEAD, MLSTM_W), F32), pltpu.VMEM((HEAD, MLSTM_W), F32),
                        pltpu.VMEM((8, MLSTM_W), F32)],
        compiler_params=pltpu.CompilerParams(
            dimension_semantics=("arbitrary", "arbitrary"), vmem_limit_bytes=VMEM_LIMIT),
        name="mlstm",
    )(qkvc, gc, onorm, tri, i4, bd)


TM_FFN = 1024
FF_CHUNK = 1024


def _out_ffn_body(x_ref, ya_ref, yb_ref, yc_ref, wo_ref, n2_ref, w1_ref, w2_ref, o_ref, a_ref):
    x1 = (x_ref[...] + _dot(ya_ref[...], wo_ref[0:SWA_Q, :])
          + _dot(yb_ref[...], wo_ref[SWA_Q:SWA_Q + FOX_W, :])
          + _dot(yc_ref[...], wo_ref[SWA_Q + FOX_W:, :]))
    ms = jnp.mean(x1 * x1, axis=-1, keepdims=True)
    h2 = ((x1 * lax.rsqrt(ms + EPS)) * n2_ref[...]).astype(BF16)
    for j in range(D_FF // FF_CHUNK):
        cols = slice(j * FF_CHUNK, (j + 1) * FF_CHUNK)
        r = jnp.maximum(_dot(h2, w1_ref[:, cols]), 0.0)
        a_ref[:, cols] = (r * r).astype(BF16)
    o_ref[...] = x1 + _dot(a_ref[...], w2_ref[...])


def _out_ffn(layer, x2d, ya, yb, yc, wo, n2, w1, w2):
    T = x2d.shape[0]
    tm = min(TM_FFN, T)
    tok = lambda w: pl.BlockSpec((tm, w), lambda i: (i, 0))
    return pl.pallas_call(
        _out_ffn_body,
        grid=(T // tm,),
        in_specs=[
            tok(D_MODEL), tok(SWA_Q), tok(FOX_W), tok(MLSTM_W),
            _layer_spec((D_MODEL, D_MODEL), layer),
            _layer_spec((1, D_MODEL), layer),
            _const_spec((D_MODEL, D_FF)),
            _const_spec((D_FF, D_MODEL)),
        ],
        out_specs=tok(D_MODEL),
        out_shape=jax.ShapeDtypeStruct((T, D_MODEL), F32),
        scratch_shapes=[pltpu.VMEM((tm, D_FF), BF16)],
        compiler_params=pltpu.CompilerParams(
            dimension_semantics=("arbitrary",), vmem_limit_bytes=VMEM_LIMIT),
        name="out_ffn",
    )(x2d, ya, yb, yc, wo, n2, w1, w2)


def _rope_tables(S):
    inv = ROPE_THETA ** (-np.arange(HALF, dtype=np.float64) / HALF)
    ang = np.arange(S, dtype=np.float64)[:, None] * inv[None, :]
    cos = np.tile(np.cos(ang), (1, LANES // HALF))
    sin = np.tile(np.sin(ang), (1, LANES // HALF))
    sign = np.where(np.arange(LANES) < HEAD, -1.0, 1.0)
    return jnp.asarray(cos, F32), jnp.asarray(sin * sign[None, :], F32)


def _prep_w_in(w_in):
    select = lambda w, p: jnp.einsum("ldk,kn->ldn", w.astype(BF16), jnp.asarray(p, BF16),
                                     preferred_element_type=BF16)
    w_b = w_in[:, :, W_A:W_A + W_B].astype(BF16)
    return select(w_in[:, :, :W_A], _PERM_A), w_b, select(w_in[:, :, W_A + W_B:], _PERM_C)


def _prep_vecs(swa_q_norm, swa_k_norm, fox_q_norm, fox_k_norm, fox_f_bias, mlstm_i_bias, mlstm_f_bias):
    L = swa_q_norm.shape[0]

    def swa_lanes(g):
        return jnp.concatenate([g[:, :HALF], g[:, :HALF], g[:, HALF:], g[:, HALF:]], axis=-1)

    pad = lambda v: jnp.pad(v, ((0, 0), (0, MLSTM_W - v.shape[1])))
    return jnp.stack([
        pad(swa_lanes(swa_q_norm)), pad(swa_lanes(swa_k_norm)),
        pad(jnp.tile(fox_q_norm, (1, 2))), pad(jnp.tile(fox_k_norm, (1, 2))),
        pad(jnp.concatenate([jnp.tile(fox_f_bias, (1, 3)), mlstm_i_bias, mlstm_f_bias], axis=-1)),
        jnp.zeros((L, MLSTM_W), F32), jnp.zeros((L, MLSTM_W), F32), jnp.zeros((L, MLSTM_W), F32),
    ], axis=1).astype(F32)


def kernel(x, norm1, w_in, swa_q_norm, swa_k_norm, swa_sinks, fox_q_norm, fox_k_norm, fox_f_bias,
           mlstm_i_bias, mlstm_f_bias, mlstm_out_norm, w_out, norm2, w_ff1, w_ff2):
    B, S, D = x.shape
    L = norm1.shape[0]
    T = B * S
    cos, sin = _rope_tables(S)
    tri256, tri64, bd256 = (jnp.asarray(c, BF16) for c in (_TRI256, _TRI64, _BD256))
    i4 = jnp.asarray(_I4, F32)
    e3 = jnp.asarray(_gate_expand(), BF16)
    eye256 = jnp.eye(MXU_DIM, dtype=BF16)
    band = jnp.asarray(_swa_band_cap(), F32)

    w_a, w_b, w_c = _prep_w_in(w_in)
    wo = w_out.astype(BF16)
    vecs = _prep_vecs(swa_q_norm, swa_k_norm, fox_q_norm, fox_k_norm, fox_f_bias, mlstm_i_bias,
                      mlstm_f_bias)
    n1 = norm1.reshape(L, 1, D).astype(F32)
    n2 = norm2.reshape(L, 1, D).astype(F32)
    onorm = mlstm_out_norm.reshape(L, 1, MLSTM_W).astype(F32)
    sinks = swa_sinks.reshape(L * SWA_HEADS).astype(F32)

    for l in range(L):
        qa, kva, qb, kb, vb, qkvc, gc, w1, w2 = _in_proj(
            l, x, n1, w_a, w_b, w_c, vecs, cos, sin, tri256, e3, w_ff1, w_ff2)
        ya = _swa(l, sinks, qa, kva, eye256, band)
        yb = _fox(qb, kb, vb, eye256)
        yc = _mlstm(l, qkvc, gc, onorm, tri64, i4, bd256)
        x = _out_ffn(l, x.reshape(T, D), ya.reshape(T, SWA_Q), yb.reshape(T, FOX_W),
                     yc.reshape(T, MLSTM_W), wo, n2, w1, w2).reshape(B, S, D)
    return x
```

```python
import functools
import math

import numpy as np
import jax
import jax.numpy as jnp
from jax import lax
from jax.experimental import pallas as pl
from jax.experimental.pallas import tpu as pltpu

F32 = jnp.float32
BF16 = jnp.bfloat16

D_MODEL = 1024
HEAD = 64
HALF = HEAD // 2
CHUNK = 64
SWA_HEADS = 8
SWA_KV_HEADS = 2
FOX_HEADS = 4
MLSTM_HEADS = 4
SWA_Q = SWA_HEADS * HEAD
SWA_KV = SWA_KV_HEADS * HEAD
FOX_W = FOX_HEADS * HEAD
MLSTM_W = MLSTM_HEADS * HEAD
D_FF = 4 * D_MODEL
QBLK = 128
ROPE_THETA = 10000.0
EPS = 1e-6
QK_SCALE = 1.0 / math.sqrt(HEAD)
LOG2E = math.log2(math.e)

LANES = 128
MXU_DIM = 256
VMEM_LIMIT = 56 * 1024 * 1024

W_A = SWA_Q + 2 * SWA_KV
W_B = 3 * FOX_W
SEC_C = 0
SEC_CO = SEC_C + 768
SEC_G = SEC_CO + 256
W_C = SEC_G + LANES
GATE_BF = 0
GATE_CI = 12
GATE_CF = 16
IN_WIDTH = W_A + W_B + FOX_HEADS + 4 * MLSTM_W + 2 * MLSTM_HEADS


def _perm_a():
    p = np.zeros((W_A, W_A), np.float32)
    lane = np.arange(LANES)
    quarter, i = lane // HALF, lane % HALF
    dim, is_b = i + HALF * (quarter // 2), quarter % 2
    for j in range(SWA_HEADS // 2):
        p[(j + 4 * is_b) * HEAD + dim, j * LANES + lane] = 1.0
    p[SWA_Q + is_b * HEAD + dim, SWA_Q + lane] = 1.0
    p[SWA_Q + SWA_KV + lane, SWA_Q + SWA_KV + lane] = 1.0
    return p


def _perm_c():
    src = IN_WIDTH - W_A - W_B
    o_bf, o_cqkv = 0, FOX_HEADS
    o_ci = o_cqkv + 3 * MLSTM_W
    o_cf, o_co = o_ci + MLSTM_HEADS, o_ci + 2 * MLSTM_HEADS
    p = np.zeros((src, W_C), np.float32)
    w = np.arange(3 * MLSTM_W)
    p[o_cqkv + w, SEC_C + w] = 1.0
    w = np.arange(MLSTM_W)
    p[o_co + w, SEC_CO + w] = 1.0
    h = np.arange(FOX_HEADS)
    for rep in range(3):
        p[o_bf + h, SEC_G + GATE_BF + rep * FOX_HEADS + h] = 1.0
    h = np.arange(MLSTM_HEADS)
    p[o_ci + h, SEC_G + GATE_CI + h] = 1.0
    p[o_cf + h, SEC_G + GATE_CF + h] = 1.0
    return p


def _gate_expand():
    e = np.zeros((LANES, 2 * MLSTM_W), np.float32)
    for h in range(MLSTM_HEADS):
        e[GATE_CI + h, h * HEAD:(h + 1) * HEAD] = 1.0
        e[GATE_CF + h, MLSTM_W + h * HEAD:MLSTM_W + (h + 1) * HEAD] = 1.0
    return np.concatenate([e, e, e], axis=0)


_PERM_A = _perm_a()
_PERM_C = _perm_c()


def _block_diag_ones(head_of_lane):
    return (head_of_lane[:, None] == head_of_lane[None, :]).astype(np.float32)


_TRI256 = np.tril(np.ones((MXU_DIM, MXU_DIM), np.float32))
_TRI64 = np.tril(np.ones((CHUNK, CHUNK), np.float32))
_w256 = np.arange(MLSTM_W)
_BD256 = _block_diag_ones(_w256 // HEAD)
_I4 = (np.arange(CHUNK)[:, None] == (_w256 % HEAD)[None, :]).astype(np.float32)


def _swa_band_cap():
    key_chunk = np.arange(2 * QBLK)[:, None] // CHUNK
    qry_chunk = (np.arange(MXU_DIM)[None, :] % QBLK) // CHUNK
    ahead = key_chunk - qry_chunk
    return np.where((ahead >= 0) & (ahead <= 2), np.inf, -np.inf).astype(np.float32)


def _const_spec(shape):
    nd = len(shape)
    return pl.BlockSpec(shape, lambda *_: (0,) * nd, pipeline_mode=pl.Buffered(1))


def _layer_spec(shape, layer):
    nd = len(shape)
    return pl.BlockSpec((None,) + tuple(shape), lambda *_: (layer,) + (0,) * nd,
                        pipeline_mode=pl.Buffered(1))


def _log_sigmoid(x):
    return jnp.minimum(x, 0.0) - jnp.log(1.0 + jnp.exp(-jnp.abs(x)))


def _split2(x):
    hi = x.astype(BF16)
    lo = (x - hi.astype(F32)).astype(BF16)
    return hi, lo


def _split3_by_lane(x, lane):
    hi = x.astype(BF16).astype(F32)
    r1 = x - hi
    mid = r1.astype(BF16).astype(F32)
    low = r1 - mid
    return jnp.where(lane < 4, hi, jnp.where(lane < 8, mid, low)).astype(BF16)


def _dot(a, b):
    return jnp.dot(a, b, preferred_element_type=F32)


def _dot_nt(a, b):
    return lax.dot_general(a, b, (((1,), (1,)), ((), ())), preferred_element_type=F32)


def _dot_tn(a, b):
    return lax.dot_general(a, b, (((0,), (0,)), ((), ())), preferred_element_type=F32)


TM_IN = 1024


def _in_proj_body(x_ref, n1_ref, wa_ref, wb_ref, wc_ref, vec_ref, cos_ref, sin_ref, tri_ref, e3_ref,
                  wff1_ref, wff2_ref,
                  qa_ref, kva_ref, qb_ref, kb_ref, vb_ref, qkvc_ref, gc_ref, wff1_bf_ref, wff2_bf_ref,
                  carry_ref):
    tm = x_ref.shape[1]
    wff1_bf_ref[...] = wff1_ref[...].astype(BF16)
    wff2_bf_ref[...] = wff2_ref[...].astype(BF16)

    @pl.when(pl.program_id(1) == 0)
    def _():
        carry_ref[...] = jnp.zeros_like(carry_ref)

    x = x_ref[0]
    ms = jnp.mean(x * x, axis=-1, keepdims=True)
    h = ((x * lax.rsqrt(ms + EPS)) * n1_ref[...]).astype(BF16)

    def proj(w_ref, c0, width):
        return _dot(h, w_ref[:, c0:c0 + width])

    lane_row = lax.broadcasted_iota(jnp.int32, (1, LANES), 1)
    first_swa = (lane_row // HALF) % 2 == 0
    first_nat = lane_row < HEAD

    def head_norm(z, first, gain):
        sq = z * z
        s_first = jnp.sum(jnp.where(first, sq, 0.0), axis=-1, keepdims=True)
        s_second = jnp.sum(jnp.where(first, 0.0, sq), axis=-1, keepdims=True)
        ss = jnp.where(first, s_first, s_second)
        return (z * lax.rsqrt(ss * (1.0 / HEAD) + EPS)) * gain

    cos = cos_ref[...]
    sin = sin_ref[...]

    def rope(z):
        return z * cos + pltpu.roll(z, HEAD, 1) * sin

    g_aq = vec_ref[0:1, 0:LANES]
    g_ak = vec_ref[1:2, 0:LANES]
    g_bq = vec_ref[2:3, 0:LANES]
    g_bk = vec_ref[3:4, 0:LANES]
    gate_bias = vec_ref[4:5, 0:LANES]

    lane = lax.broadcasted_iota(jnp.int32, (MXU_DIM, LANES), 1)
    zg = proj(wc_ref, SEC_G, LANES) + gate_bias
    log_sig = _log_sigmoid(zg)
    log_f = jnp.where(lane[0:1] < GATE_CI, log_sig, 0.0)
    f_terms = [_split3_by_lane(log_f[blk * MXU_DIM:(blk + 1) * MXU_DIM], lane)
               for blk in range(tm // MXU_DIM)]
    gz = jnp.where(lane[0:1] < GATE_CI, 0.0,
                   jnp.where(lane[0:1] < GATE_CF, zg,
                             jnp.where(lane[0:1] < GATE_CF + MLSTM_HEADS, log_sig, 0.0)))
    gz_hi = gz.astype(BF16)
    gz_r = gz - gz_hi.astype(F32)
    gz_mid = gz_r.astype(BF16)
    gz_terms = jnp.concatenate([gz_hi, gz_mid, (gz_r - gz_mid.astype(F32)).astype(BF16)], axis=1)

    za = proj(wa_ref, 0, W_A)
    for j in range(SWA_HEADS // 2):
        q = rope(head_norm(za[:, j * LANES:(j + 1) * LANES], first_swa, g_aq)) * (QK_SCALE * LOG2E)
        qa_ref[0, :, j * LANES:(j + 1) * LANES] = q.astype(BF16)
    k = rope(head_norm(za[:, SWA_Q:SWA_Q + LANES], first_swa, g_ak))
    kva_ref[0, :, 0:LANES] = k.astype(BF16)
    kva_ref[0, :, LANES:2 * LANES] = za[:, SWA_Q + SWA_KV:].astype(BF16)

    zb = proj(wb_ref, 0, W_B)
    for p in range(FOX_HEADS // 2):
        q = head_norm(zb[:, p * LANES:(p + 1) * LANES], first_nat, g_bq) * (QK_SCALE * LOG2E)
        qb_ref[0, :, p * LANES:(p + 1) * LANES] = q.astype(BF16)
        k = head_norm(zb[:, FOX_W + p * LANES:FOX_W + (p + 1) * LANES], first_nat, g_bk)
        kb_ref[0, :, 2 * p * LANES:(2 * p + 1) * LANES] = k.astype(BF16)
    vb_ref[0] = zb[:, 2 * FOX_W:].astype(BF16)

    zc = proj(wc_ref, SEC_C, 768)
    qkvc_ref[0, :, 0:MLSTM_W] = zc[:, 0:MLSTM_W].astype(BF16)
    qkvc_ref[0, :, MLSTM_W:2 * MLSTM_W] = (zc[:, MLSTM_W:2 * MLSTM_W] * QK_SCALE).astype(BF16)
    qkvc_ref[0, :, 2 * MLSTM_W:] = zc[:, 2 * MLSTM_W:].astype(BF16)
    gc_ref[0, :, 2 * MLSTM_W:] = proj(wc_ref, SEC_CO, MLSTM_W)

    gc_ref[0, :, 0:2 * MLSTM_W] = _dot(gz_terms, e3_ref[...])

    carry = carry_ref[0:1, :]
    for blk in range(tm // MXU_DIM):
        rows = slice(blk * MXU_DIM, (blk + 1) * MXU_DIM)
        cs = _dot(tri_ref[...], f_terms[blk])
        f_loc = cs + pltpu.roll(cs, LANES - 4, 1) + pltpu.roll(cs, LANES - 8, 1)
        f_cum = jnp.where(lane < 4, f_loc, 0.0) + carry
        carry = f_cum[MXU_DIM - 1:MXU_DIM, :]
        f_rep = (f_cum + pltpu.roll(f_cum, 4, 1) + pltpu.roll(f_cum, 8, 1)) * LOG2E
        aug = _split3_by_lane(f_rep, lane)
        kb_ref[0, rows, LANES:2 * LANES] = aug
        kb_ref[0, rows, 3 * LANES:4 * LANES] = aug
    carry_ref[...] = jnp.broadcast_to(carry, carry_ref.shape)


def _in_proj(layer, x, n1, w_a, w_b, w_c, vecs, cos, sin, tri, e3, w_ff1, w_ff2):
    B, S, _ = x.shape
    tm = min(TM_IN, S)
    n_i = S // tm
    grid = (B, n_i)
    steps = B * n_i
    assert D_MODEL % (16 * steps) == 0, "FFN weight slabs must be whole bf16 sublane tiles"
    r1, r2 = D_MODEL // steps, D_FF // steps
    tok = lambda w: pl.BlockSpec((1, tm, w), lambda b, i: (b, i, 0))
    out_shapes = (
        jax.ShapeDtypeStruct((B, S, SWA_Q), BF16),
        jax.ShapeDtypeStruct((B, S, 2 * SWA_KV), BF16),
        jax.ShapeDtypeStruct((B, S, FOX_W), BF16),
        jax.ShapeDtypeStruct((B, S, 2 * MXU_DIM), BF16),
        jax.ShapeDtypeStruct((B, S, FOX_W), BF16),
        jax.ShapeDtypeStruct((B, S, 3 * MLSTM_W), BF16),
        jax.ShapeDtypeStruct((B, S, 3 * MLSTM_W), F32),
        jax.ShapeDtypeStruct((D_MODEL, D_FF), BF16),
        jax.ShapeDtypeStruct((D_FF, D_MODEL), BF16),
    )
    slab_out = [pl.BlockSpec((r1, D_FF), lambda b, i: (b * n_i + i, 0)),
                pl.BlockSpec((r2, D_MODEL), lambda b, i: (b * n_i + i, 0))]
    return pl.pallas_call(
        _in_proj_body,
        grid=grid,
        in_specs=[
            tok(D_MODEL),
            _layer_spec((1, D_MODEL), layer),
            _layer_spec((D_MODEL, W_A), layer),
            _layer_spec((D_MODEL, W_B), layer),
            _layer_spec((D_MODEL, W_C), layer),
            _layer_spec((8, MLSTM_W), layer),
            pl.BlockSpec((tm, LANES), lambda b, i: (i, 0)),
            pl.BlockSpec((tm, LANES), lambda b, i: (i, 0)),
            _const_spec((MXU_DIM, MXU_DIM)),
            _const_spec((3 * LANES, 2 * MLSTM_W)),
            pl.BlockSpec((None, r1, D_FF), lambda b, i: (layer, b * n_i + i, 0)),
            pl.BlockSpec((None, r2, D_MODEL), lambda b, i: (layer, b * n_i + i, 0)),
        ],
        out_specs=[tok(s.shape[-1]) for s in out_shapes[:-2]] + slab_out,
        out_shape=out_shapes,
        scratch_shapes=[pltpu.VMEM((8, LANES), F32)],
        compiler_params=pltpu.CompilerParams(
            dimension_semantics=("arbitrary", "arbitrary"), vmem_limit_bytes=VMEM_LIMIT),
        name="in_proj",
    )(x, n1, w_a, w_b, w_c, vecs, cos, sin, tri, e3, w_ff1, w_ff2)


TS_SWA = 2048
SWA_DEPTH = 8
VT_ROWS = HEAD + 16


def _swa_body(sink_base, sink_ref, q_ref, kvc_ref, kvp_ref, eye_ref, band_ref, o_ref):
    step = pl.program_id(1)
    nblk = q_ref.shape[1] // QBLK
    per_group = SWA_HEADS // SWA_KV_HEADS
    eye = eye_ref[0:LANES, 0:LANES]

    band_cap = band_ref[...]
    key = lax.broadcasted_iota(jnp.int32, (2 * QBLK, MXU_DIM), 0)
    first_cap = jnp.where((key >= QBLK) | (step > 0), band_cap, -jnp.inf)
    lane = lax.broadcasted_iota(jnp.int32, (QBLK, LANES), 1)
    lane_a = (lane // HALF) % 2 == 0
    first_head = lax.broadcasted_iota(jnp.int32, (1, MXU_DIM), 1) < QBLK
    zero = jnp.zeros((QBLK, LANES), BF16)
    ones = jnp.ones((VT_ROWS - HEAD, 2 * QBLK), BF16)
    kv_all = jnp.concatenate([kvp_ref[0], kvc_ref[0]], axis=0)
    rows = [slice(blk * QBLK, (blk + 1) * QBLK) for blk in range(nblk)]
    bands = [kv_all[blk * QBLK:(blk + 2) * QBLK] for blk in range(nblk)]

    vt_aug = []
    for blk in range(nblk):
        v_t = _dot_nt(eye, bands[blk][:, LANES:]).astype(BF16)
        vt_aug.append([jnp.concatenate([v_t[g * HEAD:(g + 1) * HEAD], ones], axis=0)
                       for g in range(SWA_KV_HEADS)])

    units = [(blk, g, part) for blk in range(nblk) for g in range(SWA_KV_HEADS) for part in range(2)]

    def score(blk, g, part):
        keep = lane_a if g == 0 else jnp.logical_not(lane_a)
        q_cols = [q_ref[0, rows[blk], j * LANES:(j + 1) * LANES] for j in (2 * part, 2 * part + 1)]
        q_g = jnp.concatenate([jnp.where(keep, q2, zero) for q2 in q_cols], axis=0)
        return _dot_nt(bands[blk][:, 0:LANES], q_g)

    def fold(s, blk, g, part):
        s = jnp.minimum(s, band_cap if blk > 0 else first_cap)
        head = sink_base + g * per_group + 2 * part
        sink = jnp.where(first_head, sink_ref[head], sink_ref[head + 1]) * LOG2E
        m = jnp.maximum(jnp.max(s, axis=0, keepdims=True), sink)
        p = jnp.exp2(s - m)
        o_aug = _dot(vt_aug[blk][g], p.astype(BF16))
        den = o_aug[HEAD:HEAD + 1] + jnp.exp2(sink - m)
        return (o_aug[0:HEAD] / den).astype(BF16)

    def write_block(blk, outs):
        for g in range(SWA_KV_HEADS):
            for part in range(2):
                o = outs[(blk, g, part)]
                pair_t = jnp.concatenate([o[:, 0:QBLK], o[:, QBLK:]], axis=0)
                j = 2 * g + part
                o_ref[0, rows[blk], j * LANES:(j + 1) * LANES] = _dot_nt(eye, pair_t).astype(BF16)

    outs = {}
    queue = [score(*unit) for unit in units[:SWA_DEPTH]]
    for n, unit in enumerate(units):
        s_cur = queue.pop(0)
        if n + SWA_DEPTH < len(units):
            queue.append(score(*units[n + SWA_DEPTH]))
        outs[unit] = fold(s_cur, *unit)
        blk = unit[0]
        if unit[1:] == (SWA_KV_HEADS - 1, 1) and blk > 0:
            write_block(blk - 1, outs)
    write_block(nblk - 1, outs)


def _swa(layer, sinks, qa, kva, eye, band):
    B, S, _ = qa.shape
    ts = min(TS_SWA, S)
    per_step = ts // QBLK
    return pl.pallas_call(
        functools.partial(_swa_body, layer * SWA_HEADS),
        grid=(B, S // ts),
        in_specs=[
            pl.BlockSpec(memory_space=pltpu.SMEM),
            pl.BlockSpec((1, ts, SWA_Q), lambda b, i: (b, i, 0)),
            pl.BlockSpec((1, ts, 2 * SWA_KV), lambda b, i: (b, i, 0)),
            pl.BlockSpec((1, QBLK, 2 * SWA_KV), lambda b, i: (b, jnp.maximum(i * per_step - 1, 0), 0)),
            _const_spec((MXU_DIM, MXU_DIM)),
            _const_spec((2 * QBLK, MXU_DIM)),
        ],
        out_specs=pl.BlockSpec((1, ts, SWA_Q), lambda b, i: (b, i, 0)),
        out_shape=jax.ShapeDtypeStruct((B, S, SWA_Q), BF16),
        compiler_params=pltpu.CompilerParams(
            dimension_semantics=("arbitrary", "arbitrary"), vmem_limit_bytes=VMEM_LIMIT),
        name="swa",
    )(sinks, qa, kva, kva, eye, band)


TQ_FOX = 1024
TK_FOX = 256
NEG_BIG = -1e30


FOX_UNROLL = 4
FOX_DEPTH = 8


def _fox_body(q_ref, k_ref, v_ref, eye_ref, o_ref, vt_ref, qt_ref, acc_ref):
    tq = q_ref.shape[1]
    tk = TK_FOX
    S = k_ref.shape[1]
    pair = pl.program_id(1)
    qi = pl.program_id(2)
    eye = eye_ref[...]
    n_tiles = 2 * tq // MXU_DIM
    tiles_per_head = tq // MXU_DIM

    @pl.when(qi == 0)
    def _():
        ones = jnp.ones((VT_ROWS - HEAD, tk), BF16)
        for j in range(S // tk):
            blk = slice(j * tk, (j + 1) * tk)
            v_t = _dot_nt(eye[0:LANES, 0:LANES], v_ref[0, blk, :]).astype(BF16)
            vt_ref[:, blk] = jnp.concatenate([v_t[0:HEAD], ones, v_t[HEAD:], ones], axis=0)

    lane = lax.broadcasted_iota(jnp.int32, (tq, LANES), 1)
    lane_a = lane < HEAD
    q2 = q_ref[0]
    zero = jnp.zeros_like(q2)

    def minus_one_at(head):
        hit = (lane == head) | (lane == head + 4) | (lane == head + 8)
        return jnp.where(hit, -1.0, 0.0).astype(BF16)

    q_aug = [jnp.concatenate([jnp.where(lane_a, q2, zero), minus_one_at(2 * pair)], axis=1),
             jnp.concatenate([jnp.where(lane_a, zero, q2), minus_one_at(2 * pair + 1)], axis=1)]
    for t in range(n_tiles):
        hd, part = divmod(t, tiles_per_head)
        rows = q_aug[hd][part * MXU_DIM:(part + 1) * MXU_DIM]
        qt_ref[:, t * MXU_DIM:(t + 1) * MXU_DIM] = _dot_nt(eye, rows).astype(BF16)
    acc_ref[...] = jnp.zeros_like(acc_ref)

    key = lax.broadcasted_iota(jnp.int32, (tk // 2, MXU_DIM), 0)
    qry = lax.broadcasted_iota(jnp.int32, (tk // 2, MXU_DIM), 1)

    def fold(s, t, vt_blk, m, triangular):
        hd = t // tiles_per_head
        vt_h = vt_blk[hd * VT_ROWS:(hd + 1) * VT_ROWS]
        half = tk // 2
        for h in range(2):
            sh = s[h * half:(h + 1) * half]
            if triangular:
                valid = key + h * half <= qry
                sh = jnp.where(valid, sh, NEG_BIG)
            m_new = jnp.maximum(m, jnp.max(sh, axis=0, keepdims=True))
            alpha = jnp.exp2(m - m_new)
            p = jnp.exp2(sh - m_new)
            if triangular:
                p = jnp.where(valid, p, 0.0)
            acc_ref[t] = alpha * acc_ref[t] + _dot(vt_h[:, h * half:(h + 1) * half], p.astype(BF16))
            m = m_new
        return m

    def region(j0, ms, plans):
        k_blks, vt_blks, todo = [], [], []
        for b, plan in enumerate(plans):
            r0 = pl.multiple_of((j0 + b) * tk, tk)
            k_blks.append(k_ref[0, pl.ds(r0, tk), :])
            vt_blks.append(vt_ref[:, pl.ds(r0, tk)])
            todo += [(b, t) for t in range(n_tiles) if plan[t] is not None]
        score = lambda b, t: _dot(k_blks[b], qt_ref[:, t * MXU_DIM:(t + 1) * MXU_DIM])
        ms = list(ms)
        queue = [score(*unit) for unit in todo[:FOX_DEPTH]]
        for n, (b, t) in enumerate(todo):
            s_cur = queue.pop(0)
            if n + FOX_DEPTH < len(todo):
                queue.append(score(*todo[n + FOX_DEPTH]))
            ms[t] = fold(s_cur, t, vt_blks[b], ms[t], plans[b][t])
        return tuple(ms)

    blocks_per_tile = tq // tk
    full = (False,) * n_tiles
    n_full = blocks_per_tile * qi
    m_init = jnp.full((1, MXU_DIM), NEG_BIG, F32)
    ms = lax.fori_loop(0, n_full // FOX_UNROLL,
                       lambda g, c: region(g * FOX_UNROLL, c, (full,) * FOX_UNROLL), (m_init,) * n_tiles)
    if FOX_UNROLL > blocks_per_tile:
        done = (n_full // FOX_UNROLL) * FOX_UNROLL
        ms = lax.fori_loop(0, (n_full - done) // blocks_per_tile,
                           lambda g, c: region(done + g * blocks_per_tile, c, (full,) * blocks_per_tile), ms)
    diag = [tuple(None if t % tiles_per_head < d else t % tiles_per_head == d for t in range(n_tiles))
            for d in range(blocks_per_tile)]
    ms = region(n_full, ms, diag)

    for hd in range(2):
        cols = []
        for part in range(tiles_per_head):
            acc = acc_ref[hd * tiles_per_head + part]
            cols.append(acc[0:HEAD] / acc[HEAD:HEAD + 1])
        o_h = jnp.concatenate(cols, axis=1)
        if hd == 0:
            o_t = o_h
        else:
            o_t = jnp.concatenate([o_t, o_h], axis=0).astype(BF16)
    for c in range(tq // MXU_DIM):
        rows = slice(c * MXU_DIM, (c + 1) * MXU_DIM)
        o_ref[0, rows, :] = _dot_nt(eye, o_t[:, rows]).astype(BF16)


def _fox(qb, kb, vb, eye):
    B, S, _ = qb.shape
    tq = min(TQ_FOX, S)
    return pl.pallas_call(
        _fox_body,
        grid=(B, FOX_HEADS // 2, S // tq),
        in_specs=[
            pl.BlockSpec((1, tq, LANES), lambda b, p, i: (b, i, p)),
            pl.BlockSpec((1, S, MXU_DIM), lambda b, p, i: (b, 0, p)),
            pl.BlockSpec((1, S, LANES), lambda b, p, i: (b, 0, p)),
            _const_spec((MXU_DIM, MXU_DIM)),
        ],
        out_specs=pl.BlockSpec((1, tq, LANES), lambda b, p, i: (b, i, p)),
        out_shape=jax.ShapeDtypeStruct((B, S, FOX_W), BF16),
        scratch_shapes=[
            pltpu.VMEM((2 * VT_ROWS, S), BF16),
            pltpu.VMEM((MXU_DIM, 2 * tq), BF16),
            pltpu.VMEM((2 * tq // MXU_DIM, VT_ROWS, MXU_DIM), F32),
        ],
        compiler_params=pltpu.CompilerParams(
            dimension_semantics=("arbitrary", "arbitrary", "arbitrary"),
            vmem_limit_bytes=VMEM_LIMIT),
        name="fox",
    )(qb, kb, vb, eye)


TS_MLSTM = 1024


def _mlstm_body(qkv_ref, g_ref, onorm_ref, tri_ref, i4_ref, bd_ref, o_ref, c_ref, n_ref, m_ref):
    ts = qkv_ref.shape[1]
    W = MLSTM_W

    @pl.when(pl.program_id(1) == 0)
    def _():
        c_ref[...] = jnp.zeros_like(c_ref)
        n_ref[...] = jnp.zeros_like(n_ref)
        m_ref[...] = jnp.zeros_like(m_ref)

    row = lax.broadcasted_iota(jnp.int32, (CHUNK, W), 0)
    lane = lax.broadcasted_iota(jnp.int32, (CHUNK, W), 1)
    causal = (lane % HEAD) <= row
    lane_head = lane // HEAD
    bd = bd_ref[...]
    bd2 = jnp.concatenate([bd, bd], axis=1)
    i4_bf = i4_ref[...].astype(BF16)

    def block_diag(x):
        tiled = jnp.concatenate([x] * MLSTM_HEADS, axis=0)
        return tiled * (bd if x.shape[1] == W else bd2)

    def seg_max(d):
        out = jnp.zeros_like(d)
        for hd in range(MLSTM_HEADS):
            sel = lane_head == hd
            mx = jnp.max(jnp.where(sel, d, -jnp.inf), axis=-1, keepdims=True)
            out = jnp.where(sel, mx, out)
        return out

    chunks = range(ts // CHUNK)
    rows = [pl.ds(c * CHUNK, CHUNK) for c in chunks]

    b, g, u, dm, dm_max, a_max, wa = [], [], [], [], [], [], []
    for c in chunks:
        ig = g_ref[0, rows[c], 0:W]
        lf_hi, lf_lo = _split2(g_ref[0, rows[c], W:2 * W])
        b_c = _dot(tri_ref[...], lf_hi) + _dot(tri_ref[...], lf_lo)
        g_c = b_c[CHUNK - 1:CHUNK, :]
        u_c = b_c - ig
        u_row = jnp.sum(u_c * i4_ref[...], axis=0, keepdims=True)
        dm_c = jnp.where(causal, b_c - u_row, -jnp.inf)
        a_c = g_c - u_c
        am_c = jnp.max(a_c, axis=0, keepdims=True)
        b.append(b_c); g.append(g_c); u.append(u_c); dm.append(dm_c)
        dm_max.append(seg_max(dm_c)); a_max.append(am_c); wa.append(jnp.exp(a_c - am_c))

    m_prev, s_old, s_loc = [m_ref[0:1, :]], [], []
    for c in chunks:
        gm = g[c] + m_prev[c]
        m_new = jnp.maximum(gm, a_max[c])
        s_old.append(jnp.exp(gm - m_new))
        s_loc.append(jnp.exp(a_max[c] - m_new))
        m_prev.append(m_new)

    m_t, w_inter, sm_hi, sm_lo, k_t = [], [], [], [], []
    for c in chunks:
        inter = b[c] + m_prev[c]
        mt_c = jnp.maximum(inter, dm_max[c])
        decay = jnp.exp(dm[c] - mt_c)
        q = qkv_ref[0, rows[c], 0:W]
        k = qkv_ref[0, rows[c], W:2 * W]
        qk = _dot_nt(jnp.concatenate([q, i4_bf], axis=0), block_diag(k))
        hi, lo = _split2(qk[0:CHUNK] * decay)
        m_t.append(mt_c); w_inter.append(jnp.exp(inter - mt_c)); sm_hi.append(hi); sm_lo.append(lo)
        k_t.append(qk[CHUNK:].astype(BF16))

    c_cur, n_cur = c_ref[...], n_ref[...]
    state_bf = []
    for c in chunks:
        state_bf.append(jnp.concatenate([c_cur.astype(BF16), n_cur.astype(BF16)], axis=1))
        v = qkv_ref[0, rows[c], 2 * W:]
        weighted = jnp.concatenate([(wa[c] * v.astype(F32)).astype(BF16), wa[c].astype(BF16)], axis=1)
        loc = _dot(k_t[c], block_diag(weighted))
        c_cur = s_old[c] * c_cur + s_loc[c] * loc[:, 0:W]
        n_cur = s_old[c] * n_cur + s_loc[c] * loc[:, W:]
    c_ref[...] = c_cur
    n_ref[...] = n_cur
    m_ref[...] = jnp.broadcast_to(m_prev[-1], m_ref.shape)

    hval = []
    for c in chunks:
        q = qkv_ref[0, rows[c], 0:W]
        v = qkv_ref[0, rows[c], 2 * W:]
        inter_cn = _dot(q, block_diag(state_bf[c]))
        intra = _dot(sm_hi[c], jnp.concatenate([block_diag(v), bd], axis=1))
        num = w_inter[c] * inter_cn[:, 0:W] + intra[:, 0:W]
        den = w_inter[c] * inter_cn[:, W:] + intra[:, W:] + _dot(sm_lo[c], bd)
        hval.append(num / jnp.maximum(jnp.abs(den), jnp.exp(-m_t[c])))

    for c in chunks:
        hs_hi, hs_lo = _split2(hval[c] * hval[c])
        ss = _dot(hs_hi, bd) + _dot(hs_lo, bd)
        hc = (hval[c] * lax.rsqrt(ss * (1.0 / HEAD) + EPS)) * onorm_ref[...]
        co = g_ref[0, rows[c], 2 * W:]
        o_ref[0, rows[c], :] = (jax.nn.sigmoid(co) * hc).astype(BF16)


def _mlstm(layer, qkvc, gc, onorm, tri, i4, bd):
    B, S, _ = qkvc.shape
    ts = min(TS_MLSTM, S)
    return pl.pallas_call(
        _mlstm_body,
        grid=(B, S // ts),
        in_specs=[
            pl.BlockSpec((1, ts, 3 * MLSTM_W), lambda b, i: (b, i, 0)),
            pl.BlockSpec((1, ts, 3 * MLSTM_W), lambda b, i: (b, i, 0)),
            _layer_spec((1, MLSTM_W), layer),
            _const_spec((CHUNK, CHUNK)),
            _const_spec((CHUNK, MLSTM_W)),
            _const_spec((MLSTM_W, MLSTM_W)),
        ],
        out_specs=pl.BlockSpec((1, ts, MLSTM_W), lambda b, i: (b, i, 0)),
        out_shape=jax.ShapeDtypeStruct((B, S, MLSTM_W), BF16),
        scratch_shapes=[pltpu.VMEM((HEAD, MLSTM_W), F32), pltpu.VMEM((HEAD, MLSTM_W), F32),
                        pltpu.VMEM((8, MLSTM_W), F32)],
        compiler_params=pltpu.CompilerParams(
            dimension_semantics=("arbitrary", "arbitrary"), vmem_limit_bytes=VMEM_LIMIT),
        name="mlstm",
    )(qkvc, gc, onorm, tri, i4, bd)


TM_FFN = 1024
FF_CHUNK = 1024


def _out_ffn_body(x_ref, ya_ref, yb_ref, yc_ref, wo_ref, n2_ref, w1_ref, w2_ref, o_ref, a_ref):
    x1 = (x_ref[...] + _dot(ya_ref[...], wo_ref[0:SWA_Q, :])
          + _dot(yb_ref[...], wo_ref[SWA_Q:SWA_Q + FOX_W, :])
          + _dot(yc_ref[...], wo_ref[SWA_Q + FOX_W:, :]))
    ms = jnp.mean(x1 * x1, axis=-1, keepdims=True)
    h2 = ((x1 * lax.rsqrt(ms + EPS)) * n2_ref[...]).astype(BF16)
    for j in range(D_FF // FF_CHUNK):
        cols = slice(j * FF_CHUNK, (j + 1) * FF_CHUNK)
        r = jnp.maximum(_dot(h2, w1_ref[:, cols]), 0.0)
        a_ref[:, cols] = (r * r).astype(BF16)
    o_ref[...] = x1 + _dot(a_ref[...], w2_ref[...])


def _out_ffn(layer, x2d, ya, yb, yc, wo, n2, w1, w2):
    T = x2d.shape[0]
    tm = min(TM_FFN, T)
    tok = lambda w: pl.BlockSpec((tm, w), lambda i: (i, 0))
    return pl.pallas_call(
        _out_ffn_body,
        grid=(T // tm,),
        in_specs=[
            tok(D_MODEL), tok(SWA_Q), tok(FOX_W), tok(MLSTM_W),
            _layer_spec((D_MODEL, D_MODEL), layer),
            _layer_spec((1, D_MODEL), layer),
            _const_spec((D_MODEL, D_FF)),
            _const_spec((D_FF, D_MODEL)),
        ],
        out_specs=tok(D_MODEL),
        out_shape=jax.ShapeDtypeStruct((T, D_MODEL), F32),
        scratch_shapes=[pltpu.VMEM((tm, D_FF), BF16)],
        compiler_params=pltpu.CompilerParams(
            dimension_semantics=("arbitrary",), vmem_limit_bytes=VMEM_LIMIT),
        name="out_ffn",
    )(x2d, ya, yb, yc, wo, n2, w1, w2)


def _rope_tables(S):
    inv = ROPE_THETA ** (-np.arange(HALF, dtype=np.float64) / HALF)
    ang = np.arange(S, dtype=np.float64)[:, None] * inv[None, :]
    cos = np.tile(np.cos(ang), (1, LANES // HALF))
    sin = np.tile(np.sin(ang), (1, LANES // HALF))
    sign = np.where(np.arange(LANES) < HEAD, -1.0, 1.0)
    return jnp.asarray(cos, F32), jnp.asarray(sin * sign[None, :], F32)


def _prep_w_in(w_in):
    select = lambda w, p: jnp.einsum("ldk,kn->ldn", w.astype(BF16), jnp.asarray(p, BF16),
                                     preferred_element_type=BF16)
    w_b = w_in[:, :, W_A:W_A + W_B].astype(BF16)
    return select(w_in[:, :, :W_A], _PERM_A), w_b, select(w_in[:, :, W_A + W_B:], _PERM_C)


def _prep_vecs(swa_q_norm, swa_k_norm, fox_q_norm, fox_k_norm, fox_f_bias, mlstm_i_bias, mlstm_f_bias):
    L = swa_q_norm.shape[0]

    def swa_lanes(g):
        return jnp.concatenate([g[:, :HALF], g[:, :HALF], g[:, HALF:], g[:, HALF:]], axis=-1)

    pad = lambda v: jnp.pad(v, ((0, 0), (0, MLSTM_W - v.shape[1])))
    return jnp.stack([
        pad(swa_lanes(swa_q_norm)), pad(swa_lanes(swa_k_norm)),
        pad(jnp.tile(fox_q_norm, (1, 2))), pad(jnp.tile(fox_k_norm, (1, 2))),
        pad(jnp.concatenate([jnp.tile(fox_f_bias, (1, 3)), mlstm_i_bias, mlstm_f_bias], axis=-1)),
        jnp.zeros((L, MLSTM_W), F32), jnp.zeros((L, MLSTM_W), F32), jnp.zeros((L, MLSTM_W), F32),
    ], axis=1).astype(F32)


def kernel(x, norm1, w_in, swa_q_norm, swa_k_norm, swa_sinks, fox_q_norm, fox_k_norm, fox_f_bias,
           mlstm_i_bias, mlstm_f_bias, mlstm_out_norm, w_out, norm2, w_ff1, w_ff2):
    B, S, D = x.shape
    L = norm1.shape[0]
    T = B * S
    cos, sin = _rope_tables(S)
    tri256, tri64, bd256 = (jnp.asarray(c, BF16) for c in (_TRI256, _TRI64, _BD256))
    i4 = jnp.asarray(_I4, F32)
    e3 = jnp.asarray(_gate_expand(), BF16)
    eye256 = jnp.eye(MXU_DIM, dtype=BF16)
    band = jnp.asarray(_swa_band_cap(), F32)

    w_a, w_b, w_c = _prep_w_in(w_in)
    wo = w_out.astype(BF16)
    vecs = _prep_vecs(swa_q_norm, swa_k_norm, fox_q_norm, fox_k_norm, fox_f_bias, mlstm_i_bias,
                      mlstm_f_bias)
    n1 = norm1.reshape(L, 1, D).astype(F32)
    n2 = norm2.reshape(L, 1, D).astype(F32)
    onorm = mlstm_out_norm.reshape(L, 1, MLSTM_W).astype(F32)
    sinks = swa_sinks.reshape(L * SWA_HEADS).astype(F32)

    for l in range(L):
        qa, kva, qb, kb, vb, qkvc, gc, w1, w2 = _in_proj(
            l, x, n1, w_a, w_b, w_c, vecs, cos, sin, tri256, e3, w_ff1, w_ff2)
        ya = _swa(l, sinks, qa, kva, eye256, band)
        yb = _fox(qb, kb, vb, eye256)
        yc = _mlstm(l, qkvc, gc, onorm, tri64, i4, bd256)
        x = _out_ffn(l, x.reshape(T, D), ya.reshape(T, SWA_Q), yb.reshape(T, FOX_W),
                     yc.reshape(T, MLSTM_W), wo, n2, w1, w2).reshape(B, S, D)
    return x
```

```python
import functools
import math

import numpy as np
import jax
import jax.numpy as jnp
from jax import lax
from jax.experimental import pallas as pl
from jax.experimental.pallas import tpu as pltpu

F32 = jnp.float32
BF16 = jnp.bfloat16

D_MODEL = 1024
HEAD = 64
HALF = HEAD // 2
CHUNK = 64
SWA_HEADS = 8
SWA_KV_HEADS = 2
FOX_HEADS = 4
MLSTM_HEADS = 4
SWA_Q = SWA_HEADS * HEAD
SWA_KV = SWA_KV_HEADS * HEAD
FOX_W = FOX_HEADS * HEAD
MLSTM_W = MLSTM_HEADS * HEAD
D_FF = 4 * D_MODEL
QBLK = 128
ROPE_THETA = 10000.0
EPS = 1e-6
QK_SCALE = 1.0 / math.sqrt(HEAD)
LOG2E = math.log2(math.e)

LANES = 128
MXU_DIM = 256
VMEM_LIMIT = 56 * 1024 * 1024

W_A = SWA_Q + 2 * SWA_KV
W_B = 3 * FOX_W
SEC_C = 0
SEC_CO = SEC_C + 768
SEC_G = SEC_CO + 256
W_C = SEC_G + LANES
GATE_BF = 0
GATE_CI = 12
GATE_CF = 16
IN_WIDTH = W_A + W_B + FOX_HEADS + 4 * MLSTM_W + 2 * MLSTM_HEADS


def _perm_a():
    p = np.zeros((W_A, W_A), np.float32)
    lane = np.arange(LANES)
    quarter, i = lane // HALF, lane % HALF
    dim, is_b = i + HALF * (quarter // 2), quarter % 2
    for j in range(SWA_HEADS // 2):
        p[(j + 4 * is_b) * HEAD + dim, j * LANES + lane] = 1.0
    p[SWA_Q + is_b * HEAD + dim, SWA_Q + lane] = 1.0
    p[SWA_Q + SWA_KV + lane, SWA_Q + SWA_KV + lane] = 1.0
    return p


def _perm_c():
    src = IN_WIDTH - W_A - W_B
    o_bf, o_cqkv = 0, FOX_HEADS
    o_ci = o_cqkv + 3 * MLSTM_W
    o_cf, o_co = o_ci + MLSTM_HEADS, o_ci + 2 * MLSTM_HEADS
    p = np.zeros((src, W_C), np.float32)
    w = np.arange(3 * MLSTM_W)
    p[o_cqkv + w, SEC_C + w] = 1.0
    w = np.arange(MLSTM_W)
    p[o_co + w, SEC_CO + w] = 1.0
    h = np.arange(FOX_HEADS)
    for rep in range(3):
        p[o_bf + h, SEC_G + GATE_BF + rep * FOX_HEADS + h] = 1.0
    h = np.arange(MLSTM_HEADS)
    p[o_ci + h, SEC_G + GATE_CI + h] = 1.0
    p[o_cf + h, SEC_G + GATE_CF + h] = 1.0
    return p


def _gate_expand():
    e = np.zeros((LANES, 2 * MLSTM_W), np.float32)
    for h in range(MLSTM_HEADS):
        e[GATE_CI + h, h * HEAD:(h + 1) * HEAD] = 1.0
        e[GATE_CF + h, MLSTM_W + h * HEAD:MLSTM_W + (h + 1) * HEAD] = 1.0
    return np.concatenate([e, e, e], axis=0)


_PERM_A = _perm_a()
_PERM_C = _perm_c()


def _block_diag_ones(head_of_lane):
    return (head_of_lane[:, None] == head_of_lane[None, :]).astype(np.float32)


_TRI256 = np.tril(np.ones((MXU_DIM, MXU_DIM), np.float32))
_TRI64 = np.tril(np.ones((CHUNK, CHUNK), np.float32))
_w256 = np.arange(MLSTM_W)
_BD256 = _block_diag_ones(_w256 // HEAD)
_I4 = (np.arange(CHUNK)[:, None] == (_w256 % HEAD)[None, :]).astype(np.float32)


def _swa_band_cap():
    key_chunk = np.arange(2 * QBLK)[:, None] // CHUNK
    qry_chunk = (np.arange(MXU_DIM)[None, :] % QBLK) // CHUNK
    ahead = key_chunk - qry_chunk
    return np.where((ahead >= 0) & (ahead <= 2), np.inf, -np.inf).astype(np.float32)


def _const_spec(shape):
    nd = len(shape)
    return pl.BlockSpec(shape, lambda *_: (0,) * nd, pipeline_mode=pl.Buffered(1))


def _layer_spec(shape, layer):
    nd = len(shape)
    return pl.BlockSpec((None,) + tuple(shape), lambda *_: (layer,) + (0,) * nd,
                        pipeline_mode=pl.Buffered(1))


def _log_sigmoid(x):
    return jnp.minimum(x, 0.0) - jnp.log(1.0 + jnp.exp(-jnp.abs(x)))


def _split2(x):
    hi = x.astype(BF16)
    lo = (x - hi.astype(F32)).astype(BF16)
    return hi, lo


def _split3_by_lane(x, lane):
    hi = x.astype(BF16).astype(F32)
    r1 = x - hi
    mid = r1.astype(BF16).astype(F32)
    low = r1 - mid
    return jnp.where(lane < 4, hi, jnp.where(lane < 8, mid, low)).astype(BF16)


def _dot(a, b):
    return jnp.dot(a, b, preferred_element_type=F32)


def _dot_nt(a, b):
    return lax.dot_general(a, b, (((1,), (1,)), ((), ())), preferred_element_type=F32)


TM_IN = 1024


def _in_proj_body(x_ref, n1_ref, wa_ref, wb_ref, wc_ref, vec_ref, cos_ref, sin_ref, tri_ref, e3_ref,
                  wff1_ref, wff2_ref,
                  qa_ref, kva_ref, qb_ref, kb_ref, vb_ref, qkvc_ref, gc_ref, wff1_bf_ref, wff2_bf_ref,
                  carry_ref):
    tm = x_ref.shape[1]
    wff1_bf_ref[...] = wff1_ref[...].astype(BF16)
    wff2_bf_ref[...] = wff2_ref[...].astype(BF16)

    @pl.when(pl.program_id(1) == 0)
    def _():
        carry_ref[...] = jnp.zeros_like(carry_ref)

    x = x_ref[0]
    ms = jnp.mean(x * x, axis=-1, keepdims=True)
    h = ((x * lax.rsqrt(ms + EPS)) * n1_ref[...]).astype(BF16)

    def proj(w_ref, c0, width):
        return _dot(h, w_ref[:, c0:c0 + width])

    lane_row = lax.broadcasted_iota(jnp.int32, (1, LANES), 1)
    first_swa = (lane_row // HALF) % 2 == 0
    first_nat = lane_row < HEAD

    def head_norm(z, first, gain):
        sq = z * z
        s_first = jnp.sum(jnp.where(first, sq, 0.0), axis=-1, keepdims=True)
        s_second = jnp.sum(jnp.where(first, 0.0, sq), axis=-1, keepdims=True)
        ss = jnp.where(first, s_first, s_second)
        return (z * lax.rsqrt(ss * (1.0 / HEAD) + EPS)) * gain

    cos = cos_ref[...]
    sin = sin_ref[...]

    def rope(z):
        return z * cos + pltpu.roll(z, HEAD, 1) * sin

    g_aq = vec_ref[0:1, 0:LANES]
    g_ak = vec_ref[1:2, 0:LANES]
    g_bq = vec_ref[2:3, 0:LANES]
    g_bk = vec_ref[3:4, 0:LANES]
    gate_bias = vec_ref[4:5, 0:LANES]

    lane = lax.broadcasted_iota(jnp.int32, (MXU_DIM, LANES), 1)
    zg = proj(wc_ref, SEC_G, LANES) + gate_bias
    log_sig = _log_sigmoid(zg)
    log_f = jnp.where(lane[0:1] < GATE_CI, log_sig, 0.0)
    f_terms = [_split3_by_lane(log_f[blk * MXU_DIM:(blk + 1) * MXU_DIM], lane)
               for blk in range(tm // MXU_DIM)]
    gz = jnp.where(lane[0:1] < GATE_CI, 0.0,
                   jnp.where(lane[0:1] < GATE_CF, zg,
                             jnp.where(lane[0:1] < GATE_CF + MLSTM_HEADS, log_sig, 0.0)))
    gz_hi = gz.astype(BF16)
    gz_r = gz - gz_hi.astype(F32)
    gz_mid = gz_r.astype(BF16)
    gz_terms = jnp.concatenate([gz_hi, gz_mid, (gz_r - gz_mid.astype(F32)).astype(BF16)], axis=1)

    za = proj(wa_ref, 0, W_A)
    for j in range(SWA_HEADS // 2):
        q = rope(head_norm(za[:, j * LANES:(j + 1) * LANES], first_swa, g_aq)) * (QK_SCALE * LOG2E)
        qa_ref[0, :, j * LANES:(j + 1) * LANES] = q.astype(BF16)
    k = rope(head_norm(za[:, SWA_Q:SWA_Q + LANES], first_swa, g_ak))
    kva_ref[0, :, 0:LANES] = k.astype(BF16)
    kva_ref[0, :, LANES:2 * LANES] = za[:, SWA_Q + SWA_KV:].astype(BF16)

    zb = proj(wb_ref, 0, W_B)
    for p in range(FOX_HEADS // 2):
        q = head_norm(zb[:, p * LANES:(p + 1) * LANES], first_nat, g_bq) * (QK_SCALE * LOG2E)
        qb_ref[0, :, p * LANES:(p + 1) * LANES] = q.astype(BF16)
        k = head_norm(zb[:, FOX_W + p * LANES:FOX_W + (p + 1) * LANES], first_nat, g_bk)
        kb_ref[0, :, 2 * p * LANES:(2 * p + 1) * LANES] = k.astype(BF16)
    vb_ref[0] = zb[:, 2 * FOX_W:].astype(BF16)

    zc = proj(wc_ref, SEC_C, 768)
    qkvc_ref[0, :, 0:MLSTM_W] = zc[:, 0:MLSTM_W].astype(BF16)
    qkvc_ref[0, :, MLSTM_W:2 * MLSTM_W] = (zc[:, MLSTM_W:2 * MLSTM_W] * QK_SCALE).astype(BF16)
    qkvc_ref[0, :, 2 * MLSTM_W:] = zc[:, 2 * MLSTM_W:].astype(BF16)
    gc_ref[0, :, 2 * MLSTM_W:] = proj(wc_ref, SEC_CO, MLSTM_W)

    gc_ref[0, :, 0:2 * MLSTM_W] = _dot(gz_terms, e3_ref[...])

    carry = carry_ref[0:1, :]
    for blk in range(tm // MXU_DIM):
        rows = slice(blk * MXU_DIM, (blk + 1) * MXU_DIM)
        cs = _dot(tri_ref[...], f_terms[blk])
        f_loc = cs + pltpu.roll(cs, LANES - 4, 1) + pltpu.roll(cs, LANES - 8, 1)
        f_cum = jnp.where(lane < 4, f_loc, 0.0) + carry
        carry = f_cum[MXU_DIM - 1:MXU_DIM, :]
        f_rep = (f_cum + pltpu.roll(f_cum, 4, 1) + pltpu.roll(f_cum, 8, 1)) * LOG2E
        aug = _split3_by_lane(f_rep, lane)
        kb_ref[0, rows, LANES:2 * LANES] = aug
        kb_ref[0, rows, 3 * LANES:4 * LANES] = aug
    carry_ref[...] = jnp.broadcast_to(carry, carry_ref.shape)


def _in_proj(layer, x, n1, w_a, w_b, w_c, vecs, cos, sin, tri, e3, w_ff1, w_ff2):
    B, S, _ = x.shape
    tm = min(TM_IN, S)
    n_i = S // tm
    grid = (B, n_i)
    steps = B * n_i
    assert D_MODEL % (16 * steps) == 0, "FFN weight slabs must be whole bf16 sublane tiles"
    r1, r2 = D_MODEL // steps, D_FF // steps
    tok = lambda w: pl.BlockSpec((1, tm, w), lambda b, i: (b, i, 0))
    out_shapes = (
        jax.ShapeDtypeStruct((B, S, SWA_Q), BF16),
        jax.ShapeDtypeStruct((B, S, 2 * SWA_KV), BF16),
        jax.ShapeDtypeStruct((B, S, FOX_W), BF16),
        jax.ShapeDtypeStruct((B, S, 2 * MXU_DIM), BF16),
        jax.ShapeDtypeStruct((B, S, FOX_W), BF16),
        jax.ShapeDtypeStruct((B, S, 3 * MLSTM_W), BF16),
        jax.ShapeDtypeStruct((B, S, 3 * MLSTM_W), F32),
        jax.ShapeDtypeStruct((D_MODEL, D_FF), BF16),
        jax.ShapeDtypeStruct((D_FF, D_MODEL), BF16),
    )
    slab_out = [pl.BlockSpec((r1, D_FF), lambda b, i: (b * n_i + i, 0)),
                pl.BlockSpec((r2, D_MODEL), lambda b, i: (b * n_i + i, 0))]
    return pl.pallas_call(
        _in_proj_body,
        grid=grid,
        in_specs=[
            tok(D_MODEL),
            _layer_spec((1, D_MODEL), layer),
            _layer_spec((D_MODEL, W_A), layer),
            _layer_spec((D_MODEL, W_B), layer),
            _layer_spec((D_MODEL, W_C), layer),
            _layer_spec((8, MLSTM_W), layer),
            pl.BlockSpec((tm, LANES), lambda b, i: (i, 0)),
            pl.BlockSpec((tm, LANES), lambda b, i: (i, 0)),
            _const_spec((MXU_DIM, MXU_DIM)),
            _const_spec((3 * LANES, 2 * MLSTM_W)),
            pl.BlockSpec((None, r1, D_FF), lambda b, i: (layer, b * n_i + i, 0)),
            pl.BlockSpec((None, r2, D_MODEL), lambda b, i: (layer, b * n_i + i, 0)),
        ],
        out_specs=[tok(s.shape[-1]) for s in out_shapes[:-2]] + slab_out,
        out_shape=out_shapes,
        scratch_shapes=[pltpu.VMEM((8, LANES), F32)],
        compiler_params=pltpu.CompilerParams(
            dimension_semantics=("arbitrary", "arbitrary"), vmem_limit_bytes=VMEM_LIMIT),
        name="in_proj",
    )(x, n1, w_a, w_b, w_c, vecs, cos, sin, tri, e3, w_ff1, w_ff2)


TS_SWA = 1024
SWA_DEPTH = 8
VT_ROWS = HEAD + 16


def _swa_body(sink_base, sink_ref, q_ref, kvc_ref, kvp_ref, eye_ref, band_ref, o_ref):
    step = pl.program_id(1)
    nblk = q_ref.shape[1] // QBLK
    per_group = SWA_HEADS // SWA_KV_HEADS
    eye = eye_ref[0:LANES, 0:LANES]

    band_cap = band_ref[...]
    key = lax.broadcasted_iota(jnp.int32, (2 * QBLK, MXU_DIM), 0)
    first_cap = jnp.where((key >= QBLK) | (step > 0), band_cap, -jnp.inf)
    lane = lax.broadcasted_iota(jnp.int32, (QBLK, LANES), 1)
    lane_a = (lane // HALF) % 2 == 0
    first_head = lax.broadcasted_iota(jnp.int32, (1, MXU_DIM), 1) < QBLK
    zero = jnp.zeros((QBLK, LANES), BF16)
    ones = jnp.ones((VT_ROWS - HEAD, 2 * QBLK), BF16)
    kv_all = jnp.concatenate([kvp_ref[0], kvc_ref[0]], axis=0)
    rows = [slice(blk * QBLK, (blk + 1) * QBLK) for blk in range(nblk)]
    bands = [kv_all[blk * QBLK:(blk + 2) * QBLK] for blk in range(nblk)]

    vt_aug = []
    for blk in range(nblk):
        v_t = _dot_nt(eye, bands[blk][:, LANES:]).astype(BF16)
        vt_aug.append([jnp.concatenate([v_t[g * HEAD:(g + 1) * HEAD], ones], axis=0)
                       for g in range(SWA_KV_HEADS)])

    units = [(blk, g, part) for blk in range(nblk) for g in range(SWA_KV_HEADS) for part in range(2)]

    def score(blk, g, part):
        keep = lane_a if g == 0 else jnp.logical_not(lane_a)
        q_cols = [q_ref[0, rows[blk], j * LANES:(j + 1) * LANES] for j in (2 * part, 2 * part + 1)]
        q_g = jnp.concatenate([jnp.where(keep, q2, zero) for q2 in q_cols], axis=0)
        return _dot_nt(bands[blk][:, 0:LANES], q_g)

    def fold(s, blk, g, part):
        s = jnp.minimum(s, band_cap if blk > 0 else first_cap)
        head = sink_base + g * per_group + 2 * part
        sink = jnp.where(first_head, sink_ref[head], sink_ref[head + 1]) * LOG2E
        m = jnp.maximum(jnp.max(s, axis=0, keepdims=True), sink)
        p = jnp.exp2(s - m)
        o_aug = _dot(vt_aug[blk][g], p.astype(BF16))
        den = o_aug[HEAD:HEAD + 1] + jnp.exp2(sink - m)
        return (o_aug[0:HEAD] / den).astype(BF16)

    def write_block(blk, outs):
        for g in range(SWA_KV_HEADS):
            for part in range(2):
                o = outs[(blk, g, part)]
                pair_t = jnp.concatenate([o[:, 0:QBLK], o[:, QBLK:]], axis=0)
                j = 2 * g + part
                o_ref[0, rows[blk], j * LANES:(j + 1) * LANES] = _dot_nt(eye, pair_t).astype(BF16)

    outs = {}
    queue = [score(*unit) for unit in units[:SWA_DEPTH]]
    for n, unit in enumerate(units):
        s_cur = queue.pop(0)
        if n + SWA_DEPTH < len(units):
            queue.append(score(*units[n + SWA_DEPTH]))
        outs[unit] = fold(s_cur, *unit)
        blk = unit[0]
        if unit[1:] == (SWA_KV_HEADS - 1, 1) and blk > 0:
            write_block(blk - 1, outs)
    write_block(nblk - 1, outs)


def _swa(layer, sinks, qa, kva, eye, band):
    B, S, _ = qa.shape
    ts = min(TS_SWA, S)
    per_step = ts // QBLK
    return pl.pallas_call(
        functools.partial(_swa_body, layer * SWA_HEADS),
        grid=(B, S // ts),
        in_specs=[
            pl.BlockSpec(memory_space=pltpu.SMEM),
            pl.BlockSpec((1, ts, SWA_Q), lambda b, i: (b, i, 0)),
            pl.BlockSpec((1, ts, 2 * SWA_KV), lambda b, i: (b, i, 0)),
            pl.BlockSpec((1, QBLK, 2 * SWA_KV), lambda b, i: (b, jnp.maximum(i * per_step - 1, 0), 0)),
            _const_spec((MXU_DIM, MXU_DIM)),
            _const_spec((2 * QBLK, MXU_DIM)),
        ],
        out_specs=pl.BlockSpec((1, ts, SWA_Q), lambda b, i: (b, i, 0)),
        out_shape=jax.ShapeDtypeStruct((B, S, SWA_Q), BF16),
        compiler_params=pltpu.CompilerParams(
            dimension_semantics=("arbitrary", "arbitrary"), vmem_limit_bytes=VMEM_LIMIT),
        name="swa",
    )(sinks, qa, kva, kva, eye, band)


TQ_FOX = 1024
TK_FOX = 256
NEG_BIG = -1e30


FOX_UNROLL = 4
FOX_DEPTH = 8


def _fox_body(q_ref, k_ref, v_ref, eye_ref, o_ref, vt_ref, qt_ref, acc_ref):
    tq = q_ref.shape[1]
    tk = TK_FOX
    S = k_ref.shape[1]
    pair = pl.program_id(1)
    qi = pl.program_id(2)
    eye = eye_ref[...]
    n_tiles = 2 * tq // MXU_DIM
    tiles_per_head = tq // MXU_DIM

    @pl.when(qi == 0)
    def _():
        ones = jnp.ones((VT_ROWS - HEAD, tk), BF16)
        for j in range(S // tk):
            blk = slice(j * tk, (j + 1) * tk)
            v_t = _dot_nt(eye[0:LANES, 0:LANES], v_ref[0, blk, :]).astype(BF16)
            vt_ref[:, blk] = jnp.concatenate([v_t[0:HEAD], ones, v_t[HEAD:], ones], axis=0)

    lane = lax.broadcasted_iota(jnp.int32, (tq, LANES), 1)
    lane_a = lane < HEAD
    q2 = q_ref[0]
    zero = jnp.zeros_like(q2)

    def minus_one_at(head):
        hit = (lane == head) | (lane == head + 4) | (lane == head + 8)
        return jnp.where(hit, -1.0, 0.0).astype(BF16)

    q_aug = [jnp.concatenate([jnp.where(lane_a, q2, zero), minus_one_at(2 * pair)], axis=1),
             jnp.concatenate([jnp.where(lane_a, zero, q2), minus_one_at(2 * pair + 1)], axis=1)]
    for t in range(n_tiles):
        hd, part = divmod(t, tiles_per_head)
        rows = q_aug[hd][part * MXU_DIM:(part + 1) * MXU_DIM]
        qt_ref[:, t * MXU_DIM:(t + 1) * MXU_DIM] = _dot_nt(eye, rows).astype(BF16)
    acc_ref[...] = jnp.zeros_like(acc_ref)

    key = lax.broadcasted_iota(jnp.int32, (tk // 2, MXU_DIM), 0)
    qry = lax.broadcasted_iota(jnp.int32, (tk // 2, MXU_DIM), 1)

    def fold(s, t, vt_blk, m, triangular):
        hd = t // tiles_per_head
        vt_h = vt_blk[hd * VT_ROWS:(hd + 1) * VT_ROWS]
        half = tk // 2
        for h in range(2):
            sh = s[h * half:(h + 1) * half]
            if triangular:
                valid = key + h * half <= qry
                sh = jnp.where(valid, sh, NEG_BIG)
            m_new = jnp.maximum(m, jnp.max(sh, axis=0, keepdims=True))
            alpha = jnp.exp2(m - m_new)
            p = jnp.exp2(sh - m_new)
            if triangular:
                p = jnp.where(valid, p, 0.0)
            acc_ref[t] = alpha * acc_ref[t] + _dot(vt_h[:, h * half:(h + 1) * half], p.astype(BF16))
            m = m_new
        return m

    def region(j0, ms, plans):
        k_blks, vt_blks, todo = [], [], []
        for b, plan in enumerate(plans):
            r0 = pl.multiple_of((j0 + b) * tk, tk)
            k_blks.append(k_ref[0, pl.ds(r0, tk), :])
            vt_blks.append(vt_ref[:, pl.ds(r0, tk)])
            todo += [(b, t) for t in range(n_tiles) if plan[t] is not None]
        score = lambda b, t: _dot(k_blks[b], qt_ref[:, t * MXU_DIM:(t + 1) * MXU_DIM])
        ms = list(ms)
        queue = [score(*unit) for unit in todo[:FOX_DEPTH]]
        for n, (b, t) in enumerate(todo):
            s_cur = queue.pop(0)
            if n + FOX_DEPTH < len(todo):
                queue.append(score(*todo[n + FOX_DEPTH]))
            ms[t] = fold(s_cur, t, vt_blks[b], ms[t], plans[b][t])
        return tuple(ms)

    blocks_per_tile = tq // tk
    full = (False,) * n_tiles
    n_full = blocks_per_tile * qi
    m_init = jnp.full((1, MXU_DIM), NEG_BIG, F32)
    ms = lax.fori_loop(0, n_full // FOX_UNROLL,
                       lambda g, c: region(g * FOX_UNROLL, c, (full,) * FOX_UNROLL), (m_init,) * n_tiles)
    if FOX_UNROLL > blocks_per_tile:
        done = (n_full // FOX_UNROLL) * FOX_UNROLL
        ms = lax.fori_loop(0, (n_full - done) // blocks_per_tile,
                           lambda g, c: region(done + g * blocks_per_tile, c, (full,) * blocks_per_tile), ms)
    diag = [tuple(None if t % tiles_per_head < d else t % tiles_per_head == d for t in range(n_tiles))
            for d in range(blocks_per_tile)]
    ms = region(n_full, ms, diag)

    for hd in range(2):
        cols = []
        for part in range(tiles_per_head):
            acc = acc_ref[hd * tiles_per_head + part]
            cols.append(acc[0:HEAD] / acc[HEAD:HEAD + 1])
        o_h = jnp.concatenate(cols, axis=1)
        if hd == 0:
            o_t = o_h
        else:
            o_t = jnp.concatenate([o_t, o_h], axis=0).astype(BF16)
    for c in range(tq // MXU_DIM):
        rows = slice(c * MXU_DIM, (c + 1) * MXU_DIM)
        o_ref[0, rows, :] = _dot_nt(eye, o_t[:, rows]).astype(BF16)


def _fox(qb, kb, vb, eye):
    B, S, _ = qb.shape
    tq = min(TQ_FOX, S)
    return pl.pallas_call(
        _fox_body,
        grid=(B, FOX_HEADS // 2, S // tq),
        in_specs=[
            pl.BlockSpec((1, tq, LANES), lambda b, p, i: (b, i, p)),
            pl.BlockSpec((1, S, MXU_DIM), lambda b, p, i: (b, 0, p)),
            pl.BlockSpec((1, S, LANES), lambda b, p, i: (b, 0, p)),
            _const_spec((MXU_DIM, MXU_DIM)),
        ],
        out_specs=pl.BlockSpec((1, tq, LANES), lambda b, p, i: (b, i, p)),
        out_shape=jax.ShapeDtypeStruct((B, S, FOX_W), BF16),
        scratch_shapes=[
            pltpu.VMEM((2 * VT_ROWS, S), BF16),
            pltpu.VMEM((MXU_DIM, 2 * tq), BF16),
            pltpu.VMEM((2 * tq // MXU_DIM, VT_ROWS, MXU_DIM), F32),
        ],
        compiler_params=pltpu.CompilerParams(
            dimension_semantics=("arbitrary", "arbitrary", "arbitrary"),
            vmem_limit_bytes=VMEM_LIMIT),
        name="fox",
    )(qb, kb, vb, eye)


TS_MLSTM = 1024


def _mlstm_body(qkv_ref, g_ref, onorm_ref, tri_ref, i4_ref, bd_ref, o_ref, c_ref, n_ref, m_ref):
    ts = qkv_ref.shape[1]
    W = MLSTM_W

    @pl.when(pl.program_id(1) == 0)
    def _():
        c_ref[...] = jnp.zeros_like(c_ref)
        n_ref[...] = jnp.zeros_like(n_ref)
        m_ref[...] = jnp.zeros_like(m_ref)

    row = lax.broadcasted_iota(jnp.int32, (CHUNK, W), 0)
    lane = lax.broadcasted_iota(jnp.int32, (CHUNK, W), 1)
    causal = (lane % HEAD) <= row
    lane_head = lane // HEAD
    bd = bd_ref[...]
    bd2 = jnp.concatenate([bd, bd], axis=1)
    i4_bf = i4_ref[...].astype(BF16)

    def block_diag(x):
        tiled = jnp.concatenate([x] * MLSTM_HEADS, axis=0)
        return tiled * (bd if x.shape[1] == W else bd2)

    def seg_max(d):
        out = jnp.zeros_like(d)
        for hd in range(MLSTM_HEADS):
            sel = lane_head == hd
            mx = jnp.max(jnp.where(sel, d, -jnp.inf), axis=-1, keepdims=True)
            out = jnp.where(sel, mx, out)
        return out

    chunks = range(ts // CHUNK)
    rows = [pl.ds(c * CHUNK, CHUNK) for c in chunks]

    b, g, u, dm, dm_max, a_max, wa = [], [], [], [], [], [], []
    for c in chunks:
        ig = g_ref[0, rows[c], 0:W]
        lf_hi, lf_lo = _split2(g_ref[0, rows[c], W:2 * W])
        b_c = _dot(tri_ref[...], lf_hi) + _dot(tri_ref[...], lf_lo)
        g_c = b_c[CHUNK - 1:CHUNK, :]
        u_c = b_c - ig
        u_row = jnp.sum(u_c * i4_ref[...], axis=0, keepdims=True)
        dm_c = jnp.where(causal, b_c - u_row, -jnp.inf)
        a_c = g_c - u_c
        am_c = jnp.max(a_c, axis=0, keepdims=True)
        b.append(b_c); g.append(g_c); u.append(u_c); dm.append(dm_c)
        dm_max.append(seg_max(dm_c)); a_max.append(am_c); wa.append(jnp.exp(a_c - am_c))

    m_prev, s_old, s_loc = [m_ref[0:1, :]], [], []
    for c in chunks:
        gm = g[c] + m_prev[c]
        m_new = jnp.maximum(gm, a_max[c])
        s_old.append(jnp.exp(gm - m_new))
        s_loc.append(jnp.exp(a_max[c] - m_new))
        m_prev.append(m_new)

    m_t, w_inter, sm_hi, sm_lo, k_t = [], [], [], [], []
    for c in chunks:
        inter = b[c] + m_prev[c]
        mt_c = jnp.maximum(inter, dm_max[c])
        decay = jnp.exp(dm[c] - mt_c)
        q = qkv_ref[0, rows[c], 0:W]
        k = qkv_ref[0, rows[c], W:2 * W]
        qk = _dot_nt(jnp.concatenate([q, i4_bf], axis=0), block_diag(k))
        hi, lo = _split2(qk[0:CHUNK] * decay)
        m_t.append(mt_c); w_inter.append(jnp.exp(inter - mt_c)); sm_hi.append(hi); sm_lo.append(lo)
        k_t.append(qk[CHUNK:].astype(BF16))

    c_cur, n_cur = c_ref[...], n_ref[...]
    state_bf = []
    for c in chunks:
        state_bf.append(jnp.concatenate([c_cur.astype(BF16), n_cur.astype(BF16)], axis=1))
        v = qkv_ref[0, rows[c], 2 * W:]
        weighted = jnp.concatenate([(wa[c] * v.astype(F32)).astype(BF16), wa[c].astype(BF16)], axis=1)
        loc = _dot(k_t[c], block_diag(weighted))
        c_cur = s_old[c] * c_cur + s_loc[c] * loc[:, 0:W]
        n_cur = s_old[c] * n_cur + s_loc[c] * loc[:, W:]
    c_ref[...] = c_cur
    n_ref[...] = n_cur
    m_ref[...] = jnp.broadcast_to(m_prev[-1], m_ref.shape)

    hval = []
    for c in chunks:
        q = qkv_ref[0, rows[c], 0:W]
        v = qkv_ref[0, rows[c], 2 * W:]
        inter_cn = _dot(q, block_diag(state_bf[c]))
        intra = _dot(sm_hi[c], jnp.concatenate([block_diag(v), bd], axis=1))
        num = w_inter[c] * inter_cn[:, 0:W] + intra[:, 0:W]
        den = w_inter[c] * inter_cn[:, W:] + intra[:, W:] + _dot(sm_lo[c], bd)
        hval.append(num / jnp.maximum(jnp.abs(den), jnp.exp(-m_t[c])))

    for c in chunks:
        hs_hi, hs_lo = _split2(hval[c] * hval[c])
        ss = _dot(hs_hi, bd) + _dot(hs_lo, bd)
        hc = (hval[c] * lax.rsqrt(ss * (1.0 / HEAD) + EPS)) * onorm_ref[...]
        co = g_ref[0, rows[c], 2 * W:]
        o_ref[0, rows[c], :] = (jax.nn.sigmoid(co) * hc).astype(BF16)


def _mlstm(layer, qkvc, gc, onorm, tri, i4, bd):
    B, S, _ = qkvc.shape
    ts = min(TS_MLSTM, S)
    return pl.pallas_call(
        _mlstm_body,
        grid=(B, S // ts),
        in_specs=[
            pl.BlockSpec((1, ts, 3 * MLSTM_W), lambda b, i: (b, i, 0)),
            pl.BlockSpec((1, ts, 3 * MLSTM_W), lambda b, i: (b, i, 0)),
            _layer_spec((1, MLSTM_W), layer),
            _const_spec((CHUNK, CHUNK)),
            _const_spec((CHUNK, MLSTM_W)),
            _const_spec((MLSTM_W, MLSTM_W)),
        ],
        out_specs=pl.BlockSpec((1, ts, MLSTM_W), lambda b, i: (b, i, 0)),
        out_shape=jax.ShapeDtypeStruct((B, S, MLSTM_W), BF16),
        scratch_shapes=[pltpu.VMEM((HEAD, MLSTM_W), F32), pltpu.VMEM((HEAD, MLSTM_W), F32),
                        pltpu.VMEM((8, MLSTM_W), F32)],
        compiler_params=pltpu.CompilerParams(
            dimension_semantics=("arbitrary", "arbitrary"), vmem_limit_bytes=VMEM_LIMIT),
        name="mlstm",
    )(qkvc, gc, onorm, tri, i4, bd)


TM_FFN = 1024
FF_CHUNK = 1024


def _out_ffn_body(x_ref, ya_ref, yb_ref, yc_ref, wo_ref, n2_ref, w1_ref, w2_ref, o_ref, a_ref):
    x1 = (x_ref[...] + _dot(ya_ref[...], wo_ref[0:SWA_Q, :])
          + _dot(yb_ref[...], wo_ref[SWA_Q:SWA_Q + FOX_W, :])
          + _dot(yc_ref[...], wo_ref[SWA_Q + FOX_W:, :]))
    ms = jnp.mean(x1 * x1, axis=-1, keepdims=True)
    h2 = ((x1 * lax.rsqrt(ms + EPS)) * n2_ref[...]).astype(BF16)
    for j in range(D_FF // FF_CHUNK):
        cols = slice(j * FF_CHUNK, (j + 1) * FF_CHUNK)
        r = jnp.maximum(_dot(h2, w1_ref[:, cols]), 0.0)
        a_ref[:, cols] = (r * r).astype(BF16)
    o_ref[...] = x1 + _dot(a_ref[...], w2_ref[...])


def _out_ffn(layer, x2d, ya, yb, yc, wo, n2, w1, w2):
    T = x2d.shape[0]
    tm = min(TM_FFN, T)
    tok = lambda w: pl.BlockSpec((tm, w), lambda i: (i, 0))
    return pl.pallas_call(
        _out_ffn_body,
        grid=(T // tm,),
        in_specs=[
            tok(D_MODEL), tok(SWA_Q), tok(FOX_W), tok(MLSTM_W),
            _layer_spec((D_MODEL, D_MODEL), layer),
            _layer_spec((1, D_MODEL), layer),
            _const_spec((D_MODEL, D_FF)),
            _const_spec((D_FF, D_MODEL)),
        ],
        out_specs=tok(D_MODEL),
        out_shape=jax.ShapeDtypeStruct((T, D_MODEL), F32),
        scratch_shapes=[pltpu.VMEM((tm, D_FF), BF16)],
        compiler_params=pltpu.CompilerParams(
            dimension_semantics=("arbitrary",), vmem_limit_bytes=VMEM_LIMIT),
        name="out_ffn",
    )(x2d, ya, yb, yc, wo, n2, w1, w2)


def _rope_tables(S):
    inv = ROPE_THETA ** (-np.arange(HALF, dtype=np.float64) / HALF)
    ang = np.arange(S, dtype=np.float64)[:, None] * inv[None, :]
    cos = np.tile(np.cos(ang), (1, LANES // HALF))
    sin = np.tile(np.sin(ang), (1, LANES // HALF))
    sign = np.where(np.arange(LANES) < HEAD, -1.0, 1.0)
    return jnp.asarray(cos, F32), jnp.asarray(sin * sign[None, :], F32)


def _prep_w_in(w_in):
    select = lambda w, p: jnp.einsum("ldk,kn->ldn", w.astype(BF16), jnp.asarray(p, BF16),
                                     preferred_element_type=BF16)
    w_b = w_in[:, :, W_A:W_A + W_B].astype(BF16)
    return select(w_in[:, :, :W_A], _PERM_A), w_b, select(w_in[:, :, W_A + W_B:], _PERM_C)


def _prep_vecs(swa_q_norm, swa_k_norm, fox_q_norm, fox_k_norm, fox_f_bias, mlstm_i_bias, mlstm_f_bias):
    L = swa_q_norm.shape[0]

    def swa_lanes(g):
        return jnp.concatenate([g[:, :HALF], g[:, :HALF], g[:, HALF:], g[:, HALF:]], axis=-1)

    pad = lambda v: jnp.pad(v, ((0, 0), (0, MLSTM_W - v.shape[1])))
    return jnp.stack([
        pad(swa_lanes(swa_q_norm)), pad(swa_lanes(swa_k_norm)),
        pad(jnp.tile(fox_q_norm, (1, 2))), pad(jnp.tile(fox_k_norm, (1, 2))),
        pad(jnp.concatenate([jnp.tile(fox_f_bias, (1, 3)), mlstm_i_bias, mlstm_f_bias], axis=-1)),
        jnp.zeros((L, MLSTM_W), F32), jnp.zeros((L, MLSTM_W), F32), jnp.zeros((L, MLSTM_W), F32),
    ], axis=1).astype(F32)


def kernel(x, norm1, w_in, swa_q_norm, swa_k_norm, swa_sinks, fox_q_norm, fox_k_norm, fox_f_bias,
           mlstm_i_bias, mlstm_f_bias, mlstm_out_norm, w_out, norm2, w_ff1, w_ff2):
    B, S, D = x.shape
    L = norm1.shape[0]
    T = B * S
    cos, sin = _rope_tables(S)
    tri256, tri64, bd256 = (jnp.asarray(c, BF16) for c in (_TRI256, _TRI64, _BD256))
    i4 = jnp.asarray(_I4, F32)
    e3 = jnp.asarray(_gate_expand(), BF16)
    eye256 = jnp.eye(MXU_DIM, dtype=BF16)
    band = jnp.asarray(_swa_band_cap(), F32)

    w_a, w_b, w_c = _prep_w_in(w_in)
    wo = w_out.astype(BF16)
    vecs = _prep_vecs(swa_q_norm, swa_k_norm, fox_q_norm, fox_k_norm, fox_f_bias, mlstm_i_bias,
                      mlstm_f_bias)
    n1 = norm1.reshape(L, 1, D).astype(F32)
    n2 = norm2.reshape(L, 1, D).astype(F32)
    onorm = mlstm_out_norm.reshape(L, 1, MLSTM_W).astype(F32)
    sinks = swa_sinks.reshape(L * SWA_HEADS).astype(F32)

    for l in range(L):
        qa, kva, qb, kb, vb, qkvc, gc, w1, w2 = _in_proj(
            l, x, n1, w_a, w_b, w_c, vecs, cos, sin, tri256, e3, w_ff1, w_ff2)
        ya = _swa(l, sinks, qa, kva, eye256, band)
        yb = _fox(qb, kb, vb, eye256)
        yc = _mlstm(l, qkvc, gc, onorm, tri64, i4, bd256)
        x = _out_ffn(l, x.reshape(T, D), ya.reshape(T, SWA_Q), yb.reshape(T, FOX_W),
                     yc.reshape(T, MLSTM_W), wo, n2, w1, w2).reshape(B, S, D)
    return x
```

```python
import functools
import math

import numpy as np
import jax
import jax.numpy as jnp
from jax import lax
from jax.experimental import pallas as pl
from jax.experimental.pallas import tpu as pltpu

F32 = jnp.float32
BF16 = jnp.bfloat16

D_MODEL = 1024
HEAD = 64
HALF = HEAD // 2
CHUNK = 64
SWA_HEADS = 8
SWA_KV_HEADS = 2
FOX_HEADS = 4
MLSTM_HEADS = 4
SWA_Q = SWA_HEADS * HEAD
SWA_KV = SWA_KV_HEADS * HEAD
FOX_W = FOX_HEADS * HEAD
MLSTM_W = MLSTM_HEADS * HEAD
D_FF = 4 * D_MODEL
QBLK = 128
ROPE_THETA = 10000.0
EPS = 1e-6
QK_SCALE = 1.0 / math.sqrt(HEAD)
LOG2E = math.log2(math.e)

LANES = 128
MXU_DIM = 256
VMEM_LIMIT = 56 * 1024 * 1024

W_A = SWA_Q + 2 * SWA_KV
W_B = 3 * FOX_W
SEC_C = 0
SEC_CO = SEC_C + 768
SEC_G = SEC_CO + 256
W_C = SEC_G + LANES
GATE_BF = 0
GATE_CI = 12
GATE_CF = 16
IN_WIDTH = W_A + W_B + FOX_HEADS + 4 * MLSTM_W + 2 * MLSTM_HEADS


def _perm_a():
    p = np.zeros((W_A, W_A), np.float32)
    lane = np.arange(LANES)
    quarter, i = lane // HALF, lane % HALF
    dim, is_b = i + HALF * (quarter // 2), quarter % 2
    for j in range(SWA_HEADS // 2):
        p[(j + 4 * is_b) * HEAD + dim, j * LANES + lane] = 1.0
    p[SWA_Q + is_b * HEAD + dim, SWA_Q + lane] = 1.0
    p[SWA_Q + SWA_KV + lane, SWA_Q + SWA_KV + lane] = 1.0
    return p


def _perm_c():
    src = IN_WIDTH - W_A - W_B
    o_bf, o_cqkv = 0, FOX_HEADS
    o_ci = o_cqkv + 3 * MLSTM_W
    o_cf, o_co = o_ci + MLSTM_HEADS, o_ci + 2 * MLSTM_HEADS
    p = np.zeros((src, W_C), np.float32)
    w = np.arange(3 * MLSTM_W)
    p[o_cqkv + w, SEC_C + w] = 1.0
    w = np.arange(MLSTM_W)
    p[o_co + w, SEC_CO + w] = 1.0
    h = np.arange(FOX_HEADS)
    for rep in range(3):
        p[o_bf + h, SEC_G + GATE_BF + rep * FOX_HEADS + h] = 1.0
    h = np.arange(MLSTM_HEADS)
    p[o_ci + h, SEC_G + GATE_CI + h] = 1.0
    p[o_cf + h, SEC_G + GATE_CF + h] = 1.0
    return p


def _gate_expand():
    e = np.zeros((LANES, 2 * MLSTM_W), np.float32)
    for h in range(MLSTM_HEADS):
        e[GATE_CI + h, h * HEAD:(h + 1) * HEAD] = 1.0
        e[GATE_CF + h, MLSTM_W + h * HEAD:MLSTM_W + (h + 1) * HEAD] = 1.0
    return np.concatenate([e, e, e], axis=0)


_PERM_A = _perm_a()
_PERM_C = _perm_c()


def _block_diag_ones(head_of_lane):
    return (head_of_lane[:, None] == head_of_lane[None, :]).astype(np.float32)


_TRI256 = np.tril(np.ones((MXU_DIM, MXU_DIM), np.float32))
_TRI64 = np.tril(np.ones((CHUNK, CHUNK), np.float32))
_w256 = np.arange(MLSTM_W)
_BD256 = _block_diag_ones(_w256 // HEAD)
_I4 = (np.arange(CHUNK)[:, None] == (_w256 % HEAD)[None, :]).astype(np.float32)


def _swa_band_cap():
    key_chunk = np.arange(2 * QBLK)[:, None] // CHUNK
    qry_chunk = (np.arange(MXU_DIM)[None, :] % QBLK) // CHUNK
    ahead = key_chunk - qry_chunk
    return np.where((ahead >= 0) & (ahead <= 2), np.inf, -np.inf).astype(np.float32)


def _const_spec(shape):
    nd = len(shape)
    return pl.BlockSpec(shape, lambda *_: (0,) * nd, pipeline_mode=pl.Buffered(1))


def _layer_spec(shape, layer):
    nd = len(shape)
    return pl.BlockSpec((None,) + tuple(shape), lambda *_: (layer,) + (0,) * nd,
                        pipeline_mode=pl.Buffered(1))


def _log_sigmoid(x):
    return jnp.minimum(x, 0.0) - jnp.log(1.0 + jnp.exp(-jnp.abs(x)))


def _split2(x):
    hi = x.astype(BF16)
    lo = (x - hi.astype(F32)).astype(BF16)
    return hi, lo


def _split3_by_lane(x, lane):
    hi = x.astype(BF16).astype(F32)
    r1 = x - hi
    mid = r1.astype(BF16).astype(F32)
    low = r1 - mid
    return jnp.where(lane < 4, hi, jnp.where(lane < 8, mid, low)).astype(BF16)


def _dot(a, b):
    return jnp.dot(a, b, preferred_element_type=F32)


def _dot_nt(a, b):
    return lax.dot_general(a, b, (((1,), (1,)), ((), ())), preferred_element_type=F32)


TM_IN = 1024


def _in_proj_body(x_ref, n1_ref, wa_ref, wb_ref, wc_ref, vec_ref, cos_ref, sin_ref, tri_ref, e3_ref,
                  wff1_ref, wff2_ref,
                  qa_ref, kva_ref, qb_ref, kb_ref, vb_ref, qkvc_ref, gc_ref, wff1_bf_ref, wff2_bf_ref,
                  carry_ref):
    tm = x_ref.shape[1]
    wff1_bf_ref[...] = wff1_ref[...].astype(BF16)
    wff2_bf_ref[...] = wff2_ref[...].astype(BF16)

    @pl.when(pl.program_id(1) == 0)
    def _():
        carry_ref[...] = jnp.zeros_like(carry_ref)

    x = x_ref[0]
    ms = jnp.mean(x * x, axis=-1, keepdims=True)
    h = ((x * lax.rsqrt(ms + EPS)) * n1_ref[...]).astype(BF16)

    def proj(w_ref, c0, width):
        return _dot(h, w_ref[:, c0:c0 + width])

    lane_row = lax.broadcasted_iota(jnp.int32, (1, LANES), 1)
    first_swa = (lane_row // HALF) % 2 == 0
    first_nat = lane_row < HEAD

    def head_norm(z, first, gain):
        sq = z * z
        s_first = jnp.sum(jnp.where(first, sq, 0.0), axis=-1, keepdims=True)
        s_second = jnp.sum(jnp.where(first, 0.0, sq), axis=-1, keepdims=True)
        ss = jnp.where(first, s_first, s_second)
        return (z * lax.rsqrt(ss * (1.0 / HEAD) + EPS)) * gain

    cos = cos_ref[...]
    sin = sin_ref[...]

    def rope(z):
        return z * cos + pltpu.roll(z, HEAD, 1) * sin

    g_aq = vec_ref[0:1, 0:LANES]
    g_ak = vec_ref[1:2, 0:LANES]
    g_bq = vec_ref[2:3, 0:LANES]
    g_bk = vec_ref[3:4, 0:LANES]
    gate_bias = vec_ref[4:5, 0:LANES]

    lane = lax.broadcasted_iota(jnp.int32, (MXU_DIM, LANES), 1)
    zg = proj(wc_ref, SEC_G, LANES) + gate_bias
    log_sig = _log_sigmoid(zg)
    log_f = jnp.where(lane[0:1] < GATE_CI, log_sig, 0.0)
    f_terms = [_split3_by_lane(log_f[blk * MXU_DIM:(blk + 1) * MXU_DIM], lane)
               for blk in range(tm // MXU_DIM)]
    gz = jnp.where(lane[0:1] < GATE_CI, 0.0,
                   jnp.where(lane[0:1] < GATE_CF, zg,
                             jnp.where(lane[0:1] < GATE_CF + MLSTM_HEADS, log_sig, 0.0)))
    gz_hi = gz.astype(BF16)
    gz_r = gz - gz_hi.astype(F32)
    gz_mid = gz_r.astype(BF16)
    gz_terms = jnp.concatenate([gz_hi, gz_mid, (gz_r - gz_mid.astype(F32)).astype(BF16)], axis=1)

    za = proj(wa_ref, 0, W_A)
    for j in range(SWA_HEADS // 2):
        q = rope(head_norm(za[:, j * LANES:(j + 1) * LANES], first_swa, g_aq)) * (QK_SCALE * LOG2E)
        qa_ref[0, :, j * LANES:(j + 1) * LANES] = q.astype(BF16)
    k = rope(head_norm(za[:, SWA_Q:SWA_Q + LANES], first_swa, g_ak))
    kva_ref[0, :, 0:LANES] = k.astype(BF16)
    kva_ref[0, :, LANES:2 * LANES] = za[:, SWA_Q + SWA_KV:].astype(BF16)

    zb = proj(wb_ref, 0, W_B)
    for p in range(FOX_HEADS // 2):
        q = head_norm(zb[:, p * LANES:(p + 1) * LANES], first_nat, g_bq) * (QK_SCALE * LOG2E)
        qb_ref[0, :, p * LANES:(p + 1) * LANES] = q.astype(BF16)
        k = head_norm(zb[:, FOX_W + p * LANES:FOX_W + (p + 1) * LANES], first_nat, g_bk)
        kb_ref[0, :, 2 * p * LANES:(2 * p + 1) * LANES] = k.astype(BF16)
    vb_ref[0] = zb[:, 2 * FOX_W:].astype(BF16)

    zc = proj(wc_ref, SEC_C, 768)
    qkvc_ref[0, :, 0:MLSTM_W] = zc[:, 0:MLSTM_W].astype(BF16)
    qkvc_ref[0, :, MLSTM_W:2 * MLSTM_W] = (zc[:, MLSTM_W:2 * MLSTM_W] * QK_SCALE).astype(BF16)
    qkvc_ref[0, :, 2 * MLSTM_W:] = zc[:, 2 * MLSTM_W:].astype(BF16)
    gc_ref[0, :, 2 * MLSTM_W:] = proj(wc_ref, SEC_CO, MLSTM_W)

    gc_ref[0, :, 0:2 * MLSTM_W] = _dot(gz_terms, e3_ref[...])

    carry = carry_ref[0:1, :]
    for blk in range(tm // MXU_DIM):
        rows = slice(blk * MXU_DIM, (blk + 1) * MXU_DIM)
        cs = _dot(tri_ref[...], f_terms[blk])
        f_loc = cs + pltpu.roll(cs, LANES - 4, 1) + pltpu.roll(cs, LANES - 8, 1)
        f_cum = jnp.where(lane < 4, f_loc, 0.0) + carry
        carry = f_cum[MXU_DIM - 1:MXU_DIM, :]
        f_rep = (f_cum + pltpu.roll(f_cum, 4, 1) + pltpu.roll(f_cum, 8, 1)) * LOG2E
        aug = _split3_by_lane(f_rep, lane)
        kb_ref[0, rows, LANES:2 * LANES] = aug
        kb_ref[0, rows, 3 * LANES:4 * LANES] = aug
    carry_ref[...] = jnp.broadcast_to(carry, carry_ref.shape)


def _in_proj(layer, x, n1, w_a, w_b, w_c, vecs, cos, sin, tri, e3, w_ff1, w_ff2):
    B, S, _ = x.shape
    tm = min(TM_IN, S)
    n_i = S // tm
    grid = (B, n_i)
    steps = B * n_i
    assert D_MODEL % (16 * steps) == 0, "FFN weight slabs must be whole bf16 sublane tiles"
    r1, r2 = D_MODEL // steps, D_FF // steps
    tok = lambda w: pl.BlockSpec((1, tm, w), lambda b, i: (b, i, 0))
    out_shapes = (
        jax.ShapeDtypeStruct((B, S, SWA_Q), BF16),
        jax.ShapeDtypeStruct((B, S, 2 * SWA_KV), BF16),
        jax.ShapeDtypeStruct((B, S, FOX_W), BF16),
        jax.ShapeDtypeStruct((B, S, 2 * MXU_DIM), BF16),
        jax.ShapeDtypeStruct((B, S, FOX_W), BF16),
        jax.ShapeDtypeStruct((B, S, 3 * MLSTM_W), BF16),
        jax.ShapeDtypeStruct((B, S, 3 * MLSTM_W), F32),
        jax.ShapeDtypeStruct((D_MODEL, D_FF), BF16),
        jax.ShapeDtypeStruct((D_FF, D_MODEL), BF16),
    )
    slab_out = [pl.BlockSpec((r1, D_FF), lambda b, i: (b * n_i + i, 0)),
                pl.BlockSpec((r2, D_MODEL), lambda b, i: (b * n_i + i, 0))]
    return pl.pallas_call(
        _in_proj_body,
        grid=grid,
        in_specs=[
            tok(D_MODEL),
            _layer_spec((1, D_MODEL), layer),
            _layer_spec((D_MODEL, W_A), layer),
            _layer_spec((D_MODEL, W_B), layer),
            _layer_spec((D_MODEL, W_C), layer),
            _layer_spec((8, MLSTM_W), layer),
            pl.BlockSpec((tm, LANES), lambda b, i: (i, 0)),
            pl.BlockSpec((tm, LANES), lambda b, i: (i, 0)),
            _const_spec((MXU_DIM, MXU_DIM)),
            _const_spec((3 * LANES, 2 * MLSTM_W)),
            pl.BlockSpec((None, r1, D_FF), lambda b, i: (layer, b * n_i + i, 0)),
            pl.BlockSpec((None, r2, D_MODEL), lambda b, i: (layer, b * n_i + i, 0)),
        ],
        out_specs=[tok(s.shape[-1]) for s in out_shapes[:-2]] + slab_out,
        out_shape=out_shapes,
        scratch_shapes=[pltpu.VMEM((8, LANES), F32)],
        compiler_params=pltpu.CompilerParams(
            dimension_semantics=("arbitrary", "arbitrary"), vmem_limit_bytes=VMEM_LIMIT),
        name="in_proj",
    )(x, n1, w_a, w_b, w_c, vecs, cos, sin, tri, e3, w_ff1, w_ff2)


TS_SWA = 1024
SWA_DEPTH = 8
VT_ROWS = HEAD + 16


def _swa_body(sink_base, sink_ref, q_ref, kvc_ref, kvp_ref, band_ref, o_ref):
    step = pl.program_id(1)
    nblk = q_ref.shape[1] // QBLK
    per_group = SWA_HEADS // SWA_KV_HEADS

    band_cap = band_ref[...]
    key = lax.broadcasted_iota(jnp.int32, (2 * QBLK, MXU_DIM), 0)
    first_cap = jnp.where((key >= QBLK) | (step > 0), band_cap, -jnp.inf)
    lane = lax.broadcasted_iota(jnp.int32, (QBLK, LANES), 1)
    lane_a = (lane // HALF) % 2 == 0
    first_head = lax.broadcasted_iota(jnp.int32, (1, MXU_DIM), 1) < QBLK
    zero = jnp.zeros((QBLK, LANES), BF16)
    ones = jnp.ones((VT_ROWS - HEAD, 2 * QBLK), BF16)
    kv_all = jnp.concatenate([kvp_ref[0], kvc_ref[0]], axis=0)
    rows = [slice(blk * QBLK, (blk + 1) * QBLK) for blk in range(nblk)]
    bands = [kv_all[blk * QBLK:(blk + 2) * QBLK] for blk in range(nblk)]

    vt_aug = []
    for blk in range(nblk):
        v_t = bands[blk][:, LANES:].astype(F32).T.astype(BF16)
        vt_aug.append([jnp.concatenate([v_t[g * HEAD:(g + 1) * HEAD], ones], axis=0)
                       for g in range(SWA_KV_HEADS)])

    units = [(blk, g, part) for blk in range(nblk) for g in range(SWA_KV_HEADS) for part in range(2)]

    def score(blk, g, part):
        keep = lane_a if g == 0 else jnp.logical_not(lane_a)
        q_cols = [q_ref[0, rows[blk], j * LANES:(j + 1) * LANES] for j in (2 * part, 2 * part + 1)]
        q_g = jnp.concatenate([jnp.where(keep, q2, zero) for q2 in q_cols], axis=0)
        return _dot_nt(bands[blk][:, 0:LANES], q_g)

    def fold(s, blk, g, part):
        if blk > 0:
            s = jnp.concatenate([jnp.minimum(s[0:CHUNK], band_cap[0:CHUNK]), s[CHUNK:3 * CHUNK],
                                 jnp.minimum(s[3 * CHUNK:], band_cap[3 * CHUNK:])], axis=0)
        else:
            s = jnp.minimum(s, first_cap)
        head = sink_base + g * per_group + 2 * part
        sink = jnp.where(first_head, sink_ref[head], sink_ref[head + 1]) * LOG2E
        m = jnp.maximum(jnp.max(s, axis=0, keepdims=True), sink)
        p = jnp.exp2(s - m)
        o_aug = _dot(vt_aug[blk][g], p.astype(BF16))
        den = o_aug[HEAD:HEAD + 1] + jnp.exp2(sink - m)
        return o_aug[0:HEAD] / den

    def write_block(blk, outs):
        for g in range(SWA_KV_HEADS):
            for part in range(2):
                o = outs[(blk, g, part)]
                pair_t = jnp.concatenate([o[:, 0:QBLK], o[:, QBLK:]], axis=0)
                j = 2 * g + part
                o_ref[0, rows[blk], j * LANES:(j + 1) * LANES] = pair_t.T.astype(BF16)

    outs = {}
    queue = [score(*unit) for unit in units[:SWA_DEPTH]]
    for n, unit in enumerate(units):
        s_cur = queue.pop(0)
        if n + SWA_DEPTH < len(units):
            queue.append(score(*units[n + SWA_DEPTH]))
        outs[unit] = fold(s_cur, *unit)
        blk = unit[0]
        if unit[1:] == (SWA_KV_HEADS - 1, 1) and blk > 0:
            write_block(blk - 1, outs)
    write_block(nblk - 1, outs)


def _swa(layer, sinks, qa, kva, band):
    B, S, _ = qa.shape
    ts = min(TS_SWA, S)
    per_step = ts // QBLK
    return pl.pallas_call(
        functools.partial(_swa_body, layer * SWA_HEADS),
        grid=(B, S // ts),
        in_specs=[
            pl.BlockSpec(memory_space=pltpu.SMEM),
            pl.BlockSpec((1, ts, SWA_Q), lambda b, i: (b, i, 0)),
            pl.BlockSpec((1, ts, 2 * SWA_KV), lambda b, i: (b, i, 0)),
            pl.BlockSpec((1, QBLK, 2 * SWA_KV), lambda b, i: (b, jnp.maximum(i * per_step - 1, 0), 0)),
            _const_spec((2 * QBLK, MXU_DIM)),
        ],
        out_specs=pl.BlockSpec((1, ts, SWA_Q), lambda b, i: (b, i, 0)),
        out_shape=jax.ShapeDtypeStruct((B, S, SWA_Q), BF16),
        compiler_params=pltpu.CompilerParams(
            dimension_semantics=("arbitrary", "arbitrary"), vmem_limit_bytes=VMEM_LIMIT),
        name="swa",
    )(sinks, qa, kva, kva, band)


TQ_FOX = 1024
TK_FOX = 256
NEG_BIG = -1e30


FOX_UNROLL = 4
FOX_DEPTH = 8


def _fox_body(q_ref, k_ref, v_ref, o_ref, vt_ref, qt_ref, acc_ref):
    tq = q_ref.shape[1]
    tk = TK_FOX
    S = k_ref.shape[1]
    pair = pl.program_id(1)
    qi = pl.program_id(2)
    n_tiles = 2 * tq // MXU_DIM
    tiles_per_head = tq // MXU_DIM

    @pl.when(qi == 0)
    def _():
        ones = jnp.ones((VT_ROWS - HEAD, tk), BF16)
        for j in range(S // tk):
            blk = slice(j * tk, (j + 1) * tk)
            v_t = v_ref[0, blk, :].astype(F32).T.astype(BF16)
            vt_ref[:, blk] = jnp.concatenate([v_t[0:HEAD], ones, v_t[HEAD:], ones], axis=0)

    lane = lax.broadcasted_iota(jnp.int32, (tq, LANES), 1)
    lane_a = lane < HEAD
    q2 = q_ref[0]
    zero = jnp.zeros_like(q2)

    def minus_one_at(head):
        hit = (lane == head) | (lane == head + 4) | (lane == head + 8)
        return jnp.where(hit, -1.0, 0.0).astype(BF16)

    q_aug = [jnp.concatenate([jnp.where(lane_a, q2, zero), minus_one_at(2 * pair)], axis=1),
             jnp.concatenate([jnp.where(lane_a, zero, q2), minus_one_at(2 * pair + 1)], axis=1)]
    for t in range(n_tiles):
        hd, part = divmod(t, tiles_per_head)
        rows = q_aug[hd][part * MXU_DIM:(part + 1) * MXU_DIM]
        qt_ref[:, t * MXU_DIM:(t + 1) * MXU_DIM] = rows.astype(F32).T.astype(BF16)
    acc_ref[...] = jnp.zeros_like(acc_ref)

    key = lax.broadcasted_iota(jnp.int32, (tk // 2, MXU_DIM), 0)
    qry = lax.broadcasted_iota(jnp.int32, (tk // 2, MXU_DIM), 1)

    def fold(s, t, vt_blk, m, triangular):
        hd = t // tiles_per_head
        vt_h = vt_blk[hd * VT_ROWS:(hd + 1) * VT_ROWS]
        half = tk // 2
        for h in range(2):
            sh = s[h * half:(h + 1) * half]
            if triangular:
                valid = key + h * half <= qry
                sh = jnp.where(valid, sh, NEG_BIG)
            m_new = jnp.maximum(m, jnp.max(sh, axis=0, keepdims=True))
            alpha = jnp.exp2(m - m_new)
            p = jnp.exp2(sh - m_new)
            if triangular:
                p = jnp.where(valid, p, 0.0)
            acc_ref[t] = alpha * acc_ref[t] + _dot(vt_h[:, h * half:(h + 1) * half], p.astype(BF16))
            m = m_new
        return m

    def region(j0, ms, plans):
        k_blks, vt_blks, todo = [], [], []
        for b, plan in enumerate(plans):
            r0 = pl.multiple_of((j0 + b) * tk, tk)
            k_blks.append(k_ref[0, pl.ds(r0, tk), :])
            vt_blks.append(vt_ref[:, pl.ds(r0, tk)])
            todo += [(b, t) for t in range(n_tiles) if plan[t] is not None]
        score = lambda b, t: _dot(k_blks[b], qt_ref[:, t * MXU_DIM:(t + 1) * MXU_DIM])
        ms = list(ms)
        queue = [score(*unit) for unit in todo[:FOX_DEPTH]]
        for n, (b, t) in enumerate(todo):
            s_cur = queue.pop(0)
            if n + FOX_DEPTH < len(todo):
                queue.append(score(*todo[n + FOX_DEPTH]))
            ms[t] = fold(s_cur, t, vt_blks[b], ms[t], plans[b][t])
        return tuple(ms)

    blocks_per_tile = tq // tk
    full = (False,) * n_tiles
    n_full = blocks_per_tile * qi
    m_init = jnp.full((1, MXU_DIM), NEG_BIG, F32)
    ms = lax.fori_loop(0, n_full // FOX_UNROLL,
                       lambda g, c: region(g * FOX_UNROLL, c, (full,) * FOX_UNROLL), (m_init,) * n_tiles)
    if FOX_UNROLL > blocks_per_tile:
        done = (n_full // FOX_UNROLL) * FOX_UNROLL
        ms = lax.fori_loop(0, (n_full - done) // blocks_per_tile,
                           lambda g, c: region(done + g * blocks_per_tile, c, (full,) * blocks_per_tile), ms)
    diag = [tuple(None if t % tiles_per_head < d else t % tiles_per_head == d for t in range(n_tiles))
            for d in range(blocks_per_tile)]
    ms = region(n_full, ms, diag)

    for hd in range(2):
        cols = []
        for part in range(tiles_per_head):
            acc = acc_ref[hd * tiles_per_head + part]
            cols.append(acc[0:HEAD] / acc[HEAD:HEAD + 1])
        o_h = jnp.concatenate(cols, axis=1)
        if hd == 0:
            o_t = o_h
        else:
            o_t = jnp.concatenate([o_t, o_h], axis=0)
    for c in range(tq // MXU_DIM):
        rows = slice(c * MXU_DIM, (c + 1) * MXU_DIM)
        o_ref[0, rows, :] = o_t[:, rows].T.astype(BF16)


def _fox(qb, kb, vb):
    B, S, _ = qb.shape
    tq = min(TQ_FOX, S)
    return pl.pallas_call(
        _fox_body,
        grid=(B, FOX_HEADS // 2, S // tq),
        in_specs=[
            pl.BlockSpec((1, tq, LANES), lambda b, p, i: (b, i, p)),
            pl.BlockSpec((1, S, MXU_DIM), lambda b, p, i: (b, 0, p)),
            pl.BlockSpec((1, S, LANES), lambda b, p, i: (b, 0, p)),
        ],
        out_specs=pl.BlockSpec((1, tq, LANES), lambda b, p, i: (b, i, p)),
        out_shape=jax.ShapeDtypeStruct((B, S, FOX_W), BF16),
        scratch_shapes=[
            pltpu.VMEM((2 * VT_ROWS, S), BF16),
            pltpu.VMEM((MXU_DIM, 2 * tq), BF16),
            pltpu.VMEM((2 * tq // MXU_DIM, VT_ROWS, MXU_DIM), F32),
        ],
        compiler_params=pltpu.CompilerParams(
            dimension_semantics=("arbitrary", "arbitrary", "arbitrary"),
            vmem_limit_bytes=VMEM_LIMIT),
        name="fox",
    )(qb, kb, vb)


TS_MLSTM = 1024


def _mlstm_body(qkv_ref, g_ref, onorm_ref, tri_ref, i4_ref, bd_ref, o_ref, c_ref, n_ref, m_ref):
    ts = qkv_ref.shape[1]
    W = MLSTM_W

    @pl.when(pl.program_id(1) == 0)
    def _():
        c_ref[...] = jnp.zeros_like(c_ref)
        n_ref[...] = jnp.zeros_like(n_ref)
        m_ref[...] = jnp.zeros_like(m_ref)

    row = lax.broadcasted_iota(jnp.int32, (CHUNK, W), 0)
    lane = lax.broadcasted_iota(jnp.int32, (CHUNK, W), 1)
    causal = (lane % HEAD) <= row
    lane_head = lane // HEAD
    bd = bd_ref[...]
    bd2 = jnp.concatenate([bd, bd], axis=1)
    i4_bf = i4_ref[...].astype(BF16)

    def block_diag(x):
        tiled = jnp.concatenate([x] * MLSTM_HEADS, axis=0)
        return tiled * (bd if x.shape[1] == W else bd2)

    def seg_max(d):
        out = jnp.zeros_like(d)
        for hd in range(MLSTM_HEADS):
            sel = lane_head == hd
            mx = jnp.max(jnp.where(sel, d, -jnp.inf), axis=-1, keepdims=True)
            out = jnp.where(sel, mx, out)
        return out

    chunks = range(ts // CHUNK)
    rows = [pl.ds(c * CHUNK, CHUNK) for c in chunks]

    b, g, u, dm, dm_max, a_max, wa = [], [], [], [], [], [], []
    for c in chunks:
        ig = g_ref[0, rows[c], 0:W]
        lf_hi, lf_lo = _split2(g_ref[0, rows[c], W:2 * W])
        b_c = _dot(tri_ref[...], lf_hi) + _dot(tri_ref[...], lf_lo)
        g_c = b_c[CHUNK - 1:CHUNK, :]
        u_c = b_c - ig
        u_row = jnp.sum(u_c * i4_ref[...], axis=0, keepdims=True)
        dm_c = jnp.where(causal, b_c - u_row, -jnp.inf)
        a_c = g_c - u_c
        am_c = jnp.max(a_c, axis=0, keepdims=True)
        b.append(b_c); g.append(g_c); u.append(u_c); dm.append(dm_c)
        dm_max.append(seg_max(dm_c)); a_max.append(am_c); wa.append(jnp.exp(a_c - am_c))

    m_prev, s_old, s_loc = [m_ref[0:1, :]], [], []
    for c in chunks:
        gm = g[c] + m_prev[c]
        m_new = jnp.maximum(gm, a_max[c])
        s_old.append(jnp.exp(gm - m_new))
        s_loc.append(jnp.exp(a_max[c] - m_new))
        m_prev.append(m_new)

    m_t, w_inter, sm_hi, sm_lo, k_t = [], [], [], [], []
    for c in chunks:
        inter = b[c] + m_prev[c]
        mt_c = jnp.maximum(inter, dm_max[c])
        decay = jnp.exp(dm[c] - mt_c)
        q = qkv_ref[0, rows[c], 0:W]
        k = qkv_ref[0, rows[c], W:2 * W]
        qk = _dot_nt(jnp.concatenate([q, i4_bf], axis=0), block_diag(k))
        hi, lo = _split2(qk[0:CHUNK] * decay)
        m_t.append(mt_c); w_inter.append(jnp.exp(inter - mt_c)); sm_hi.append(hi); sm_lo.append(lo)
        k_t.append(qk[CHUNK:].astype(BF16))

    c_cur, n_cur = c_ref[...], n_ref[...]
    state_bf = []
    for c in chunks:
        state_bf.append(jnp.concatenate([c_cur.astype(BF16), n_cur.astype(BF16)], axis=1))
        v = qkv_ref[0, rows[c], 2 * W:]
        weighted = jnp.concatenate([(wa[c] * v.astype(F32)).astype(BF16), wa[c].astype(BF16)], axis=1)
        loc = _dot(k_t[c], block_diag(weighted))
        c_cur = s_old[c] * c_cur + s_loc[c] * loc[:, 0:W]
        n_cur = s_old[c] * n_cur + s_loc[c] * loc[:, W:]
    c_ref[...] = c_cur
    n_ref[...] = n_cur
    m_ref[...] = jnp.broadcast_to(m_prev[-1], m_ref.shape)

    hval = []
    for c in chunks:
        q = qkv_ref[0, rows[c], 0:W]
        v = qkv_ref[0, rows[c], 2 * W:]
        inter_cn = _dot(q, block_diag(state_bf[c]))
        intra = _dot(sm_hi[c], jnp.concatenate([block_diag(v), bd], axis=1))
        num = w_inter[c] * inter_cn[:, 0:W] + intra[:, 0:W]
        den = w_inter[c] * inter_cn[:, W:] + intra[:, W:] + _dot(sm_lo[c], bd)
        hval.append(num / jnp.maximum(jnp.abs(den), jnp.exp(-m_t[c])))

    for c in chunks:
        hs_hi, hs_lo = _split2(hval[c] * hval[c])
        ss = _dot(hs_hi, bd) + _dot(hs_lo, bd)
        hc = (hval[c] * lax.rsqrt(ss * (1.0 / HEAD) + EPS)) * onorm_ref[...]
        co = g_ref[0, rows[c], 2 * W:]
        o_ref[0, rows[c], :] = (jax.nn.sigmoid(co) * hc).astype(BF16)


def _mlstm(layer, qkvc, gc, onorm, tri, i4, bd):
    B, S, _ = qkvc.shape
    ts = min(TS_MLSTM, S)
    return pl.pallas_call(
        _mlstm_body,
        grid=(B, S // ts),
        in_specs=[
            pl.BlockSpec((1, ts, 3 * MLSTM_W), lambda b, i: (b, i, 0)),
            pl.BlockSpec((1, ts, 3 * MLSTM_W), lambda b, i: (b, i, 0)),
            _layer_spec((1, MLSTM_W), layer),
            _const_spec((CHUNK, CHUNK)),
            _const_spec((CHUNK, MLSTM_W)),
            _const_spec((MLSTM_W, MLSTM_W)),
        ],
        out_specs=pl.BlockSpec((1, ts, MLSTM_W), lambda b, i: (b, i, 0)),
        out_shape=jax.ShapeDtypeStruct((B, S, MLSTM_W), BF16),
        scratch_shapes=[pltpu.VMEM((HEAD, MLSTM_W), F32), pltpu.VMEM((HEAD, MLSTM_W), F32),
                        pltpu.VMEM((8, MLSTM_W), F32)],
        compiler_params=pltpu.CompilerParams(
            dimension_semantics=("arbitrary", "arbitrary"), vmem_limit_bytes=VMEM_LIMIT),
        name="mlstm",
    )(qkvc, gc, onorm, tri, i4, bd)


TM_FFN = 1024
FF_CHUNK = 1024


def _out_ffn_body(x_ref, ya_ref, yb_ref, yc_ref, wo_ref, n2_ref, w1_ref, w2_ref, o_ref, a_ref):
    x1 = (x_ref[...] + _dot(ya_ref[...], wo_ref[0:SWA_Q, :])
          + _dot(yb_ref[...], wo_ref[SWA_Q:SWA_Q + FOX_W, :])
          + _dot(yc_ref[...], wo_ref[SWA_Q + FOX_W:, :]))
    ms = jnp.mean(x1 * x1, axis=-1, keepdims=True)
    h2 = ((x1 * lax.rsqrt(ms + EPS)) * n2_ref[...]).astype(BF16)
    for j in range(D_FF // FF_CHUNK):
        cols = slice(j * FF_CHUNK, (j + 1) * FF_CHUNK)
        r = jnp.maximum(_dot(h2, w1_ref[:, cols]), 0.0)
        a_ref[:, cols] = (r * r).astype(BF16)
    o_ref[...] = x1 + _dot(a_ref[...], w2_ref[...])


def _out_ffn(layer, x2d, ya, yb, yc, wo, n2, w1, w2):
    T = x2d.shape[0]
    tm = min(TM_FFN, T)
    tok = lambda w: pl.BlockSpec((tm, w), lambda i: (i, 0))
    return pl.pallas_call(
        _out_ffn_body,
        grid=(T // tm,),
        in_specs=[
            tok(D_MODEL), tok(SWA_Q), tok(FOX_W), tok(MLSTM_W),
            _layer_spec((D_MODEL, D_MODEL), layer),
            _layer_spec((1, D_MODEL), layer),
            _const_spec((D_MODEL, D_FF)),
            _const_spec((D_FF, D_MODEL)),
        ],
        out_specs=tok(D_MODEL),
        out_shape=jax.ShapeDtypeStruct((T, D_MODEL), F32),
        scratch_shapes=[pltpu.VMEM((tm, D_FF), BF16)],
        compiler_params=pltpu.CompilerParams(
            dimension_semantics=("arbitrary",), vmem_limit_bytes=VMEM_LIMIT),
        name="out_ffn",
    )(x2d, ya, yb, yc, wo, n2, w1, w2)


def _rope_tables(S):
    inv = ROPE_THETA ** (-np.arange(HALF, dtype=np.float64) / HALF)
    ang = np.arange(S, dtype=np.float64)[:, None] * inv[None, :]
    cos = np.tile(np.cos(ang), (1, LANES // HALF))
    sin = np.tile(np.sin(ang), (1, LANES // HALF))
    sign = np.where(np.arange(LANES) < HEAD, -1.0, 1.0)
    return jnp.asarray(cos, F32), jnp.asarray(sin * sign[None, :], F32)


def _prep_w_in(w_in):
    select = lambda w, p: jnp.einsum("ldk,kn->ldn", w.astype(BF16), jnp.asarray(p, BF16),
                                     preferred_element_type=BF16)
    w_b = w_in[:, :, W_A:W_A + W_B].astype(BF16)
    return select(w_in[:, :, :W_A], _PERM_A), w_b, select(w_in[:, :, W_A + W_B:], _PERM_C)


def _prep_vecs(swa_q_norm, swa_k_norm, fox_q_norm, fox_k_norm, fox_f_bias, mlstm_i_bias, mlstm_f_bias):
    L = swa_q_norm.shape[0]

    def swa_lanes(g):
        return jnp.concatenate([g[:, :HALF], g[:, :HALF], g[:, HALF:], g[:, HALF:]], axis=-1)

    pad = lambda v: jnp.pad(v, ((0, 0), (0, MLSTM_W - v.shape[1])))
    return jnp.stack([
        pad(swa_lanes(swa_q_norm)), pad(swa_lanes(swa_k_norm)),
        pad(jnp.tile(fox_q_norm, (1, 2))), pad(jnp.tile(fox_k_norm, (1, 2))),
        pad(jnp.concatenate([jnp.tile(fox_f_bias, (1, 3)), mlstm_i_bias, mlstm_f_bias], axis=-1)),
        jnp.zeros((L, MLSTM_W), F32), jnp.zeros((L, MLSTM_W), F32), jnp.zeros((L, MLSTM_W), F32),
    ], axis=1).astype(F32)


def kernel(x, norm1, w_in, swa_q_norm, swa_k_norm, swa_sinks, fox_q_norm, fox_k_norm, fox_f_bias,
           mlstm_i_bias, mlstm_f_bias, mlstm_out_norm, w_out, norm2, w_ff1, w_ff2):
    B, S, D = x.shape
    L = norm1.shape[0]
    T = B * S
    cos, sin = _rope_tables(S)
    tri256, tri64, bd256 = (jnp.asarray(c, BF16) for c in (_TRI256, _TRI64, _BD256))
    i4 = jnp.asarray(_I4, F32)
    e3 = jnp.asarray(_gate_expand(), BF16)
    band = jnp.asarray(_swa_band_cap(), F32)

    w_a, w_b, w_c = _prep_w_in(w_in)
    wo = w_out.astype(BF16)
    vecs = _prep_vecs(swa_q_norm, swa_k_norm, fox_q_norm, fox_k_norm, fox_f_bias, mlstm_i_bias,
                      mlstm_f_bias)
    n1 = norm1.reshape(L, 1, D).astype(F32)
    n2 = norm2.reshape(L, 1, D).astype(F32)
    onorm = mlstm_out_norm.reshape(L, 1, MLSTM_W).astype(F32)
    sinks = swa_sinks.reshape(L * SWA_HEADS).astype(F32)

    for l in range(L):
        qa, kva, qb, kb, vb, qkvc, gc, w1, w2 = _in_proj(
            l, x, n1, w_a, w_b, w_c, vecs, cos, sin, tri256, e3, w_ff1, w_ff2)
        ya = _swa(l, sinks, qa, kva, band)
        yb = _fox(qb, kb, vb)
        yc = _mlstm(l, qkvc, gc, onorm, tri64, i4, bd256)
        x = _out_ffn(l, x.reshape(T, D), ya.reshape(T, SWA_Q), yb.reshape(T, FOX_W),
                     yc.reshape(T, MLSTM_W), wo, n2, w1, w2).reshape(B, S, D)
    return x
```

```python
import functools
import math

import numpy as np
import jax
import jax.numpy as jnp
from jax import lax
from jax.experimental import pallas as pl
from jax.experimental.pallas import tpu as pltpu

F32 = jnp.float32
BF16 = jnp.bfloat16

D_MODEL = 1024
HEAD = 64
HALF = HEAD // 2
CHUNK = 64
SWA_HEADS = 8
SWA_KV_HEADS = 2
FOX_HEADS = 4
MLSTM_HEADS = 4
SWA_Q = SWA_HEADS * HEAD
SWA_KV = SWA_KV_HEADS * HEAD
FOX_W = FOX_HEADS * HEAD
MLSTM_W = MLSTM_HEADS * HEAD
D_FF = 4 * D_MODEL
QBLK = 128
ROPE_THETA = 10000.0
EPS = 1e-6
QK_SCALE = 1.0 / math.sqrt(HEAD)
LOG2E = math.log2(math.e)

LANES = 128
MXU_DIM = 256
VMEM_LIMIT = 56 * 1024 * 1024

W_A = SWA_Q + 2 * SWA_KV
W_B = 3 * FOX_W
SEC_C = 0
SEC_CO = SEC_C + 768
SEC_G = SEC_CO + 256
W_C = SEC_G + LANES
GATE_BF = 0
GATE_CI = 12
GATE_CF = 16
IN_WIDTH = W_A + W_B + FOX_HEADS + 4 * MLSTM_W + 2 * MLSTM_HEADS


def _perm_a():
    p = np.zeros((W_A, W_A), np.float32)
    lane = np.arange(LANES)
    quarter, i = lane // HALF, lane % HALF
    dim, is_b = i + HALF * (quarter // 2), quarter % 2
    for j in range(SWA_HEADS // 2):
        p[(j + 4 * is_b) * HEAD + dim, j * LANES + lane] = 1.0
    p[SWA_Q + is_b * HEAD + dim, SWA_Q + lane] = 1.0
    p[SWA_Q + SWA_KV + lane, SWA_Q + SWA_KV + lane] = 1.0
    return p


def _perm_c():
    src = IN_WIDTH - W_A - W_B
    o_bf, o_cqkv = 0, FOX_HEADS
    o_ci = o_cqkv + 3 * MLSTM_W
    o_cf, o_co = o_ci + MLSTM_HEADS, o_ci + 2 * MLSTM_HEADS
    p = np.zeros((src, W_C), np.float32)
    w = np.arange(3 * MLSTM_W)
    p[o_cqkv + w, SEC_C + w] = 1.0
    w = np.arange(MLSTM_W)
    p[o_co + w, SEC_CO + w] = 1.0
    h = np.arange(FOX_HEADS)
    for rep in range(3):
        p[o_bf + h, SEC_G + GATE_BF + rep * FOX_HEADS + h] = 1.0
    h = np.arange(MLSTM_HEADS)
    p[o_ci + h, SEC_G + GATE_CI + h] = 1.0
    p[o_cf + h, SEC_G + GATE_CF + h] = 1.0
    return p


def _gate_expand():
    e = np.zeros((LANES, 2 * MLSTM_W), np.float32)
    for h in range(MLSTM_HEADS):
        e[GATE_CI + h, h * HEAD:(h + 1) * HEAD] = 1.0
        e[GATE_CF + h, MLSTM_W + h * HEAD:MLSTM_W + (h + 1) * HEAD] = 1.0
    return np.concatenate([e, e, e], axis=0)


_PERM_A = _perm_a()
_PERM_C = _perm_c()


def _block_diag_ones(head_of_lane):
    return (head_of_lane[:, None] == head_of_lane[None, :]).astype(np.float32)


_TRI256 = np.tril(np.ones((MXU_DIM, MXU_DIM), np.float32))
_TRI64 = np.tril(np.ones((CHUNK, CHUNK), np.float32))
_w256 = np.arange(MLSTM_W)
_BD256 = _block_diag_ones(_w256 // HEAD)
_I4 = (np.arange(CHUNK)[:, None] == (_w256 % HEAD)[None, :]).astype(np.float32)


def _swa_band_cap():
    key_chunk = np.arange(2 * QBLK)[:, None] // CHUNK
    qry_chunk = (np.arange(MXU_DIM)[None, :] % QBLK) // CHUNK
    ahead = key_chunk - qry_chunk
    return np.where((ahead >= 0) & (ahead <= 2), np.inf, -np.inf).astype(np.float32)


def _const_spec(shape):
    nd = len(shape)
    return pl.BlockSpec(shape, lambda *_: (0,) * nd, pipeline_mode=pl.Buffered(1))


def _layer_spec(shape, layer):
    nd = len(shape)
    return pl.BlockSpec((None,) + tuple(shape), lambda *_: (layer,) + (0,) * nd,
                        pipeline_mode=pl.Buffered(1))


def _log_sigmoid(x):
    return jnp.minimum(x, 0.0) - jnp.log(1.0 + jnp.exp(-jnp.abs(x)))


def _split2(x):
    hi = x.astype(BF16)
    lo = (x - hi.astype(F32)).astype(BF16)
    return hi, lo


def _split3_by_lane(x, lane):
    hi = x.astype(BF16).astype(F32)
    r1 = x - hi
    mid = r1.astype(BF16).astype(F32)
    low = r1 - mid
    return jnp.where(lane < 4, hi, jnp.where(lane < 8, mid, low)).astype(BF16)


def _dot(a, b):
    return jnp.dot(a, b, preferred_element_type=F32)


def _dot_nt(a, b):
    return lax.dot_general(a, b, (((1,), (1,)), ((), ())), preferred_element_type=F32)


TM_IN = 1024


def _in_proj_body(x_ref, n1_ref, wa_ref, wb_ref, wc_ref, vec_ref, cos_ref, sin_ref, tri_ref, e3_ref,
                  wff1_ref, wff2_ref,
                  qa_ref, kva_ref, qb_ref, kb_ref, vb_ref, qkvc_ref, gc_ref, wff1_bf_ref, wff2_bf_ref,
                  carry_ref):
    tm = x_ref.shape[1]
    wff1_bf_ref[...] = wff1_ref[...].astype(BF16)
    wff2_bf_ref[...] = wff2_ref[...].astype(BF16)

    @pl.when(pl.program_id(1) == 0)
    def _():
        carry_ref[...] = jnp.zeros_like(carry_ref)

    x = x_ref[0]
    ms = jnp.mean(x * x, axis=-1, keepdims=True)
    h = ((x * lax.rsqrt(ms + EPS)) * n1_ref[...]).astype(BF16)

    def proj(w_ref, c0, width):
        return _dot(h, w_ref[:, c0:c0 + width])

    lane_row = lax.broadcasted_iota(jnp.int32, (1, LANES), 1)
    first_swa = (lane_row // HALF) % 2 == 0
    first_nat = lane_row < HEAD

    def head_norm(z, first, gain):
        sq = z * z
        s_first = jnp.sum(jnp.where(first, sq, 0.0), axis=-1, keepdims=True)
        s_second = jnp.sum(jnp.where(first, 0.0, sq), axis=-1, keepdims=True)
        ss = jnp.where(first, s_first, s_second)
        return (z * lax.rsqrt(ss * (1.0 / HEAD) + EPS)) * gain

    cos = cos_ref[...]
    sin = sin_ref[...]

    def rope(z):
        return z * cos + pltpu.roll(z, HEAD, 1) * sin

    g_aq = vec_ref[0:1, 0:LANES]
    g_ak = vec_ref[1:2, 0:LANES]
    g_bq = vec_ref[2:3, 0:LANES]
    g_bk = vec_ref[3:4, 0:LANES]
    gate_bias = vec_ref[4:5, 0:LANES]

    lane = lax.broadcasted_iota(jnp.int32, (MXU_DIM, LANES), 1)
    zg = proj(wc_ref, SEC_G, LANES) + gate_bias
    log_sig = _log_sigmoid(zg)
    log_f = jnp.where(lane[0:1] < GATE_CI, log_sig, 0.0)
    f_terms = [_split3_by_lane(log_f[blk * MXU_DIM:(blk + 1) * MXU_DIM], lane)
               for blk in range(tm // MXU_DIM)]
    gz = jnp.where(lane[0:1] < GATE_CI, 0.0,
                   jnp.where(lane[0:1] < GATE_CF, zg,
                             jnp.where(lane[0:1] < GATE_CF + MLSTM_HEADS, log_sig, 0.0)))
    gz_hi = gz.astype(BF16)
    gz_r = gz - gz_hi.astype(F32)
    gz_mid = gz_r.astype(BF16)
    gz_terms = jnp.concatenate([gz_hi, gz_mid, (gz_r - gz_mid.astype(F32)).astype(BF16)], axis=1)

    za = proj(wa_ref, 0, W_A)
    for j in range(SWA_HEADS // 2):
        q = rope(head_norm(za[:, j * LANES:(j + 1) * LANES], first_swa, g_aq)) * (QK_SCALE * LOG2E)
        qa_ref[0, :, j * LANES:(j + 1) * LANES] = q.astype(BF16)
    k = rope(head_norm(za[:, SWA_Q:SWA_Q + LANES], first_swa, g_ak))
    kva_ref[0, :, 0:LANES] = k.astype(BF16)
    kva_ref[0, :, LANES:2 * LANES] = za[:, SWA_Q + SWA_KV:].astype(BF16)

    zb = proj(wb_ref, 0, W_B)
    for p in range(FOX_HEADS // 2):
        q = head_norm(zb[:, p * LANES:(p + 1) * LANES], first_nat, g_bq) * (QK_SCALE * LOG2E)
        qb_ref[0, :, p * LANES:(p + 1) * LANES] = q.astype(BF16)
        k = head_norm(zb[:, FOX_W + p * LANES:FOX_W + (p + 1) * LANES], first_nat, g_bk)
        kb_ref[0, :, 2 * p * LANES:(2 * p + 1) * LANES] = k.astype(BF16)
    vb_ref[0] = zb[:, 2 * FOX_W:].astype(BF16)

    zc = proj(wc_ref, SEC_C, 768)
    qkvc_ref[0, :, 0:MLSTM_W] = zc[:, 0:MLSTM_W].astype(BF16)
    qkvc_ref[0, :, MLSTM_W:2 * MLSTM_W] = (zc[:, MLSTM_W:2 * MLSTM_W] * QK_SCALE).astype(BF16)
    qkvc_ref[0, :, 2 * MLSTM_W:] = zc[:, 2 * MLSTM_W:].astype(BF16)
    gc_ref[0, :, 2 * MLSTM_W:] = proj(wc_ref, SEC_CO, MLSTM_W)

    gc_ref[0, :, 0:2 * MLSTM_W] = _dot(gz_terms, e3_ref[...])

    carry = carry_ref[0:1, :]
    for blk in range(tm // MXU_DIM):
        rows = slice(blk * MXU_DIM, (blk + 1) * MXU_DIM)
        cs = _dot(tri_ref[...], f_terms[blk])
        f_loc = cs + pltpu.roll(cs, LANES - 4, 1) + pltpu.roll(cs, LANES - 8, 1)
        f_cum = jnp.where(lane < 4, f_loc, 0.0) + carry
        carry = f_cum[MXU_DIM - 1:MXU_DIM, :]
        f_rep = (f_cum + pltpu.roll(f_cum, 4, 1) + pltpu.roll(f_cum, 8, 1)) * LOG2E
        aug = _split3_by_lane(f_rep, lane)
        kb_ref[0, rows, LANES:2 * LANES] = aug
        kb_ref[0, rows, 3 * LANES:4 * LANES] = aug
    carry_ref[...] = jnp.broadcast_to(carry, carry_ref.shape)


def _in_proj(layer, x, n1, w_a, w_b, w_c, vecs, cos, sin, tri, e3, w_ff1, w_ff2):
    B, S, _ = x.shape
    tm = min(TM_IN, S)
    n_i = S // tm
    grid = (B, n_i)
    steps = B * n_i
    assert D_MODEL % (16 * steps) == 0, "FFN weight slabs must be whole bf16 sublane tiles"
    r1, r2 = D_MODEL // steps, D_FF // steps
    tok = lambda w: pl.BlockSpec((1, tm, w), lambda b, i: (b, i, 0))
    out_shapes = (
        jax.ShapeDtypeStruct((B, S, SWA_Q), BF16),
        jax.ShapeDtypeStruct((B, S, 2 * SWA_KV), BF16),
        jax.ShapeDtypeStruct((B, S, FOX_W), BF16),
        jax.ShapeDtypeStruct((B, S, 2 * MXU_DIM), BF16),
        jax.ShapeDtypeStruct((B, S, FOX_W), BF16),
        jax.ShapeDtypeStruct((B, S, 3 * MLSTM_W), BF16),
        jax.ShapeDtypeStruct((B, S, 3 * MLSTM_W), F32),
        jax.ShapeDtypeStruct((D_MODEL, D_FF), BF16),
        jax.ShapeDtypeStruct((D_FF, D_MODEL), BF16),
    )
    slab_out = [pl.BlockSpec((r1, D_FF), lambda b, i: (b * n_i + i, 0)),
                pl.BlockSpec((r2, D_MODEL), lambda b, i: (b * n_i + i, 0))]
    return pl.pallas_call(
        _in_proj_body,
        grid=grid,
        in_specs=[
            tok(D_MODEL),
            _layer_spec((1, D_MODEL), layer),
            _layer_spec((D_MODEL, W_A), layer),
            _layer_spec((D_MODEL, W_B), layer),
            _layer_spec((D_MODEL, W_C), layer),
            _layer_spec((8, MLSTM_W), layer),
            pl.BlockSpec((tm, LANES), lambda b, i: (i, 0)),
            pl.BlockSpec((tm, LANES), lambda b, i: (i, 0)),
            _const_spec((MXU_DIM, MXU_DIM)),
            _const_spec((3 * LANES, 2 * MLSTM_W)),
            pl.BlockSpec((None, r1, D_FF), lambda b, i: (layer, b * n_i + i, 0)),
            pl.BlockSpec((None, r2, D_MODEL), lambda b, i: (layer, b * n_i + i, 0)),
        ],
        out_specs=[tok(s.shape[-1]) for s in out_shapes[:-2]] + slab_out,
        out_shape=out_shapes,
        scratch_shapes=[pltpu.VMEM((8, LANES), F32)],
        compiler_params=pltpu.CompilerParams(
            dimension_semantics=("arbitrary", "arbitrary"), vmem_limit_bytes=VMEM_LIMIT),
        name="in_proj",
    )(x, n1, w_a, w_b, w_c, vecs, cos, sin, tri, e3, w_ff1, w_ff2)


TS_SWA = 1024
SWA_DEPTH = 5
VT_ROWS = HEAD + 16


def _swa_body(sink_base, sink_ref, q_ref, kvc_ref, kvp_ref, band_ref, o_ref):
    step = pl.program_id(1)
    nblk = q_ref.shape[1] // QBLK
    per_group = SWA_HEADS // SWA_KV_HEADS

    band_cap = band_ref[...]
    key = lax.broadcasted_iota(jnp.int32, (2 * QBLK, MXU_DIM), 0)
    first_cap = jnp.where((key >= QBLK) | (step > 0), band_cap, -jnp.inf)
    lane = lax.broadcasted_iota(jnp.int32, (QBLK, LANES), 1)
    lane_a = (lane // HALF) % 2 == 0
    first_head = lax.broadcasted_iota(jnp.int32, (1, MXU_DIM), 1) < QBLK
    zero = jnp.zeros((QBLK, LANES), BF16)
    ones = jnp.ones((VT_ROWS - HEAD, 2 * QBLK), BF16)
    kv_all = jnp.concatenate([kvp_ref[0], kvc_ref[0]], axis=0)
    rows = [slice(blk * QBLK, (blk + 1) * QBLK) for blk in range(nblk)]
    bands = [kv_all[blk * QBLK:(blk + 2) * QBLK] for blk in range(nblk)]

    vt_aug = []
    for blk in range(nblk):
        v_t = bands[blk][:, LANES:].astype(F32).T.astype(BF16)
        vt_aug.append([jnp.concatenate([v_t[g * HEAD:(g + 1) * HEAD], ones], axis=0)
                       for g in range(SWA_KV_HEADS)])

    units = [(blk, g, part) for blk in range(nblk) for g in range(SWA_KV_HEADS) for part in range(2)]

    def score(blk, g, part):
        keep = lane_a if g == 0 else jnp.logical_not(lane_a)
        q_cols = [q_ref[0, rows[blk], j * LANES:(j + 1) * LANES] for j in (2 * part, 2 * part + 1)]
        q_g = jnp.concatenate([jnp.where(keep, q2, zero) for q2 in q_cols], axis=0)
        return _dot_nt(bands[blk][:, 0:LANES], q_g)

    def fold(s, blk, g, part):
        if blk > 0:
            s = jnp.concatenate([jnp.minimum(s[0:CHUNK], band_cap[0:CHUNK]), s[CHUNK:3 * CHUNK],
                                 jnp.minimum(s[3 * CHUNK:], band_cap[3 * CHUNK:])], axis=0)
        else:
            s = jnp.minimum(s, first_cap)
        head = sink_base + g * per_group + 2 * part
        sink = jnp.where(first_head, sink_ref[head], sink_ref[head + 1]) * LOG2E
        m = jnp.maximum(jnp.max(s, axis=0, keepdims=True), sink)
        p = jnp.exp2(s - m)
        o_aug = _dot(vt_aug[blk][g], p.astype(BF16))
        den = o_aug[HEAD:HEAD + 1] + jnp.exp2(sink - m)
        return o_aug[0:HEAD] / den

    def write_block(blk, outs):
        for g in range(SWA_KV_HEADS):
            for part in range(2):
                o = outs[(blk, g, part)]
                pair_t = jnp.concatenate([o[:, 0:QBLK], o[:, QBLK:]], axis=0)
                j = 2 * g + part
                o_ref[0, rows[blk], j * LANES:(j + 1) * LANES] = pair_t.T.astype(BF16)

    outs = {}
    queue = [score(*unit) for unit in units[:SWA_DEPTH]]
    for n, unit in enumerate(units):
        s_cur = queue.pop(0)
        if n + SWA_DEPTH < len(units):
            queue.append(score(*units[n + SWA_DEPTH]))
        outs[unit] = fold(s_cur, *unit)
        blk = unit[0]
        if unit[1:] == (SWA_KV_HEADS - 1, 1) and blk > 0:
            write_block(blk - 1, outs)
    write_block(nblk - 1, outs)


def _swa(layer, sinks, qa, kva, band):
    B, S, _ = qa.shape
    ts = min(TS_SWA, S)
    per_step = ts // QBLK
    return pl.pallas_call(
        functools.partial(_swa_body, layer * SWA_HEADS),
        grid=(B, S // ts),
        in_specs=[
            pl.BlockSpec(memory_space=pltpu.SMEM),
            pl.BlockSpec((1, ts, SWA_Q), lambda b, i: (b, i, 0)),
            pl.BlockSpec((1, ts, 2 * SWA_KV), lambda b, i: (b, i, 0)),
            pl.BlockSpec((1, QBLK, 2 * SWA_KV), lambda b, i: (b, jnp.maximum(i * per_step - 1, 0), 0)),
            _const_spec((2 * QBLK, MXU_DIM)),
        ],
        out_specs=pl.BlockSpec((1, ts, SWA_Q), lambda b, i: (b, i, 0)),
        out_shape=jax.ShapeDtypeStruct((B, S, SWA_Q), BF16),
        compiler_params=pltpu.CompilerParams(
            dimension_semantics=("arbitrary", "arbitrary"), vmem_limit_bytes=VMEM_LIMIT),
        name="swa",
    )(sinks, qa, kva, kva, band)


TQ_FOX = 1024
TK_FOX = 256
NEG_BIG = -1e30


FOX_UNROLL = 4
FOX_DEPTH = 10


def _fox_body(q_ref, k_ref, v_ref, o_ref, vt_ref, qt_ref, acc_ref):
    tq = q_ref.shape[1]
    tk = TK_FOX
    S = k_ref.shape[1]
    pair = pl.program_id(1)
    qi = pl.program_id(2)
    n_tiles = 2 * tq // MXU_DIM
    tiles_per_head = tq // MXU_DIM

    @pl.when(qi == 0)
    def _():
        ones = jnp.ones((VT_ROWS - HEAD, tk), BF16)
        for j in range(S // tk):
            blk = slice(j * tk, (j + 1) * tk)
            v_t = v_ref[0, blk, :].astype(F32).T.astype(BF16)
            vt_ref[:, blk] = jnp.concatenate([v_t[0:HEAD], ones, v_t[HEAD:], ones], axis=0)

    lane = lax.broadcasted_iota(jnp.int32, (tq, LANES), 1)
    lane_a = lane < HEAD
    q2 = q_ref[0]
    zero = jnp.zeros_like(q2)

    def minus_one_at(head):
        hit = (lane == head) | (lane == head + 4) | (lane == head + 8)
        return jnp.where(hit, -1.0, 0.0).astype(BF16)

    q_aug = [jnp.concatenate([jnp.where(lane_a, q2, zero), minus_one_at(2 * pair)], axis=1),
             jnp.concatenate([jnp.where(lane_a, zero, q2), minus_one_at(2 * pair + 1)], axis=1)]
    for t in range(n_tiles):
        hd, part = divmod(t, tiles_per_head)
        rows = q_aug[hd][part * MXU_DIM:(part + 1) * MXU_DIM]
        qt_ref[:, t * MXU_DIM:(t + 1) * MXU_DIM] = rows.astype(F32).T.astype(BF16)
    acc_ref[...] = jnp.zeros_like(acc_ref)

    key = lax.broadcasted_iota(jnp.int32, (tk // 2, MXU_DIM), 0)
    qry = lax.broadcasted_iota(jnp.int32, (tk // 2, MXU_DIM), 1)

    def fold(s, t, vt_blk, m, triangular):
        hd = t // tiles_per_head
        vt_h = vt_blk[hd * VT_ROWS:(hd + 1) * VT_ROWS]
        half = tk // 2
        for h in range(2):
            sh = s[h * half:(h + 1) * half]
            if triangular:
                valid = key + h * half <= qry
                sh = jnp.where(valid, sh, NEG_BIG)
            m_new = jnp.maximum(m, jnp.max(sh, axis=0, keepdims=True))
            alpha = jnp.exp2(m - m_new)
            p = jnp.exp2(sh - m_new)
            if triangular:
                p = jnp.where(valid, p, 0.0)
            acc_ref[t] = alpha * acc_ref[t] + _dot(vt_h[:, h * half:(h + 1) * half], p.astype(BF16))
            m = m_new
        return m

    def region(j0, ms, plans):
        k_blks, vt_blks, todo = [], [], []
        for b, plan in enumerate(plans):
            r0 = pl.multiple_of((j0 + b) * tk, tk)
            k_blks.append(k_ref[0, pl.ds(r0, tk), :])
            vt_blks.append(vt_ref[:, pl.ds(r0, tk)])
            todo += [(b, t) for t in range(n_tiles) if plan[t] is not None]
        score = lambda b, t: _dot(k_blks[b], qt_ref[:, t * MXU_DIM:(t + 1) * MXU_DIM])
        ms = list(ms)
        queue = [score(*unit) for unit in todo[:FOX_DEPTH]]
        for n, (b, t) in enumerate(todo):
            s_cur = queue.pop(0)
            if n + FOX_DEPTH < len(todo):
                queue.append(score(*todo[n + FOX_DEPTH]))
            ms[t] = fold(s_cur, t, vt_blks[b], ms[t], plans[b][t])
        return tuple(ms)

    blocks_per_tile = tq // tk
    full = (False,) * n_tiles
    n_full = blocks_per_tile * qi
    m_init = jnp.full((1, MXU_DIM), NEG_BIG, F32)
    ms = lax.fori_loop(0, n_full // FOX_UNROLL,
                       lambda g, c: region(g * FOX_UNROLL, c, (full,) * FOX_UNROLL), (m_init,) * n_tiles)
    if FOX_UNROLL > blocks_per_tile:
        done = (n_full // FOX_UNROLL) * FOX_UNROLL
        ms = lax.fori_loop(0, (n_full - done) // blocks_per_tile,
                           lambda g, c: region(done + g * blocks_per_tile, c, (full,) * blocks_per_tile), ms)
    diag = [tuple(None if t % tiles_per_head < d else t % tiles_per_head == d for t in range(n_tiles))
            for d in range(blocks_per_tile)]
    ms = region(n_full, ms, diag)

    for hd in range(2):
        cols = []
        for part in range(tiles_per_head):
            acc = acc_ref[hd * tiles_per_head + part]
            cols.append(acc[0:HEAD] / acc[HEAD:HEAD + 1])
        o_h = jnp.concatenate(cols, axis=1)
        if hd == 0:
            o_t = o_h
        else:
            o_t = jnp.concatenate([o_t, o_h], axis=0)
    for c in range(tq // MXU_DIM):
        rows = slice(c * MXU_DIM, (c + 1) * MXU_DIM)
        o_ref[0, rows, :] = o_t[:, rows].T.astype(BF16)


def _fox(qb, kb, vb):
    B, S, _ = qb.shape
    tq = min(TQ_FOX, S)
    return pl.pallas_call(
        _fox_body,
        grid=(B, FOX_HEADS // 2, S // tq),
        in_specs=[
            pl.BlockSpec((1, tq, LANES), lambda b, p, i: (b, i, p)),
            pl.BlockSpec((1, S, MXU_DIM), lambda b, p, i: (b, 0, p)),
            pl.BlockSpec((1, S, LANES), lambda b, p, i: (b, 0, p)),
        ],
        out_specs=pl.BlockSpec((1, tq, LANES), lambda b, p, i: (b, i, p)),
        out_shape=jax.ShapeDtypeStruct((B, S, FOX_W), BF16),
        scratch_shapes=[
            pltpu.VMEM((2 * VT_ROWS, S), BF16),
            pltpu.VMEM((MXU_DIM, 2 * tq), BF16),
            pltpu.VMEM((2 * tq // MXU_DIM, VT_ROWS, MXU_DIM), F32),
        ],
        compiler_params=pltpu.CompilerParams(
            dimension_semantics=("arbitrary", "arbitrary", "arbitrary"),
            vmem_limit_bytes=VMEM_LIMIT),
        name="fox",
    )(qb, kb, vb)


TS_MLSTM = 1024


def _mlstm_body(qkv_ref, g_ref, onorm_ref, tri_ref, i4_ref, bd_ref, o_ref, c_ref, n_ref, m_ref):
    ts = qkv_ref.shape[1]
    W = MLSTM_W

    @pl.when(pl.program_id(1) == 0)
    def _():
        c_ref[...] = jnp.zeros_like(c_ref)
        n_ref[...] = jnp.zeros_like(n_ref)
        m_ref[...] = jnp.zeros_like(m_ref)

    row = lax.broadcasted_iota(jnp.int32, (CHUNK, W), 0)
    lane = lax.broadcasted_iota(jnp.int32, (CHUNK, W), 1)
    causal = (lane % HEAD) <= row
    lane_head = lane // HEAD
    bd = bd_ref[...]
    bd2 = jnp.concatenate([bd, bd], axis=1)
    i4_bf = i4_ref[...].astype(BF16)

    def block_diag(x):
        tiled = jnp.concatenate([x] * MLSTM_HEADS, axis=0)
        return tiled * (bd if x.shape[1] == W else bd2)

    def seg_max(d):
        out = jnp.zeros_like(d)
        for hd in range(MLSTM_HEADS):
            sel = lane_head == hd
            mx = jnp.max(jnp.where(sel, d, -jnp.inf), axis=-1, keepdims=True)
            out = jnp.where(sel, mx, out)
        return out

    chunks = range(ts // CHUNK)
    rows = [pl.ds(c * CHUNK, CHUNK) for c in chunks]

    b, g, u, dm, dm_max, a_max, wa = [], [], [], [], [], [], []
    for c in chunks:
        ig = g_ref[0, rows[c], 0:W]
        lf_hi, lf_lo = _split2(g_ref[0, rows[c], W:2 * W])
        b_c = _dot(tri_ref[...], lf_hi) + _dot(tri_ref[...], lf_lo)
        g_c = b_c[CHUNK - 1:CHUNK, :]
        u_c = b_c - ig
        u_row = jnp.sum(u_c * i4_ref[...], axis=0, keepdims=True)
        dm_c = jnp.where(causal, b_c - u_row, -jnp.inf)
        a_c = g_c - u_c
        am_c = jnp.max(a_c, axis=0, keepdims=True)
        b.append(b_c); g.append(g_c); u.append(u_c); dm.append(dm_c)
        dm_max.append(seg_max(dm_c)); a_max.append(am_c); wa.append(jnp.exp(a_c - am_c))

    m_prev, s_old, s_loc = [m_ref[0:1, :]], [], []
    for c in chunks:
        gm = g[c] + m_prev[c]
        m_new = jnp.maximum(gm, a_max[c])
        s_old.append(jnp.exp(gm - m_new))
        s_loc.append(jnp.exp(a_max[c] - m_new))
        m_prev.append(m_new)

    m_t, w_inter, sm_hi, sm_lo, k_t = [], [], [], [], []
    for c in chunks:
        inter = b[c] + m_prev[c]
        mt_c = jnp.maximum(inter, dm_max[c])
        decay = jnp.exp(dm[c] - mt_c)
        q = qkv_ref[0, rows[c], 0:W]
        k = qkv_ref[0, rows[c], W:2 * W]
        qk = _dot_nt(jnp.concatenate([q, i4_bf], axis=0), block_diag(k))
        hi, lo = _split2(qk[0:CHUNK] * decay)
        m_t.append(mt_c); w_inter.append(jnp.exp(inter - mt_c)); sm_hi.append(hi); sm_lo.append(lo)
        k_t.append(qk[CHUNK:].astype(BF16))

    c_cur, n_cur = c_ref[...], n_ref[...]
    state_bf = []
    for c in chunks:
        state_bf.append(jnp.concatenate([c_cur.astype(BF16), n_cur.astype(BF16)], axis=1))
        v = qkv_ref[0, rows[c], 2 * W:]
        weighted = jnp.concatenate([(wa[c] * v.astype(F32)).astype(BF16), wa[c].astype(BF16)], axis=1)
        loc = _dot(k_t[c], block_diag(weighted))
        c_cur = s_old[c] * c_cur + s_loc[c] * loc[:, 0:W]
        n_cur = s_old[c] * n_cur + s_loc[c] * loc[:, W:]
    c_ref[...] = c_cur
    n_ref[...] = n_cur
    m_ref[...] = jnp.broadcast_to(m_prev[-1], m_ref.shape)

    hval = []
    for c in chunks:
        q = qkv_ref[0, rows[c], 0:W]
        v = qkv_ref[0, rows[c], 2 * W:]
        inter_cn = _dot(q, block_diag(state_bf[c]))
        intra = _dot(sm_hi[c], jnp.concatenate([block_diag(v), bd], axis=1))
        num = w_inter[c] * inter_cn[:, 0:W] + intra[:, 0:W]
        den = w_inter[c] * inter_cn[:, W:] + intra[:, W:] + _dot(sm_lo[c], bd)
        hval.append(num / jnp.maximum(jnp.abs(den), jnp.exp(-m_t[c])))

    for c in chunks:
        hs_hi, hs_lo = _split2(hval[c] * hval[c])
        ss = _dot(hs_hi, bd) + _dot(hs_lo, bd)
        hc = (hval[c] * lax.rsqrt(ss * (1.0 / HEAD) + EPS)) * onorm_ref[...]
        co = g_ref[0, rows[c], 2 * W:]
        o_ref[0, rows[c], :] = (jax.nn.sigmoid(co) * hc).astype(BF16)


def _mlstm(layer, qkvc, gc, onorm, tri, i4, bd):
    B, S, _ = qkvc.shape
    ts = min(TS_MLSTM, S)
    return pl.pallas_call(
        _mlstm_body,
        grid=(B, S // ts),
        in_specs=[
            pl.BlockSpec((1, ts, 3 * MLSTM_W), lambda b, i: (b, i, 0)),
            pl.BlockSpec((1, ts, 3 * MLSTM_W), lambda b, i: (b, i, 0)),
            _layer_spec((1, MLSTM_W), layer),
            _const_spec((CHUNK, CHUNK)),
            _const_spec((CHUNK, MLSTM_W)),
            _const_spec((MLSTM_W, MLSTM_W)),
        ],
        out_specs=pl.BlockSpec((1, ts, MLSTM_W), lambda b, i: (b, i, 0)),
        out_shape=jax.ShapeDtypeStruct((B, S, MLSTM_W), BF16),
        scratch_shapes=[pltpu.VMEM((HEAD, MLSTM_W), F32), pltpu.VMEM((HEAD, MLSTM_W), F32),
                        pltpu.VMEM((8, MLSTM_W), F32)],
        compiler_params=pltpu.CompilerParams(
            dimension_semantics=("arbitrary", "arbitrary"), vmem_limit_bytes=VMEM_LIMIT),
        name="mlstm",
    )(qkvc, gc, onorm, tri, i4, bd)


TM_FFN = 1024
FF_CHUNK = 1024


def _out_ffn_body(x_ref, ya_ref, yb_ref, yc_ref, wo_ref, n2_ref, w1_ref, w2_ref, o_ref, a_ref):
    x1 = (x_ref[...] + _dot(ya_ref[...], wo_ref[0:SWA_Q, :])
          + _dot(yb_ref[...], wo_ref[SWA_Q:SWA_Q + FOX_W, :])
          + _dot(yc_ref[...], wo_ref[SWA_Q + FOX_W:, :]))
    ms = jnp.mean(x1 * x1, axis=-1, keepdims=True)
    h2 = ((x1 * lax.rsqrt(ms + EPS)) * n2_ref[...]).astype(BF16)
    for j in range(D_FF // FF_CHUNK):
        cols = slice(j * FF_CHUNK, (j + 1) * FF_CHUNK)
        r = jnp.maximum(_dot(h2, w1_ref[:, cols]), 0.0)
        a_ref[:, cols] = (r * r).astype(BF16)
    o_ref[...] = x1 + _dot(a_ref[...], w2_ref[...])


def _out_ffn(layer, x2d, ya, yb, yc, wo, n2, w1, w2):
    T = x2d.shape[0]
    tm = min(TM_FFN, T)
    tok = lambda w: pl.BlockSpec((tm, w), lambda i: (i, 0))
    return pl.pallas_call(
        _out_ffn_body,
        grid=(T // tm,),
        in_specs=[
            tok(D_MODEL), tok(SWA_Q), tok(FOX_W), tok(MLSTM_W),
            _layer_spec((D_MODEL, D_MODEL), layer),
            _layer_spec((1, D_MODEL), layer),
            _const_spec((D_MODEL, D_FF)),
            _const_spec((D_FF, D_MODEL)),
        ],
        out_specs=tok(D_MODEL),
        out_shape=jax.ShapeDtypeStruct((T, D_MODEL), F32),
        scratch_shapes=[pltpu.VMEM((tm, D_FF), BF16)],
        compiler_params=pltpu.CompilerParams(
            dimension_semantics=("arbitrary",), vmem_limit_bytes=VMEM_LIMIT),
        name="out_ffn",
    )(x2d, ya, yb, yc, wo, n2, w1, w2)


def _rope_tables(S):
    inv = ROPE_THETA ** (-np.arange(HALF, dtype=np.float64) / HALF)
    ang = np.arange(S, dtype=np.float64)[:, None] * inv[None, :]
    cos = np.tile(np.cos(ang), (1, LANES // HALF))
    sin = np.tile(np.sin(ang), (1, LANES // HALF))
    sign = np.where(np.arange(LANES) < HEAD, -1.0, 1.0)
    return jnp.asarray(cos, F32), jnp.asarray(sin * sign[None, :], F32)


def _prep_w_in(w_in):
    select = lambda w, p: jnp.einsum("ldk,kn->ldn", w.astype(BF16), jnp.asarray(p, BF16),
                                     preferred_element_type=BF16)
    w_b = w_in[:, :, W_A:W_A + W_B].astype(BF16)
    return select(w_in[:, :, :W_A], _PERM_A), w_b, select(w_in[:, :, W_A + W_B:], _PERM_C)


def _prep_vecs(swa_q_norm, swa_k_norm, fox_q_norm, fox_k_norm, fox_f_bias, mlstm_i_bias, mlstm_f_bias):
    L = swa_q_norm.shape[0]

    def swa_lanes(g):
        return jnp.concatenate([g[:, :HALF], g[:, :HALF], g[:, HALF:], g[:, HALF:]], axis=-1)

    pad = lambda v: jnp.pad(v, ((0, 0), (0, MLSTM_W - v.shape[1])))
    return jnp.stack([
        pad(swa_lanes(swa_q_norm)), pad(swa_lanes(swa_k_norm)),
        pad(jnp.tile(fox_q_norm, (1, 2))), pad(jnp.tile(fox_k_norm, (1, 2))),
        pad(jnp.concatenate([jnp.tile(fox_f_bias, (1, 3)), mlstm_i_bias, mlstm_f_bias], axis=-1)),
        jnp.zeros((L, MLSTM_W), F32), jnp.zeros((L, MLSTM_W), F32), jnp.zeros((L, MLSTM_W), F32),
    ], axis=1).astype(F32)


def kernel(x, norm1, w_in, swa_q_norm, swa_k_norm, swa_sinks, fox_q_norm, fox_k_norm, fox_f_bias,
           mlstm_i_bias, mlstm_f_bias, mlstm_out_norm, w_out, norm2, w_ff1, w_ff2):
    B, S, D = x.shape
    L = norm1.shape[0]
    T = B * S
    cos, sin = _rope_tables(S)
    tri256, tri64, bd256 = (jnp.asarray(c, BF16) for c in (_TRI256, _TRI64, _BD256))
    i4 = jnp.asarray(_I4, F32)
    e3 = jnp.asarray(_gate_expand(), BF16)
    band = jnp.asarray(_swa_band_cap(), F32)

    w_a, w_b, w_c = _prep_w_in(w_in)
    wo = w_out.astype(BF16)
    vecs = _prep_vecs(swa_q_norm, swa_k_norm, fox_q_norm, fox_k_norm, fox_f_bias, mlstm_i_bias,
                      mlstm_f_bias)
    n1 = norm1.reshape(L, 1, D).astype(F32)
    n2 = norm2.reshape(L, 1, D).astype(F32)
    onorm = mlstm_out_norm.reshape(L, 1, MLSTM_W).astype(F32)
    sinks = swa_sinks.reshape(L * SWA_HEADS).astype(F32)

    for l in range(L):
        qa, kva, qb, kb, vb, qkvc, gc, w1, w2 = _in_proj(
            l, x, n1, w_a, w_b, w_c, vecs, cos, sin, tri256, e3, w_ff1, w_ff2)
        ya = _swa(l, sinks, qa, kva, band)
        yb = _fox(qb, kb, vb)
        yc = _mlstm(l, qkvc, gc, onorm, tri64, i4, bd256)
        x = _out_ffn(l, x.reshape(T, D), ya.reshape(T, SWA_Q), yb.reshape(T, FOX_W),
                     yc.reshape(T, MLSTM_W), wo, n2, w1, w2).reshape(B, S, D)
    return x
```

```python
import functools
import math

import numpy as np
import jax
import jax.numpy as jnp
from jax import lax
from jax.experimental import pallas as pl
from jax.experimental.pallas import tpu as pltpu

F32 = jnp.float32
BF16 = jnp.bfloat16

D_MODEL = 1024
HEAD = 64
HALF = HEAD // 2
CHUNK = 64
SWA_HEADS = 8
SWA_KV_HEADS = 2
FOX_HEADS = 4
MLSTM_HEADS = 4
SWA_Q = SWA_HEADS * HEAD
SWA_KV = SWA_KV_HEADS * HEAD
FOX_W = FOX_HEADS * HEAD
MLSTM_W = MLSTM_HEADS * HEAD
D_FF = 4 * D_MODEL
QBLK = 128
ROPE_THETA = 10000.0
EPS = 1e-6
QK_SCALE = 1.0 / math.sqrt(HEAD)
LOG2E = math.log2(math.e)

LANES = 128
MXU_DIM = 256
VMEM_LIMIT = 56 * 1024 * 1024

W_A = SWA_Q + 2 * SWA_KV
W_B = 3 * FOX_W
SEC_C = 0
SEC_CO = SEC_C + 768
SEC_G = SEC_CO + 256
W_C = SEC_G + LANES
GATE_BF = 0
GATE_CI = 12
GATE_CF = 16
IN_WIDTH = W_A + W_B + FOX_HEADS + 4 * MLSTM_W + 2 * MLSTM_HEADS


def _perm_a():
    p = np.zeros((W_A, W_A), np.float32)
    lane = np.arange(LANES)
    quarter, i = lane // HALF, lane % HALF
    dim, is_b = i + HALF * (quarter // 2), quarter % 2
    for j in range(SWA_HEADS // 2):
        p[(j + 4 * is_b) * HEAD + dim, j * LANES + lane] = 1.0
    p[SWA_Q + is_b * HEAD + dim, SWA_Q + lane] = 1.0
    p[SWA_Q + SWA_KV + lane, SWA_Q + SWA_KV + lane] = 1.0
    return p


def _perm_c():
    src = IN_WIDTH - W_A - W_B
    o_bf, o_cqkv = 0, FOX_HEADS
    o_ci = o_cqkv + 3 * MLSTM_W
    o_cf, o_co = o_ci + MLSTM_HEADS, o_ci + 2 * MLSTM_HEADS
    p = np.zeros((src, W_C), np.float32)
    w = np.arange(3 * MLSTM_W)
    p[o_cqkv + w, SEC_C + w] = 1.0
    w = np.arange(MLSTM_W)
    p[o_co + w, SEC_CO + w] = 1.0
    h = np.arange(FOX_HEADS)
    for rep in range(3):
        p[o_bf + h, SEC_G + GATE_BF + rep * FOX_HEADS + h] = 1.0
    h = np.arange(MLSTM_HEADS)
    p[o_ci + h, SEC_G + GATE_CI + h] = 1.0
    p[o_cf + h, SEC_G + GATE_CF + h] = 1.0
    return p


def _gate_expand():
    e = np.zeros((LANES, 2 * MLSTM_W), np.float32)
    for h in range(MLSTM_HEADS):
        e[GATE_CI + h, h * HEAD:(h + 1) * HEAD] = 1.0
        e[GATE_CF + h, MLSTM_W + h * HEAD:MLSTM_W + (h + 1) * HEAD] = 1.0
    return np.concatenate([e, e, e], axis=0)


_PERM_A = _perm_a()
_PERM_C = _perm_c()


def _block_diag_ones(head_of_lane):
    return (head_of_lane[:, None] == head_of_lane[None, :]).astype(np.float32)


_TRI256 = np.tril(np.ones((MXU_DIM, MXU_DIM), np.float32))
_TRI64 = np.tril(np.ones((CHUNK, CHUNK), np.float32))
_w256 = np.arange(MLSTM_W)
_BD256 = _block_diag_ones(_w256 // HEAD)
_I4 = (np.arange(CHUNK)[:, None] == (_w256 % HEAD)[None, :]).astype(np.float32)


def _swa_band_cap():
    key_chunk = np.arange(2 * QBLK)[:, None] // CHUNK
    qry_chunk = (np.arange(MXU_DIM)[None, :] % QBLK) // CHUNK
    ahead = key_chunk - qry_chunk
    return np.where((ahead >= 0) & (ahead <= 2), np.inf, -np.inf).astype(np.float32)


def _const_spec(shape):
    nd = len(shape)
    return pl.BlockSpec(shape, lambda *_: (0,) * nd, pipeline_mode=pl.Buffered(1))


def _layer_spec(shape, layer):
    nd = len(shape)
    return pl.BlockSpec((None,) + tuple(shape), lambda *_: (layer,) + (0,) * nd,
                        pipeline_mode=pl.Buffered(1))


def _log_sigmoid(x):
    return jnp.minimum(x, 0.0) - jnp.log(1.0 + jnp.exp(-jnp.abs(x)))


def _split2(x):
    hi = x.astype(BF16)
    lo = (x - hi.astype(F32)).astype(BF16)
    return hi, lo


def _split3_by_lane(x, lane):
    hi = x.astype(BF16).astype(F32)
    r1 = x - hi
    mid = r1.astype(BF16).astype(F32)
    low = r1 - mid
    return jnp.where(lane < 4, hi, jnp.where(lane < 8, mid, low)).astype(BF16)


def _dot(a, b):
    return jnp.dot(a, b, preferred_element_type=F32)


def _dot_nt(a, b):
    return lax.dot_general(a, b, (((1,), (1,)), ((), ())), preferred_element_type=F32)


TM_IN = 1024


def _in_proj_body(x_ref, n1_ref, wa_ref, wb_ref, wc_ref, vec_ref, cos_ref, sin_ref, tri_ref, e3_ref,
                  wff1_ref, wff2_ref,
                  qa_ref, kva_ref, qb_ref, kb_ref, vb_ref, qkvc_ref, gc_ref, wff1_bf_ref, wff2_bf_ref,
                  carry_ref):
    tm = x_ref.shape[1]
    wff1_bf_ref[...] = wff1_ref[...].astype(BF16)
    wff2_bf_ref[...] = wff2_ref[...].astype(BF16)

    @pl.when(pl.program_id(1) == 0)
    def _():
        carry_ref[...] = jnp.zeros_like(carry_ref)

    x = x_ref[0]
    ms = jnp.mean(x * x, axis=-1, keepdims=True)
    h = ((x * lax.rsqrt(ms + EPS)) * n1_ref[...]).astype(BF16)

    def proj(w_ref, c0, width):
        return _dot(h, w_ref[:, c0:c0 + width])

    lane_row = lax.broadcasted_iota(jnp.int32, (1, LANES), 1)
    first_swa = (lane_row // HALF) % 2 == 0
    first_nat = lane_row < HEAD

    def head_norm(z, first, gain):
        sq = z * z
        s_first = jnp.sum(jnp.where(first, sq, 0.0), axis=-1, keepdims=True)
        s_second = jnp.sum(jnp.where(first, 0.0, sq), axis=-1, keepdims=True)
        ss = jnp.where(first, s_first, s_second)
        return (z * lax.rsqrt(ss * (1.0 / HEAD) + EPS)) * gain

    cos = cos_ref[...]
    sin = sin_ref[...]

    def rope(z):
        return z * cos + pltpu.roll(z, HEAD, 1) * sin

    g_aq = vec_ref[0:1, 0:LANES]
    g_ak = vec_ref[1:2, 0:LANES]
    g_bq = vec_ref[2:3, 0:LANES]
    g_bk = vec_ref[3:4, 0:LANES]
    gate_bias = vec_ref[4:5, 0:LANES]

    lane = lax.broadcasted_iota(jnp.int32, (MXU_DIM, LANES), 1)
    zg = proj(wc_ref, SEC_G, LANES) + gate_bias
    log_sig = _log_sigmoid(zg)
    log_f = jnp.where(lane[0:1] < GATE_CI, log_sig, 0.0)
    f_terms = [_split3_by_lane(log_f[blk * MXU_DIM:(blk + 1) * MXU_DIM], lane)
               for blk in range(tm // MXU_DIM)]
    gz = jnp.where(lane[0:1] < GATE_CI, 0.0,
                   jnp.where(lane[0:1] < GATE_CF, zg,
                             jnp.where(lane[0:1] < GATE_CF + MLSTM_HEADS, log_sig, 0.0)))
    gz_hi = gz.astype(BF16)
    gz_r = gz - gz_hi.astype(F32)
    gz_mid = gz_r.astype(BF16)
    gz_terms = jnp.concatenate([gz_hi, gz_mid, (gz_r - gz_mid.astype(F32)).astype(BF16)], axis=1)

    za = proj(wa_ref, 0, W_A)
    for j in range(SWA_HEADS // 2):
        q = rope(head_norm(za[:, j * LANES:(j + 1) * LANES], first_swa, g_aq)) * (QK_SCALE * LOG2E)
        qa_ref[0, :, j * LANES:(j + 1) * LANES] = q.astype(BF16)
    k = rope(head_norm(za[:, SWA_Q:SWA_Q + LANES], first_swa, g_ak))
    kva_ref[0, :, 0:LANES] = k.astype(BF16)
    kva_ref[0, :, LANES:2 * LANES] = za[:, SWA_Q + SWA_KV:].astype(BF16)

    zb = proj(wb_ref, 0, W_B)
    for p in range(FOX_HEADS // 2):
        q = head_norm(zb[:, p * LANES:(p + 1) * LANES], first_nat, g_bq) * (QK_SCALE * LOG2E)
        qb_ref[0, :, p * LANES:(p + 1) * LANES] = q.astype(BF16)
        k = head_norm(zb[:, FOX_W + p * LANES:FOX_W + (p + 1) * LANES], first_nat, g_bk)
        kb_ref[0, :, 2 * p * LANES:(2 * p + 1) * LANES] = k.astype(BF16)
    vb_ref[0] = zb[:, 2 * FOX_W:].astype(BF16)

    zc = proj(wc_ref, SEC_C, 768)
    qkvc_ref[0, :, 0:MLSTM_W] = zc[:, 0:MLSTM_W].astype(BF16)
    qkvc_ref[0, :, MLSTM_W:2 * MLSTM_W] = (zc[:, MLSTM_W:2 * MLSTM_W] * QK_SCALE).astype(BF16)
    qkvc_ref[0, :, 2 * MLSTM_W:] = zc[:, 2 * MLSTM_W:].astype(BF16)
    gc_ref[0, :, 2 * MLSTM_W:] = proj(wc_ref, SEC_CO, MLSTM_W)

    gc_ref[0, :, 0:2 * MLSTM_W] = _dot(gz_terms, e3_ref[...])

    carry = carry_ref[0:1, :]
    for blk in range(tm // MXU_DIM):
        rows = slice(blk * MXU_DIM, (blk + 1) * MXU_DIM)
        cs = _dot(tri_ref[...], f_terms[blk])
        f_loc = cs + pltpu.roll(cs, LANES - 4, 1) + pltpu.roll(cs, LANES - 8, 1)
        f_cum = jnp.where(lane < 4, f_loc, 0.0) + carry
        carry = f_cum[MXU_DIM - 1:MXU_DIM, :]
        f_rep = (f_cum + pltpu.roll(f_cum, 4, 1) + pltpu.roll(f_cum, 8, 1)) * LOG2E
        aug = _split3_by_lane(f_rep, lane)
        kb_ref[0, rows, LANES:2 * LANES] = aug
        kb_ref[0, rows, 3 * LANES:4 * LANES] = aug
    carry_ref[...] = jnp.broadcast_to(carry, carry_ref.shape)


def _in_proj(layer, x, n1, w_a, w_b, w_c, vecs, cos, sin, tri, e3, w_ff1, w_ff2):
    B, S, _ = x.shape
    tm = min(TM_IN, S)
    n_i = S // tm
    grid = (B, n_i)
    steps = B * n_i
    assert D_MODEL % (16 * steps) == 0, "FFN weight slabs must be whole bf16 sublane tiles"
    r1, r2 = D_MODEL // steps, D_FF // steps
    tok = lambda w: pl.BlockSpec((1, tm, w), lambda b, i: (b, i, 0))
    out_shapes = (
        jax.ShapeDtypeStruct((B, S, SWA_Q), BF16),
        jax.ShapeDtypeStruct((B, S, 2 * SWA_KV), BF16),
        jax.ShapeDtypeStruct((B, S, FOX_W), BF16),
        jax.ShapeDtypeStruct((B, S, 2 * MXU_DIM), BF16),
        jax.ShapeDtypeStruct((B, S, FOX_W), BF16),
        jax.ShapeDtypeStruct((B, S, 3 * MLSTM_W), BF16),
        jax.ShapeDtypeStruct((B, S, 3 * MLSTM_W), F32),
        jax.ShapeDtypeStruct((D_MODEL, D_FF), BF16),
        jax.ShapeDtypeStruct((D_FF, D_MODEL), BF16),
    )
    slab_out = [pl.BlockSpec((r1, D_FF), lambda b, i: (b * n_i + i, 0)),
                pl.BlockSpec((r2, D_MODEL), lambda b, i: (b * n_i + i, 0))]
    return pl.pallas_call(
        _in_proj_body,
        grid=grid,
        in_specs=[
            tok(D_MODEL),
            _layer_spec((1, D_MODEL), layer),
            _layer_spec((D_MODEL, W_A), layer),
            _layer_spec((D_MODEL, W_B), layer),
            _layer_spec((D_MODEL, W_C), layer),
            _layer_spec((8, MLSTM_W), layer),
            pl.BlockSpec((tm, LANES), lambda b, i: (i, 0)),
            pl.BlockSpec((tm, LANES), lambda b, i: (i, 0)),
            _const_spec((MXU_DIM, MXU_DIM)),
            _const_spec((3 * LANES, 2 * MLSTM_W)),
            pl.BlockSpec((None, r1, D_FF), lambda b, i: (layer, b * n_i + i, 0)),
            pl.BlockSpec((None, r2, D_MODEL), lambda b, i: (layer, b * n_i + i, 0)),
        ],
        out_specs=[tok(s.shape[-1]) for s in out_shapes[:-2]] + slab_out,
        out_shape=out_shapes,
        scratch_shapes=[pltpu.VMEM((8, LANES), F32)],
        compiler_params=pltpu.CompilerParams(
            dimension_semantics=("arbitrary", "arbitrary"), vmem_limit_bytes=VMEM_LIMIT),
        name="in_proj",
    )(x, n1, w_a, w_b, w_c, vecs, cos, sin, tri, e3, w_ff1, w_ff2)


TS_SWA = 1024
SWA_DEPTH = 5
VT_ROWS = HEAD + 16


def _swa_body(sink_base, sink_ref, q_ref, kvc_ref, kvp_ref, band_ref, o_ref):
    step = pl.program_id(1)
    nblk = q_ref.shape[1] // QBLK
    per_group = SWA_HEADS // SWA_KV_HEADS

    band_cap = band_ref[...]
    key = lax.broadcasted_iota(jnp.int32, (2 * QBLK, MXU_DIM), 0)
    first_cap = jnp.where((key >= QBLK) | (step > 0), band_cap, -jnp.inf)
    lane = lax.broadcasted_iota(jnp.int32, (QBLK, LANES), 1)
    lane_a = (lane // HALF) % 2 == 0
    first_head = lax.broadcasted_iota(jnp.int32, (1, MXU_DIM), 1) < QBLK
    zero = jnp.zeros((QBLK, LANES), BF16)
    ones = jnp.ones((VT_ROWS - HEAD, 2 * QBLK), BF16)
    kv_all = jnp.concatenate([kvp_ref[0], kvc_ref[0]], axis=0)
    rows = [slice(blk * QBLK, (blk + 1) * QBLK) for blk in range(nblk)]
    bands = [kv_all[blk * QBLK:(blk + 2) * QBLK] for blk in range(nblk)]

    vt_aug = []
    for blk in range(nblk):
        v_t = bands[blk][:, LANES:].astype(F32).T.astype(BF16)
        vt_aug.append([jnp.concatenate([v_t[g * HEAD:(g + 1) * HEAD], ones], axis=0)
                       for g in range(SWA_KV_HEADS)])

    units = [(blk, g, part) for blk in range(nblk) for g in range(SWA_KV_HEADS) for part in range(2)]

    def score(blk, g, part):
        keep = lane_a if g == 0 else jnp.logical_not(lane_a)
        q_cols = [q_ref[0, rows[blk], j * LANES:(j + 1) * LANES] for j in (2 * part, 2 * part + 1)]
        q_g = jnp.concatenate([jnp.where(keep, q2, zero) for q2 in q_cols], axis=0)
        return _dot_nt(bands[blk][:, 0:LANES], q_g)

    def fold(s, blk, g, part):
        if blk > 0:
            s = jnp.concatenate([jnp.minimum(s[0:CHUNK], band_cap[0:CHUNK]), s[CHUNK:3 * CHUNK],
                                 jnp.minimum(s[3 * CHUNK:], band_cap[3 * CHUNK:])], axis=0)
        else:
            s = jnp.minimum(s, first_cap)
        head = sink_base + g * per_group + 2 * part
        sink = jnp.where(first_head, sink_ref[head], sink_ref[head + 1]) * LOG2E
        m = jnp.maximum(jnp.max(s, axis=0, keepdims=True), sink)
        p = jnp.exp2(s - m)
        o_aug = _dot(vt_aug[blk][g], p.astype(BF16))
        den = o_aug[HEAD:HEAD + 1] + jnp.exp2(sink - m)
        return o_aug[0:HEAD] / den

    def write_block(blk, outs):
        for g in range(SWA_KV_HEADS):
            for part in range(2):
                o = outs[(blk, g, part)]
                pair_t = jnp.concatenate([o[:, 0:QBLK], o[:, QBLK:]], axis=0)
                j = 2 * g + part
                o_ref[0, rows[blk], j * LANES:(j + 1) * LANES] = pair_t.T.astype(BF16)

    outs = {}
    queue = [score(*unit) for unit in units[:SWA_DEPTH]]
    for n, unit in enumerate(units):
        s_cur = queue.pop(0)
        if n + SWA_DEPTH < len(units):
            queue.append(score(*units[n + SWA_DEPTH]))
        outs[unit] = fold(s_cur, *unit)
        blk = unit[0]
        if unit[1:] == (SWA_KV_HEADS - 1, 1) and blk > 0:
            write_block(blk - 1, outs)
    write_block(nblk - 1, outs)


def _swa(layer, sinks, qa, kva, band):
    B, S, _ = qa.shape
    ts = min(TS_SWA, S)
    per_step = ts // QBLK
    return pl.pallas_call(
        functools.partial(_swa_body, layer * SWA_HEADS),
        grid=(B, S // ts),
        in_specs=[
            pl.BlockSpec(memory_space=pltpu.SMEM),
            pl.BlockSpec((1, ts, SWA_Q), lambda b, i: (b, i, 0)),
            pl.BlockSpec((1, ts, 2 * SWA_KV), lambda b, i: (b, i, 0)),
            pl.BlockSpec((1, QBLK, 2 * SWA_KV), lambda b, i: (b, jnp.maximum(i * per_step - 1, 0), 0)),
            _const_spec((2 * QBLK, MXU_DIM)),
        ],
        out_specs=pl.BlockSpec((1, ts, SWA_Q), lambda b, i: (b, i, 0)),
        out_shape=jax.ShapeDtypeStruct((B, S, SWA_Q), BF16),
        compiler_params=pltpu.CompilerParams(
            dimension_semantics=("arbitrary", "arbitrary"), vmem_limit_bytes=VMEM_LIMIT),
        name="swa",
    )(sinks, qa, kva, kva, band)


TQ_FOX = 1024
TK_FOX = 256
NEG_BIG = -1e30


FOX_UNROLL = 4
FOX_DEPTH = 10


def _fox_body(q_ref, k_ref, v_ref, o_ref, vt_ref, qt_ref, acc_ref):
    tq = q_ref.shape[1]
    tk = TK_FOX
    S = k_ref.shape[1]
    pair = pl.program_id(1)
    qi = pl.program_id(2)
    n_tiles = 2 * tq // MXU_DIM
    tiles_per_head = tq // MXU_DIM

    @pl.when(qi == 0)
    def _():
        ones = jnp.ones((VT_ROWS - HEAD, tk), BF16)
        for j in range(S // tk):
            blk = slice(j * tk, (j + 1) * tk)
            v_t = v_ref[0, blk, :].astype(F32).T.astype(BF16)
            vt_ref[:, blk] = jnp.concatenate([v_t[0:HEAD], ones, v_t[HEAD:], ones], axis=0)

    lane = lax.broadcasted_iota(jnp.int32, (tq, LANES), 1)
    lane_a = lane < HEAD
    q2 = q_ref[0]
    zero = jnp.zeros_like(q2)

    def minus_one_at(head):
        hit = (lane == head) | (lane == head + 4) | (lane == head + 8)
        return jnp.where(hit, -1.0, 0.0).astype(BF16)

    q_aug = [jnp.concatenate([jnp.where(lane_a, q2, zero), minus_one_at(2 * pair)], axis=1),
             jnp.concatenate([jnp.where(lane_a, zero, q2), minus_one_at(2 * pair + 1)], axis=1)]
    for t in range(n_tiles):
        hd, part = divmod(t, tiles_per_head)
        rows = q_aug[hd][part * MXU_DIM:(part + 1) * MXU_DIM]
        qt_ref[:, t * MXU_DIM:(t + 1) * MXU_DIM] = rows.astype(F32).T.astype(BF16)
    acc_ref[...] = jnp.zeros_like(acc_ref)

    key = lax.broadcasted_iota(jnp.int32, (tk // 2, MXU_DIM), 0)
    qry = lax.broadcasted_iota(jnp.int32, (tk // 2, MXU_DIM), 1)

    def fold(s, t, vt_blk, m, triangular):
        hd = t // tiles_per_head
        vt_h = vt_blk[hd * VT_ROWS:(hd + 1) * VT_ROWS]
        half = tk // 2
        for h in range(2):
            sh = s[h * half:(h + 1) * half]
            if triangular:
                valid = key + h * half <= qry
                sh = jnp.where(valid, sh, NEG_BIG)
            m_new = jnp.maximum(m, jnp.max(sh, axis=0, keepdims=True))
            alpha = jnp.exp2(m - m_new)
            p = jnp.exp2(sh - m_new)
            if triangular:
                p = jnp.where(valid, p, 0.0)
            acc_ref[t] = alpha * acc_ref[t] + _dot(vt_h[:, h * half:(h + 1) * half], p.astype(BF16))
            m = m_new
        return m

    def region(j0, ms, plans):
        k_blks, vt_blks, todo = [], [], []
        for b, plan in enumerate(plans):
            r0 = pl.multiple_of((j0 + b) * tk, tk)
            k_blks.append(k_ref[0, pl.ds(r0, tk), :])
            vt_blks.append(vt_ref[:, pl.ds(r0, tk)])
            todo += [(b, t) for t in range(n_tiles) if plan[t] is not None]
        score = lambda b, t: _dot(k_blks[b], qt_ref[:, t * MXU_DIM:(t + 1) * MXU_DIM])
        ms = list(ms)
        queue = [score(*unit) for unit in todo[:FOX_DEPTH]]
        for n, (b, t) in enumerate(todo):
            s_cur = queue.pop(0)
            if n + FOX_DEPTH < len(todo):
                queue.append(score(*todo[n + FOX_DEPTH]))
            ms[t] = fold(s_cur, t, vt_blks[b], ms[t], plans[b][t])
        return tuple(ms)

    blocks_per_tile = tq // tk
    full = (False,) * n_tiles
    n_full = blocks_per_tile * qi
    m_init = jnp.full((1, MXU_DIM), NEG_BIG, F32)
    ms = lax.fori_loop(0, n_full // FOX_UNROLL,
                       lambda g, c: region(g * FOX_UNROLL, c, (full,) * FOX_UNROLL), (m_init,) * n_tiles)
    if FOX_UNROLL > blocks_per_tile:
        done = (n_full // FOX_UNROLL) * FOX_UNROLL
        ms = lax.fori_loop(0, (n_full - done) // blocks_per_tile,
                           lambda g, c: region(done + g * blocks_per_tile, c, (full,) * blocks_per_tile), ms)
    diag = [tuple(None if t % tiles_per_head < d else t % tiles_per_head == d for t in range(n_tiles))
            for d in range(blocks_per_tile)]
    ms = region(n_full, ms, diag)

    for hd in range(2):
        cols = []
        for part in range(tiles_per_head):
            acc = acc_ref[hd * tiles_per_head + part]
            cols.append(acc[0:HEAD] / acc[HEAD:HEAD + 1])
        o_h = jnp.concatenate(cols, axis=1)
        if hd == 0:
            o_t = o_h
        else:
            o_t = jnp.concatenate([o_t, o_h], axis=0)
    for c in range(tq // MXU_DIM):
        rows = slice(c * MXU_DIM, (c + 1) * MXU_DIM)
        o_ref[0, rows, :] = o_t[:, rows].T.astype(BF16)


def _fox(qb, kb, vb):
    B, S, _ = qb.shape
    tq = min(TQ_FOX, S)
    return pl.pallas_call(
        _fox_body,
        grid=(B, FOX_HEADS // 2, S // tq),
        in_specs=[
            pl.BlockSpec((1, tq, LANES), lambda b, p, i: (b, i, p)),
            pl.BlockSpec((1, S, MXU_DIM), lambda b, p, i: (b, 0, p)),
            pl.BlockSpec((1, S, LANES), lambda b, p, i: (b, 0, p)),
        ],
        out_specs=pl.BlockSpec((1, tq, LANES), lambda b, p, i: (b, i, p)),
        out_shape=jax.ShapeDtypeStruct((B, S, FOX_W), BF16),
        scratch_shapes=[
            pltpu.VMEM((2 * VT_ROWS, S), BF16),
            pltpu.VMEM((MXU_DIM, 2 * tq), BF16),
            pltpu.VMEM((2 * tq // MXU_DIM, VT_ROWS, MXU_DIM), F32),
        ],
        compiler_params=pltpu.CompilerParams(
            dimension_semantics=("arbitrary", "arbitrary", "arbitrary"),
            vmem_limit_bytes=VMEM_LIMIT),
        name="fox",
    )(qb, kb, vb)


TS_MLSTM = 1024


def _mlstm_body(qkv_ref, g_ref, onorm_ref, tri_ref, i4_ref, bd_ref, o_ref, c_ref, n_ref, m_ref):
    ts = qkv_ref.shape[1]
    W = MLSTM_W

    @pl.when(pl.program_id(1) == 0)
    def _():
        c_ref[...] = jnp.zeros_like(c_ref)
        n_ref[...] = jnp.zeros_like(n_ref)
        m_ref[...] = jnp.zeros_like(m_ref)

    row = lax.broadcasted_iota(jnp.int32, (CHUNK, W), 0)
    lane = lax.broadcasted_iota(jnp.int32, (CHUNK, W), 1)
    causal = (lane % HEAD) <= row
    bd = bd_ref[...]
    bd2 = jnp.concatenate([bd, bd], axis=1)
    i4_bf = i4_ref[...].astype(BF16)

    def block_diag(x):
        tiled = jnp.concatenate([x] * MLSTM_HEADS, axis=0)
        return tiled * (bd if x.shape[1] == W else bd2)

    first_half = lax.broadcasted_iota(jnp.int32, (1, LANES), 1) < HEAD

    def seg_max(d):
        tiles = []
        for j in range(W // LANES):
            tile = d[:, j * LANES:(j + 1) * LANES]
            m_first = jnp.max(jnp.where(first_half, tile, -jnp.inf), axis=-1, keepdims=True)
            m_second = jnp.max(jnp.where(first_half, -jnp.inf, tile), axis=-1, keepdims=True)
            tiles.append(jnp.where(first_half, m_first, m_second))
        return jnp.concatenate(tiles, axis=1)

    chunks = range(ts // CHUNK)
    rows = [pl.ds(c * CHUNK, CHUNK) for c in chunks]

    b, g, u, dm, dm_max, a_max, wa = [], [], [], [], [], [], []
    for c in chunks:
        ig = g_ref[0, rows[c], 0:W]
        lf_hi, lf_lo = _split2(g_ref[0, rows[c], W:2 * W])
        b_c = _dot(tri_ref[...], lf_hi) + _dot(tri_ref[...], lf_lo)
        g_c = b_c[CHUNK - 1:CHUNK, :]
        u_c = b_c - ig
        u_row = jnp.sum(u_c * i4_ref[...], axis=0, keepdims=True)
        dm_c = jnp.where(causal, b_c - u_row, -jnp.inf)
        a_c = g_c - u_c
        am_c = jnp.max(a_c, axis=0, keepdims=True)
        b.append(b_c); g.append(g_c); u.append(u_c); dm.append(dm_c)
        dm_max.append(seg_max(dm_c)); a_max.append(am_c); wa.append(jnp.exp(a_c - am_c))

    m_prev, s_old, s_loc = [m_ref[0:1, :]], [], []
    for c in chunks:
        gm = g[c] + m_prev[c]
        m_new = jnp.maximum(gm, a_max[c])
        s_old.append(jnp.exp(gm - m_new))
        s_loc.append(jnp.exp(a_max[c] - m_new))
        m_prev.append(m_new)

    m_t, w_inter, sm_hi, sm_lo, k_t = [], [], [], [], []
    for c in chunks:
        inter = b[c] + m_prev[c]
        mt_c = jnp.maximum(inter, dm_max[c])
        decay = jnp.exp(dm[c] - mt_c)
        q = qkv_ref[0, rows[c], 0:W]
        k = qkv_ref[0, rows[c], W:2 * W]
        qk = _dot_nt(jnp.concatenate([q, i4_bf], axis=0), block_diag(k))
        hi, lo = _split2(qk[0:CHUNK] * decay)
        m_t.append(mt_c); w_inter.append(jnp.exp(inter - mt_c)); sm_hi.append(hi); sm_lo.append(lo)
        k_t.append(qk[CHUNK:].astype(BF16))

    c_cur, n_cur = c_ref[...], n_ref[...]
    state_bf = []
    for c in chunks:
        state_bf.append(jnp.concatenate([c_cur.astype(BF16), n_cur.astype(BF16)], axis=1))
        v = qkv_ref[0, rows[c], 2 * W:]
        weighted = jnp.concatenate([(wa[c] * v.astype(F32)).astype(BF16), wa[c].astype(BF16)], axis=1)
        loc = _dot(k_t[c], block_diag(weighted))
        c_cur = s_old[c] * c_cur + s_loc[c] * loc[:, 0:W]
        n_cur = s_old[c] * n_cur + s_loc[c] * loc[:, W:]
    c_ref[...] = c_cur
    n_ref[...] = n_cur
    m_ref[...] = jnp.broadcast_to(m_prev[-1], m_ref.shape)

    hval = []
    for c in chunks:
        q = qkv_ref[0, rows[c], 0:W]
        v = qkv_ref[0, rows[c], 2 * W:]
        inter_cn = _dot(q, block_diag(state_bf[c]))
        intra = _dot(sm_hi[c], jnp.concatenate([block_diag(v), bd], axis=1))
        num = w_inter[c] * inter_cn[:, 0:W] + intra[:, 0:W]
        den = w_inter[c] * inter_cn[:, W:] + intra[:, W:] + _dot(sm_lo[c], bd)
        hval.append(num / jnp.maximum(jnp.abs(den), jnp.exp(-m_t[c])))

    for c in chunks:
        hs_hi, hs_lo = _split2(hval[c] * hval[c])
        ss = _dot(hs_hi, bd) + _dot(hs_lo, bd)
        hc = (hval[c] * lax.rsqrt(ss * (1.0 / HEAD) + EPS)) * onorm_ref[...]
        co = g_ref[0, rows[c], 2 * W:]
        o_ref[0, rows[c], :] = (jax.nn.sigmoid(co) * hc).astype(BF16)


def _mlstm(layer, qkvc, gc, onorm, tri, i4, bd):
    B, S, _ = qkvc.shape
    ts = min(TS_MLSTM, S)
    return pl.pallas_call(
        _mlstm_body,
        grid=(B, S // ts),
        in_specs=[
            pl.BlockSpec((1, ts, 3 * MLSTM_W), lambda b, i: (b, i, 0)),
            pl.BlockSpec((1, ts, 3 * MLSTM_W), lambda b, i: (b, i, 0)),
            _layer_spec((1, MLSTM_W), layer),
            _const_spec((CHUNK, CHUNK)),
            _const_spec((CHUNK, MLSTM_W)),
            _const_spec((MLSTM_W, MLSTM_W)),
        ],
        out_specs=pl.BlockSpec((1, ts, MLSTM_W), lambda b, i: (b, i, 0)),
        out_shape=jax.ShapeDtypeStruct((B, S, MLSTM_W), BF16),
        scratch_shapes=[pltpu.VMEM((HEAD, MLSTM_W), F32), pltpu.VMEM((HEAD, MLSTM_W), F32),
                        pltpu.VMEM((8, MLSTM_W), F32)],
        compiler_params=pltpu.CompilerParams(
            dimension_semantics=("arbitrary", "arbitrary"), vmem_limit_bytes=VMEM_LIMIT),
        name="mlstm",
    )(qkvc, gc, onorm, tri, i4, bd)


TM_FFN = 1024
FF_CHUNK = 1024


def _out_ffn_body(x_ref, ya_ref, yb_ref, yc_ref, wo_ref, n2_ref, w1_ref, w2_ref, o_ref, a_ref):
    x1 = (x_ref[...] + _dot(ya_ref[...], wo_ref[0:SWA_Q, :])
          + _dot(yb_ref[...], wo_ref[SWA_Q:SWA_Q + FOX_W, :])
          + _dot(yc_ref[...], wo_ref[SWA_Q + FOX_W:, :]))
    ms = jnp.mean(x1 * x1, axis=-1, keepdims=True)
    h2 = ((x1 * lax.rsqrt(ms + EPS)) * n2_ref[...]).astype(BF16)
    for j in range(D_FF // FF_CHUNK):
        cols = slice(j * FF_CHUNK, (j + 1) * FF_CHUNK)
        r = jnp.maximum(_dot(h2, w1_ref[:, cols]), 0.0)
        a_ref[:, cols] = (r * r).astype(BF16)
    o_ref[...] = x1 + _dot(a_ref[...], w2_ref[...])


def _out_ffn(layer, x2d, ya, yb, yc, wo, n2, w1, w2):
    T = x2d.shape[0]
    tm = min(TM_FFN, T)
    tok = lambda w: pl.BlockSpec((tm, w), lambda i: (i, 0))
    return pl.pallas_call(
        _out_ffn_body,
        grid=(T // tm,),
        in_specs=[
            tok(D_MODEL), tok(SWA_Q), tok(FOX_W), tok(MLSTM_W),
            _layer_spec((D_MODEL, D_MODEL), layer),
            _layer_spec((1, D_MODEL), layer),
            _const_spec((D_MODEL, D_FF)),
            _const_spec((D_FF, D_MODEL)),
        ],
        out_specs=tok(D_MODEL),
        out_shape=jax.ShapeDtypeStruct((T, D_MODEL), F32),
        scratch_shapes=[pltpu.VMEM((tm, D_FF), BF16)],
        compiler_params=pltpu.CompilerParams(
            dimension_semantics=("arbitrary",), vmem_limit_bytes=VMEM_LIMIT),
        name="out_ffn",
    )(x2d, ya, yb, yc, wo, n2, w1, w2)


def _rope_tables(S):
    inv = ROPE_THETA ** (-np.arange(HALF, dtype=np.float64) / HALF)
    ang = np.arange(S, dtype=np.float64)[:, None] * inv[None, :]
    cos = np.tile(np.cos(ang), (1, LANES // HALF))
    sin = np.tile(np.sin(ang), (1, LANES // HALF))
    sign = np.where(np.arange(LANES) < HEAD, -1.0, 1.0)
    return jnp.asarray(cos, F32), jnp.asarray(sin * sign[None, :], F32)


def _prep_w_in(w_in):
    select = lambda w, p: jnp.einsum("ldk,kn->ldn", w.astype(BF16), jnp.asarray(p, BF16),
                                     preferred_element_type=BF16)
    w_b = w_in[:, :, W_A:W_A + W_B].astype(BF16)
    return select(w_in[:, :, :W_A], _PERM_A), w_b, select(w_in[:, :, W_A + W_B:], _PERM_C)


def _prep_vecs(swa_q_norm, swa_k_norm, fox_q_norm, fox_k_norm, fox_f_bias, mlstm_i_bias, mlstm_f_bias):
    L = swa_q_norm.shape[0]

    def swa_lanes(g):
        return jnp.concatenate([g[:, :HALF], g[:, :HALF], g[:, HALF:], g[:, HALF:]], axis=-1)

    pad = lambda v: jnp.pad(v, ((0, 0), (0, MLSTM_W - v.shape[1])))
    return jnp.stack([
        pad(swa_lanes(swa_q_norm)), pad(swa_lanes(swa_k_norm)),
        pad(jnp.tile(fox_q_norm, (1, 2))), pad(jnp.tile(fox_k_norm, (1, 2))),
        pad(jnp.concatenate([jnp.tile(fox_f_bias, (1, 3)), mlstm_i_bias, mlstm_f_bias], axis=-1)),
        jnp.zeros((L, MLSTM_W), F32), jnp.zeros((L, MLSTM_W), F32), jnp.zeros((L, MLSTM_W), F32),
    ], axis=1).astype(F32)


def kernel(x, norm1, w_in, swa_q_norm, swa_k_norm, swa_sinks, fox_q_norm, fox_k_norm, fox_f_bias,
           mlstm_i_bias, mlstm_f_bias, mlstm_out_norm, w_out, norm2, w_ff1, w_ff2):
    B, S, D = x.shape
    L = norm1.shape[0]
    T = B * S
    cos, sin = _rope_tables(S)
    tri256, tri64, bd256 = (jnp.asarray(c, BF16) for c in (_TRI256, _TRI64, _BD256))
    i4 = jnp.asarray(_I4, F32)
    e3 = jnp.asarray(_gate_expand(), BF16)
    band = jnp.asarray(_swa_band_cap(), F32)

    w_a, w_b, w_c = _prep_w_in(w_in)
    wo = w_out.astype(BF16)
    vecs = _prep_vecs(swa_q_norm, swa_k_norm, fox_q_norm, fox_k_norm, fox_f_bias, mlstm_i_bias,
                      mlstm_f_bias)
    n1 = norm1.reshape(L, 1, D).astype(F32)
    n2 = norm2.reshape(L, 1, D).astype(F32)
    onorm = mlstm_out_norm.reshape(L, 1, MLSTM_W).astype(F32)
    sinks = swa_sinks.reshape(L * SWA_HEADS).astype(F32)

    for l in range(L):
        qa, kva, qb, kb, vb, qkvc, gc, w1, w2 = _in_proj(
            l, x, n1, w_a, w_b, w_c, vecs, cos, sin, tri256, e3, w_ff1, w_ff2)
        ya = _swa(l, sinks, qa, kva, band)
        yb = _fox(qb, kb, vb)
        yc = _mlstm(l, qkvc, gc, onorm, tri64, i4, bd256)
        x = _out_ffn(l, x.reshape(T, D), ya.reshape(T, SWA_Q), yb.reshape(T, FOX_W),
                     yc.reshape(T, MLSTM_W), wo, n2, w1, w2).reshape(B, S, D)
    return x
```

```python
import functools
import math

import numpy as np
import jax
import jax.numpy as jnp
from jax import lax
from jax.experimental import pallas as pl
from jax.experimental.pallas import tpu as pltpu

F32 = jnp.float32
BF16 = jnp.bfloat16

D_MODEL = 1024
HEAD = 64
HALF = HEAD // 2
CHUNK = 64
SWA_HEADS = 8
SWA_KV_HEADS = 2
FOX_HEADS = 4
MLSTM_HEADS = 4
SWA_Q = SWA_HEADS * HEAD
SWA_KV = SWA_KV_HEADS * HEAD
FOX_W = FOX_HEADS * HEAD
MLSTM_W = MLSTM_HEADS * HEAD
D_FF = 4 * D_MODEL
QBLK = 128
ROPE_THETA = 10000.0
EPS = 1e-6
QK_SCALE = 1.0 / math.sqrt(HEAD)
LOG2E = math.log2(math.e)

LANES = 128
MXU_DIM = 256
VMEM_LIMIT = 56 * 1024 * 1024

W_A = SWA_Q + 2 * SWA_KV
W_B = 3 * FOX_W
SEC_C = 0
SEC_CO = SEC_C + 768
SEC_G = SEC_CO + 256
W_C = SEC_G + LANES
GATE_BF = 0
GATE_CI = 12
GATE_CF = 16
IN_WIDTH = W_A + W_B + FOX_HEADS + 4 * MLSTM_W + 2 * MLSTM_HEADS


def _perm_a():
    p = np.zeros((W_A, W_A), np.float32)
    lane = np.arange(LANES)
    quarter, i = lane // HALF, lane % HALF
    dim, is_b = i + HALF * (quarter // 2), quarter % 2
    for j in range(SWA_HEADS // 2):
        p[(j + 4 * is_b) * HEAD + dim, j * LANES + lane] = 1.0
    p[SWA_Q + is_b * HEAD + dim, SWA_Q + lane] = 1.0
    p[SWA_Q + SWA_KV + lane, SWA_Q + SWA_KV + lane] = 1.0
    return p


def _perm_c():
    src = IN_WIDTH - W_A - W_B
    o_bf, o_cqkv = 0, FOX_HEADS
    o_ci = o_cqkv + 3 * MLSTM_W
    o_cf, o_co = o_ci + MLSTM_HEADS, o_ci + 2 * MLSTM_HEADS
    p = np.zeros((src, W_C), np.float32)
    w = np.arange(3 * MLSTM_W)
    p[o_cqkv + w, SEC_C + w] = 1.0
    w = np.arange(MLSTM_W)
    p[o_co + w, SEC_CO + w] = 1.0
    h = np.arange(FOX_HEADS)
    for rep in range(3):
        p[o_bf + h, SEC_G + GATE_BF + rep * FOX_HEADS + h] = 1.0
    h = np.arange(MLSTM_HEADS)
    p[o_ci + h, SEC_G + GATE_CI + h] = 1.0
    p[o_cf + h, SEC_G + GATE_CF + h] = 1.0
    return p


def _gate_expand():
    e = np.zeros((LANES, 2 * MLSTM_W), np.float32)
    for h in range(MLSTM_HEADS):
        e[GATE_CI + h, h * HEAD:(h + 1) * HEAD] = 1.0
        e[GATE_CF + h, MLSTM_W + h * HEAD:MLSTM_W + (h + 1) * HEAD] = 1.0
    return np.concatenate([e, e, e], axis=0)


_PERM_A = _perm_a()
_PERM_C = _perm_c()


def _block_diag_ones(head_of_lane):
    return (head_of_lane[:, None] == head_of_lane[None, :]).astype(np.float32)


_TRI256 = np.tril(np.ones((MXU_DIM, MXU_DIM), np.float32))
_TRI64 = np.tril(np.ones((CHUNK, CHUNK), np.float32))
_w256 = np.arange(MLSTM_W)
_BD256 = _block_diag_ones(_w256 // HEAD)
_I4 = (np.arange(CHUNK)[:, None] == (_w256 % HEAD)[None, :]).astype(np.float32)


def _swa_band_cap():
    key_chunk = np.arange(2 * QBLK)[:, None] // CHUNK
    qry_chunk = (np.arange(MXU_DIM)[None, :] % QBLK) // CHUNK
    ahead = key_chunk - qry_chunk
    return np.where((ahead >= 0) & (ahead <= 2), np.inf, -np.inf).astype(np.float32)


def _const_spec(shape):
    nd = len(shape)
    return pl.BlockSpec(shape, lambda *_: (0,) * nd, pipeline_mode=pl.Buffered(1))


def _layer_spec(shape, layer):
    nd = len(shape)
    return pl.BlockSpec((None,) + tuple(shape), lambda *_: (layer,) + (0,) * nd,
                        pipeline_mode=pl.Buffered(1))


def _log_sigmoid(x):
    return jnp.minimum(x, 0.0) - jnp.log(1.0 + jnp.exp(-jnp.abs(x)))


def _split2(x):
    hi = x.astype(BF16)
    lo = (x - hi.astype(F32)).astype(BF16)
    return hi, lo


def _split3_by_lane(x, lane):
    hi = x.astype(BF16).astype(F32)
    r1 = x - hi
    mid = r1.astype(BF16).astype(F32)
    low = r1 - mid
    return jnp.where(lane < 4, hi, jnp.where(lane < 8, mid, low)).astype(BF16)


def _dot(a, b):
    return jnp.dot(a, b, preferred_element_type=F32)


def _dot_nt(a, b):
    return lax.dot_general(a, b, (((1,), (1,)), ((), ())), preferred_element_type=F32)


TM_IN = 1024


def _in_proj_body(x_ref, n1_ref, wa_ref, wb_ref, wc_ref, vec_ref, cos_ref, sin_ref, tri_ref, e3_ref,
                  wff1_ref, wff2_ref,
                  qa_ref, kva_ref, qb_ref, kb_ref, vb_ref, qkvc_ref, gc_ref, wff1_bf_ref, wff2_bf_ref,
                  carry_ref):
    tm = x_ref.shape[1]
    wff1_bf_ref[...] = wff1_ref[...].astype(BF16)
    wff2_bf_ref[...] = wff2_ref[...].astype(BF16)

    @pl.when(pl.program_id(1) == 0)
    def _():
        carry_ref[...] = jnp.zeros_like(carry_ref)

    x = x_ref[0]
    ms = jnp.mean(x * x, axis=-1, keepdims=True)
    h = ((x * lax.rsqrt(ms + EPS)) * n1_ref[...]).astype(BF16)

    def proj(w_ref, c0, width):
        return _dot(h, w_ref[:, c0:c0 + width])

    lane_row = lax.broadcasted_iota(jnp.int32, (1, LANES), 1)
    first_swa = (lane_row // HALF) % 2 == 0
    first_nat = lane_row < HEAD

    def head_norm(z, first, gain):
        sq = z * z
        s_first = jnp.sum(jnp.where(first, sq, 0.0), axis=-1, keepdims=True)
        s_second = jnp.sum(jnp.where(first, 0.0, sq), axis=-1, keepdims=True)
        ss = jnp.where(first, s_first, s_second)
        return (z * lax.rsqrt(ss * (1.0 / HEAD) + EPS)) * gain

    cos = cos_ref[...]
    sin = sin_ref[...]

    def rope(z):
        return z * cos + pltpu.roll(z, HEAD, 1) * sin

    g_aq = vec_ref[0:1, 0:LANES]
    g_ak = vec_ref[1:2, 0:LANES]
    g_bq = vec_ref[2:3, 0:LANES]
    g_bk = vec_ref[3:4, 0:LANES]
    gate_bias = vec_ref[4:5, 0:LANES]

    lane = lax.broadcasted_iota(jnp.int32, (MXU_DIM, LANES), 1)
    zg = proj(wc_ref, SEC_G, LANES) + gate_bias
    log_sig = _log_sigmoid(zg)
    log_f = jnp.where(lane[0:1] < GATE_CI, log_sig, 0.0)
    f_terms = [_split3_by_lane(log_f[blk * MXU_DIM:(blk + 1) * MXU_DIM], lane)
               for blk in range(tm // MXU_DIM)]
    gz = jnp.where(lane[0:1] < GATE_CI, 0.0,
                   jnp.where(lane[0:1] < GATE_CF, zg,
                             jnp.where(lane[0:1] < GATE_CF + MLSTM_HEADS, log_sig, 0.0))) * LOG2E
    gz_hi = gz.astype(BF16)
    gz_r = gz - gz_hi.astype(F32)
    gz_mid = gz_r.astype(BF16)
    gz_terms = jnp.concatenate([gz_hi, gz_mid, (gz_r - gz_mid.astype(F32)).astype(BF16)], axis=1)

    za = proj(wa_ref, 0, W_A)
    for j in range(SWA_HEADS // 2):
        q = rope(head_norm(za[:, j * LANES:(j + 1) * LANES], first_swa, g_aq)) * (QK_SCALE * LOG2E)
        qa_ref[0, :, j * LANES:(j + 1) * LANES] = q.astype(BF16)
    k = rope(head_norm(za[:, SWA_Q:SWA_Q + LANES], first_swa, g_ak))
    kva_ref[0, :, 0:LANES] = k.astype(BF16)
    kva_ref[0, :, LANES:2 * LANES] = za[:, SWA_Q + SWA_KV:].astype(BF16)

    zb = proj(wb_ref, 0, W_B)
    for p in range(FOX_HEADS // 2):
        q = head_norm(zb[:, p * LANES:(p + 1) * LANES], first_nat, g_bq) * (QK_SCALE * LOG2E)
        qb_ref[0, :, p * LANES:(p + 1) * LANES] = q.astype(BF16)
        k = head_norm(zb[:, FOX_W + p * LANES:FOX_W + (p + 1) * LANES], first_nat, g_bk)
        kb_ref[0, :, 2 * p * LANES:(2 * p + 1) * LANES] = k.astype(BF16)
    vb_ref[0] = zb[:, 2 * FOX_W:].astype(BF16)

    zc = proj(wc_ref, SEC_C, 768)
    qkvc_ref[0, :, 0:MLSTM_W] = zc[:, 0:MLSTM_W].astype(BF16)
    qkvc_ref[0, :, MLSTM_W:2 * MLSTM_W] = (zc[:, MLSTM_W:2 * MLSTM_W] * QK_SCALE).astype(BF16)
    qkvc_ref[0, :, 2 * MLSTM_W:] = zc[:, 2 * MLSTM_W:].astype(BF16)
    gc_ref[0, :, 2 * MLSTM_W:] = proj(wc_ref, SEC_CO, MLSTM_W)

    gc_ref[0, :, 0:2 * MLSTM_W] = _dot(gz_terms, e3_ref[...])

    carry = carry_ref[0:1, :]
    for blk in range(tm // MXU_DIM):
        rows = slice(blk * MXU_DIM, (blk + 1) * MXU_DIM)
        cs = _dot(tri_ref[...], f_terms[blk])
        f_loc = cs + pltpu.roll(cs, LANES - 4, 1) + pltpu.roll(cs, LANES - 8, 1)
        f_cum = jnp.where(lane < 4, f_loc, 0.0) + carry
        carry = f_cum[MXU_DIM - 1:MXU_DIM, :]
        f_rep = (f_cum + pltpu.roll(f_cum, 4, 1) + pltpu.roll(f_cum, 8, 1)) * LOG2E
        aug = _split3_by_lane(f_rep, lane)
        kb_ref[0, rows, LANES:2 * LANES] = aug
        kb_ref[0, rows, 3 * LANES:4 * LANES] = aug
    carry_ref[...] = jnp.broadcast_to(carry, carry_ref.shape)


def _in_proj(layer, x, n1, w_a, w_b, w_c, vecs, cos, sin, tri, e3, w_ff1, w_ff2):
    B, S, _ = x.shape
    tm = min(TM_IN, S)
    n_i = S // tm
    grid = (B, n_i)
    steps = B * n_i
    assert D_MODEL % (16 * steps) == 0, "FFN weight slabs must be whole bf16 sublane tiles"
    r1, r2 = D_MODEL // steps, D_FF // steps
    tok = lambda w: pl.BlockSpec((1, tm, w), lambda b, i: (b, i, 0))
    out_shapes = (
        jax.ShapeDtypeStruct((B, S, SWA_Q), BF16),
        jax.ShapeDtypeStruct((B, S, 2 * SWA_KV), BF16),
        jax.ShapeDtypeStruct((B, S, FOX_W), BF16),
        jax.ShapeDtypeStruct((B, S, 2 * MXU_DIM), BF16),
        jax.ShapeDtypeStruct((B, S, FOX_W), BF16),
        jax.ShapeDtypeStruct((B, S, 3 * MLSTM_W), BF16),
        jax.ShapeDtypeStruct((B, S, 3 * MLSTM_W), F32),
        jax.ShapeDtypeStruct((D_MODEL, D_FF), BF16),
        jax.ShapeDtypeStruct((D_FF, D_MODEL), BF16),
    )
    slab_out = [pl.BlockSpec((r1, D_FF), lambda b, i: (b * n_i + i, 0)),
                pl.BlockSpec((r2, D_MODEL), lambda b, i: (b * n_i + i, 0))]
    return pl.pallas_call(
        _in_proj_body,
        grid=grid,
        in_specs=[
            tok(D_MODEL),
            _layer_spec((1, D_MODEL), layer),
            _layer_spec((D_MODEL, W_A), layer),
            _layer_spec((D_MODEL, W_B), layer),
            _layer_spec((D_MODEL, W_C), layer),
            _layer_spec((8, MLSTM_W), layer),
            pl.BlockSpec((tm, LANES), lambda b, i: (i, 0)),
            pl.BlockSpec((tm, LANES), lambda b, i: (i, 0)),
            _const_spec((MXU_DIM, MXU_DIM)),
            _const_spec((3 * LANES, 2 * MLSTM_W)),
            pl.BlockSpec((None, r1, D_FF), lambda b, i: (layer, b * n_i + i, 0)),
            pl.BlockSpec((None, r2, D_MODEL), lambda b, i: (layer, b * n_i + i, 0)),
        ],
        out_specs=[tok(s.shape[-1]) for s in out_shapes[:-2]] + slab_out,
        out_shape=out_shapes,
        scratch_shapes=[pltpu.VMEM((8, LANES), F32)],
        compiler_params=pltpu.CompilerParams(
            dimension_semantics=("arbitrary", "arbitrary"), vmem_limit_bytes=VMEM_LIMIT),
        name="in_proj",
    )(x, n1, w_a, w_b, w_c, vecs, cos, sin, tri, e3, w_ff1, w_ff2)


TS_SWA = 1024
SWA_DEPTH = 5
VT_ROWS = HEAD + 16


def _swa_body(sink_base, sink_ref, q_ref, kvc_ref, kvp_ref, band_ref, o_ref):
    step = pl.program_id(1)
    nblk = q_ref.shape[1] // QBLK
    per_group = SWA_HEADS // SWA_KV_HEADS

    band_cap = band_ref[...]
    key = lax.broadcasted_iota(jnp.int32, (2 * QBLK, MXU_DIM), 0)
    first_cap = jnp.where((key >= QBLK) | (step > 0), band_cap, -jnp.inf)
    lane = lax.broadcasted_iota(jnp.int32, (QBLK, LANES), 1)
    lane_a = (lane // HALF) % 2 == 0
    first_head = lax.broadcasted_iota(jnp.int32, (1, MXU_DIM), 1) < QBLK
    zero = jnp.zeros((QBLK, LANES), BF16)
    ones = jnp.ones((VT_ROWS - HEAD, 2 * QBLK), BF16)
    kv_all = jnp.concatenate([kvp_ref[0], kvc_ref[0]], axis=0)
    rows = [slice(blk * QBLK, (blk + 1) * QBLK) for blk in range(nblk)]
    bands = [kv_all[blk * QBLK:(blk + 2) * QBLK] for blk in range(nblk)]

    vt_aug = []
    for blk in range(nblk):
        v_t = bands[blk][:, LANES:].astype(F32).T.astype(BF16)
        vt_aug.append([jnp.concatenate([v_t[g * HEAD:(g + 1) * HEAD], ones], axis=0)
                       for g in range(SWA_KV_HEADS)])

    units = [(blk, g, part) for blk in range(nblk) for g in range(SWA_KV_HEADS) for part in range(2)]

    def score(blk, g, part):
        keep = lane_a if g == 0 else jnp.logical_not(lane_a)
        q_cols = [q_ref[0, rows[blk], j * LANES:(j + 1) * LANES] for j in (2 * part, 2 * part + 1)]
        q_g = jnp.concatenate([jnp.where(keep, q2, zero) for q2 in q_cols], axis=0)
        return _dot_nt(bands[blk][:, 0:LANES], q_g)

    def fold(s, blk, g, part):
        if blk > 0:
            s = jnp.concatenate([jnp.minimum(s[0:CHUNK], band_cap[0:CHUNK]), s[CHUNK:3 * CHUNK],
                                 jnp.minimum(s[3 * CHUNK:], band_cap[3 * CHUNK:])], axis=0)
        else:
            s = jnp.minimum(s, first_cap)
        head = sink_base + g * per_group + 2 * part
        sink = jnp.where(first_head, sink_ref[head], sink_ref[head + 1]) * LOG2E
        m = jnp.maximum(jnp.max(s, axis=0, keepdims=True), sink)
        p = jnp.exp2(s - m)
        o_aug = _dot(vt_aug[blk][g], p.astype(BF16))
        den = o_aug[HEAD:HEAD + 1] + jnp.exp2(sink - m)
        return o_aug[0:HEAD] / den

    def write_block(blk, outs):
        for g in range(SWA_KV_HEADS):
            for part in range(2):
                o = outs[(blk, g, part)]
                pair_t = jnp.concatenate([o[:, 0:QBLK], o[:, QBLK:]], axis=0)
                j = 2 * g + part
                o_ref[0, rows[blk], j * LANES:(j + 1) * LANES] = pair_t.T.astype(BF16)

    outs = {}
    queue = [score(*unit) for unit in units[:SWA_DEPTH]]
    for n, unit in enumerate(units):
        s_cur = queue.pop(0)
        if n + SWA_DEPTH < len(units):
            queue.append(score(*units[n + SWA_DEPTH]))
        outs[unit] = fold(s_cur, *unit)
        blk = unit[0]
        if unit[1:] == (SWA_KV_HEADS - 1, 1) and blk > 0:
            write_block(blk - 1, outs)
    write_block(nblk - 1, outs)


def _swa(layer, sinks, qa, kva, band):
    B, S, _ = qa.shape
    ts = min(TS_SWA, S)
    per_step = ts // QBLK
    return pl.pallas_call(
        functools.partial(_swa_body, layer * SWA_HEADS),
        grid=(B, S // ts),
        in_specs=[
            pl.BlockSpec(memory_space=pltpu.SMEM),
            pl.BlockSpec((1, ts, SWA_Q), lambda b, i: (b, i, 0)),
            pl.BlockSpec((1, ts, 2 * SWA_KV), lambda b, i: (b, i, 0)),
            pl.BlockSpec((1, QBLK, 2 * SWA_KV), lambda b, i: (b, jnp.maximum(i * per_step - 1, 0), 0)),
            _const_spec((2 * QBLK, MXU_DIM)),
        ],
        out_specs=pl.BlockSpec((1, ts, SWA_Q), lambda b, i: (b, i, 0)),
        out_shape=jax.ShapeDtypeStruct((B, S, SWA_Q), BF16),
        compiler_params=pltpu.CompilerParams(
            dimension_semantics=("arbitrary", "arbitrary"), vmem_limit_bytes=VMEM_LIMIT),
        name="swa",
    )(sinks, qa, kva, kva, band)


TQ_FOX = 1024
TK_FOX = 256
NEG_BIG = -1e30


FOX_UNROLL = 4
FOX_DEPTH = 10


def _fox_body(q_ref, k_ref, v_ref, o_ref, vt_ref, qt_ref, acc_ref):
    tq = q_ref.shape[1]
    tk = TK_FOX
    S = k_ref.shape[1]
    pair = pl.program_id(1)
    qi = pl.program_id(2)
    n_tiles = 2 * tq // MXU_DIM
    tiles_per_head = tq // MXU_DIM

    @pl.when(qi == 0)
    def _():
        ones = jnp.ones((VT_ROWS - HEAD, tk), BF16)
        for j in range(S // tk):
            blk = slice(j * tk, (j + 1) * tk)
            v_t = v_ref[0, blk, :].astype(F32).T.astype(BF16)
            vt_ref[:, blk] = jnp.concatenate([v_t[0:HEAD], ones, v_t[HEAD:], ones], axis=0)

    lane = lax.broadcasted_iota(jnp.int32, (tq, LANES), 1)
    lane_a = lane < HEAD
    q2 = q_ref[0]
    zero = jnp.zeros_like(q2)

    def minus_one_at(head):
        hit = (lane == head) | (lane == head + 4) | (lane == head + 8)
        return jnp.where(hit, -1.0, 0.0).astype(BF16)

    q_aug = [jnp.concatenate([jnp.where(lane_a, q2, zero), minus_one_at(2 * pair)], axis=1),
             jnp.concatenate([jnp.where(lane_a, zero, q2), minus_one_at(2 * pair + 1)], axis=1)]
    for t in range(n_tiles):
        hd, part = divmod(t, tiles_per_head)
        rows = q_aug[hd][part * MXU_DIM:(part + 1) * MXU_DIM]
        qt_ref[:, t * MXU_DIM:(t + 1) * MXU_DIM] = rows.astype(F32).T.astype(BF16)
    acc_ref[...] = jnp.zeros_like(acc_ref)

    key = lax.broadcasted_iota(jnp.int32, (tk // 2, MXU_DIM), 0)
    qry = lax.broadcasted_iota(jnp.int32, (tk // 2, MXU_DIM), 1)

    def fold(s, t, vt_blk, m, triangular):
        hd = t // tiles_per_head
        vt_h = vt_blk[hd * VT_ROWS:(hd + 1) * VT_ROWS]
        half = tk // 2
        for h in range(2):
            sh = s[h * half:(h + 1) * half]
            if triangular:
                valid = key + h * half <= qry
                sh = jnp.where(valid, sh, NEG_BIG)
            m_new = jnp.maximum(m, jnp.max(sh, axis=0, keepdims=True))
            alpha = jnp.exp2(m - m_new)
            p = jnp.exp2(sh - m_new)
            if triangular:
                p = jnp.where(valid, p, 0.0)
            acc_ref[t] = alpha * acc_ref[t] + _dot(vt_h[:, h * half:(h + 1) * half], p.astype(BF16))
            m = m_new
        return m

    def region(j0, ms, plans):
        k_blks, vt_blks, todo = [], [], []
        for b, plan in enumerate(plans):
            r0 = pl.multiple_of((j0 + b) * tk, tk)
            k_blks.append(k_ref[0, pl.ds(r0, tk), :])
            vt_blks.append(vt_ref[:, pl.ds(r0, tk)])
            todo += [(b, t) for t in range(n_tiles) if plan[t] is not None]
        score = lambda b, t: _dot(k_blks[b], qt_ref[:, t * MXU_DIM:(t + 1) * MXU_DIM])
        ms = list(ms)
        queue = [score(*unit) for unit in todo[:FOX_DEPTH]]
        for n, (b, t) in enumerate(todo):
            s_cur = queue.pop(0)
            if n + FOX_DEPTH < len(todo):
                queue.append(score(*todo[n + FOX_DEPTH]))
            ms[t] = fold(s_cur, t, vt_blks[b], ms[t], plans[b][t])
        return tuple(ms)

    blocks_per_tile = tq // tk
    full = (False,) * n_tiles
    n_full = blocks_per_tile * qi
    m_init = jnp.full((1, MXU_DIM), NEG_BIG, F32)
    ms = lax.fori_loop(0, n_full // FOX_UNROLL,
                       lambda g, c: region(g * FOX_UNROLL, c, (full,) * FOX_UNROLL), (m_init,) * n_tiles)
    if FOX_UNROLL > blocks_per_tile:
        done = (n_full // FOX_UNROLL) * FOX_UNROLL
        ms = lax.fori_loop(0, (n_full - done) // blocks_per_tile,
                           lambda g, c: region(done + g * blocks_per_tile, c, (full,) * blocks_per_tile), ms)
    diag = [tuple(None if t % tiles_per_head < d else t % tiles_per_head == d for t in range(n_tiles))
            for d in range(blocks_per_tile)]
    ms = region(n_full, ms, diag)

    for hd in range(2):
        cols = []
        for part in range(tiles_per_head):
            acc = acc_ref[hd * tiles_per_head + part]
            cols.append(acc[0:HEAD] / acc[HEAD:HEAD + 1])
        o_h = jnp.concatenate(cols, axis=1)
        if hd == 0:
            o_t = o_h
        else:
            o_t = jnp.concatenate([o_t, o_h], axis=0)
    for c in range(tq // MXU_DIM):
        rows = slice(c * MXU_DIM, (c + 1) * MXU_DIM)
        o_ref[0, rows, :] = o_t[:, rows].T.astype(BF16)


def _fox(qb, kb, vb):
    B, S, _ = qb.shape
    tq = min(TQ_FOX, S)
    return pl.pallas_call(
        _fox_body,
        grid=(B, FOX_HEADS // 2, S // tq),
        in_specs=[
            pl.BlockSpec((1, tq, LANES), lambda b, p, i: (b, i, p)),
            pl.BlockSpec((1, S, MXU_DIM), lambda b, p, i: (b, 0, p)),
            pl.BlockSpec((1, S, LANES), lambda b, p, i: (b, 0, p)),
        ],
        out_specs=pl.BlockSpec((1, tq, LANES), lambda b, p, i: (b, i, p)),
        out_shape=jax.ShapeDtypeStruct((B, S, FOX_W), BF16),
        scratch_shapes=[
            pltpu.VMEM((2 * VT_ROWS, S), BF16),
            pltpu.VMEM((MXU_DIM, 2 * tq), BF16),
            pltpu.VMEM((2 * tq // MXU_DIM, VT_ROWS, MXU_DIM), F32),
        ],
        compiler_params=pltpu.CompilerParams(
            dimension_semantics=("arbitrary", "arbitrary", "arbitrary"),
            vmem_limit_bytes=VMEM_LIMIT),
        name="fox",
    )(qb, kb, vb)


TS_MLSTM = 1024


def _mlstm_body(qkv_ref, g_ref, onorm_ref, tri_ref, i4_ref, bd_ref, o_ref, c_ref, n_ref, m_ref):
    ts = qkv_ref.shape[1]
    W = MLSTM_W

    @pl.when(pl.program_id(1) == 0)
    def _():
        c_ref[...] = jnp.zeros_like(c_ref)
        n_ref[...] = jnp.zeros_like(n_ref)
        m_ref[...] = jnp.zeros_like(m_ref)

    row = lax.broadcasted_iota(jnp.int32, (CHUNK, W), 0)
    lane = lax.broadcasted_iota(jnp.int32, (CHUNK, W), 1)
    causal = (lane % HEAD) <= row
    bd = bd_ref[...]
    bd2 = jnp.concatenate([bd, bd], axis=1)
    i4_bf = i4_ref[...].astype(BF16)

    def block_diag(x):
        tiled = jnp.concatenate([x] * MLSTM_HEADS, axis=0)
        return tiled * (bd if x.shape[1] == W else bd2)

    first_half = lax.broadcasted_iota(jnp.int32, (1, LANES), 1) < HEAD

    def seg_max(d):
        tiles = []
        for j in range(W // LANES):
            tile = d[:, j * LANES:(j + 1) * LANES]
            m_first = jnp.max(jnp.where(first_half, tile, -jnp.inf), axis=-1, keepdims=True)
            m_second = jnp.max(jnp.where(first_half, -jnp.inf, tile), axis=-1, keepdims=True)
            tiles.append(jnp.where(first_half, m_first, m_second))
        return jnp.concatenate(tiles, axis=1)

    chunks = range(ts // CHUNK)
    rows = [pl.ds(c * CHUNK, CHUNK) for c in chunks]

    b, g, u, dm, dm_max, a_max, wa = [], [], [], [], [], [], []
    for c in chunks:
        ig = g_ref[0, rows[c], 0:W]
        lf_hi, lf_lo = _split2(g_ref[0, rows[c], W:2 * W])
        b_c = _dot(tri_ref[...], lf_hi) + _dot(tri_ref[...], lf_lo)
        g_c = b_c[CHUNK - 1:CHUNK, :]
        u_c = b_c - ig
        u_row = jnp.sum(u_c * i4_ref[...], axis=0, keepdims=True)
        dm_c = jnp.where(causal, b_c - u_row, -jnp.inf)
        a_c = g_c - u_c
        am_c = jnp.max(a_c, axis=0, keepdims=True)
        b.append(b_c); g.append(g_c); u.append(u_c); dm.append(dm_c)
        dm_max.append(seg_max(dm_c)); a_max.append(am_c); wa.append(jnp.exp2(a_c - am_c))

    m_prev, s_old, s_loc = [m_ref[0:1, :]], [], []
    for c in chunks:
        gm = g[c] + m_prev[c]
        m_new = jnp.maximum(gm, a_max[c])
        s_old.append(jnp.exp2(gm - m_new))
        s_loc.append(jnp.exp2(a_max[c] - m_new))
        m_prev.append(m_new)

    m_t, w_inter, sm_hi, sm_lo, k_t = [], [], [], [], []
    for c in chunks:
        inter = b[c] + m_prev[c]
        mt_c = jnp.maximum(inter, dm_max[c])
        decay = jnp.exp2(dm[c] - mt_c)
        q = qkv_ref[0, rows[c], 0:W]
        k = qkv_ref[0, rows[c], W:2 * W]
        qk = _dot_nt(jnp.concatenate([q, i4_bf], axis=0), block_diag(k))
        hi, lo = _split2(qk[0:CHUNK] * decay)
        m_t.append(mt_c); w_inter.append(jnp.exp2(inter - mt_c)); sm_hi.append(hi); sm_lo.append(lo)
        k_t.append(qk[CHUNK:].astype(BF16))

    c_cur, n_cur = c_ref[...], n_ref[...]
    state_bf = []
    for c in chunks:
        state_bf.append(jnp.concatenate([c_cur.astype(BF16), n_cur.astype(BF16)], axis=1))
        v = qkv_ref[0, rows[c], 2 * W:]
        weighted = jnp.concatenate([(wa[c] * v.astype(F32)).astype(BF16), wa[c].astype(BF16)], axis=1)
        loc = _dot(k_t[c], block_diag(weighted))
        c_cur = s_old[c] * c_cur + s_loc[c] * loc[:, 0:W]
        n_cur = s_old[c] * n_cur + s_loc[c] * loc[:, W:]
    c_ref[...] = c_cur
    n_ref[...] = n_cur
    m_ref[...] = jnp.broadcast_to(m_prev[-1], m_ref.shape)

    hval = []
    for c in chunks:
        q = qkv_ref[0, rows[c], 0:W]
        v = qkv_ref[0, rows[c], 2 * W:]
        inter_cn = _dot(q, block_diag(state_bf[c]))
        intra = _dot(sm_hi[c], jnp.concatenate([block_diag(v), bd], axis=1))
        num = w_inter[c] * inter_cn[:, 0:W] + intra[:, 0:W]
        den = w_inter[c] * inter_cn[:, W:] + intra[:, W:] + _dot(sm_lo[c], bd)
        hval.append(num / jnp.maximum(jnp.abs(den), jnp.exp2(-m_t[c])))

    for c in chunks:
        hs_hi, hs_lo = _split2(hval[c] * hval[c])
        ss = _dot(hs_hi, bd) + _dot(hs_lo, bd)
        hc = (hval[c] * lax.rsqrt(ss * (1.0 / HEAD) + EPS)) * onorm_ref[...]
        co = g_ref[0, rows[c], 2 * W:]
        o_ref[0, rows[c], :] = (jax.nn.sigmoid(co) * hc).astype(BF16)


def _mlstm(layer, qkvc, gc, onorm, tri, i4, bd):
    B, S, _ = qkvc.shape
    ts = min(TS_MLSTM, S)
    return pl.pallas_call(
        _mlstm_body,
        grid=(B, S // ts),
        in_specs=[
            pl.BlockSpec((1, ts, 3 * MLSTM_W), lambda b, i: (b, i, 0)),
            pl.BlockSpec((1, ts, 3 * MLSTM_W), lambda b, i: (b, i, 0)),
            _layer_spec((1, MLSTM_W), layer),
            _const_spec((CHUNK, CHUNK)),
            _const_spec((CHUNK, MLSTM_W)),
            _const_spec((MLSTM_W, MLSTM_W)),
        ],
        out_specs=pl.BlockSpec((1, ts, MLSTM_W), lambda b, i: (b, i, 0)),
        out_shape=jax.ShapeDtypeStruct((B, S, MLSTM_W), BF16),
        scratch_shapes=[pltpu.VMEM((HEAD, MLSTM_W), F32), pltpu.VMEM((HEAD, MLSTM_W), F32),
                        pltpu.VMEM((8, MLSTM_W), F32)],
        compiler_params=pltpu.CompilerParams(
            dimension_semantics=("arbitrary", "arbitrary"), vmem_limit_bytes=VMEM_LIMIT),
        name="mlstm",
    )(qkvc, gc, onorm, tri, i4, bd)


TM_FFN = 1024
FF_CHUNK = 1024


def _out_ffn_body(x_ref, ya_ref, yb_ref, yc_ref, wo_ref, n2_ref, w1_ref, w2_ref, o_ref, a_ref):
    x1 = (x_ref[...] + _dot(ya_ref[...], wo_ref[0:SWA_Q, :])
          + _dot(yb_ref[...], wo_ref[SWA_Q:SWA_Q + FOX_W, :])
          + _dot(yc_ref[...], wo_ref[SWA_Q + FOX_W:, :]))
    ms = jnp.mean(x1 * x1, axis=-1, keepdims=True)
    h2 = ((x1 * lax.rsqrt(ms + EPS)) * n2_ref[...]).astype(BF16)
    for j in range(D_FF // FF_CHUNK):
        cols = slice(j * FF_CHUNK, (j + 1) * FF_CHUNK)
        r = jnp.maximum(_dot(h2, w1_ref[:, cols]), 0.0)
        a_ref[:, cols] = (r * r).astype(BF16)
    o_ref[...] = x1 + _dot(a_ref[...], w2_ref[...])


def _out_ffn(layer, x2d, ya, yb, yc, wo, n2, w1, w2):
    T = x2d.shape[0]
    tm = min(TM_FFN, T)
    tok = lambda w: pl.BlockSpec((tm, w), lambda i: (i, 0))
    return pl.pallas_call(
        _out_ffn_body,
        grid=(T // tm,),
        in_specs=[
            tok(D_MODEL), tok(SWA_Q), tok(FOX_W), tok(MLSTM_W),
            _layer_spec((D_MODEL, D_MODEL), layer),
            _layer_spec((1, D_MODEL), layer),
            _const_spec((D_MODEL, D_FF)),
            _const_spec((D_FF, D_MODEL)),
        ],
        out_specs=tok(D_MODEL),
        out_shape=jax.ShapeDtypeStruct((T, D_MODEL), F32),
        scratch_shapes=[pltpu.VMEM((tm, D_FF), BF16)],
        compiler_params=pltpu.CompilerParams(
            dimension_semantics=("arbitrary",), vmem_limit_bytes=VMEM_LIMIT),
        name="out_ffn",
    )(x2d, ya, yb, yc, wo, n2, w1, w2)


def _rope_tables(S):
    inv = ROPE_THETA ** (-np.arange(HALF, dtype=np.float64) / HALF)
    ang = np.arange(S, dtype=np.float64)[:, None] * inv[None, :]
    cos = np.tile(np.cos(ang), (1, LANES // HALF))
    sin = np.tile(np.sin(ang), (1, LANES // HALF))
    sign = np.where(np.arange(LANES) < HEAD, -1.0, 1.0)
    return jnp.asarray(cos, F32), jnp.asarray(sin * sign[None, :], F32)


def _prep_w_in(w_in):
    select = lambda w, p: jnp.einsum("ldk,kn->ldn", w.astype(BF16), jnp.asarray(p, BF16),
                                     preferred_element_type=BF16)
    w_b = w_in[:, :, W_A:W_A + W_B].astype(BF16)
    return select(w_in[:, :, :W_A], _PERM_A), w_b, select(w_in[:, :, W_A + W_B:], _PERM_C)


def _prep_vecs(swa_q_norm, swa_k_norm, fox_q_norm, fox_k_norm, fox_f_bias, mlstm_i_bias, mlstm_f_bias):
    L = swa_q_norm.shape[0]

    def swa_lanes(g):
        return jnp.concatenate([g[:, :HALF], g[:, :HALF], g[:, HALF:], g[:, HALF:]], axis=-1)

    pad = lambda v: jnp.pad(v, ((0, 0), (0, MLSTM_W - v.shape[1])))
    return jnp.stack([
        pad(swa_lanes(swa_q_norm)), pad(swa_lanes(swa_k_norm)),
        pad(jnp.tile(fox_q_norm, (1, 2))), pad(jnp.tile(fox_k_norm, (1, 2))),
        pad(jnp.concatenate([jnp.tile(fox_f_bias, (1, 3)), mlstm_i_bias, mlstm_f_bias], axis=-1)),
        jnp.zeros((L, MLSTM_W), F32), jnp.zeros((L, MLSTM_W), F32), jnp.zeros((L, MLSTM_W), F32),
    ], axis=1).astype(F32)


def kernel(x, norm1, w_in, swa_q_norm, swa_k_norm, swa_sinks, fox_q_norm, fox_k_norm, fox_f_bias,
           mlstm_i_bias, mlstm_f_bias, mlstm_out_norm, w_out, norm2, w_ff1, w_ff2):
    B, S, D = x.shape
    L = norm1.shape[0]
    T = B * S
    cos, sin = _rope_tables(S)
    tri256, tri64, bd256 = (jnp.asarray(c, BF16) for c in (_TRI256, _TRI64, _BD256))
    i4 = jnp.asarray(_I4, F32)
    e3 = jnp.asarray(_gate_expand(), BF16)
    band = jnp.asarray(_swa_band_cap(), F32)

    w_a, w_b, w_c = _prep_w_in(w_in)
    wo = w_out.astype(BF16)
    vecs = _prep_vecs(swa_q_norm, swa_k_norm, fox_q_norm, fox_k_norm, fox_f_bias, mlstm_i_bias,
                      mlstm_f_bias)
    n1 = norm1.reshape(L, 1, D).astype(F32)
    n2 = norm2.reshape(L, 1, D).astype(F32)
    onorm = mlstm_out_norm.reshape(L, 1, MLSTM_W).astype(F32)
    sinks = swa_sinks.reshape(L * SWA_HEADS).astype(F32)

    for l in range(L):
        qa, kva, qb, kb, vb, qkvc, gc, w1, w2 = _in_proj(
            l, x, n1, w_a, w_b, w_c, vecs, cos, sin, tri256, e3, w_ff1, w_ff2)
        ya = _swa(l, sinks, qa, kva, band)
        yb = _fox(qb, kb, vb)
        yc = _mlstm(l, qkvc, gc, onorm, tri64, i4, bd256)
        x = _out_ffn(l, x.reshape(T, D), ya.reshape(T, SWA_Q), yb.reshape(T, FOX_W),
                     yc.reshape(T, MLSTM_W), wo, n2, w1, w2).reshape(B, S, D)
    return x
```

```python
import functools
import math

import numpy as np
import jax
import jax.numpy as jnp
from jax import lax
from jax.experimental import pallas as pl
from jax.experimental.pallas import tpu as pltpu

F32 = jnp.float32
BF16 = jnp.bfloat16

D_MODEL = 1024
HEAD = 64
HALF = HEAD // 2
CHUNK = 64
SWA_HEADS = 8
SWA_KV_HEADS = 2
FOX_HEADS = 4
MLSTM_HEADS = 4
SWA_Q = SWA_HEADS * HEAD
SWA_KV = SWA_KV_HEADS * HEAD
FOX_W = FOX_HEADS * HEAD
MLSTM_W = MLSTM_HEADS * HEAD
D_FF = 4 * D_MODEL
QBLK = 128
ROPE_THETA = 10000.0
EPS = 1e-6
QK_SCALE = 1.0 / math.sqrt(HEAD)
LOG2E = math.log2(math.e)

LANES = 128
MXU_DIM = 256
VMEM_LIMIT = 56 * 1024 * 1024

W_A = SWA_Q + 2 * SWA_KV
W_B = 3 * FOX_W
SEC_C = 0
SEC_CO = SEC_C + 768
SEC_G = SEC_CO + 256
W_C = SEC_G + LANES
GATE_BF = 0
GATE_CI = 12
GATE_CF = 16
IN_WIDTH = W_A + W_B + FOX_HEADS + 4 * MLSTM_W + 2 * MLSTM_HEADS


def _perm_a():
    p = np.zeros((W_A, W_A), np.float32)
    lane = np.arange(LANES)
    quarter, i = lane // HALF, lane % HALF
    dim, is_b = i + HALF * (quarter // 2), quarter % 2
    for j in range(SWA_HEADS // 2):
        p[(j + 4 * is_b) * HEAD + dim, j * LANES + lane] = 1.0
    p[SWA_Q + is_b * HEAD + dim, SWA_Q + lane] = 1.0
    p[SWA_Q + SWA_KV + lane, SWA_Q + SWA_KV + lane] = 1.0
    return p


def _perm_c():
    src = IN_WIDTH - W_A - W_B
    o_bf, o_cqkv = 0, FOX_HEADS
    o_ci = o_cqkv + 3 * MLSTM_W
    o_cf, o_co = o_ci + MLSTM_HEADS, o_ci + 2 * MLSTM_HEADS
    p = np.zeros((src, W_C), np.float32)
    w = np.arange(3 * MLSTM_W)
    p[o_cqkv + w, SEC_C + w] = 1.0
    w = np.arange(MLSTM_W)
    p[o_co + w, SEC_CO + w] = 1.0
    h = np.arange(FOX_HEADS)
    for rep in range(3):
        p[o_bf + h, SEC_G + GATE_BF + rep * FOX_HEADS + h] = 1.0
    h = np.arange(MLSTM_HEADS)
    p[o_ci + h, SEC_G + GATE_CI + h] = 1.0
    p[o_cf + h, SEC_G + GATE_CF + h] = 1.0
    return p


def _gate_expand():
    e = np.zeros((LANES, 2 * MLSTM_W), np.float32)
    for h in range(MLSTM_HEADS):
        e[GATE_CI + h, h * HEAD:(h + 1) * HEAD] = 1.0
        e[GATE_CF + h, MLSTM_W + h * HEAD:MLSTM_W + (h + 1) * HEAD] = 1.0
    return np.concatenate([e, e, e], axis=0)


_PERM_A = _perm_a()
_PERM_C = _perm_c()


def _block_diag_ones(head_of_lane):
    return (head_of_lane[:, None] == head_of_lane[None, :]).astype(np.float32)


_TRI256 = np.tril(np.ones((MXU_DIM, MXU_DIM), np.float32))
_TRI64 = np.tril(np.ones((CHUNK, CHUNK), np.float32))
_w256 = np.arange(MLSTM_W)
_BD256 = _block_diag_ones(_w256 // HEAD)
_I4 = (np.arange(CHUNK)[:, None] == (_w256 % HEAD)[None, :]).astype(np.float32)


def _swa_band_cap():
    key_chunk = np.arange(2 * QBLK)[:, None] // CHUNK
    qry_chunk = (np.arange(MXU_DIM)[None, :] % QBLK) // CHUNK
    ahead = key_chunk - qry_chunk
    return np.where((ahead >= 0) & (ahead <= 2), np.inf, -np.inf).astype(np.float32)


def _const_spec(shape):
    nd = len(shape)
    return pl.BlockSpec(shape, lambda *_: (0,) * nd, pipeline_mode=pl.Buffered(1))


def _layer_spec(shape, layer):
    nd = len(shape)
    return pl.BlockSpec((None,) + tuple(shape), lambda *_: (layer,) + (0,) * nd,
                        pipeline_mode=pl.Buffered(1))


def _log_sigmoid(x):
    return jnp.minimum(x, 0.0) - jnp.log(1.0 + jnp.exp(-jnp.abs(x)))


def _split2(x):
    hi = x.astype(BF16)
    lo = (x - hi.astype(F32)).astype(BF16)
    return hi, lo


def _split3_by_lane(x, lane):
    hi = x.astype(BF16).astype(F32)
    r1 = x - hi
    mid = r1.astype(BF16).astype(F32)
    low = r1 - mid
    return jnp.where(lane < 4, hi, jnp.where(lane < 8, mid, low)).astype(BF16)


def _dot(a, b):
    return jnp.dot(a, b, preferred_element_type=F32)


def _dot_nt(a, b):
    return lax.dot_general(a, b, (((1,), (1,)), ((), ())), preferred_element_type=F32)


TM_IN = 1024


def _in_proj_body(x_ref, n1_ref, wa_ref, wb_ref, wc_ref, vec_ref, cos_ref, sin_ref, tri_ref, e3_ref,
                  wff1_ref, wff2_ref,
                  qa_ref, kva_ref, qb_ref, kb_ref, vb_ref, qkvc_ref, gc_ref, wff1_bf_ref, wff2_bf_ref,
                  carry_ref):
    tm = x_ref.shape[1]
    wff1_bf_ref[...] = wff1_ref[...].astype(BF16)
    wff2_bf_ref[...] = wff2_ref[...].astype(BF16)

    @pl.when(pl.program_id(1) == 0)
    def _():
        carry_ref[...] = jnp.zeros_like(carry_ref)

    x = x_ref[0]
    ms = jnp.mean(x * x, axis=-1, keepdims=True)
    h = ((x * lax.rsqrt(ms + EPS)) * n1_ref[...]).astype(BF16)

    def proj(w_ref, c0, width):
        return _dot(h, w_ref[:, c0:c0 + width])

    lane_row = lax.broadcasted_iota(jnp.int32, (1, LANES), 1)
    first_swa = (lane_row // HALF) % 2 == 0
    first_nat = lane_row < HEAD

    def head_norm(z, first, gain):
        sq = z * z
        s_first = jnp.sum(jnp.where(first, sq, 0.0), axis=-1, keepdims=True)
        s_second = jnp.sum(jnp.where(first, 0.0, sq), axis=-1, keepdims=True)
        ss = jnp.where(first, s_first, s_second)
        return (z * lax.rsqrt(ss * (1.0 / HEAD) + EPS)) * gain

    cos = cos_ref[...]
    sin = sin_ref[...]

    def rope(z):
        return z * cos + pltpu.roll(z, HEAD, 1) * sin

    g_aq = vec_ref[0:1, 0:LANES]
    g_ak = vec_ref[1:2, 0:LANES]
    g_bq = vec_ref[2:3, 0:LANES]
    g_bk = vec_ref[3:4, 0:LANES]
    gate_bias = vec_ref[4:5, 0:LANES]

    lane = lax.broadcasted_iota(jnp.int32, (MXU_DIM, LANES), 1)
    zg = proj(wc_ref, SEC_G, LANES) + gate_bias
    log_sig = _log_sigmoid(zg)
    log_f = jnp.where(lane[0:1] < GATE_CI, log_sig, 0.0)
    f_terms = [_split3_by_lane(log_f[blk * MXU_DIM:(blk + 1) * MXU_DIM], lane)
               for blk in range(tm // MXU_DIM)]
    gz = jnp.where(lane[0:1] < GATE_CI, 0.0,
                   jnp.where(lane[0:1] < GATE_CF, zg,
                             jnp.where(lane[0:1] < GATE_CF + MLSTM_HEADS, log_sig, 0.0))) * LOG2E
    gz_hi = gz.astype(BF16)
    gz_r = gz - gz_hi.astype(F32)
    gz_mid = gz_r.astype(BF16)
    gz_terms = jnp.concatenate([gz_hi, gz_mid, (gz_r - gz_mid.astype(F32)).astype(BF16)], axis=1)

    za = proj(wa_ref, 0, W_A)
    for j in range(SWA_HEADS // 2):
        q = rope(head_norm(za[:, j * LANES:(j + 1) * LANES], first_swa, g_aq)) * (QK_SCALE * LOG2E)
        qa_ref[0, :, j * LANES:(j + 1) * LANES] = q.astype(BF16)
    k = rope(head_norm(za[:, SWA_Q:SWA_Q + LANES], first_swa, g_ak))
    kva_ref[0, :, 0:LANES] = k.astype(BF16)
    kva_ref[0, :, LANES:2 * LANES] = za[:, SWA_Q + SWA_KV:].astype(BF16)

    zb = proj(wb_ref, 0, W_B)
    for p in range(FOX_HEADS // 2):
        q = head_norm(zb[:, p * LANES:(p + 1) * LANES], first_nat, g_bq) * (QK_SCALE * LOG2E)
        qb_ref[0, :, p * LANES:(p + 1) * LANES] = q.astype(BF16)
        k = head_norm(zb[:, FOX_W + p * LANES:FOX_W + (p + 1) * LANES], first_nat, g_bk)
        kb_ref[0, :, 2 * p * LANES:(2 * p + 1) * LANES] = k.astype(BF16)
    vb_ref[0] = zb[:, 2 * FOX_W:].astype(BF16)

    zc = proj(wc_ref, SEC_C, 768)
    qkvc_ref[0, :, 0:MLSTM_W] = zc[:, 0:MLSTM_W].astype(BF16)
    qkvc_ref[0, :, MLSTM_W:2 * MLSTM_W] = (zc[:, MLSTM_W:2 * MLSTM_W] * QK_SCALE).astype(BF16)
    qkvc_ref[0, :, 2 * MLSTM_W:] = zc[:, 2 * MLSTM_W:].astype(BF16)
    gc_ref[0, :, 2 * MLSTM_W:] = proj(wc_ref, SEC_CO, MLSTM_W)

    gc_ref[0, :, 0:2 * MLSTM_W] = _dot(gz_terms, e3_ref[...])

    carry = carry_ref[0:1, :]
    for blk in range(tm // MXU_DIM):
        rows = slice(blk * MXU_DIM, (blk + 1) * MXU_DIM)
        cs = _dot(tri_ref[...], f_terms[blk])
        f_loc = cs + pltpu.roll(cs, LANES - 4, 1) + pltpu.roll(cs, LANES - 8, 1)
        f_cum = jnp.where(lane < 4, f_loc, 0.0) + carry
        carry = f_cum[MXU_DIM - 1:MXU_DIM, :]
        f_rep = (f_cum + pltpu.roll(f_cum, 4, 1) + pltpu.roll(f_cum, 8, 1)) * LOG2E
        aug = _split3_by_lane(f_rep, lane)
        kb_ref[0, rows, LANES:2 * LANES] = aug
        kb_ref[0, rows, 3 * LANES:4 * LANES] = aug
    carry_ref[...] = jnp.broadcast_to(carry, carry_ref.shape)


def _in_proj(layer, x, n1, w_a, w_b, w_c, vecs, cos, sin, tri, e3, w_ff1, w_ff2):
    B, S, _ = x.shape
    tm = min(TM_IN, S)
    n_i = S // tm
    grid = (B, n_i)
    steps = B * n_i
    assert D_MODEL % (16 * steps) == 0, "FFN weight slabs must be whole bf16 sublane tiles"
    r1, r2 = D_MODEL // steps, D_FF // steps
    tok = lambda w: pl.BlockSpec((1, tm, w), lambda b, i: (b, i, 0))
    out_shapes = (
        jax.ShapeDtypeStruct((B, S, SWA_Q), BF16),
        jax.ShapeDtypeStruct((B, S, 2 * SWA_KV), BF16),
        jax.ShapeDtypeStruct((B, S, FOX_W), BF16),
        jax.ShapeDtypeStruct((B, S, 2 * MXU_DIM), BF16),
        jax.ShapeDtypeStruct((B, S, FOX_W), BF16),
        jax.ShapeDtypeStruct((B, S, 3 * MLSTM_W), BF16),
        jax.ShapeDtypeStruct((B, S, 3 * MLSTM_W), F32),
        jax.ShapeDtypeStruct((D_MODEL, D_FF), BF16),
        jax.ShapeDtypeStruct((D_FF, D_MODEL), BF16),
    )
    slab_out = [pl.BlockSpec((r1, D_FF), lambda b, i: (b * n_i + i, 0)),
                pl.BlockSpec((r2, D_MODEL), lambda b, i: (b * n_i + i, 0))]
    return pl.pallas_call(
        _in_proj_body,
        grid=grid,
        in_specs=[
            tok(D_MODEL),
            _layer_spec((1, D_MODEL), layer),
            _layer_spec((D_MODEL, W_A), layer),
            _layer_spec((D_MODEL, W_B), layer),
            _layer_spec((D_MODEL, W_C), layer),
            _layer_spec((8, MLSTM_W), layer),
            pl.BlockSpec((tm, LANES), lambda b, i: (i, 0)),
            pl.BlockSpec((tm, LANES), lambda b, i: (i, 0)),
            _const_spec((MXU_DIM, MXU_DIM)),
            _const_spec((3 * LANES, 2 * MLSTM_W)),
            pl.BlockSpec((None, r1, D_FF), lambda b, i: (layer, b * n_i + i, 0)),
            pl.BlockSpec((None, r2, D_MODEL), lambda b, i: (layer, b * n_i + i, 0)),
        ],
        out_specs=[tok(s.shape[-1]) for s in out_shapes[:-2]] + slab_out,
        out_shape=out_shapes,
        scratch_shapes=[pltpu.VMEM((8, LANES), F32)],
        compiler_params=pltpu.CompilerParams(
            dimension_semantics=("arbitrary", "arbitrary"), vmem_limit_bytes=VMEM_LIMIT),
        name="in_proj",
    )(x, n1, w_a, w_b, w_c, vecs, cos, sin, tri, e3, w_ff1, w_ff2)


TS_SWA = 1024
SWA_DEPTH = 5
VT_ROWS = HEAD + 16


def _swa_body(sink_base, sink_ref, q_ref, kvc_ref, kvp_ref, band_ref, o_ref):
    step = pl.program_id(1)
    nblk = q_ref.shape[1] // QBLK
    per_group = SWA_HEADS // SWA_KV_HEADS

    band_cap = band_ref[...]
    key = lax.broadcasted_iota(jnp.int32, (2 * QBLK, MXU_DIM), 0)
    first_cap = jnp.where((key >= QBLK) | (step > 0), band_cap, -jnp.inf)
    lane = lax.broadcasted_iota(jnp.int32, (QBLK, LANES), 1)
    lane_a = (lane // HALF) % 2 == 0
    first_head = lax.broadcasted_iota(jnp.int32, (1, MXU_DIM), 1) < QBLK
    zero = jnp.zeros((QBLK, LANES), BF16)
    ones = jnp.ones((VT_ROWS - HEAD, 2 * QBLK), BF16)
    kv_all = jnp.concatenate([kvp_ref[0], kvc_ref[0]], axis=0)
    rows = [slice(blk * QBLK, (blk + 1) * QBLK) for blk in range(nblk)]
    bands = [kv_all[blk * QBLK:(blk + 2) * QBLK] for blk in range(nblk)]

    vt_aug = []
    for blk in range(nblk):
        v_t = bands[blk][:, LANES:].astype(F32).T.astype(BF16)
        vt_aug.append([jnp.concatenate([v_t[g * HEAD:(g + 1) * HEAD], ones], axis=0)
                       for g in range(SWA_KV_HEADS)])

    units = [(blk, g, part) for blk in range(nblk) for g in range(SWA_KV_HEADS) for part in range(2)]

    def score(blk, g, part):
        keep = lane_a if g == 0 else jnp.logical_not(lane_a)
        q_cols = [q_ref[0, rows[blk], j * LANES:(j + 1) * LANES] for j in (2 * part, 2 * part + 1)]
        q_g = jnp.concatenate([jnp.where(keep, q2, zero) for q2 in q_cols], axis=0)
        return _dot_nt(bands[blk][:, 0:LANES], q_g)

    def fold(s, blk, g, part):
        if blk > 0:
            s = jnp.concatenate([jnp.minimum(s[0:CHUNK], band_cap[0:CHUNK]), s[CHUNK:3 * CHUNK],
                                 jnp.minimum(s[3 * CHUNK:], band_cap[3 * CHUNK:])], axis=0)
        else:
            s = jnp.minimum(s, first_cap)
        head = sink_base + g * per_group + 2 * part
        sink = jnp.where(first_head, sink_ref[head], sink_ref[head + 1]) * LOG2E
        m = jnp.maximum(jnp.max(s, axis=0, keepdims=True), sink)
        p = jnp.exp2(s - m)
        o_aug = _dot(vt_aug[blk][g], p.astype(BF16))
        den = o_aug[HEAD:HEAD + 1] + jnp.exp2(sink - m)
        return o_aug[0:HEAD] / den

    def write_block(blk, outs):
        for g in range(SWA_KV_HEADS):
            for part in range(2):
                o = outs[(blk, g, part)]
                pair_t = jnp.concatenate([o[:, 0:QBLK], o[:, QBLK:]], axis=0)
                j = 2 * g + part
                o_ref[0, rows[blk], j * LANES:(j + 1) * LANES] = pair_t.T.astype(BF16)

    outs = {}
    queue = [score(*unit) for unit in units[:SWA_DEPTH]]
    for n, unit in enumerate(units):
        s_cur = queue.pop(0)
        if n + SWA_DEPTH < len(units):
            queue.append(score(*units[n + SWA_DEPTH]))
        outs[unit] = fold(s_cur, *unit)
        blk = unit[0]
        if unit[1:] == (SWA_KV_HEADS - 1, 1) and blk > 0:
            write_block(blk - 1, outs)
    write_block(nblk - 1, outs)


def _swa(layer, sinks, qa, kva, band):
    B, S, _ = qa.shape
    ts = min(TS_SWA, S)
    per_step = ts // QBLK
    return pl.pallas_call(
        functools.partial(_swa_body, layer * SWA_HEADS),
        grid=(B, S // ts),
        in_specs=[
            pl.BlockSpec(memory_space=pltpu.SMEM),
            pl.BlockSpec((1, ts, SWA_Q), lambda b, i: (b, i, 0)),
            pl.BlockSpec((1, ts, 2 * SWA_KV), lambda b, i: (b, i, 0)),
            pl.BlockSpec((1, QBLK, 2 * SWA_KV), lambda b, i: (b, jnp.maximum(i * per_step - 1, 0), 0)),
            _const_spec((2 * QBLK, MXU_DIM)),
        ],
        out_specs=pl.BlockSpec((1, ts, SWA_Q), lambda b, i: (b, i, 0)),
        out_shape=jax.ShapeDtypeStruct((B, S, SWA_Q), BF16),
        compiler_params=pltpu.CompilerParams(
            dimension_semantics=("arbitrary", "arbitrary"), vmem_limit_bytes=VMEM_LIMIT),
        name="swa",
    )(sinks, qa, kva, kva, band)


TQ_FOX = 1024
TK_FOX = 256
NEG_BIG = -1e30


FOX_UNROLL = 4
FOX_DEPTH = 10


def _fox_body(q_ref, k_ref, v_ref, o_ref, vt_ref, qt_ref, acc_ref):
    tq = q_ref.shape[1]
    tk = TK_FOX
    S = k_ref.shape[1]
    pair = pl.program_id(1)
    qi = pl.program_id(2)
    n_tiles = 2 * tq // MXU_DIM
    tiles_per_head = tq // MXU_DIM

    @pl.when(qi == 0)
    def _():
        ones = jnp.ones((VT_ROWS - HEAD, tk), BF16)
        for j in range(S // tk):
            blk = slice(j * tk, (j + 1) * tk)
            v_t = v_ref[0, blk, :].astype(F32).T.astype(BF16)
            vt_ref[:, blk] = jnp.concatenate([v_t[0:HEAD], ones, v_t[HEAD:], ones], axis=0)

    lane = lax.broadcasted_iota(jnp.int32, (tq, LANES), 1)
    lane_a = lane < HEAD
    q2 = q_ref[0]
    zero = jnp.zeros_like(q2)

    def minus_one_at(head):
        hit = (lane == head) | (lane == head + 4) | (lane == head + 8)
        return jnp.where(hit, -1.0, 0.0).astype(BF16)

    q_aug = [jnp.concatenate([jnp.where(lane_a, q2, zero), minus_one_at(2 * pair)], axis=1),
             jnp.concatenate([jnp.where(lane_a, zero, q2), minus_one_at(2 * pair + 1)], axis=1)]
    for t in range(n_tiles):
        hd, part = divmod(t, tiles_per_head)
        rows = q_aug[hd][part * MXU_DIM:(part + 1) * MXU_DIM]
        qt_ref[:, t * MXU_DIM:(t + 1) * MXU_DIM] = rows.astype(F32).T.astype(BF16)
    acc_ref[...] = jnp.zeros_like(acc_ref)

    key = lax.broadcasted_iota(jnp.int32, (tk // 2, MXU_DIM), 0)
    qry = lax.broadcasted_iota(jnp.int32, (tk // 2, MXU_DIM), 1)

    def fold(s, t, vt_blk, m, triangular):
        hd = t // tiles_per_head
        vt_h = vt_blk[hd * VT_ROWS:(hd + 1) * VT_ROWS]
        half = tk // 2
        for h in range(2):
            sh = s[h * half:(h + 1) * half]
            if triangular:
                valid = key + h * half <= qry
                sh = jnp.where(valid, sh, NEG_BIG)
            m_new = jnp.maximum(m, jnp.max(sh, axis=0, keepdims=True))
            alpha = jnp.exp2(m - m_new)
            p = jnp.exp2(sh - m_new)
            if triangular:
                p = jnp.where(valid, p, 0.0)
            acc_ref[t] = alpha * acc_ref[t] + _dot(vt_h[:, h * half:(h + 1) * half], p.astype(BF16))
            m = m_new
        return m

    def region(j0, ms, plans):
        k_blks, vt_blks, todo = [], [], []
        for b, plan in enumerate(plans):
            r0 = pl.multiple_of((j0 + b) * tk, tk)
            k_blks.append(k_ref[0, pl.ds(r0, tk), :])
            vt_blks.append(vt_ref[:, pl.ds(r0, tk)])
            todo += [(b, t) for t in range(n_tiles) if plan[t] is not None]
        score = lambda b, t: _dot(k_blks[b], qt_ref[:, t * MXU_DIM:(t + 1) * MXU_DIM])
        ms = list(ms)
        queue = [score(*unit) for unit in todo[:FOX_DEPTH]]
        for n, (b, t) in enumerate(todo):
            s_cur = queue.pop(0)
            if n + FOX_DEPTH < len(todo):
                queue.append(score(*todo[n + FOX_DEPTH]))
            ms[t] = fold(s_cur, t, vt_blks[b], ms[t], plans[b][t])
        return tuple(ms)

    blocks_per_tile = tq // tk
    full = (False,) * n_tiles
    n_full = blocks_per_tile * qi
    m_init = jnp.full((1, MXU_DIM), NEG_BIG, F32)
    ms = lax.fori_loop(0, n_full // FOX_UNROLL,
                       lambda g, c: region(g * FOX_UNROLL, c, (full,) * FOX_UNROLL), (m_init,) * n_tiles)
    if FOX_UNROLL > blocks_per_tile:
        done = (n_full // FOX_UNROLL) * FOX_UNROLL
        ms = lax.fori_loop(0, (n_full - done) // blocks_per_tile,
                           lambda g, c: region(done + g * blocks_per_tile, c, (full,) * blocks_per_tile), ms)
    diag = [tuple(None if t % tiles_per_head < d else t % tiles_per_head == d for t in range(n_tiles))
            for d in range(blocks_per_tile)]
    ms = region(n_full, ms, diag)

    for hd in range(2):
        cols = []
        for part in range(tiles_per_head):
            acc = acc_ref[hd * tiles_per_head + part]
            cols.append(acc[0:HEAD] / acc[HEAD:HEAD + 1])
        o_h = jnp.concatenate(cols, axis=1)
        if hd == 0:
            o_t = o_h
        else:
            o_t = jnp.concatenate([o_t, o_h], axis=0)
    for c in range(tq // MXU_DIM):
        rows = slice(c * MXU_DIM, (c + 1) * MXU_DIM)
        o_ref[0, rows, :] = o_t[:, rows].T.astype(BF16)


def _fox(qb, kb, vb):
    B, S, _ = qb.shape
    tq = min(TQ_FOX, S)
    return pl.pallas_call(
        _fox_body,
        grid=(B, FOX_HEADS // 2, S // tq),
        in_specs=[
            pl.BlockSpec((1, tq, LANES), lambda b, p, i: (b, i, p)),
            pl.BlockSpec((1, S, MXU_DIM), lambda b, p, i: (b, 0, p)),
            pl.BlockSpec((1, S, LANES), lambda b, p, i: (b, 0, p)),
        ],
        out_specs=pl.BlockSpec((1, tq, LANES), lambda b, p, i: (b, i, p)),
        out_shape=jax.ShapeDtypeStruct((B, S, FOX_W), BF16),
        scratch_shapes=[
            pltpu.VMEM((2 * VT_ROWS, S), BF16),
            pltpu.VMEM((MXU_DIM, 2 * tq), BF16),
            pltpu.VMEM((2 * tq // MXU_DIM, VT_ROWS, MXU_DIM), F32),
        ],
        compiler_params=pltpu.CompilerParams(
            dimension_semantics=("arbitrary", "arbitrary", "arbitrary"),
            vmem_limit_bytes=VMEM_LIMIT),
        name="fox",
    )(qb, kb, vb)


TS_MLSTM = 1024


def _mlstm_body(qkv_ref, g_ref, onorm_ref, tri_ref, i4_ref, bd_ref, o_ref, c_ref, n_ref, m_ref):
    ts = qkv_ref.shape[1]
    W = MLSTM_W

    @pl.when(pl.program_id(1) == 0)
    def _():
        c_ref[...] = jnp.zeros_like(c_ref)
        n_ref[...] = jnp.zeros_like(n_ref)
        m_ref[...] = jnp.zeros_like(m_ref)

    row = lax.broadcasted_iota(jnp.int32, (CHUNK, W), 0)
    lane = lax.broadcasted_iota(jnp.int32, (CHUNK, W), 1)
    causal = (lane % HEAD) <= row
    bd = bd_ref[...]
    bd2 = jnp.concatenate([bd, bd], axis=1)
    i4_bf = i4_ref[...].astype(BF16)

    def block_diag(x):
        tiled = jnp.concatenate([x] * MLSTM_HEADS, axis=0)
        return tiled * (bd if x.shape[1] == W else bd2)

    first_half = lax.broadcasted_iota(jnp.int32, (1, LANES), 1) < HEAD

    def seg_max(d):
        tiles = []
        for j in range(W // LANES):
            tile = d[:, j * LANES:(j + 1) * LANES]
            m_first = jnp.max(jnp.where(first_half, tile, -jnp.inf), axis=-1, keepdims=True)
            m_second = jnp.max(jnp.where(first_half, -jnp.inf, tile), axis=-1, keepdims=True)
            tiles.append(jnp.where(first_half, m_first, m_second))
        return jnp.concatenate(tiles, axis=1)

    chunks = range(ts // CHUNK)
    rows = [pl.ds(c * CHUNK, CHUNK) for c in chunks]

    b, g, u, dm, dm_max, a_max, wa = [], [], [], [], [], [], []
    for c in chunks:
        ig = g_ref[0, rows[c], 0:W]
        lf_hi, lf_lo = _split2(g_ref[0, rows[c], W:2 * W])
        b_c = _dot(tri_ref[...], lf_hi) + _dot(tri_ref[...], lf_lo)
        g_c = b_c[CHUNK - 1:CHUNK, :]
        u_c = b_c - ig
        u_row = jnp.sum(u_c * i4_ref[...], axis=0, keepdims=True)
        dm_c = jnp.where(causal, b_c - u_row, -jnp.inf)
        a_c = g_c - u_c
        am_c = jnp.max(a_c, axis=0, keepdims=True)
        b.append(b_c); g.append(g_c); u.append(u_c); dm.append(dm_c)
        dm_max.append(seg_max(dm_c)); a_max.append(am_c); wa.append(jnp.exp2(a_c - am_c))

    m_prev, s_old, s_loc = [m_ref[0:1, :]], [], []
    for c in chunks:
        gm = g[c] + m_prev[c]
        m_new = jnp.maximum(gm, a_max[c])
        s_old.append(jnp.exp2(gm - m_new))
        s_loc.append(jnp.exp2(a_max[c] - m_new))
        m_prev.append(m_new)

    m_t, w_inter, sm_hi, sm_lo, k_t = [], [], [], [], []
    for c in chunks:
        inter = b[c] + m_prev[c]
        mt_c = jnp.maximum(inter, dm_max[c])
        decay = jnp.exp2(dm[c] - mt_c)
        q = qkv_ref[0, rows[c], 0:W]
        k = qkv_ref[0, rows[c], W:2 * W]
        qk = _dot_nt(jnp.concatenate([q, i4_bf], axis=0), block_diag(k))
        hi, lo = _split2(qk[0:CHUNK] * decay)
        m_t.append(mt_c); w_inter.append(jnp.exp2(inter - mt_c)); sm_hi.append(hi); sm_lo.append(lo)
        k_t.append(qk[CHUNK:].astype(BF16))

    c_cur, n_cur = c_ref[...], n_ref[...]
    state_bf = []
    for c in chunks:
        state_bf.append(jnp.concatenate([c_cur.astype(BF16), n_cur.astype(BF16)], axis=1))
        v = qkv_ref[0, rows[c], 2 * W:]
        wa_bf = wa[c].astype(BF16)
        weighted = jnp.concatenate([wa_bf * v, wa_bf], axis=1)
        loc = _dot(k_t[c], block_diag(weighted))
        c_cur = s_old[c] * c_cur + s_loc[c] * loc[:, 0:W]
        n_cur = s_old[c] * n_cur + s_loc[c] * loc[:, W:]
    c_ref[...] = c_cur
    n_ref[...] = n_cur
    m_ref[...] = jnp.broadcast_to(m_prev[-1], m_ref.shape)

    hval = []
    for c in chunks:
        q = qkv_ref[0, rows[c], 0:W]
        v = qkv_ref[0, rows[c], 2 * W:]
        inter_cn = _dot(q, block_diag(state_bf[c]))
        intra = _dot(sm_hi[c], jnp.concatenate([block_diag(v), bd], axis=1))
        num = w_inter[c] * inter_cn[:, 0:W] + intra[:, 0:W]
        den = w_inter[c] * inter_cn[:, W:] + intra[:, W:] + _dot(sm_lo[c], bd)
        hval.append(num / jnp.maximum(jnp.abs(den), jnp.exp2(-m_t[c])))

    for c in chunks:
        hs_hi, hs_lo = _split2(hval[c] * hval[c])
        ss = _dot(hs_hi, bd) + _dot(hs_lo, bd)
        hc = (hval[c] * lax.rsqrt(ss * (1.0 / HEAD) + EPS)) * onorm_ref[...]
        co = g_ref[0, rows[c], 2 * W:]
        o_ref[0, rows[c], :] = (jax.nn.sigmoid(co) * hc).astype(BF16)


def _mlstm(layer, qkvc, gc, onorm, tri, i4, bd):
    B, S, _ = qkvc.shape
    ts = min(TS_MLSTM, S)
    return pl.pallas_call(
        _mlstm_body,
        grid=(B, S // ts),
        in_specs=[
            pl.BlockSpec((1, ts, 3 * MLSTM_W), lambda b, i: (b, i, 0)),
            pl.BlockSpec((1, ts, 3 * MLSTM_W), lambda b, i: (b, i, 0)),
            _layer_spec((1, MLSTM_W), layer),
            _const_spec((CHUNK, CHUNK)),
            _const_spec((CHUNK, MLSTM_W)),
            _const_spec((MLSTM_W, MLSTM_W)),
        ],
        out_specs=pl.BlockSpec((1, ts, MLSTM_W), lambda b, i: (b, i, 0)),
        out_shape=jax.ShapeDtypeStruct((B, S, MLSTM_W), BF16),
        scratch_shapes=[pltpu.VMEM((HEAD, MLSTM_W), F32), pltpu.VMEM((HEAD, MLSTM_W), F32),
                        pltpu.VMEM((8, MLSTM_W), F32)],
        compiler_params=pltpu.CompilerParams(
            dimension_semantics=("arbitrary", "arbitrary"), vmem_limit_bytes=VMEM_LIMIT),
        name="mlstm",
    )(qkvc, gc, onorm, tri, i4, bd)


TM_FFN = 1024
FF_CHUNK = 1024


def _out_ffn_body(x_ref, ya_ref, yb_ref, yc_ref, wo_ref, n2_ref, w1_ref, w2_ref, o_ref, a_ref):
    x1 = (x_ref[...] + _dot(ya_ref[...], wo_ref[0:SWA_Q, :])
          + _dot(yb_ref[...], wo_ref[SWA_Q:SWA_Q + FOX_W, :])
          + _dot(yc_ref[...], wo_ref[SWA_Q + FOX_W:, :]))
    ms = jnp.mean(x1 * x1, axis=-1, keepdims=True)
    h2 = ((x1 * lax.rsqrt(ms + EPS)) * n2_ref[...]).astype(BF16)
    for j in range(D_FF // FF_CHUNK):
        cols = slice(j * FF_CHUNK, (j + 1) * FF_CHUNK)
        r = jnp.maximum(_dot(h2, w1_ref[:, cols]), 0.0)
        a_ref[:, cols] = (r * r).astype(BF16)
    o_ref[...] = x1 + _dot(a_ref[...], w2_ref[...])


def _out_ffn(layer, x2d, ya, yb, yc, wo, n2, w1, w2):
    T = x2d.shape[0]
    tm = min(TM_FFN, T)
    tok = lambda w: pl.BlockSpec((tm, w), lambda i: (i, 0))
    return pl.pallas_call(
        _out_ffn_body,
        grid=(T // tm,),
        in_specs=[
            tok(D_MODEL), tok(SWA_Q), tok(FOX_W), tok(MLSTM_W),
            _layer_spec((D_MODEL, D_MODEL), layer),
            _layer_spec((1, D_MODEL), layer),
            _const_spec((D_MODEL, D_FF)),
            _const_spec((D_FF, D_MODEL)),
        ],
        out_specs=tok(D_MODEL),
        out_shape=jax.ShapeDtypeStruct((T, D_MODEL), F32),
        scratch_shapes=[pltpu.VMEM((tm, D_FF), BF16)],
        compiler_params=pltpu.CompilerParams(
            dimension_semantics=("arbitrary",), vmem_limit_bytes=VMEM_LIMIT),
        name="out_ffn",
    )(x2d, ya, yb, yc, wo, n2, w1, w2)


def _rope_tables(S):
    inv = ROPE_THETA ** (-np.arange(HALF, dtype=np.float64) / HALF)
    ang = np.arange(S, dtype=np.float64)[:, None] * inv[None, :]
    cos = np.tile(np.cos(ang), (1, LANES // HALF))
    sin = np.tile(np.sin(ang), (1, LANES // HALF))
    sign = np.where(np.arange(LANES) < HEAD, -1.0, 1.0)
    return jnp.asarray(cos, F32), jnp.asarray(sin * sign[None, :], F32)


def _prep_w_in(w_in):
    select = lambda w, p: jnp.einsum("ldk,kn->ldn", w.astype(BF16), jnp.asarray(p, BF16),
                                     preferred_element_type=BF16)
    w_b = w_in[:, :, W_A:W_A + W_B].astype(BF16)
    return select(w_in[:, :, :W_A], _PERM_A), w_b, select(w_in[:, :, W_A + W_B:], _PERM_C)


def _prep_vecs(swa_q_norm, swa_k_norm, fox_q_norm, fox_k_norm, fox_f_bias, mlstm_i_bias, mlstm_f_bias):
    L = swa_q_norm.shape[0]

    def swa_lanes(g):
        return jnp.concatenate([g[:, :HALF], g[:, :HALF], g[:, HALF:], g[:, HALF:]], axis=-1)

    pad = lambda v: jnp.pad(v, ((0, 0), (0, MLSTM_W - v.shape[1])))
    return jnp.stack([
        pad(swa_lanes(swa_q_norm)), pad(swa_lanes(swa_k_norm)),
        pad(jnp.tile(fox_q_norm, (1, 2))), pad(jnp.tile(fox_k_norm, (1, 2))),
        pad(jnp.concatenate([jnp.tile(fox_f_bias, (1, 3)), mlstm_i_bias, mlstm_f_bias], axis=-1)),
        jnp.zeros((L, MLSTM_W), F32), jnp.zeros((L, MLSTM_W), F32), jnp.zeros((L, MLSTM_W), F32),
    ], axis=1).astype(F32)


def kernel(x, norm1, w_in, swa_q_norm, swa_k_norm, swa_sinks, fox_q_norm, fox_k_norm, fox_f_bias,
           mlstm_i_bias, mlstm_f_bias, mlstm_out_norm, w_out, norm2, w_ff1, w_ff2):
    B, S, D = x.shape
    L = norm1.shape[0]
    T = B * S
    cos, sin = _rope_tables(S)
    tri256, tri64, bd256 = (jnp.asarray(c, BF16) for c in (_TRI256, _TRI64, _BD256))
    i4 = jnp.asarray(_I4, F32)
    e3 = jnp.asarray(_gate_expand(), BF16)
    band = jnp.asarray(_swa_band_cap(), F32)

    w_a, w_b, w_c = _prep_w_in(w_in)
    wo = w_out.astype(BF16)
    vecs = _prep_vecs(swa_q_norm, swa_k_norm, fox_q_norm, fox_k_norm, fox_f_bias, mlstm_i_bias,
                      mlstm_f_bias)
    n1 = norm1.reshape(L, 1, D).astype(F32)
    n2 = norm2.reshape(L, 1, D).astype(F32)
    onorm = mlstm_out_norm.reshape(L, 1, MLSTM_W).astype(F32)
    sinks = swa_sinks.reshape(L * SWA_HEADS).astype(F32)

    for l in range(L):
        qa, kva, qb, kb, vb, qkvc, gc, w1, w2 = _in_proj(
            l, x, n1, w_a, w_b, w_c, vecs, cos, sin, tri256, e3, w_ff1, w_ff2)
        ya = _swa(l, sinks, qa, kva, band)
        yb = _fox(qb, kb, vb)
        yc = _mlstm(l, qkvc, gc, onorm, tri64, i4, bd256)
        x = _out_ffn(l, x.reshape(T, D), ya.reshape(T, SWA_Q), yb.reshape(T, FOX_W),
                     yc.reshape(T, MLSTM_W), wo, n2, w1, w2).reshape(B, S, D)
    return x
```
